```python
import math
import jax, jax.numpy as jnp
from jax import lax
import numpy as np

D_MODEL = 1024
BATCH = 8
SEQ = 4096
DEPTH = 4

HEAD_DIM = 64
SB_HEADS = 8
SWA_HEADS = 8
SWA_KV_HEADS = 2
WINDOW = 128
BLOCK = 128
D_FF = 4 * D_MODEL
ROPE_THETA = 10000.0
NORM_EPS = 1e-6
N_BRANCHES = 2

SB_WIDTH = SB_HEADS * HEAD_DIM
SWA_Q_WIDTH = SWA_HEADS * HEAD_DIM
SWA_KV_WIDTH = SWA_KV_HEADS * HEAD_DIM
IN_WIDTH = 3 * SB_WIDTH + SWA_Q_WIDTH + 2 * SWA_KV_WIDTH + N_BRANCHES * D_MODEL

kernel_name = "hybrid_stickbreak_swa_sink_gated_trunk"


def rms_norm(x, g):
    xf = x.astype(jnp.float32)
    y = xf * lax.rsqrt(jnp.mean(xf * xf, axis=-1, keepdims=True) + NORM_EPS)
    return (y * g.astype(jnp.float32)).astype(x.dtype)


def rope_tables(seq):
    inv_freq = 1.0 / (ROPE_THETA ** (jnp.arange(0, HEAD_DIM, 2, dtype=jnp.float32) / HEAD_DIM))
    ang = jnp.arange(seq, dtype=jnp.float32)[:, None] * inv_freq[None, :]
    return jnp.cos(ang), jnp.sin(ang)


def apply_rope(x, cos, sin):
    c = cos[None, :, None, :].astype(x.dtype)
    s = sin[None, :, None, :].astype(x.dtype)
    x1, x2 = jnp.split(x, 2, axis=-1)
    return jnp.concatenate([x1 * c - x2 * s, x2 * c + x1 * s], axis=-1)


def stick_breaking_attention(q, k, v):
    B, S, H, d = q.shape
    scale = d ** -0.5
    outs = []
    for blk in range(S // BLOCK):
        t0, t1 = blk * BLOCK, (blk + 1) * BLOCK
        qb = q[:, t0:t1]
        kb = k[:, :t1]
        vb = v[:, :t1]
        z = jnp.einsum('bqhd,bshd->bhqs', qb, kb).astype(jnp.float32) * scale
        t_idx = jnp.arange(t0, t1)[:, None]
        s_idx = jnp.arange(t1)[None, :]
        strict = s_idx < t_idx
        log_keep = jnp.where(strict, -jax.nn.softplus(z), 0.0)
        tail = lax.cumsum(log_keep, axis=3, reverse=True) - log_keep
        w = jnp.where(strict, jnp.exp(jax.nn.log_sigmoid(z) + tail), 0.0)
        outs.append(jnp.einsum('bhqs,bshd->bqhd', w.astype(v.dtype), vb))
    return jnp.concatenate(outs, axis=1)


def sliding_window_sink_attention(q, k, v, sinks):
    B, S, Hq, d = q.shape
    G = Hq // SWA_KV_HEADS
    nb = S // BLOCK
    scale = d ** -0.5
    qb = q.reshape(B, nb, BLOCK, SWA_KV_HEADS, G, d)
    pad = ((0, 0), (BLOCK, 0), (0, 0), (0, 0))
    kb = jnp.pad(k, pad).reshape(B, nb + 1, BLOCK, SWA_KV_HEADS, d)
    vb = jnp.pad(v, pad).reshape(B, nb + 1, BLOCK, SWA_KV_HEADS, d)
    k_band = jnp.concatenate([kb[:, :-1], kb[:, 1:]], axis=2)
    v_band = jnp.concatenate([vb[:, :-1], vb[:, 1:]], axis=2)
    scores = jnp.einsum('bnqkgd,bnskd->bnkgqs', qb, k_band).astype(jnp.float32) * scale
    i = jnp.arange(BLOCK)[:, None]
    j = jnp.arange(2 * BLOCK)[None, :]
    rel = j - BLOCK - i
    in_window = (rel <= 0) & (rel > -WINDOW)
    key_pos = jnp.arange(nb)[:, None, None] * BLOCK + j[None] - BLOCK
    valid = in_window[None] & (key_pos >= 0)
    scores = jnp.where(valid[None, :, None, None], scores, -jnp.inf)
    sink = jnp.broadcast_to(sinks.astype(jnp.float32).reshape(1, 1, SWA_KV_HEADS, G, 1, 1),
                            scores.shape[:-1] + (1,))
    probs = jax.nn.softmax(jnp.concatenate([scores, sink], axis=-1), axis=-1)[..., :-1]
    out = jnp.einsum('bnkgqs,bnskd->bnqkgd', probs.astype(v.dtype), v_band)
    return out.reshape(B, S, Hq * d)


def _fwd_setup_inputs(seed: int = 0) -> dict:
    key = jax.random.key(seed)
    ks = jax.random.split(key, 12)
    nrm = lambda k, shape, scale: jax.random.normal(k, shape, jnp.float32) * scale
    return {
        "x": nrm(ks[0], (BATCH, SEQ, D_MODEL), 1.0),
        "mix_norm_g": 1.0 + nrm(ks[1], (DEPTH, D_MODEL), 0.02),
        "w_in": nrm(ks[2], (DEPTH, D_MODEL, IN_WIDTH), D_MODEL ** -0.5),
        "q_norm_g": 1.0 + nrm(ks[3], (DEPTH, HEAD_DIM), 0.02),
        "k_norm_g": 1.0 + nrm(ks[4], (DEPTH, HEAD_DIM), 0.02),
        "sinks": nrm(ks[5], (DEPTH, SWA_HEADS), 0.5),
        "w_branch_sb": nrm(ks[6], (DEPTH, SB_WIDTH, D_MODEL), SB_WIDTH ** -0.5),
        "w_branch_swa": nrm(ks[7], (DEPTH, SWA_Q_WIDTH, D_MODEL), SWA_Q_WIDTH ** -0.5),
        "w_out": nrm(ks[8], (DEPTH, D_MODEL, D_MODEL), D_MODEL ** -0.5),
        "mlp_norm_g": 1.0 + nrm(ks[9], (DEPTH, D_MODEL), 0.02),
        "w_up": nrm(ks[10], (DEPTH, D_MODEL, D_FF), D_MODEL ** -0.5),
        "w_down": nrm(ks[11], (DEPTH, D_FF, D_MODEL), D_FF ** -0.5),
    }


def _fwd_reference(x, mix_norm_g, w_in, q_norm_g, k_norm_g, sinks, w_branch_sb, w_branch_swa,
              w_out, mlp_norm_g, w_up, w_down):
    B, S, D = x.shape
    cos, sin = rope_tables(S)
    split_at = np.cumsum([SB_WIDTH, SB_WIDTH, SB_WIDTH, SWA_Q_WIDTH, SWA_KV_WIDTH, SWA_KV_WIDTH]).tolist()
    for l in range(DEPTH):
        h = rms_norm(x, mix_norm_g[l])
        proj = h @ w_in[l]
        sb_q, sb_k, sb_v, sw_q, sw_k, sw_v, gate_logits = jnp.split(proj, split_at, axis=-1)

        to_heads = lambda t, n: t.reshape(B, S, n, HEAD_DIM)
        o_sb = stick_breaking_attention(to_heads(sb_q, SB_HEADS), to_heads(sb_k, SB_HEADS),
                                        to_heads(sb_v, SB_HEADS)).reshape(B, S, SB_WIDTH)
        y_sb = o_sb @ w_branch_sb[l]

        q = apply_rope(rms_norm(to_heads(sw_q, SWA_HEADS), q_norm_g[l]), cos, sin)
        k = apply_rope(rms_norm(to_heads(sw_k, SWA_KV_HEADS), k_norm_g[l]), cos, sin)
        v = to_heads(sw_v, SWA_KV_HEADS)
        y_swa = sliding_window_sink_attention(q, k, v, sinks[l]) @ w_branch_swa[l]

        gates = jax.nn.sigmoid(gate_logits.astype(jnp.float32)).astype(x.dtype).reshape(B, S, N_BRANCHES, D)
        merged = gates[:, :, 0] * y_sb + gates[:, :, 1] * y_swa
        x = x + merged @ w_out[l]

        h2 = rms_norm(x, mlp_norm_g[l])
        x = x + jnp.square(jax.nn.relu(h2 @ w_up[l])) @ w_down[l]
    return x


import jax as _jax
import jax.numpy as _jnp

TWIN_FORMAT = 'train_step'
FWD_PARAMS = ['x', 'mix_norm_g', 'w_in', 'q_norm_g', 'k_norm_g', 'sinks', 'w_branch_sb', 'w_branch_swa', 'w_out', 'mlp_norm_g', 'w_up', 'w_down']
TWIN_WEIGHTS = ['mix_norm_g', 'w_in', 'q_norm_g', 'k_norm_g', 'sinks', 'w_branch_sb', 'w_branch_swa', 'w_out', 'mlp_norm_g', 'w_up', 'w_down']
TWIN_DIFF_INPUT = 'x'
TWIN_INPUTS = ['x', 'mix_norm_g', 'w_in', 'q_norm_g', 'k_norm_g', 'sinks', 'w_branch_sb', 'w_branch_swa', 'w_out', 'mlp_norm_g', 'w_up', 'w_down', 'loss_target', 'm_mix_norm_g', 'm_w_in', 'm_q_norm_g', 'm_k_norm_g', 'm_sinks', 'm_w_branch_sb', 'm_w_branch_swa', 'm_w_out', 'm_mlp_norm_g', 'm_w_up', 'm_w_down', 'v_mix_norm_g', 'v_w_in', 'v_q_norm_g', 'v_k_norm_g', 'v_sinks', 'v_w_branch_sb', 'v_w_branch_swa', 'v_w_out', 'v_mlp_norm_g', 'v_w_up', 'v_w_down']
TWIN_OUTPUTS = ['loss', 'grad_x', 'grad_mix_norm_g', 'grad_w_in', 'grad_q_norm_g', 'grad_k_norm_g', 'grad_sinks', 'grad_w_branch_sb', 'grad_w_branch_swa', 'grad_w_out', 'grad_mlp_norm_g', 'grad_w_up', 'grad_w_down', 'delta_mix_norm_g', 'delta_w_in', 'delta_q_norm_g', 'delta_k_norm_g', 'delta_sinks', 'delta_w_branch_sb', 'delta_w_branch_swa', 'delta_w_out', 'delta_mlp_norm_g', 'delta_w_up', 'delta_w_down', 'new_m_mix_norm_g', 'new_m_w_in', 'new_m_q_norm_g', 'new_m_k_norm_g', 'new_m_sinks', 'new_m_w_branch_sb', 'new_m_w_branch_swa', 'new_m_w_out', 'new_m_mlp_norm_g', 'new_m_w_up', 'new_m_w_down', 'new_v_mix_norm_g', 'new_v_w_in', 'new_v_q_norm_g', 'new_v_k_norm_g', 'new_v_sinks', 'new_v_w_branch_sb', 'new_v_w_branch_swa', 'new_v_w_out', 'new_v_mlp_norm_g', 'new_v_w_up', 'new_v_w_down']
TWIN_LEAF_KINDS = {'loss': 'loss', 'grad_x': 'grad_x', 'grad_mix_norm_g': 'grad_w', 'grad_w_in': 'grad_w', 'grad_q_norm_g': 'grad_w', 'grad_k_norm_g': 'grad_w', 'grad_sinks': 'grad_w', 'grad_w_branch_sb': 'grad_w', 'grad_w_branch_swa': 'grad_w', 'grad_w_out': 'grad_w', 'grad_mlp_norm_g': 'grad_w', 'grad_w_up': 'grad_w', 'grad_w_down': 'grad_w', 'delta_mix_norm_g': 'delta_w', 'delta_w_in': 'delta_w', 'delta_q_norm_g': 'delta_w', 'delta_k_norm_g': 'delta_w', 'delta_sinks': 'delta_w', 'delta_w_branch_sb': 'delta_w', 'delta_w_branch_swa': 'delta_w', 'delta_w_out': 'delta_w', 'delta_mlp_norm_g': 'delta_w', 'delta_w_up': 'delta_w', 'delta_w_down': 'delta_w', 'new_m_mix_norm_g': 'new_m', 'new_m_w_in': 'new_m', 'new_m_q_norm_g': 'new_m', 'new_m_k_norm_g': 'new_m', 'new_m_sinks': 'new_m', 'new_m_w_branch_sb': 'new_m', 'new_m_w_branch_swa': 'new_m', 'new_m_w_out': 'new_m', 'new_m_mlp_norm_g': 'new_m', 'new_m_w_up': 'new_m', 'new_m_w_down': 'new_m', 'new_v_mix_norm_g': 'new_v', 'new_v_w_in': 'new_v', 'new_v_q_norm_g': 'new_v', 'new_v_k_norm_g': 'new_v', 'new_v_sinks': 'new_v', 'new_v_w_branch_sb': 'new_v', 'new_v_w_branch_swa': 'new_v', 'new_v_w_out': 'new_v', 'new_v_mlp_norm_g': 'new_v', 'new_v_w_up': 'new_v', 'new_v_w_down': 'new_v'}


def _forward(args):
    return _fwd_reference(*[args[k] for k in FWD_PARAMS])


def _output_shape():
    out = _jax.eval_shape(lambda: _forward(_fwd_setup_inputs(0)))
    return out.shape, out.dtype

N_MICROBATCH = 1
ADAM_LR = 0.001
ADAM_B1 = 0.9
ADAM_B2 = 0.999
ADAM_EPS = 1e-08
ADAM_WD = 0.01
ADAM_STEP = 10
PER_EXAMPLE_BATCH_AXIS = {'x': 0, 'loss_target': 0}
SHARED_INPUTS = []
_WEIGHT_DTYPES = {'mix_norm_g': _jnp.float32, 'w_in': _jnp.float32, 'q_norm_g': _jnp.float32, 'k_norm_g': _jnp.float32, 'sinks': _jnp.float32, 'w_branch_sb': _jnp.float32, 'w_branch_swa': _jnp.float32, 'w_out': _jnp.float32, 'mlp_norm_g': _jnp.float32, 'w_up': _jnp.float32, 'w_down': _jnp.float32}
MOMENT_SCALE = {'mix_norm_g': 1.590050e+01, 'w_in': 6.246719e+00, 'q_norm_g': 1.353175e+00, 'k_norm_g': 1.317651e+00, 'sinks': 8.282041e-01, 'w_branch_sb': 8.871294e+00, 'w_branch_swa': 9.162030e+00, 'w_out': 1.251007e+01, 'mlp_norm_g': 9.915140e+01, 'w_up': 8.513653e+00, 'w_down': 3.190968e+01}


def _to_microbatches(a, axis):
    t = _jnp.moveaxis(a, axis, 0)
    t = t.reshape((N_MICROBATCH, t.shape[0] // N_MICROBATCH) + t.shape[1:])
    return _jnp.moveaxis(t, 1, axis + 1)


def setup_inputs(seed: int = 0) -> dict:
    inp = _fwd_setup_inputs(seed)
    key = _jax.random.fold_in(_jax.random.key(seed), 7919)
    shape, _ = _output_shape()
    out = dict(inp)
    out["loss_target"] = _jax.random.normal(_jax.random.fold_in(key, 0), shape, _jnp.float32)
    for i, name in enumerate(TWIN_WEIGHTS):
        w = inp[name].astype(_jnp.float32)
        if MOMENT_SCALE is None:
            s = _jnp.sqrt(_jnp.mean(_jnp.square(w)) + 1e-30)
        else:
            s = MOMENT_SCALE[name]
        km, kv = _jax.random.split(_jax.random.fold_in(key, i + 1))
        out[name] = w
        out["m_" + name] = s * _jax.random.normal(km, w.shape, _jnp.float32)
        out["v_" + name] = (s * s) * _jax.random.uniform(kv, w.shape, _jnp.float32, 0.5, 1.5)
    if N_MICROBATCH > 1:
        for name, axis in PER_EXAMPLE_BATCH_AXIS.items():
            out[name] = _to_microbatches(out[name], axis)
    return {'x': out['x'], 'mix_norm_g': out['mix_norm_g'], 'w_in': out['w_in'], 'q_norm_g': out['q_norm_g'], 'k_norm_g': out['k_norm_g'], 'sinks': out['sinks'], 'w_branch_sb': out['w_branch_sb'], 'w_branch_swa': out['w_branch_swa'], 'w_out': out['w_out'], 'mlp_norm_g': out['mlp_norm_g'], 'w_up': out['w_up'], 'w_down': out['w_down'], 'loss_target': out['loss_target'], 'm_mix_norm_g': out['m_mix_norm_g'], 'm_w_in': out['m_w_in'], 'm_q_norm_g': out['m_q_norm_g'], 'm_k_norm_g': out['m_k_norm_g'], 'm_sinks': out['m_sinks'], 'm_w_branch_sb': out['m_w_branch_sb'], 'm_w_branch_swa': out['m_w_branch_swa'], 'm_w_out': out['m_w_out'], 'm_mlp_norm_g': out['m_mlp_norm_g'], 'm_w_up': out['m_w_up'], 'm_w_down': out['m_w_down'], 'v_mix_norm_g': out['v_mix_norm_g'], 'v_w_in': out['v_w_in'], 'v_q_norm_g': out['v_q_norm_g'], 'v_k_norm_g': out['v_k_norm_g'], 'v_sinks': out['v_sinks'], 'v_w_branch_sb': out['v_w_branch_sb'], 'v_w_branch_swa': out['v_w_branch_swa'], 'v_w_out': out['v_w_out'], 'v_mlp_norm_g': out['v_mlp_norm_g'], 'v_w_up': out['v_w_up'], 'v_w_down': out['v_w_down']}


def _loss(weights, diff, rest, loss_target):
    with _jax.named_scope("forward"):
        args = {**rest, TWIN_DIFF_INPUT: diff, **{k: w.astype(_WEIGHT_DTYPES[k]) for k, w in weights.items()}}
        y = _forward(args)
    with _jax.named_scope("loss_head"):
        err = _jnp.square(y.astype(_jnp.float32) - loss_target)
        return 0.5 * _jnp.sum(_jnp.mean(err, axis=-1)) if err.ndim else 0.5 * err


def _adamw(w, g, m, v):
    m = ADAM_B1 * m + (1.0 - ADAM_B1) * g
    v = ADAM_B2 * v + (1.0 - ADAM_B2) * _jnp.square(g)
    m_hat = m / (1.0 - ADAM_B1 ** ADAM_STEP)
    v_hat = v / (1.0 - ADAM_B2 ** ADAM_STEP)
    delta = -ADAM_LR * (m_hat / (_jnp.sqrt(v_hat) + ADAM_EPS) + ADAM_WD * w)
    return delta, m, v


def reference(x, mix_norm_g, w_in, q_norm_g, k_norm_g, sinks, w_branch_sb, w_branch_swa, w_out, mlp_norm_g, w_up, w_down, loss_target, m_mix_norm_g, m_w_in, m_q_norm_g, m_k_norm_g, m_sinks, m_w_branch_sb, m_w_branch_swa, m_w_out, m_mlp_norm_g, m_w_up, m_w_down, v_mix_norm_g, v_w_in, v_q_norm_g, v_k_norm_g, v_sinks, v_w_branch_sb, v_w_branch_swa, v_w_out, v_mlp_norm_g, v_w_up, v_w_down):
    given = dict(x=x, mix_norm_g=mix_norm_g, w_in=w_in, q_norm_g=q_norm_g, k_norm_g=k_norm_g, sinks=sinks, w_branch_sb=w_branch_sb, w_branch_swa=w_branch_swa, w_out=w_out, mlp_norm_g=mlp_norm_g, w_up=w_up, w_down=w_down, loss_target=loss_target, m_mix_norm_g=m_mix_norm_g, m_w_in=m_w_in, m_q_norm_g=m_q_norm_g, m_k_norm_g=m_k_norm_g, m_sinks=m_sinks, m_w_branch_sb=m_w_branch_sb, m_w_branch_swa=m_w_branch_swa, m_w_out=m_w_out, m_mlp_norm_g=m_mlp_norm_g, m_w_up=m_w_up, m_w_down=m_w_down, v_mix_norm_g=v_mix_norm_g, v_w_in=v_w_in, v_q_norm_g=v_q_norm_g, v_k_norm_g=v_k_norm_g, v_sinks=v_sinks, v_w_branch_sb=v_w_branch_sb, v_w_branch_swa=v_w_branch_swa, v_w_out=v_w_out, v_mlp_norm_g=v_mlp_norm_g, v_w_up=v_w_up, v_w_down=v_w_down)
    weights = {n: given[n] for n in TWIN_WEIGHTS}
    shared = {n: given[n] for n in SHARED_INPUTS}
    per_example = {n: given[n] for n in ['x']}
    grad_fn = _jax.value_and_grad(_loss, argnums=(0, 1))

    def one_microbatch(ex, loss_target):
        ex = dict(ex)
        diff = ex.pop(TWIN_DIFF_INPUT)
        return grad_fn(weights, diff, {**shared, **ex}, loss_target)

    if N_MICROBATCH == 1:
        loss, (grad_w, grad_x) = one_microbatch(per_example, given["loss_target"])
    else:
        def body(carry, xs):
            loss_sum, grad_sum = carry
            l_k, (gw_k, gx_k) = one_microbatch(xs[0], xs[1])
            with _jax.named_scope("update"):
                return (loss_sum + l_k, _jax.tree.map(_jnp.add, grad_sum, gw_k)), gx_k

        init = (_jnp.zeros((), _jnp.float32), _jax.tree.map(_jnp.zeros_like, weights))
        (loss, grad_w), grad_x = _jax.lax.scan(body, init, (per_example, given["loss_target"]))
    with _jax.named_scope("update"):
        delta_w, new_m, new_v = {}, {}, {}
        for n in TWIN_WEIGHTS:
            delta_w[n], new_m[n], new_v[n] = _adamw(weights[n], grad_w[n], given["m_" + n], given["v_" + n])
    return (loss, grad_x, *[grad_w[n] for n in TWIN_WEIGHTS], *[delta_w[n] for n in TWIN_WEIGHTS],
            *[new_m[n] for n in TWIN_WEIGHTS], *[new_v[n] for n in TWIN_WEIGHTS])
```

```python
import functools

import jax
import jax.numpy as jnp
from jax import lax
from jax.experimental import pallas as pl
from jax.experimental.pallas import tpu as pltpu

F32 = jnp.float32
BF16 = jnp.bfloat16

D_MODEL = 1024
DEPTH = 4
HEAD_DIM = 64
SB_WIDTH = 512
SWA_Q_WIDTH = 512
SWA_KV_WIDTH = 128
D_FF = 4096
IN_WIDTH = 4352
GATE_OFF = 2304
ROPE_THETA = 10000.0
NORM_EPS = 1e-6
SCALE = HEAD_DIM ** -0.5
N_CHIPS = 4
N_DEV = 8

ADAM_LR = 0.001
ADAM_B1 = 0.9
ADAM_B2 = 0.999
ADAM_EPS = 1e-08
ADAM_WD = 0.01
ADAM_STEP = 10

LANES = 128
SB_TILE = 256
SWA_BLOCK = 128
VMEM_LIMIT = 56 << 20

NN = (((1,), (0,)), ((), ()))
NT = (((1,), (1,)), ((), ()))
TN = (((0,), (0,)), ((), ()))
MESH = pl.DeviceIdType.MESH

PACK_ROWS = (("w_in", 1088), ("w_branch_sb", 128), ("w_branch_swa", 128), ("w_out", 256), ("w_up", 1024), ("w_down", 1024))
PACK_R = sum(r for _, r in PACK_ROWS)
SMALL_ROWS = 72


def _dot(a, b, dims):
    return lax.dot_general(a, b, dims, preferred_element_type=F32)


def _cparams(sem):
    return pltpu.CompilerParams(dimension_semantics=sem, vmem_limit_bytes=VMEM_LIMIT)


def _resident(shape):
    nd = len(shape)
    return pl.BlockSpec(shape, lambda *_: (0,) * nd, pipeline_mode=pl.Buffered(1))


def _rows(tm, width):
    return pl.BlockSpec((tm, width), lambda i: (i, 0))


def _rms(xf):
    rstd = lax.rsqrt(jnp.mean(xf * xf, axis=-1, keepdims=True) + NORM_EPS)
    return xf * rstd, rstd


def _rms_bwd(dh, xn, rstd, g):
    dxn = dh * g
    return rstd * (dxn - xn * jnp.mean(dxn * xn, axis=-1, keepdims=True))


def _lane(shape):
    return lax.broadcasted_iota(jnp.int32, shape, len(shape) - 1)


def _head_mean(v, left):
    sl = jnp.sum(jnp.where(left, v, 0.0), axis=-1, keepdims=True)
    sr = jnp.sum(jnp.where(left, 0.0, v), axis=-1, keepdims=True)
    return jnp.where(left, sl, sr) * (1.0 / HEAD_DIM)


def _rope(y, cs, sn, first):
    up = pltpu.roll(y, 96, 1)
    dn = pltpu.roll(y, 32, 1)
    return y * cs + jnp.where(first, -up, dn) * sn


def _rope_t(d, cs, sn, first):
    t = d * jnp.where(first, -sn, sn)
    return d * cs + jnp.where(first, pltpu.roll(t, 96, 1), pltpu.roll(t, 32, 1))


def _norm_rope(p, g, cs, sn):
    lane = _lane((1, LANES))
    left = lane < HEAD_DIM
    first = (lane % HEAD_DIM) < (HEAD_DIM // 2)
    yn = p * lax.rsqrt(_head_mean(p * p, left) + NORM_EPS)
    return _rope(yn * g, cs, sn, first)


def _norm_rope_bwd(p, dout, g, cs, sn):
    lane = _lane((1, LANES))
    left = lane < HEAD_DIM
    first = (lane % HEAD_DIM) < (HEAD_DIM // 2)
    rstd = lax.rsqrt(_head_mean(p * p, left) + NORM_EPS)
    yn = p * rstd
    dyg = _rope_t(dout, cs, sn, first)
    dg = jnp.sum(dyg * yn, axis=0, keepdims=True)
    dyn = dyg * g
    dp = rstd * (dyn - yn * _head_mean(dyn * yn, left))
    return dp, dg


def _inproj_fwd(x, g, w, qg, kg, cos, sin):
    s = x.shape[0]
    tm = min(512, s)

    def body(x_ref, g_ref, w_ref, qg_ref, kg_ref, cos_ref, sin_ref,
             h_ref, sb_ref, raw_ref, qn_ref, kn_ref, v_ref, gate_ref):
        xn, _ = _rms(x_ref[...])
        h = (xn * g_ref[...]).astype(BF16)
        h_ref[...] = h
        for a in range(0, 3 * SB_WIDTH, 512):
            sb_ref[:, a:a + 512] = _dot(h, w_ref[:, a:a + 512], NN).astype(BF16)
        cs, sn = cos_ref[...], sin_ref[...]
        q0 = 3 * SB_WIDTH
        pq = _dot(h, w_ref[:, q0:q0 + SWA_Q_WIDTH], NN)
        raw_ref[:, 0:SWA_Q_WIDTH] = pq
        for b in range(SWA_Q_WIDTH // LANES):
            blk = pq[:, b * LANES:(b + 1) * LANES]
            qn_ref[:, b * LANES:(b + 1) * LANES] = _norm_rope(blk, qg_ref[...], cs, sn).astype(BF16)
        k0 = q0 + SWA_Q_WIDTH
        pk = _dot(h, w_ref[:, k0:k0 + 2 * SWA_KV_WIDTH], NN)
        raw_ref[:, SWA_Q_WIDTH:SWA_Q_WIDTH + SWA_KV_WIDTH] = pk[:, :SWA_KV_WIDTH]
        kn_ref[...] = _norm_rope(pk[:, :SWA_KV_WIDTH], kg_ref[...], cs, sn).astype(BF16)
        v_ref[...] = pk[:, SWA_KV_WIDTH:].astype(BF16)
        for a in range(GATE_OFF, IN_WIDTH, 512):
            gate_ref[:, a - GATE_OFF:a - GATE_OFF + 512] = jax.nn.sigmoid(_dot(h, w_ref[:, a:a + 512], NN))

    return pl.pallas_call(
        body, name="inproj_fwd", grid=(s // tm,),
        in_specs=[_rows(tm, D_MODEL), _resident((1, D_MODEL)), _resident((D_MODEL, IN_WIDTH)),
                  _resident((1, LANES)), _resident((1, LANES)), _rows(tm, LANES), _rows(tm, LANES)],
        out_specs=[_rows(tm, D_MODEL), _rows(tm, 3 * SB_WIDTH), _rows(tm, 640), _rows(tm, SWA_Q_WIDTH),
                   _rows(tm, SWA_KV_WIDTH), _rows(tm, SWA_KV_WIDTH), _rows(tm, 2 * D_MODEL)],
        out_shape=[jax.ShapeDtypeStruct((s, D_MODEL), BF16), jax.ShapeDtypeStruct((s, 3 * SB_WIDTH), BF16),
                   jax.ShapeDtypeStruct((s, 640), F32), jax.ShapeDtypeStruct((s, SWA_Q_WIDTH), BF16),
                   jax.ShapeDtypeStruct((s, SWA_KV_WIDTH), BF16), jax.ShapeDtypeStruct((s, SWA_KV_WIDTH), BF16),
                   jax.ShapeDtypeStruct((s, 2 * D_MODEL), F32)],
        compiler_params=_cparams(("arbitrary",)),
    )(x, g, w, qg, kg, cos, sin)


def _sb_tile_terms(qh, k, n, tile):
    z = _dot(qh, k, NT) * SCALE
    soft = jnp.log(1.0 + jnp.exp(-jnp.abs(z)))
    r = lax.broadcasted_iota(jnp.int32, (tile, tile), 0)
    c = lax.broadcasted_iota(jnp.int32, (tile, tile), 1)
    strict = c < r + n * tile
    log_keep = jnp.where(strict, -(jnp.maximum(z, 0.0) + soft), 0.0)
    log_beta = jnp.minimum(z, 0.0) - soft
    return strict, log_keep, log_beta


def _split_dot(a, u):
    hi = a.astype(BF16)
    lo = (a - hi.astype(F32)).astype(BF16)
    return _dot(hi, u, NN) + _dot(lo, u, NN)


def _sb_fwd(sbqkv):
    s = sbqkv.shape[0]
    t = min(SB_TILE, s)

    def body(q_ref, k_ref, v_ref, o_ref, ox_ref):
        i = pl.program_id(1)
        q2 = q_ref[...]
        left = _lane((1, LANES)) < HEAD_DIM
        r = lax.broadcasted_iota(jnp.int32, (t, t), 0)
        c = lax.broadcasted_iota(jnp.int32, (t, t), 1)
        u_after = (r > c).astype(BF16)
        acc = jnp.zeros((t, LANES), F32)
        rest = jnp.zeros((t, LANES), F32)
        for hm in (left, jnp.logical_not(left)):
            qh = jnp.where(hm, q2, jnp.zeros_like(q2))

            def step(n, carry, qh=qh, hm=hm):
                run, acc, rest = carry
                off = pl.multiple_of((i - n) * t, t)
                k = k_ref[pl.ds(off, t), :]
                v = v_ref[pl.ds(off, t), :]
                strict, log_keep, log_beta = _sb_tile_terms(qh, k, n, t)
                tail = run + _split_dot(log_keep, u_after)
                w = jnp.where(strict, jnp.exp(log_beta + tail), 0.0)
                vh = jnp.where(hm, v, jnp.zeros_like(v))
                wb = w.astype(BF16)
                acc = acc + _dot(wb, vh, NN)
                rest = rest + _dot((w - wb.astype(F32)).astype(BF16), vh, NN)
                return tail[:, 0:1] + log_keep[:, 0:1], acc, rest

            _, acc, rest = lax.fori_loop(0, i + 1, step, (jnp.zeros((t, 1), F32), acc, rest))
        o_ref[...] = acc
        ox_ref[...] = acc + rest

    npair = SB_WIDTH // LANES
    tile = pl.BlockSpec((t, LANES), lambda j, i: (i, j))
    return pl.pallas_call(
        body, name="sb_fwd", grid=(npair, s // t),
        in_specs=[tile,
                  pl.BlockSpec((s, LANES), lambda j, i: (0, npair + j)),
                  pl.BlockSpec((s, LANES), lambda j, i: (0, 2 * npair + j))],
        out_specs=[tile, tile],
        out_shape=[jax.ShapeDtypeStruct((s, SB_WIDTH), F32), jax.ShapeDtypeStruct((s, SB_WIDTH), F32)],
        compiler_params=_cparams(("arbitrary", "arbitrary")),
    )(sbqkv, sbqkv, sbqkv)


def _swa_band(cur_ref, prev_ref, b):
    lo = prev_ref[...] if b == 0 else cur_ref[(b - 1) * SWA_BLOCK:b * SWA_BLOCK, :]
    return jnp.concatenate([lo, cur_ref[b * SWA_BLOCK:(b + 1) * SWA_BLOCK, :]], axis=0)


def _swa_variants(band, left):
    f = band.astype(F32)
    sw = pltpu.roll(f, HEAD_DIM, 1)
    halves = (left, jnp.logical_not(left))
    return [[jnp.where(halves[p], f if p == g else sw, 0.0).astype(BF16) for p in range(2)] for g in range(2)]


def _swa_probs(qh, kv, sink, blk):
    sc = _dot(qh, kv, NT) * SCALE
    ii = lax.broadcasted_iota(jnp.int32, (SWA_BLOCK, 2 * SWA_BLOCK), 0)
    jj = lax.broadcasted_iota(jnp.int32, (SWA_BLOCK, 2 * SWA_BLOCK), 1)
    rel = jj - SWA_BLOCK - ii
    valid = (rel <= 0) & (rel > -SWA_BLOCK) & (jj + blk * SWA_BLOCK >= SWA_BLOCK)
    sc = jnp.where(valid, sc, -1e30)
    m = jnp.maximum(jnp.max(sc, axis=-1, keepdims=True), sink)
    e = jnp.exp(sc - m)
    es = jnp.exp(sink - m)
    inv = 1.0 / (jnp.sum(e, axis=-1, keepdims=True) + es)
    return e * inv, es * inv


def _swa_specs(s, t):
    nb = t // SWA_BLOCK
    cur = pl.BlockSpec((t, LANES), lambda i: (i, 0))
    prev = pl.BlockSpec((SWA_BLOCK, LANES), lambda i: (jnp.maximum(i * nb - 1, 0), 0))
    return cur, prev


def _swa_fwd(qn, kn, v, sinks):
    s = qn.shape[0]
    t = min(512, s)
    nb = t // SWA_BLOCK
    nheads = SWA_Q_WIDTH // HEAD_DIM

    def body(sink_ref, q_ref, kc_ref, kp_ref, vc_ref, vp_ref, o_ref):
        i = pl.program_id(0)
        left = _lane((1, LANES)) < HEAD_DIM
        halves = (left, jnp.logical_not(left))
        for b in range(nb):
            kvar = _swa_variants(_swa_band(kc_ref, kp_ref, b), left)
            vvar = _swa_variants(_swa_band(vc_ref, vp_ref, b), left)
            rows = slice(b * SWA_BLOCK, (b + 1) * SWA_BLOCK)
            for hb in range(nheads // 2):
                q2 = q_ref[rows, hb * LANES:(hb + 1) * LANES]
                acc = jnp.zeros((SWA_BLOCK, LANES), F32)
                for p in range(2):
                    h = 2 * hb + p
                    g = h // (nheads // 2)
                    qh = jnp.where(halves[p], q2, jnp.zeros_like(q2))
                    probs, _ = _swa_probs(qh, kvar[g][p], sink_ref[h], i * nb + b)
                    acc = acc + _dot(probs.astype(BF16), vvar[g][p], NN)
                o_ref[rows, hb * LANES:(hb + 1) * LANES] = acc.astype(BF16)

    cur, prev = _swa_specs(s, t)
    return pl.pallas_call(
        body, name="swa_fwd", grid=(s // t,),
        in_specs=[pl.BlockSpec(memory_space=pltpu.SMEM), _rows(t, SWA_Q_WIDTH), cur, prev, cur, prev],
        out_specs=_rows(t, SWA_Q_WIDTH),
        out_shape=jax.ShapeDtypeStruct((s, SWA_Q_WIDTH), BF16),
        compiler_params=_cparams(("arbitrary",)),
    )(sinks, qn, kn, kn, v, v)


def _merge_fwd(x, osb, oswa, gates, wsb, wswa, wout):
    s = x.shape[0]
    tm = min(512, s)

    def body(x_ref, osb_ref, oswa_ref, gate_ref, wsb_ref, wswa_ref, wout_ref, x1_ref):
        ysb = _dot(osb_ref[...].astype(BF16), wsb_ref[...], NN)
        yswa = _dot(oswa_ref[...], wswa_ref[...], NN)
        merged = gate_ref[:, :D_MODEL] * ysb + gate_ref[:, D_MODEL:] * yswa
        x1_ref[...] = x_ref[...] + _dot(merged.astype(BF16), wout_ref[...], NN)

    return pl.pallas_call(
        body, name="merge_fwd", grid=(s // tm,),
        in_specs=[_rows(tm, D_MODEL), _rows(tm, SB_WIDTH), _rows(tm, SWA_Q_WIDTH), _rows(tm, 2 * D_MODEL),
                  _resident((SB_WIDTH, D_MODEL)), _resident((SWA_Q_WIDTH, D_MODEL)), _resident((D_MODEL, D_MODEL))],
        out_specs=_rows(tm, D_MODEL),
        out_shape=jax.ShapeDtypeStruct((s, D_MODEL), F32),
        compiler_params=_cparams(("arbitrary",)),
    )(x, osb, oswa, gates, wsb, wswa, wout)


def _mlp_fwd(x1, g, wup, wdown):
    s = x1.shape[0]
    tm = min(512, s)
    fc = 1024

    def body(x_ref, g_ref, wup_ref, wdown_ref, x2_ref, u_ref):
        xf = x_ref[...]
        xn, _ = _rms(xf)
        h2 = (xn * g_ref[...]).astype(BF16)
        acc = xf
        for c in range(0, D_FF, fc):
            u = _dot(h2, wup_ref[:, c:c + fc], NN)
            u_ref[:, c:c + fc] = u
            r = jnp.maximum(u, 0.0)
            acc = acc + _dot((r * r).astype(BF16), wdown_ref[c:c + fc, :], NN)
        x2_ref[...] = acc

    return pl.pallas_call(
        body, name="mlp_fwd", grid=(s // tm,),
        in_specs=[_rows(tm, D_MODEL), _resident((1, D_MODEL)), _resident((D_MODEL, D_FF)), _resident((D_FF, D_MODEL))],
        out_specs=[_rows(tm, D_MODEL), _rows(tm, D_FF)],
        out_shape=[jax.ShapeDtypeStruct((s, D_MODEL), F32), jax.ShapeDtypeStruct((s, D_FF), F32)],
        compiler_params=_cparams(("arbitrary",)),
    )(x1, g, wup, wdown)


def _loss_grad(y, target):
    s = y.shape[0]
    tm = min(512, s)

    def body(y_ref, t_ref, dy_ref, part_ref):
        err = y_ref[...] - t_ref[...]
        dy_ref[...] = err * (1.0 / D_MODEL)
        tot = jnp.sum(jnp.sum(err * err, axis=-1, keepdims=True), axis=0, keepdims=True)
        part_ref[...] = jnp.broadcast_to(tot.reshape(1, 1, 1), (1, 8, LANES))

    dy, part = pl.pallas_call(
        body, name="loss_grad", grid=(s // tm,),
        in_specs=[_rows(tm, D_MODEL), _rows(tm, D_MODEL)],
        out_specs=[_rows(tm, D_MODEL), pl.BlockSpec((1, 8, LANES), lambda i: (i, 0, 0))],
        out_shape=[jax.ShapeDtypeStruct((s, D_MODEL), F32), jax.ShapeDtypeStruct((s // tm, 8, LANES), F32)],
        compiler_params=_cparams(("arbitrary",)),
    )(y, target)
    return dy, (0.5 / D_MODEL) * jnp.sum(part[:, 0, 0])


def _mlp_bwd(dx2, x1, u, g, wup, wdown):
    s = x1.shape[0]
    tm = min(256, s)
    fc = 1024

    def body(dx2_ref, x_ref, u_ref, g_ref, wup_ref, wdown_ref, dx1_ref, du_ref, a_ref, h2_ref, dg_ref):
        @pl.when(pl.program_id(0) == 0)
        def _():
            dg_ref[...] = jnp.zeros_like(dg_ref)

        gam = g_ref[...]
        xn, rstd = _rms(x_ref[...])
        h2_ref[...] = (xn * gam).astype(BF16)
        dxf = dx2_ref[...]
        dxb = dxf.astype(BF16)
        dh2 = jnp.zeros((tm, D_MODEL), F32)
        for c in range(0, D_FF, fc):
            da = _dot(dxb, wdown_ref[c:c + fc, :], NT)
            r = jnp.maximum(u_ref[:, c:c + fc], 0.0)
            a_ref[:, c:c + fc] = (r * r).astype(BF16)
            du = (da * (2.0 * r)).astype(BF16)
            du_ref[:, c:c + fc] = du
            dh2 = dh2 + _dot(du, wup_ref[:, c:c + fc], NT)
        dg_ref[...] += jnp.sum(dh2 * xn, axis=0, keepdims=True)
        dx1_ref[...] = dxf + _rms_bwd(dh2, xn, rstd, gam)

    return pl.pallas_call(
        body, name="mlp_bwd", grid=(s // tm,),
        in_specs=[_rows(tm, D_MODEL), _rows(tm, D_MODEL), _rows(tm, D_FF), _resident((1, D_MODEL)),
                  _resident((D_MODEL, D_FF)), _resident((D_FF, D_MODEL))],
        out_specs=[_rows(tm, D_MODEL), _rows(tm, D_FF), _rows(tm, D_FF), _rows(tm, D_MODEL),
                   pl.BlockSpec((1, D_MODEL), lambda i: (0, 0))],
        out_shape=[jax.ShapeDtypeStruct((s, D_MODEL), F32), jax.ShapeDtypeStruct((s, D_FF), BF16),
                   jax.ShapeDtypeStruct((s, D_FF), BF16), jax.ShapeDtypeStruct((s, D_MODEL), BF16),
                   jax.ShapeDtypeStruct((1, D_MODEL), F32)],
        compiler_params=_cparams(("arbitrary",)),
    )(dx2, x1, u, g, wup, wdown)


def _wgrad(a, b, name, shard_axis=None):
    s, m = a.shape
    n = b.shape[1]
    tm = min(512, m if shard_axis != 0 else m // N_CHIPS)
    tn = min(512, n if shard_axis != 1 else n // N_CHIPS)
    if shard_axis is None and n % tn:
        tn = 256

    def body(a_ref, b_ref, o_ref):
        res = _dot(a_ref[...].astype(BF16), b_ref[...].astype(BF16), TN)
        o_ref[...] = res.reshape(o_ref.shape)

    if shard_axis is None:
        out_shape, out_spec = (m, n), pl.BlockSpec((tm, tn), lambda i, j: (i, j))
    elif shard_axis == 0:
        per = m // N_CHIPS // tm
        out_shape, out_spec = (N_CHIPS, m // N_CHIPS, n), pl.BlockSpec((1, tm, tn), lambda i, j: (i // per, i % per, j))
    else:
        per = n // N_CHIPS // tn
        out_shape, out_spec = (N_CHIPS, m, n // N_CHIPS), pl.BlockSpec((1, tm, tn), lambda i, j: (j // per, i, j % per))
    return pl.pallas_call(
        body, name=name, grid=(m // tm, n // tn),
        in_specs=[pl.BlockSpec((s, tm), lambda i, j: (0, i)), pl.BlockSpec((s, tn), lambda i, j: (0, j))],
        out_specs=out_spec,
        out_shape=jax.ShapeDtypeStruct(out_shape, F32),
        compiler_params=_cparams(("arbitrary", "arbitrary")),
    )(a, b)


def _merge_bwd(dx1, osb, oswa, gates, wsb, wswa, wout):
    s = dx1.shape[0]
    tm = min(512, s)

    def body(dx_ref, osb_ref, oswa_ref, gate_ref, wsb_ref, wswa_ref, wout_ref,
             dosb_ref, doswa_ref, dgl_ref, merged_ref, dysb_ref, dyswa_ref):
        dm = _dot(dx_ref[...].astype(BF16), wout_ref[...], NT)
        ysb = _dot(osb_ref[...].astype(BF16), wsb_ref[...], NN)
        yswa = _dot(oswa_ref[...], wswa_ref[...], NN)
        g0 = gate_ref[:, :D_MODEL]
        g1 = gate_ref[:, D_MODEL:]
        merged_ref[...] = (g0 * ysb + g1 * yswa).astype(BF16)
        dgl_ref[:, :D_MODEL] = (dm * ysb * (g0 * (1.0 - g0))).astype(BF16)
        dgl_ref[:, D_MODEL:] = (dm * yswa * (g1 * (1.0 - g1))).astype(BF16)
        dysb = (dm * g0).astype(BF16)
        dyswa = (dm * g1).astype(BF16)
        dysb_ref[...] = dysb
        dyswa_ref[...] = dyswa
        dosb_ref[...] = _dot(dysb, wsb_ref[...], NT)
        doswa_ref[...] = _dot(dyswa, wswa_ref[...], NT)

    return pl.pallas_call(
        body, name="merge_bwd", grid=(s // tm,),
        in_specs=[_rows(tm, D_MODEL), _rows(tm, SB_WIDTH), _rows(tm, SWA_Q_WIDTH), _rows(tm, 2 * D_MODEL),
                  _resident((SB_WIDTH, D_MODEL)), _resident((SWA_Q_WIDTH, D_MODEL)), _resident((D_MODEL, D_MODEL))],
        out_specs=[_rows(tm, SB_WIDTH), _rows(tm, SWA_Q_WIDTH), _rows(tm, 2 * D_MODEL), _rows(tm, D_MODEL),
                   _rows(tm, D_MODEL), _rows(tm, D_MODEL)],
        out_shape=[jax.ShapeDtypeStruct((s, SB_WIDTH), F32), jax.ShapeDtypeStruct((s, SWA_Q_WIDTH), F32),
                   jax.ShapeDtypeStruct((s, 2 * D_MODEL), BF16), jax.ShapeDtypeStruct((s, D_MODEL), BF16),
                   jax.ShapeDtypeStruct((s, D_MODEL), BF16), jax.ShapeDtypeStruct((s, D_MODEL), BF16)],
        compiler_params=_cparams(("arbitrary",)),
    )(dx1, osb, oswa, gates, wsb, wswa, wout)


def _sb_bwd(sbqkv, osb, dosb):
    s = sbqkv.shape[0]
    t = min(SB_TILE, s)

    def body(q_ref, k_ref, v_ref, o_ref, do_ref, dq_ref, dk_ref, dv_ref):
        i = pl.program_id(1)

        @pl.when(i == 0)
        def _():
            dk_ref[...] = jnp.zeros_like(dk_ref)
            dv_ref[...] = jnp.zeros_like(dv_ref)

        q2 = q_ref[...]
        do2 = do_ref[...]
        prod = do2.astype(BF16).astype(F32) * o_ref[...]
        left = _lane((1, LANES)) < HEAD_DIM
        r = lax.broadcasted_iota(jnp.int32, (t, t), 0)
        c = lax.broadcasted_iota(jnp.int32, (t, t), 1)
        u_after = (r > c).astype(BF16)
        u_from = (r >= c).astype(BF16)
        dq = jnp.zeros((t, LANES), F32)
        for hm in (left, jnp.logical_not(left)):
            qh = jnp.where(hm, q2, jnp.zeros_like(q2))
            doh = jnp.where(hm, do2, 0.0).astype(BF16)
            delta = jnp.sum(jnp.where(hm, prod, 0.0), axis=-1, keepdims=True)

            def step(n, carry, qh=qh, doh=doh, delta=delta, hm=hm):
                run_keep, run_e, dq = carry
                off = pl.multiple_of((i - n) * t, t)
                k = k_ref[pl.ds(off, t), :]
                v = v_ref[pl.ds(off, t), :]
                strict, log_keep, log_beta = _sb_tile_terms(qh, k, n, t)
                tail = run_keep + _split_dot(log_keep, u_after)
                w = jnp.where(strict, jnp.exp(log_beta + tail), 0.0)
                e = _dot(doh, v, NT) * w
                rest = run_e + _split_dot(e, u_from)
                beta = jnp.exp(log_beta)
                dz = jnp.where(strict, e - beta * (e + delta - rest), 0.0) * SCALE
                dzb = dz.astype(BF16)
                dv_ref[pl.ds(off, t), :] += _dot(w.astype(BF16), doh, TN)
                dk_ref[pl.ds(off, t), :] += _dot(dzb, qh, TN)
                dq = dq + _dot(dzb, jnp.where(hm, k, jnp.zeros_like(k)), NN)
                return tail[:, 0:1] + log_keep[:, 0:1], rest[:, 0:1], dq

            zero = jnp.zeros((t, 1), F32)
            _, _, dq = lax.fori_loop(0, i + 1, step, (zero, zero, dq))
        dq_ref[...] = dq.astype(BF16)

    npair = SB_WIDTH // LANES
    tile = pl.BlockSpec((t, LANES), lambda j, i: (i, j))
    whole = pl.BlockSpec((s, LANES), lambda j, i: (0, j))
    return pl.pallas_call(
        body, name="sb_bwd", grid=(npair, s // t),
        in_specs=[tile, pl.BlockSpec((s, LANES), lambda j, i: (0, npair + j)),
                  pl.BlockSpec((s, LANES), lambda j, i: (0, 2 * npair + j)), tile, tile],
        out_specs=[tile, whole, whole],
        out_shape=[jax.ShapeDtypeStruct((s, SB_WIDTH), BF16), jax.ShapeDtypeStruct((s, SB_WIDTH), F32),
                   jax.ShapeDtypeStruct((s, SB_WIDTH), F32)],
        compiler_params=_cparams(("arbitrary", "arbitrary")),
    )(sbqkv, sbqkv, sbqkv, osb, dosb)


def _swa_bwd(qn, kn, v, sinks, do):
    s = qn.shape[0]
    t = min(512, s)
    nb = t // SWA_BLOCK
    nheads = SWA_Q_WIDTH // HEAD_DIM

    def body(sink_ref, q_ref, kc_ref, kp_ref, vc_ref, vp_ref, do_ref, dq_ref, dk_ref, dv_ref, dsink_ref):
        i = pl.program_id(0)

        @pl.when(i == 0)
        def _():
            dk_ref[...] = jnp.zeros_like(dk_ref)
            dv_ref[...] = jnp.zeros_like(dv_ref)

        left = _lane((1, LANES)) < HEAD_DIM
        halves = (left, jnp.logical_not(left))
        dsink = [jnp.zeros((1, 1), F32) for _ in range(nheads)]
        for b in range(nb):
            kvar = _swa_variants(_swa_band(kc_ref, kp_ref, b), left)
            vvar = _swa_variants(_swa_band(vc_ref, vp_ref, b), left)
            rows = slice(b * SWA_BLOCK, (b + 1) * SWA_BLOCK)
            dk_acc = [jnp.zeros((2 * SWA_BLOCK, LANES), F32) for _ in range(2)]
            dv_acc = [jnp.zeros((2 * SWA_BLOCK, LANES), F32) for _ in range(2)]
            for hb in range(nheads // 2):
                q2 = q_ref[rows, hb * LANES:(hb + 1) * LANES]
                do2 = do_ref[rows, hb * LANES:(hb + 1) * LANES]
                dq = jnp.zeros((SWA_BLOCK, LANES), F32)
                for p in range(2):
                    h = 2 * hb + p
                    g = h // (nheads // 2)
                    qh = jnp.where(halves[p], q2, jnp.zeros_like(q2))
                    doh = jnp.where(halves[p], do2, 0.0).astype(BF16)
                    probs, psink = _swa_probs(qh, kvar[g][p], sink_ref[h], i * nb + b)
                    dp = _dot(doh, vvar[g][p], NT)
                    delta = jnp.sum(probs * dp, axis=-1, keepdims=True)
                    ds = (probs * (dp - delta) * SCALE).astype(BF16)
                    dsink[h] = dsink[h] - jnp.sum(psink * delta, axis=0, keepdims=True)
                    dq = dq + _dot(ds, kvar[g][p], NN)
                    which = 0 if p == g else 1
                    dk_acc[which] = dk_acc[which] + _dot(ds, qh, TN)
                    dv_acc[which] = dv_acc[which] + _dot(probs.astype(BF16), doh, TN)
                dq_ref[rows, hb * LANES:(hb + 1) * LANES] = dq
            dkb = dk_acc[0] + pltpu.roll(dk_acc[1], HEAD_DIM, 1)
            dvb = dv_acc[0] + pltpu.roll(dv_acc[1], HEAD_DIM, 1)
            start = pl.multiple_of((i * nb + b) * SWA_BLOCK, SWA_BLOCK)
            dk_ref[pl.ds(start, SWA_BLOCK), :] += dkb[SWA_BLOCK:]
            dv_ref[pl.ds(start, SWA_BLOCK), :] += dvb[SWA_BLOCK:]

            @pl.when(i * nb + b > 0)
            def _(dkb=dkb, dvb=dvb, start=start):
                before = pl.multiple_of(jnp.maximum(start - SWA_BLOCK, 0), SWA_BLOCK)
                dk_ref[pl.ds(before, SWA_BLOCK), :] += dkb[:SWA_BLOCK]
                dv_ref[pl.ds(before, SWA_BLOCK), :] += dvb[:SWA_BLOCK]

        for h in range(nheads):
            dsink_ref[0, h:h + 1, :] = jnp.broadcast_to(dsink[h], (1, LANES))

    cur, prev = _swa_specs(s, t)
    whole = pl.BlockSpec((s, LANES), lambda i: (0, 0))
    return pl.pallas_call(
        body, name="swa_bwd", grid=(s // t,),
        in_specs=[pl.BlockSpec(memory_space=pltpu.SMEM), _rows(t, SWA_Q_WIDTH), cur, prev, cur, prev,
                  _rows(t, SWA_Q_WIDTH)],
        out_specs=[_rows(t, SWA_Q_WIDTH), whole, whole, pl.BlockSpec((1, 8, LANES), lambda i: (i, 0, 0))],
        out_shape=[jax.ShapeDtypeStruct((s, SWA_Q_WIDTH), F32), jax.ShapeDtypeStruct((s, SWA_KV_WIDTH), F32),
                   jax.ShapeDtypeStruct((s, SWA_KV_WIDTH), F32), jax.ShapeDtypeStruct((s // t, 8, LANES), F32)],
        compiler_params=_cparams(("arbitrary",)),
    )(sinks, qn, kn, kn, v, v, do)


def _swa_post(raw, dqn, dkn, qg, kg, cos, sin):
    s = raw.shape[0]
    tm = min(512, s)
    nq = SWA_Q_WIDTH // LANES

    def body(raw_ref, dq_ref, dk_ref, qg_ref, kg_ref, cos_ref, sin_ref, out_ref, dg_ref):
        @pl.when(pl.program_id(0) == 0)
        def _():
            dg_ref[...] = jnp.zeros_like(dg_ref)

        cs, sn = cos_ref[...], sin_ref[...]
        dgq = jnp.zeros((1, LANES), F32)
        for b in range(nq):
            cols = slice(b * LANES, (b + 1) * LANES)
            dp, dg = _norm_rope_bwd(raw_ref[:, cols], dq_ref[:, cols], qg_ref[...], cs, sn)
            out_ref[:, cols] = dp.astype(BF16)
            dgq = dgq + dg
        cols = slice(SWA_Q_WIDTH, SWA_Q_WIDTH + LANES)
        dp, dgk = _norm_rope_bwd(raw_ref[:, cols], dk_ref[...], kg_ref[...], cs, sn)
        out_ref[:, cols] = dp.astype(BF16)
        dg_ref[0:1, :] += dgq
        dg_ref[1:2, :] += dgk

    return pl.pallas_call(
        body, name="swa_post", grid=(s // tm,),
        in_specs=[_rows(tm, 640), _rows(tm, SWA_Q_WIDTH), _rows(tm, LANES), _resident((1, LANES)),
                  _resident((1, LANES)), _rows(tm, LANES), _rows(tm, LANES)],
        out_specs=[_rows(tm, 640), pl.BlockSpec((8, LANES), lambda i: (0, 0))],
        out_shape=[jax.ShapeDtypeStruct((s, 640), BF16), jax.ShapeDtypeStruct((8, LANES), F32)],
        compiler_params=_cparams(("arbitrary",)),
    )(raw, dqn, dkn, qg, kg, cos, sin)


def _inproj_bwd(dsbq, dsbk, dsbv, dswqk, dswv, dgl, x, dx1, g, w):
    s = x.shape[0]
    tm = min(256, s)

    def body(dsbq_ref, dsbk_ref, dsbv_ref, dswqk_ref, dswv_ref, dgl_ref, x_ref, dx1_ref, g_ref, w_ref,
             dx_ref, dproj_ref, dg_ref):
        @pl.when(pl.program_id(0) == 0)
        def _():
            dg_ref[...] = jnp.zeros_like(dg_ref)

        dproj_ref[:, 0:512] = dsbq_ref[...]
        dproj_ref[:, 512:1024] = dsbk_ref[...].astype(BF16)
        dproj_ref[:, 1024:1536] = dsbv_ref[...].astype(BF16)
        dproj_ref[:, 1536:2176] = dswqk_ref[...]
        dproj_ref[:, 2176:2304] = dswv_ref[...].astype(BF16)
        dproj_ref[:, GATE_OFF:IN_WIDTH] = dgl_ref[...]
        dh = jnp.zeros((tm, D_MODEL), F32)
        for a in range(0, IN_WIDTH, 512):
            b = min(a + 512, IN_WIDTH)
            dh = dh + _dot(dproj_ref[:, a:b], w_ref[:, a:b], NT)
        gam = g_ref[...]
        xn, rstd = _rms(x_ref[...])
        dg_ref[...] += jnp.sum(dh * xn, axis=0, keepdims=True)
        dx_ref[...] = dx1_ref[...] + _rms_bwd(dh, xn, rstd, gam)

    return pl.pallas_call(
        body, name="inproj_bwd", grid=(s // tm,),
        in_specs=[_rows(tm, 512), _rows(tm, 512), _rows(tm, 512), _rows(tm, 640), _rows(tm, LANES),
                  _rows(tm, 2 * D_MODEL), _rows(tm, D_MODEL), _rows(tm, D_MODEL), _resident((1, D_MODEL)),
                  _resident((D_MODEL, IN_WIDTH))],
        out_specs=[_rows(tm, D_MODEL), _rows(tm, IN_WIDTH), pl.BlockSpec((1, D_MODEL), lambda i: (0, 0))],
        out_shape=[jax.ShapeDtypeStruct((s, D_MODEL), F32), jax.ShapeDtypeStruct((s, IN_WIDTH), BF16),
                   jax.ShapeDtypeStruct((1, D_MODEL), F32)],
        compiler_params=_cparams(("arbitrary",)),
    )(dsbq, dsbk, dsbv, dswqk, dswv, dgl, x, dx1, g, w)


def _rope_tables(s):
    inv_freq = 1.0 / (ROPE_THETA ** (jnp.arange(0, HEAD_DIM, 2, dtype=F32) / HEAD_DIM))
    ang = jnp.arange(s, dtype=F32)[:, None] * inv_freq[None, :]
    reps = LANES // (HEAD_DIM // 2)
    return jnp.tile(jnp.cos(ang), (1, reps)), jnp.tile(jnp.sin(ang), (1, reps))


def _local_step(x, target, wts, small):
    s = x.shape[0]
    cos, sin = _rope_tables(s)
    two = lambda gvec: jnp.tile(gvec.reshape(1, HEAD_DIM), (1, 2))
    saved = []
    for l in range(DEPTH):
        gm = small["mix_norm_g"][l].reshape(1, D_MODEL)
        gl = small["mlp_norm_g"][l].reshape(1, D_MODEL)
        qg, kg = two(small["q_norm_g"][l]), two(small["k_norm_g"][l])
        h, sbqkv, raw, qn, kn, v, gates = _inproj_fwd(x, gm, wts["w_in"][l], qg, kg, cos, sin)
        osb, osbx = _sb_fwd(sbqkv)
        oswa = _swa_fwd(qn, kn, v, small["sinks"][l])
        x1 = _merge_fwd(x, osb, oswa, gates, wts["w_branch_sb"][l], wts["w_branch_swa"][l], wts["w_out"][l])
        x2, u = _mlp_fwd(x1, gl, wts["w_up"][l], wts["w_down"][l])
        saved.append((x, h, sbqkv, raw, qn, kn, v, gates, osb, osbx, oswa, x1, u, gm, gl, qg, kg))
        x = x2
    dx, loss = _loss_grad(x, target)

    grads = {name: [None] * DEPTH for name, _ in PACK_ROWS}
    gsmall = {name: [None] * DEPTH for name in ("mix_norm_g", "q_norm_g", "k_norm_g", "sinks", "mlp_norm_g")}
    for l in reversed(range(DEPTH)):
        x0, h, sbqkv, raw, qn, kn, v, gates, osb, osbx, oswa, x1, u, gm, gl, qg, kg = saved[l]
        dx1, du, act, h2, dgl_mlp = _mlp_bwd(dx, x1, u, gl, wts["w_up"][l], wts["w_down"][l])
        grads["w_down"][l] = _wgrad(act, dx, "wgrad_down", shard_axis=0)
        grads["w_up"][l] = _wgrad(h2, du, "wgrad_up", shard_axis=1)
        dosb, doswa, dgate, merged, dysb, dyswa = _merge_bwd(
            dx1, osb, oswa, gates, wts["w_branch_sb"][l], wts["w_branch_swa"][l], wts["w_out"][l])
        grads["w_out"][l] = _wgrad(merged, dx1, "wgrad_out", shard_axis=0)
        grads["w_branch_sb"][l] = _wgrad(osb, dysb, "wgrad_bsb", shard_axis=1)
        grads["w_branch_swa"][l] = _wgrad(oswa, dyswa, "wgrad_bswa", shard_axis=1)
        dsbq, dsbk, dsbv = _sb_bwd(sbqkv, osbx, dosb)
        dqn, dkn, dswv, dsink = _swa_bwd(qn, kn, v, small["sinks"][l], doswa)
        dswqk, dgqk = _swa_post(raw, dqn, dkn, qg, kg, cos, sin)
        dx, dproj, dg_mix = _inproj_bwd(dsbq, dsbk, dsbv, dswqk, dswv, dgate, x0, dx1, gm, wts["w_in"][l])
        dwin = _wgrad(h, dproj, "wgrad_in")
        grads["w_in"][l] = dwin.reshape(D_MODEL, N_CHIPS, IN_WIDTH // N_CHIPS).transpose(1, 0, 2)
        gsmall["mix_norm_g"][l] = dg_mix[0]
        gsmall["mlp_norm_g"][l] = dgl_mlp[0]
        gsmall["q_norm_g"][l] = dgqk[0, :HEAD_DIM] + dgqk[0, HEAD_DIM:]
        gsmall["k_norm_g"][l] = dgqk[1, :HEAD_DIM] + dgqk[1, HEAD_DIM:]
        gsmall["sinks"][l] = jnp.sum(dsink[:, :, 0], axis=0)
    gsmall = {k: jnp.stack(vs) for k, vs in gsmall.items()}
    return loss, dx, grads, gsmall


def _place():
    x, y, c = lax.axis_index("x"), lax.axis_index("y"), lax.axis_index("c")
    return x, y, c, [(1 - x, y), (x, 1 - y), (1 - x, 1 - y)]


HBM = pl.BlockSpec(memory_space=pl.ANY)


def _gather_weights(packed):
    nl = packed.shape[0]
    hl = nl // 2

    def body(src, out, ssem, rsem, lsem):
        x, y, c, chips = _place()
        me = 2 * x + y
        sib = (x, y, 1 - c)

        def slab(chip, half):
            return out.at[chip, pl.ds(half * hl, hl)]

        local = pltpu.make_async_copy(src, out.at[me], lsem)
        local.start()
        sends = []
        for j, (cx, cy) in enumerate(chips):
            cp = pltpu.make_async_remote_copy(src.at[pl.ds(c * hl, hl)], slab(me, c), ssem.at[j], rsem.at[j],
                                              device_id=(cx, cy, c), device_id_type=MESH)
            cp.start()
            sends.append(cp)
        for j, (cx, cy) in enumerate(chips):
            got = slab(2 * cx + cy, c)
            pltpu.make_async_remote_copy(got, got, ssem.at[j], rsem.at[j], device_id=sib, device_id_type=MESH).wait_recv()
            cp = pltpu.make_async_remote_copy(got, got, ssem.at[3 + j], rsem.at[3 + j], device_id=sib, device_id_type=MESH)
            cp.start()
            sends.append(cp)
        for j, (cx, cy) in enumerate(chips):
            got = slab(2 * cx + cy, 1 - c)
            pltpu.make_async_remote_copy(got, got, ssem.at[3 + j], rsem.at[3 + j], device_id=sib, device_id_type=MESH).wait_recv()
        for cp in sends:
            cp.wait_send()
        local.wait()

    return pl.pallas_call(
        body, name="gather_weights", in_specs=[HBM], out_specs=HBM,
        out_shape=jax.ShapeDtypeStruct((N_CHIPS,) + packed.shape, packed.dtype),
        scratch_shapes=[pltpu.SemaphoreType.DMA((6,)), pltpu.SemaphoreType.DMA((6,)), pltpu.SemaphoreType.DMA],
        compiler_params=pltpu.CompilerParams(has_side_effects=True),
    )(packed)


def _pair_exchange(g):
    nq, nl, r, cdim = g.shape
    r2 = r // 2

    def body(src, out, ssem, rsem):
        x, y, c, _ = _place()
        sib = (x, y, 1 - c)
        cps = []
        for q in range(nq):
            cp = pltpu.make_async_remote_copy(src.at[q, :, pl.ds((1 - c) * r2, r2)], out.at[q], ssem.at[q], rsem.at[q],
                                              device_id=sib, device_id_type=MESH)
            cp.start()
            cps.append(cp)
        for cp in cps:
            cp.wait()

    return pl.pallas_call(
        body, name="pair_exchange", in_specs=[HBM], out_specs=HBM,
        out_shape=jax.ShapeDtypeStruct((nq, nl, r2, cdim), g.dtype),
        scratch_shapes=[pltpu.SemaphoreType.DMA((nq,)), pltpu.SemaphoreType.DMA((nq,))],
        compiler_params=pltpu.CompilerParams(has_side_effects=True),
    )(g)


def _pair_add(g, other):
    nq, nl, r, cdim = g.shape
    r2 = r // 2
    tr = 456
    nblk = r2 // tr
    c = lax.axis_index("c").astype(jnp.int32).reshape(1)

    def body(c_ref, a_ref, b_ref, o_ref):
        o_ref[...] = a_ref[...] + b_ref[...]

    return pl.pallas_call(
        body, name="pair_add",
        grid_spec=pltpu.PrefetchScalarGridSpec(
            num_scalar_prefetch=1, grid=(nq, nl, nblk),
            in_specs=[pl.BlockSpec((1, 1, tr, cdim), lambda q, l, i, cr: (q, l, cr[0] * nblk + i, 0)),
                      pl.BlockSpec((1, 1, tr, cdim), lambda q, l, i, cr: (q, l, i, 0))],
            out_specs=pl.BlockSpec((1, 1, tr, cdim), lambda q, l, i, cr: (q, l, i, 0))),
        out_shape=jax.ShapeDtypeStruct((nq, nl, r2, cdim), F32),
        compiler_params=_cparams(("arbitrary", "arbitrary", "arbitrary")),
    )(c, g, other)


def _chip_exchange(hsum):
    def body(src, out, ssem, rsem, lsem):
        x, y, c, chips = _place()
        me = 2 * x + y
        local = pltpu.make_async_copy(src.at[me], out.at[me], lsem)
        local.start()
        cps = []
        for j, (cx, cy) in enumerate(chips):
            cp = pltpu.make_async_remote_copy(src.at[2 * cx + cy], out.at[me], ssem.at[j], rsem.at[j],
                                              device_id=(cx, cy, c), device_id_type=MESH)
            cp.start()
            cps.append(cp)
        for j, (cx, cy) in enumerate(chips):
            got = out.at[2 * cx + cy]
            pltpu.make_async_remote_copy(got, got, ssem.at[j], rsem.at[j], device_id=(cx, cy, c), device_id_type=MESH).wait_recv()
        for cp in cps:
            cp.wait_send()
        local.wait()

    return pl.pallas_call(
        body, name="chip_exchange", in_specs=[HBM], out_specs=HBM,
        out_shape=jax.ShapeDtypeStruct(hsum.shape, hsum.dtype),
        scratch_shapes=[pltpu.SemaphoreType.DMA((3,)), pltpu.SemaphoreType.DMA((3,)), pltpu.SemaphoreType.DMA],
        compiler_params=pltpu.CompilerParams(has_side_effects=True),
    )(hsum)


def _sum_chips(parts):
    nq, nl, r2, cdim = parts.shape
    tr = 456

    def body(p0, p1, p2, p3, o_ref):
        o_ref[...] = ((p0[0] + p1[0]) + p2[0]) + p3[0]

    spec = lambda q: pl.BlockSpec((1, 1, tr, cdim), lambda l, i, q=q: (q, l, i, 0))
    return pl.pallas_call(
        body, name="sum_chips", grid=(nl, r2 // tr),
        in_specs=[spec(q) for q in range(nq)],
        out_specs=pl.BlockSpec((1, tr, cdim), lambda l, i: (l, i, 0)),
        out_shape=jax.ShapeDtypeStruct((nl, r2, cdim), F32),
        compiler_params=_cparams(("arbitrary", "arbitrary")),
    )(parts, parts, parts, parts)


def _pair_gather(half):
    nl, r2, cdim = half.shape

    def body(src, out, ssem, rsem, lsem):
        x, y, c, _ = _place()
        sib = (x, y, 1 - c)
        local = pltpu.make_async_copy(src, out.at[:, pl.ds(c * r2, r2)], lsem)
        local.start()
        cp = pltpu.make_async_remote_copy(src, out.at[:, pl.ds(c * r2, r2)], ssem, rsem, device_id=sib, device_id_type=MESH)
        cp.start()
        theirs = out.at[:, pl.ds((1 - c) * r2, r2)]
        pltpu.make_async_remote_copy(theirs, theirs, ssem, rsem, device_id=sib, device_id_type=MESH).wait_recv()
        cp.wait_send()
        local.wait()

    return pl.pallas_call(
        body, name="pair_gather", in_specs=[HBM], out_specs=HBM,
        out_shape=jax.ShapeDtypeStruct((nl, 2 * r2, cdim), half.dtype),
        scratch_shapes=[pltpu.SemaphoreType.DMA, pltpu.SemaphoreType.DMA, pltpu.SemaphoreType.DMA],
        compiler_params=pltpu.CompilerParams(has_side_effects=True),
    )(half)


def _allreduce_small(block):
    def body(src, out, buf, ssem, rsem):
        x, y, c, _ = _place()
        me = 4 * x + 2 * y + c
        buf[me] = src[...]
        cps = []
        for k in range(1, N_DEV):
            peer = (x ^ (k >> 2), y ^ ((k >> 1) & 1), c ^ (k & 1))
            cp = pltpu.make_async_remote_copy(src, buf.at[me], ssem.at[k - 1], rsem.at[k - 1], device_id=peer, device_id_type=MESH)
            cp.start()
            cps.append(cp)
        for k in range(1, N_DEV):
            got = buf.at[me ^ k]
            pltpu.make_async_remote_copy(got, got, ssem.at[k - 1], rsem.at[k - 1], device_id=(x, y, c), device_id_type=MESH).wait_recv()
        for cp in cps:
            cp.wait_send()
        tot = buf[0]
        for d in range(1, N_DEV):
            tot = tot + buf[d]
        out[...] = tot

    vm = pl.BlockSpec(memory_space=pltpu.VMEM)
    return pl.pallas_call(
        body, name="allreduce_small", in_specs=[vm], out_specs=vm,
        out_shape=jax.ShapeDtypeStruct(block.shape, F32),
        scratch_shapes=[pltpu.VMEM((N_DEV,) + block.shape, F32), pltpu.SemaphoreType.DMA((N_DEV - 1,)),
                        pltpu.SemaphoreType.DMA((N_DEV - 1,))],
        compiler_params=pltpu.CompilerParams(has_side_effects=True),
    )(block)


def _adamw(w, g, m, v, name):
    shape = w.shape
    cols = shape[-1]
    flat = lambda t: t.reshape(-1, cols)
    rows = flat(w).shape[0]
    tr = min(512, rows)

    def body(w_ref, g_ref, m_ref, v_ref, d_ref, nm_ref, nv_ref):
        gr = g_ref[...]
        nm = ADAM_B1 * m_ref[...] + (1.0 - ADAM_B1) * gr
        nv = ADAM_B2 * v_ref[...] + (1.0 - ADAM_B2) * (gr * gr)
        m_hat = nm / (1.0 - ADAM_B1 ** ADAM_STEP)
        v_hat = nv / (1.0 - ADAM_B2 ** ADAM_STEP)
        d_ref[...] = -ADAM_LR * (m_hat / (jnp.sqrt(v_hat) + ADAM_EPS) + ADAM_WD * w_ref[...])
        nm_ref[...] = nm
        nv_ref[...] = nv

    spec = pl.BlockSpec((tr, cols), lambda i: (i, 0))
    outs = pl.pallas_call(
        body, name=name, grid=(rows // tr,),
        in_specs=[spec] * 4, out_specs=[spec] * 3,
        out_shape=[jax.ShapeDtypeStruct((rows, cols), F32)] * 3,
        compiler_params=_cparams(("arbitrary",)),
    )(flat(w), flat(g), flat(m), flat(v))
    return [o.reshape(shape) for o in outs]


def _pack(tensors):
    lead = tensors["w_in"].ndim - 2
    return jnp.concatenate(
        [tensors[name].reshape(tensors[name].shape[:lead] + (rows, 1024)) for name, rows in PACK_ROWS], axis=lead)


SHARD_SHAPES = {"w_in": (1024, 1088), "w_branch_sb": (512, 256), "w_branch_swa": (512, 256), "w_out": (256, 1024),
                "w_up": (1024, 1024), "w_down": (1024, 1024)}


def _unpack(packed):
    out, at = {}, 0
    for name, rows in PACK_ROWS:
        out[name] = packed[..., at:at + rows, :].reshape(packed.shape[:-2] + SHARD_SHAPES[name])
        at += rows
    return out


def _whole_weights(gathered):
    parts = _unpack(gathered)
    whole = {}
    for name, t in parts.items():
        if name in ("w_out", "w_down"):
            full = jnp.moveaxis(t, 0, 1).reshape(DEPTH, N_CHIPS * t.shape[2], t.shape[3])
        else:
            full = jnp.moveaxis(t, 0, 2).reshape(DEPTH, t.shape[2], N_CHIPS * t.shape[3])
        whole[name] = [full[l] for l in range(DEPTH)]
    return whole


SMALL_NAMES = ("mix_norm_g", "q_norm_g", "k_norm_g", "sinks", "mlp_norm_g")


def _pack_small(d):
    flat = jnp.concatenate([d[n].reshape(-1) for n in SMALL_NAMES])
    return jnp.pad(flat, (0, SMALL_ROWS * LANES - flat.shape[0])).reshape(SMALL_ROWS, LANES)


def _unpack_small(block, like):
    flat, out, at = block.reshape(-1), {}, 0
    for n in SMALL_NAMES:
        size = like[n].size
        out[n] = flat[at:at + size].reshape(like[n].shape)
        at += size
    return out


def kernel(x, mix_norm_g, w_in, q_norm_g, k_norm_g, sinks, w_branch_sb, w_branch_swa, w_out, mlp_norm_g, w_up, w_down, loss_target, m_mix_norm_g, m_w_in, m_q_norm_g, m_k_norm_g, m_sinks, m_w_branch_sb, m_w_branch_swa, m_w_out, m_mlp_norm_g, m_w_up, m_w_down, v_mix_norm_g, v_w_in, v_q_norm_g, v_k_norm_g, v_sinks, v_w_branch_sb, v_w_branch_swa, v_w_out, v_mlp_norm_g, v_w_up, v_w_down):
    big = dict(w_in=w_in, w_branch_sb=w_branch_sb, w_branch_swa=w_branch_swa, w_out=w_out, w_up=w_up, w_down=w_down)
    big_m = dict(w_in=m_w_in, w_branch_sb=m_w_branch_sb, w_branch_swa=m_w_branch_swa, w_out=m_w_out, w_up=m_w_up, w_down=m_w_down)
    big_v = dict(w_in=v_w_in, w_branch_sb=v_w_branch_sb, w_branch_swa=v_w_branch_swa, w_out=v_w_out, w_up=v_w_up, w_down=v_w_down)
    small = dict(mix_norm_g=mix_norm_g, q_norm_g=q_norm_g, k_norm_g=k_norm_g, sinks=sinks, mlp_norm_g=mlp_norm_g)
    small_m = dict(mix_norm_g=m_mix_norm_g, q_norm_g=m_q_norm_g, k_norm_g=m_k_norm_g, sinks=m_sinks, mlp_norm_g=m_mlp_norm_g)
    small_v = dict(mix_norm_g=v_mix_norm_g, q_norm_g=v_q_norm_g, k_norm_g=v_k_norm_g, sinks=v_sinks, mlp_norm_g=v_mlp_norm_g)

    gathered = _gather_weights(_pack({k: t.astype(BF16) for k, t in big.items()}))
    wts = _whole_weights(gathered)

    loss_part, grad_x, grads, gsmall = _local_step(x[0], loss_target[0], wts, small)
    loss = lax.psum(loss_part, ("x", "y", "c"))

    partial = _pack({k: jnp.stack(v, axis=1) for k, v in grads.items()})
    chip_sum = _pair_add(partial, _pair_exchange(partial))
    reduced = _pair_gather(_sum_chips(_chip_exchange(chip_sum)))
    g_big = _unpack(reduced)
    g_small = _unpack_small(_allreduce_small(_pack_small(gsmall)), small)

    upd = {k: _adamw(big[k], g_big[k], big_m[k], big_v[k], "adamw_" + k) for k in big}
    sm = _adamw(_pack_small(small), _pack_small(g_small), _pack_small(small_m), _pack_small(small_v), "adamw_small")
    upd_small = [_unpack_small(t, small) for t in sm]

    names = ("mix_norm_g", "w_in", "q_norm_g", "k_norm_g", "sinks", "w_branch_sb", "w_branch_swa", "w_out",
             "mlp_norm_g", "w_up", "w_down")
    grad_of = lambda n: g_big[n] if n in big else g_small[n]
    upd_of = lambda n, i: upd[n][i] if n in big else upd_small[i][n]
    return (loss, grad_x[None], *[grad_of(n) for n in names], *[upd_of(n, 0) for n in names],
            *[upd_of(n, 1) for n in names], *[upd_of(n, 2) for n in names])
```

```python
import jax
import jax.numpy as jnp
from jax import lax
from jax.experimental import pallas as pl
from jax.experimental.pallas import tpu as pltpu

F32 = jnp.float32
BF16 = jnp.bfloat16

D_MODEL = 1024
DEPTH = 4
HEAD_DIM = 64
SB_WIDTH = 512
SWA_Q_WIDTH = 512
SWA_KV_WIDTH = 128
D_FF = 4096
IN_WIDTH = 4352
GATE_OFF = 2304
ROPE_THETA = 10000.0
NORM_EPS = 1e-6
SCALE = HEAD_DIM ** -0.5
N_CHIPS = 4
N_DEV = 8

ADAM_LR = 0.001
ADAM_B1 = 0.9
ADAM_B2 = 0.999
ADAM_EPS = 1e-08
ADAM_WD = 0.01
ADAM_STEP = 10

LANES = 128
SB_TILE = 256
SWA_BLOCK = 128
VMEM_LIMIT = 56 << 20

NN = (((1,), (0,)), ((), ()))
NT = (((1,), (1,)), ((), ()))
TN = (((0,), (0,)), ((), ()))
MESH = pl.DeviceIdType.MESH
HBM = pl.BlockSpec(memory_space=pl.ANY)

BIG = ("w_in", "w_branch_sb", "w_branch_swa", "w_out", "w_up", "w_down")
SMALL_NAMES = ("mix_norm_g", "q_norm_g", "k_norm_g", "sinks", "mlp_norm_g")
PACK_ROWS = (("w_in", 1088), ("w_branch_sb", 128), ("w_branch_swa", 128), ("w_out", 256), ("w_up", 1024), ("w_down", 1024))
SHARD_SHAPES = {"w_in": (1024, 1088), "w_branch_sb": (512, 256), "w_branch_swa": (512, 256), "w_out": (256, 1024),
                "w_up": (1024, 1024), "w_down": (1024, 1024)}
ROW_SHARDED = ("w_out", "w_down")
SMALL_ROWS = 72


def _dot(a, b, dims):
    return lax.dot_general(a, b, dims, preferred_element_type=F32)


def _cparams(sem):
    return pltpu.CompilerParams(dimension_semantics=sem, vmem_limit_bytes=VMEM_LIMIT)


def _resident(shape):
    nd = len(shape)
    return pl.BlockSpec(shape, lambda *_: (0,) * nd, pipeline_mode=pl.Buffered(1))


def _rows(tm, width):
    return pl.BlockSpec((tm, width), lambda i: (i, 0))


def _rms(xf):
    rstd = lax.rsqrt(jnp.mean(xf * xf, axis=-1, keepdims=True) + NORM_EPS)
    return xf * rstd, rstd


def _rms_bwd(dh, xn, rstd, g):
    dxn = dh * g
    return rstd * (dxn - xn * jnp.mean(dxn * xn, axis=-1, keepdims=True))


def _lane(shape):
    return lax.broadcasted_iota(jnp.int32, shape, len(shape) - 1)


def _head_mean(v, left):
    sl = jnp.sum(jnp.where(left, v, 0.0), axis=-1, keepdims=True)
    sr = jnp.sum(jnp.where(left, 0.0, v), axis=-1, keepdims=True)
    return jnp.where(left, sl, sr) * (1.0 / HEAD_DIM)


def _rope(y, cs, sn, first):
    up = pltpu.roll(y, 96, 1)
    dn = pltpu.roll(y, 32, 1)
    return y * cs + jnp.where(first, -up, dn) * sn


def _rope_t(d, cs, sn, first):
    t = d * jnp.where(first, -sn, sn)
    return d * cs + jnp.where(first, pltpu.roll(t, 96, 1), pltpu.roll(t, 32, 1))


def _norm_rope(p, g, cs, sn):
    lane = _lane((1, LANES))
    left = lane < HEAD_DIM
    first = (lane % HEAD_DIM) < (HEAD_DIM // 2)
    yn = p * lax.rsqrt(_head_mean(p * p, left) + NORM_EPS)
    return _rope(yn * g, cs, sn, first)


def _norm_rope_bwd(p, dout, g, cs, sn):
    lane = _lane((1, LANES))
    left = lane < HEAD_DIM
    first = (lane % HEAD_DIM) < (HEAD_DIM // 2)
    rstd = lax.rsqrt(_head_mean(p * p, left) + NORM_EPS)
    yn = p * rstd
    dyg = _rope_t(dout, cs, sn, first)
    dg = jnp.sum(dyg * yn, axis=0, keepdims=True)
    dyn = dyg * g
    dp = rstd * (dyn - yn * _head_mean(dyn * yn, left))
    return dp, dg


def _place():
    x, y, c = lax.axis_index("x"), lax.axis_index("y"), lax.axis_index("c")
    return x, y, c, [(1 - x, y), (x, 1 - y), (1 - x, 1 - y)]


def _hosted_gather(src, out, ssem, rsem, lsem, first, mid, last):
    x, y, c, chips = _place()
    me = 2 * x + y
    sib = (x, y, 1 - c)
    r2 = src.shape[0] // 2

    def slab(chip, half):
        return out.at[chip, pl.ds(half * r2, r2)]

    def ici(j):
        cx, cy = chips[j]
        return pltpu.make_async_remote_copy(src.at[pl.ds(c * r2, r2)], slab(me, c), ssem.at[j], rsem.at[j],
                                            device_id=(cx, cy, c), device_id_type=MESH)

    def landed(j):
        got = slab(2 * chips[j][0] + chips[j][1], c)
        return pltpu.make_async_remote_copy(got, got, ssem.at[j], rsem.at[j], device_id=sib, device_id_type=MESH)

    def d2d(j, half):
        got = slab(2 * chips[j][0] + chips[j][1], half)
        return pltpu.make_async_remote_copy(got, got, ssem.at[3 + j], rsem.at[3 + j], device_id=sib, device_id_type=MESH)

    local = pltpu.make_async_copy(src, out.at[me], lsem)

    @pl.when(first)
    def _():
        local.start()
        for j in range(3):
            ici(j).start()

    @pl.when(mid)
    def _():
        for j in range(3):
            landed(j).wait_recv()
            d2d(j, c).start()

    @pl.when(last)
    def _():
        for j in range(3):
            d2d(j, 1 - c).wait_recv()
        for j in range(3):
            ici(j).wait_send()
            d2d(j, c).wait_send()
        local.wait()


def _hosted_exchange(srcs, outs, ssem, rsem, lsem, first, last):
    x, y, c, chips = _place()
    me = 2 * x + y

    def send(t, j):
        cx, cy = chips[j]
        return pltpu.make_async_remote_copy(srcs[t].at[2 * cx + cy], outs[t].at[me], ssem.at[3 * t + j], rsem.at[3 * t + j],
                                            device_id=(cx, cy, c), device_id_type=MESH)

    def landed(t, j):
        cx, cy = chips[j]
        got = outs[t].at[2 * cx + cy]
        return pltpu.make_async_remote_copy(got, got, ssem.at[3 * t + j], rsem.at[3 * t + j],
                                            device_id=(cx, cy, c), device_id_type=MESH)

    def local(t):
        return pltpu.make_async_copy(srcs[t].at[me], outs[t].at[me], lsem.at[t])

    @pl.when(first)
    def _():
        for t in range(len(srcs)):
            local(t).start()
            for j in range(3):
                send(t, j).start()

    @pl.when(last)
    def _():
        for t in range(len(srcs)):
            for j in range(3):
                landed(t, j).wait_recv()
        for t in range(len(srcs)):
            for j in range(3):
                send(t, j).wait_send()
            local(t).wait()


def _exchange_scratch(n):
    return [pltpu.SemaphoreType.DMA((3 * n,)), pltpu.SemaphoreType.DMA((3 * n,)), pltpu.SemaphoreType.DMA((n,))]


def _inproj_fwd(x, g, w, qg, kg, cos, sin):
    s = x.shape[0]
    tm = min(512, s)

    def body(x_ref, g_ref, w_ref, qg_ref, kg_ref, cos_ref, sin_ref,
             h_ref, sb_ref, raw_ref, qn_ref, kn_ref, v_ref, gate_ref):
        xn, _ = _rms(x_ref[...])
        h = (xn * g_ref[...]).astype(BF16)
        h_ref[...] = h
        for a in range(0, 3 * SB_WIDTH, 512):
            sb_ref[:, a:a + 512] = _dot(h, w_ref[:, a:a + 512], NN).astype(BF16)
        cs, sn = cos_ref[...], sin_ref[...]
        q0 = 3 * SB_WIDTH
        pq = _dot(h, w_ref[:, q0:q0 + SWA_Q_WIDTH], NN)
        raw_ref[:, 0:SWA_Q_WIDTH] = pq
        for b in range(SWA_Q_WIDTH // LANES):
            blk = pq[:, b * LANES:(b + 1) * LANES]
            qn_ref[:, b * LANES:(b + 1) * LANES] = _norm_rope(blk, qg_ref[...], cs, sn).astype(BF16)
        k0 = q0 + SWA_Q_WIDTH
        pk = _dot(h, w_ref[:, k0:k0 + 2 * SWA_KV_WIDTH], NN)
        raw_ref[:, SWA_Q_WIDTH:SWA_Q_WIDTH + SWA_KV_WIDTH] = pk[:, :SWA_KV_WIDTH]
        kn_ref[...] = _norm_rope(pk[:, :SWA_KV_WIDTH], kg_ref[...], cs, sn).astype(BF16)
        v_ref[...] = pk[:, SWA_KV_WIDTH:].astype(BF16)
        for a in range(GATE_OFF, IN_WIDTH, 512):
            gate_ref[:, a - GATE_OFF:a - GATE_OFF + 512] = jax.nn.sigmoid(_dot(h, w_ref[:, a:a + 512], NN))

    return pl.pallas_call(
        body, name="inproj_fwd", grid=(s // tm,),
        in_specs=[_rows(tm, D_MODEL), _resident((1, D_MODEL)), _resident((D_MODEL, IN_WIDTH)),
                  _resident((1, LANES)), _resident((1, LANES)), _rows(tm, LANES), _rows(tm, LANES)],
        out_specs=[_rows(tm, D_MODEL), _rows(tm, 3 * SB_WIDTH), _rows(tm, 640), _rows(tm, SWA_Q_WIDTH),
                   _rows(tm, SWA_KV_WIDTH), _rows(tm, SWA_KV_WIDTH), _rows(tm, 2 * D_MODEL)],
        out_shape=[jax.ShapeDtypeStruct((s, D_MODEL), BF16), jax.ShapeDtypeStruct((s, 3 * SB_WIDTH), BF16),
                   jax.ShapeDtypeStruct((s, 640), F32), jax.ShapeDtypeStruct((s, SWA_Q_WIDTH), BF16),
                   jax.ShapeDtypeStruct((s, SWA_KV_WIDTH), BF16), jax.ShapeDtypeStruct((s, SWA_KV_WIDTH), BF16),
                   jax.ShapeDtypeStruct((s, 2 * D_MODEL), F32)],
        compiler_params=_cparams(("arbitrary",)),
    )(x, g, w, qg, kg, cos, sin)


def _sb_tile_terms(qh, k, strict):
    z = _dot(qh, k, NT) * SCALE
    soft = jnp.log(1.0 + jnp.exp(-jnp.abs(z)))
    log_keep = -(jnp.maximum(z, 0.0) + soft)
    if strict is not None:
        log_keep = jnp.where(strict, log_keep, 0.0)
    return log_keep, jnp.minimum(z, 0.0) - soft


def _masked(strict, val):
    return val if strict is None else jnp.where(strict, val, 0.0)


def _split_dot(a, u):
    hi = a.astype(BF16)
    lo = (a - hi.astype(F32)).astype(BF16)
    return _dot(hi, u, NN) + _dot(lo, u, NN)


def _sb_fwd(sbqkv, shard=None):
    s = sbqkv.shape[0]
    t = min(SB_TILE, s)
    nq = s // t
    npair = SB_WIDTH // LANES

    def body(*refs):
        if shard is None:
            q_ref, k_ref, v_ref, o_ref, ox_ref = refs
        else:
            q_ref, k_ref, v_ref, src, o_ref, ox_ref, out, ssem, rsem, lsem = refs
        j, i = pl.program_id(0), pl.program_id(1)
        if shard is not None:
            _hosted_gather(src, out, ssem, rsem, lsem, (j == 0) & (i == 0), (j == npair // 2) & (i == 0),
                           (j == npair - 1) & (i == nq - 1))
        q2 = q_ref[...]
        left = _lane((1, LANES)) < HEAD_DIM
        halves = (left, jnp.logical_not(left))
        qh = [jnp.where(hm, q2, jnp.zeros_like(q2)) for hm in halves]
        r = lax.broadcasted_iota(jnp.int32, (t, t), 0)
        c = lax.broadcasted_iota(jnp.int32, (t, t), 1)
        u_after = (r > c).astype(BF16)

        def tile(n, carry, strict):
            runs, acc, rest = carry
            off = pl.multiple_of((i - n) * t, t)
            k = k_ref[pl.ds(off, t), :]
            v = v_ref[pl.ds(off, t), :]
            new_runs = []
            for p in range(2):
                log_keep, log_beta = _sb_tile_terms(qh[p], k, strict)
                tail = runs[p] + _split_dot(log_keep, u_after)
                w = _masked(strict, jnp.exp(log_beta + tail))
                vh = jnp.where(halves[p], v, jnp.zeros_like(v))
                wb = w.astype(BF16)
                acc = acc + _dot(wb, vh, NN)
                rest = rest + _dot((w - wb.astype(F32)).astype(BF16), vh, NN)
                new_runs.append(tail[:, 0:1] + log_keep[:, 0:1])
            return tuple(new_runs), acc, rest

        zero = jnp.zeros((t, 1), F32)
        wide = jnp.zeros((t, LANES), F32)
        carry = tile(0, ((zero, zero), wide, wide), c < r)
        _, acc, rest = lax.fori_loop(1, i + 1, lambda n, cr: tile(n, cr, None), carry)
        o_ref[...] = acc
        ox_ref[...] = acc + rest

    blk = pl.BlockSpec((t, LANES), lambda j, i: (i, j))
    in_specs = [blk, pl.BlockSpec((s, LANES), lambda j, i: (0, npair + j)),
                pl.BlockSpec((s, LANES), lambda j, i: (0, 2 * npair + j))]
    out_specs = [blk, blk]
    out_shape = [jax.ShapeDtypeStruct((s, SB_WIDTH), F32), jax.ShapeDtypeStruct((s, SB_WIDTH), F32)]
    args, scratch = [sbqkv, sbqkv, sbqkv], []
    if shard is not None:
        in_specs.append(HBM)
        out_specs.append(HBM)
        out_shape.append(jax.ShapeDtypeStruct((N_CHIPS,) + shard.shape, shard.dtype))
        args.append(shard)
        scratch = [pltpu.SemaphoreType.DMA((6,)), pltpu.SemaphoreType.DMA((6,)), pltpu.SemaphoreType.DMA]
    return pl.pallas_call(
        body, name="sb_fwd" if shard is None else "sb_fwd_gather", grid=(npair, nq),
        in_specs=in_specs, out_specs=out_specs, out_shape=out_shape, scratch_shapes=scratch,
        compiler_params=_cparams(("arbitrary", "arbitrary")),
    )(*args)


def _swa_band(cur_ref, prev_ref, b):
    lo = prev_ref[...] if b == 0 else cur_ref[(b - 1) * SWA_BLOCK:b * SWA_BLOCK, :]
    return jnp.concatenate([lo, cur_ref[b * SWA_BLOCK:(b + 1) * SWA_BLOCK, :]], axis=0)


def _swa_variants(band, left):
    f = band.astype(F32)
    sw = pltpu.roll(f, HEAD_DIM, 1)
    halves = (left, jnp.logical_not(left))
    return [[jnp.where(halves[p], f if p == g else sw, 0.0).astype(BF16) for p in range(2)] for g in range(2)]


def _swa_probs(qh, kv, sink, blk):
    sc = _dot(qh, kv, NT) * SCALE
    ii = lax.broadcasted_iota(jnp.int32, (SWA_BLOCK, 2 * SWA_BLOCK), 0)
    jj = lax.broadcasted_iota(jnp.int32, (SWA_BLOCK, 2 * SWA_BLOCK), 1)
    rel = jj - SWA_BLOCK - ii
    valid = (rel <= 0) & (rel > -SWA_BLOCK) & (jj + blk * SWA_BLOCK >= SWA_BLOCK)
    sc = jnp.where(valid, sc, -1e30)
    m = jnp.maximum(jnp.max(sc, axis=-1, keepdims=True), sink)
    e = jnp.exp(sc - m)
    es = jnp.exp(sink - m)
    inv = 1.0 / (jnp.sum(e, axis=-1, keepdims=True) + es)
    return e * inv, es * inv


def _swa_specs(s, t):
    nb = t // SWA_BLOCK
    cur = pl.BlockSpec((t, LANES), lambda i: (i, 0))
    prev = pl.BlockSpec((SWA_BLOCK, LANES), lambda i: (jnp.maximum(i * nb - 1, 0), 0))
    return cur, prev


def _swa_fwd(qn, kn, v, sinks):
    s = qn.shape[0]
    t = min(512, s)
    nb = t // SWA_BLOCK
    nheads = SWA_Q_WIDTH // HEAD_DIM

    def body(sink_ref, q_ref, kc_ref, kp_ref, vc_ref, vp_ref, o_ref):
        i = pl.program_id(0)
        left = _lane((1, LANES)) < HEAD_DIM
        halves = (left, jnp.logical_not(left))
        for b in range(nb):
            kvar = _swa_variants(_swa_band(kc_ref, kp_ref, b), left)
            vvar = _swa_variants(_swa_band(vc_ref, vp_ref, b), left)
            rows = slice(b * SWA_BLOCK, (b + 1) * SWA_BLOCK)
            for hb in range(nheads // 2):
                q2 = q_ref[rows, hb * LANES:(hb + 1) * LANES]
                acc = jnp.zeros((SWA_BLOCK, LANES), F32)
                for p in range(2):
                    h = 2 * hb + p
                    g = h // (nheads // 2)
                    qh = jnp.where(halves[p], q2, jnp.zeros_like(q2))
                    probs, _ = _swa_probs(qh, kvar[g][p], sink_ref[h], i * nb + b)
                    acc = acc + _dot(probs.astype(BF16), vvar[g][p], NN)
                o_ref[rows, hb * LANES:(hb + 1) * LANES] = acc.astype(BF16)

    cur, prev = _swa_specs(s, t)
    return pl.pallas_call(
        body, name="swa_fwd", grid=(s // t,),
        in_specs=[pl.BlockSpec(memory_space=pltpu.SMEM), _rows(t, SWA_Q_WIDTH), cur, prev, cur, prev],
        out_specs=_rows(t, SWA_Q_WIDTH),
        out_shape=jax.ShapeDtypeStruct((s, SWA_Q_WIDTH), BF16),
        compiler_params=_cparams(("arbitrary",)),
    )(sinks, qn, kn, kn, v, v)


def _merge_fwd(x, osb, oswa, gates, wsb, wswa, wout):
    s = x.shape[0]
    tm = min(512, s)

    def body(x_ref, osb_ref, oswa_ref, gate_ref, wsb_ref, wswa_ref, wout_ref, x1_ref):
        ysb = _dot(osb_ref[...].astype(BF16), wsb_ref[...], NN)
        yswa = _dot(oswa_ref[...], wswa_ref[...], NN)
        merged = gate_ref[:, :D_MODEL] * ysb + gate_ref[:, D_MODEL:] * yswa
        x1_ref[...] = x_ref[...] + _dot(merged.astype(BF16), wout_ref[...], NN)

    return pl.pallas_call(
        body, name="merge_fwd", grid=(s // tm,),
        in_specs=[_rows(tm, D_MODEL), _rows(tm, SB_WIDTH), _rows(tm, SWA_Q_WIDTH), _rows(tm, 2 * D_MODEL),
                  _resident((SB_WIDTH, D_MODEL)), _resident((SWA_Q_WIDTH, D_MODEL)), _resident((D_MODEL, D_MODEL))],
        out_specs=_rows(tm, D_MODEL),
        out_shape=jax.ShapeDtypeStruct((s, D_MODEL), F32),
        compiler_params=_cparams(("arbitrary",)),
    )(x, osb, oswa, gates, wsb, wswa, wout)


def _mlp_fwd(x1, g, wup, wdown):
    s = x1.shape[0]
    tm = min(512, s)
    fc = D_FF // N_CHIPS

    def body(x_ref, g_ref, wup_ref, wdown_ref, x2_ref, u_ref):
        xf = x_ref[...]
        xn, _ = _rms(xf)
        h2 = (xn * g_ref[...]).astype(BF16)
        acc = xf
        for q in range(N_CHIPS):
            u = _dot(h2, wup_ref[q], NN)
            u_ref[:, q * fc:(q + 1) * fc] = u
            r = jnp.maximum(u, 0.0)
            acc = acc + _dot((r * r).astype(BF16), wdown_ref[q * fc:(q + 1) * fc, :], NN)
        x2_ref[...] = acc

    return pl.pallas_call(
        body, name="mlp_fwd", grid=(s // tm,),
        in_specs=[_rows(tm, D_MODEL), _resident((1, D_MODEL)), _resident((N_CHIPS, D_MODEL, fc)), _resident((D_FF, D_MODEL))],
        out_specs=[_rows(tm, D_MODEL), _rows(tm, D_FF)],
        out_shape=[jax.ShapeDtypeStruct((s, D_MODEL), F32), jax.ShapeDtypeStruct((s, D_FF), F32)],
        compiler_params=_cparams(("arbitrary",)),
    )(x1, g, wup, wdown)


def _loss_grad(y, target):
    s = y.shape[0]
    tm = min(512, s)

    def body(y_ref, t_ref, dy_ref, part_ref):
        err = y_ref[...] - t_ref[...]
        dy_ref[...] = err * (1.0 / D_MODEL)
        tot = jnp.sum(jnp.sum(err * err, axis=-1, keepdims=True), axis=0, keepdims=True)
        part_ref[...] = jnp.broadcast_to(tot.reshape(1, 1, 1), (1, 8, LANES))

    dy, part = pl.pallas_call(
        body, name="loss_grad", grid=(s // tm,),
        in_specs=[_rows(tm, D_MODEL), _rows(tm, D_MODEL)],
        out_specs=[_rows(tm, D_MODEL), pl.BlockSpec((1, 8, LANES), lambda i: (i, 0, 0))],
        out_shape=[jax.ShapeDtypeStruct((s, D_MODEL), F32), jax.ShapeDtypeStruct((s // tm, 8, LANES), F32)],
        compiler_params=_cparams(("arbitrary",)),
    )(y, target)
    return dy, (0.5 / D_MODEL) * jnp.sum(part[:, 0, 0])


def _mlp_bwd(dx2, x1, u, g, wup, wdown):
    s = x1.shape[0]
    tm = min(256, s)
    fc = D_FF // N_CHIPS

    def body(dx2_ref, x_ref, u_ref, g_ref, wup_ref, wdown_ref, dx1_ref, du_ref, a_ref, h2_ref, dg_ref):
        @pl.when(pl.program_id(0) == 0)
        def _():
            dg_ref[...] = jnp.zeros_like(dg_ref)

        gam = g_ref[...]
        xn, rstd = _rms(x_ref[...])
        h2_ref[...] = (xn * gam).astype(BF16)
        dxf = dx2_ref[...]
        dxb = dxf.astype(BF16)
        dh2 = jnp.zeros((tm, D_MODEL), F32)
        for q in range(N_CHIPS):
            cols = slice(q * fc, (q + 1) * fc)
            da = _dot(dxb, wdown_ref[cols, :], NT)
            r = jnp.maximum(u_ref[:, cols], 0.0)
            a_ref[:, cols] = (r * r).astype(BF16)
            du = (da * (2.0 * r)).astype(BF16)
            du_ref[:, cols] = du
            dh2 = dh2 + _dot(du, wup_ref[q], NT)
        dg_ref[...] += jnp.sum(dh2 * xn, axis=0, keepdims=True)
        dx1_ref[...] = dxf + _rms_bwd(dh2, xn, rstd, gam)

    return pl.pallas_call(
        body, name="mlp_bwd", grid=(s // tm,),
        in_specs=[_rows(tm, D_MODEL), _rows(tm, D_MODEL), _rows(tm, D_FF), _resident((1, D_MODEL)),
                  _resident((N_CHIPS, D_MODEL, fc)), _resident((D_FF, D_MODEL))],
        out_specs=[_rows(tm, D_MODEL), _rows(tm, D_FF), _rows(tm, D_FF), _rows(tm, D_MODEL),
                   pl.BlockSpec((1, D_MODEL), lambda i: (0, 0))],
        out_shape=[jax.ShapeDtypeStruct((s, D_MODEL), F32), jax.ShapeDtypeStruct((s, D_FF), BF16),
                   jax.ShapeDtypeStruct((s, D_FF), BF16), jax.ShapeDtypeStruct((s, D_MODEL), BF16),
                   jax.ShapeDtypeStruct((1, D_MODEL), F32)],
        compiler_params=_cparams(("arbitrary",)),
    )(dx2, x1, u, g, wup, wdown)


def _wgrad(a, b, name, shard_axis=None):
    s, m = a.shape
    n = b.shape[1]
    tm = min(512, m if shard_axis != 0 else m // N_CHIPS)
    tn = min(512, n if shard_axis != 1 else n // N_CHIPS)
    if shard_axis is None and n % tn:
        tn = 256

    def body(a_ref, b_ref, o_ref):
        res = _dot(a_ref[...].astype(BF16), b_ref[...].astype(BF16), TN)
        o_ref[...] = res.reshape(o_ref.shape)

    if shard_axis is None:
        out_shape, out_spec = (m, n), pl.BlockSpec((tm, tn), lambda i, j: (i, j))
    elif shard_axis == 0:
        per = m // N_CHIPS // tm
        out_shape, out_spec = (N_CHIPS, m // N_CHIPS, n), pl.BlockSpec((1, tm, tn), lambda i, j: (i // per, i % per, j))
    else:
        per = n // N_CHIPS // tn
        out_shape, out_spec = (N_CHIPS, m, n // N_CHIPS), pl.BlockSpec((1, tm, tn), lambda i, j: (j // per, i, j % per))
    return pl.pallas_call(
        body, name=name, grid=(m // tm, n // tn),
        in_specs=[pl.BlockSpec((s, tm), lambda i, j: (0, i)), pl.BlockSpec((s, tn), lambda i, j: (0, j))],
        out_specs=out_spec,
        out_shape=jax.ShapeDtypeStruct(out_shape, F32),
        compiler_params=_cparams(("arbitrary", "arbitrary")),
    )(a, b)


def _merge_bwd(dx1, osb, oswa, gates, wsb, wswa, wout):
    s = dx1.shape[0]
    tm = min(512, s)

    def body(dx_ref, osb_ref, oswa_ref, gate_ref, wsb_ref, wswa_ref, wout_ref,
             dosb_ref, doswa_ref, dgl_ref, merged_ref, dysb_ref, dyswa_ref):
        dm = _dot(dx_ref[...].astype(BF16), wout_ref[...], NT)
        ysb = _dot(osb_ref[...].astype(BF16), wsb_ref[...], NN)
        yswa = _dot(oswa_ref[...], wswa_ref[...], NN)
        g0 = gate_ref[:, :D_MODEL]
        g1 = gate_ref[:, D_MODEL:]
        merged_ref[...] = (g0 * ysb + g1 * yswa).astype(BF16)
        dgl_ref[:, :D_MODEL] = (dm * ysb * (g0 * (1.0 - g0))).astype(BF16)
        dgl_ref[:, D_MODEL:] = (dm * yswa * (g1 * (1.0 - g1))).astype(BF16)
        dysb = (dm * g0).astype(BF16)
        dyswa = (dm * g1).astype(BF16)
        dysb_ref[...] = dysb
        dyswa_ref[...] = dyswa
        dosb_ref[...] = _dot(dysb, wsb_ref[...], NT)
        doswa_ref[...] = _dot(dyswa, wswa_ref[...], NT)

    return pl.pallas_call(
        body, name="merge_bwd", grid=(s // tm,),
        in_specs=[_rows(tm, D_MODEL), _rows(tm, SB_WIDTH), _rows(tm, SWA_Q_WIDTH), _rows(tm, 2 * D_MODEL),
                  _resident((SB_WIDTH, D_MODEL)), _resident((SWA_Q_WIDTH, D_MODEL)), _resident((D_MODEL, D_MODEL))],
        out_specs=[_rows(tm, SB_WIDTH), _rows(tm, SWA_Q_WIDTH), _rows(tm, 2 * D_MODEL), _rows(tm, D_MODEL),
                   _rows(tm, D_MODEL), _rows(tm, D_MODEL)],
        out_shape=[jax.ShapeDtypeStruct((s, SB_WIDTH), F32), jax.ShapeDtypeStruct((s, SWA_Q_WIDTH), F32),
                   jax.ShapeDtypeStruct((s, 2 * D_MODEL), BF16), jax.ShapeDtypeStruct((s, D_MODEL), BF16),
                   jax.ShapeDtypeStruct((s, D_MODEL), BF16), jax.ShapeDtypeStruct((s, D_MODEL), BF16)],
        compiler_params=_cparams(("arbitrary",)),
    )(dx1, osb, oswa, gates, wsb, wswa, wout)


def _sb_bwd(sbqkv, osbx, dosb, send):
    s = sbqkv.shape[0]
    t = min(SB_TILE, s)
    nq = s // t
    npair = SB_WIDTH // LANES
    ns = len(send)

    def body(*refs):
        q_ref, k_ref, v_ref, o_ref, do_ref = refs[:5]
        srcs = refs[5:5 + ns]
        dq_ref, dk_ref, dv_ref = refs[5 + ns:8 + ns]
        outs = refs[8 + ns:8 + 2 * ns]
        j, i = pl.program_id(0), pl.program_id(1)
        if ns:
            ssem, rsem, lsem = refs[8 + 2 * ns:]
            _hosted_exchange(srcs, outs, ssem, rsem, lsem, (j == 0) & (i == 0), (j == npair - 1) & (i == nq - 1))

        @pl.when(i == 0)
        def _():
            dk_ref[...] = jnp.zeros_like(dk_ref)
            dv_ref[...] = jnp.zeros_like(dv_ref)

        q2 = q_ref[...]
        do2 = do_ref[...]
        prod = do2.astype(BF16).astype(F32) * o_ref[...]
        left = _lane((1, LANES)) < HEAD_DIM
        halves = (left, jnp.logical_not(left))
        qh = [jnp.where(hm, q2, jnp.zeros_like(q2)) for hm in halves]
        doh = [jnp.where(hm, do2, 0.0).astype(BF16) for hm in halves]
        delta = [jnp.sum(jnp.where(hm, prod, 0.0), axis=-1, keepdims=True) for hm in halves]
        r = lax.broadcasted_iota(jnp.int32, (t, t), 0)
        c = lax.broadcasted_iota(jnp.int32, (t, t), 1)
        u_after = (r > c).astype(BF16)
        u_from = (r >= c).astype(BF16)

        def tile(n, carry, strict):
            runs, dq = carry
            off = pl.multiple_of((i - n) * t, t)
            k = k_ref[pl.ds(off, t), :]
            v = v_ref[pl.ds(off, t), :]
            dk = jnp.zeros((t, LANES), F32)
            dv = jnp.zeros((t, LANES), F32)
            new_runs = []
            for p in range(2):
                run_keep, run_e = runs[p]
                log_keep, log_beta = _sb_tile_terms(qh[p], k, strict)
                tail = run_keep + _split_dot(log_keep, u_after)
                w = _masked(strict, jnp.exp(log_beta + tail))
                e = _dot(doh[p], v, NT) * w
                rest = run_e + _split_dot(e, u_from)
                beta = jnp.exp(log_beta)
                dz = _masked(strict, e - beta * (e + delta[p] - rest)) * SCALE
                dzb = dz.astype(BF16)
                dv = dv + _dot(w.astype(BF16), doh[p], TN)
                dk = dk + _dot(dzb, qh[p], TN)
                dq = dq + _dot(dzb, jnp.where(halves[p], k, jnp.zeros_like(k)), NN)
                new_runs.append((tail[:, 0:1] + log_keep[:, 0:1], rest[:, 0:1]))
            dv_ref[pl.ds(off, t), :] += dv
            dk_ref[pl.ds(off, t), :] += dk
            return tuple(new_runs), dq

        zero = jnp.zeros((t, 1), F32)
        carry = tile(0, (((zero, zero), (zero, zero)), jnp.zeros((t, LANES), F32)), c < r)
        _, dq = lax.fori_loop(1, i + 1, lambda n, cr: tile(n, cr, None), carry)
        dq_ref[...] = dq.astype(BF16)

    blk = pl.BlockSpec((t, LANES), lambda j, i: (i, j))
    whole = pl.BlockSpec((s, LANES), lambda j, i: (0, j))
    return pl.pallas_call(
        body, name="sb_bwd_x%d" % ns, grid=(npair, nq),
        in_specs=[blk, pl.BlockSpec((s, LANES), lambda j, i: (0, npair + j)),
                  pl.BlockSpec((s, LANES), lambda j, i: (0, 2 * npair + j)), blk, blk] + [HBM] * ns,
        out_specs=[blk, whole, whole] + [HBM] * ns,
        out_shape=[jax.ShapeDtypeStruct((s, SB_WIDTH), BF16), jax.ShapeDtypeStruct((s, SB_WIDTH), F32),
                   jax.ShapeDtypeStruct((s, SB_WIDTH), F32)] + [jax.ShapeDtypeStruct(a.shape, a.dtype) for a in send],
        scratch_shapes=_exchange_scratch(ns) if ns else [],
        compiler_params=_cparams(("arbitrary", "arbitrary")),
    )(sbqkv, sbqkv, sbqkv, osbx, dosb, *send)


def _swa_bwd(qn, kn, v, sinks, do):
    s = qn.shape[0]
    t = min(512, s)
    nb = t // SWA_BLOCK
    nheads = SWA_Q_WIDTH // HEAD_DIM

    def body(sink_ref, q_ref, kc_ref, kp_ref, vc_ref, vp_ref, do_ref, dq_ref, dk_ref, dv_ref, dsink_ref):
        i = pl.program_id(0)

        @pl.when(i == 0)
        def _():
            dk_ref[...] = jnp.zeros_like(dk_ref)
            dv_ref[...] = jnp.zeros_like(dv_ref)

        left = _lane((1, LANES)) < HEAD_DIM
        halves = (left, jnp.logical_not(left))
        dsink = [jnp.zeros((1, 1), F32) for _ in range(nheads)]
        for b in range(nb):
            kvar = _swa_variants(_swa_band(kc_ref, kp_ref, b), left)
            vvar = _swa_variants(_swa_band(vc_ref, vp_ref, b), left)
            rows = slice(b * SWA_BLOCK, (b + 1) * SWA_BLOCK)
            dk_acc = [jnp.zeros((2 * SWA_BLOCK, LANES), F32) for _ in range(2)]
            dv_acc = [jnp.zeros((2 * SWA_BLOCK, LANES), F32) for _ in range(2)]
            for hb in range(nheads // 2):
                q2 = q_ref[rows, hb * LANES:(hb + 1) * LANES]
                do2 = do_ref[rows, hb * LANES:(hb + 1) * LANES]
                dq = jnp.zeros((SWA_BLOCK, LANES), F32)
                for p in range(2):
                    h = 2 * hb + p
                    g = h // (nheads // 2)
                    qh = jnp.where(halves[p], q2, jnp.zeros_like(q2))
                    doh = jnp.where(halves[p], do2, 0.0).astype(BF16)
                    probs, psink = _swa_probs(qh, kvar[g][p], sink_ref[h], i * nb + b)
                    dp = _dot(doh, vvar[g][p], NT)
                    delta = jnp.sum(probs * dp, axis=-1, keepdims=True)
                    ds = (probs * (dp - delta) * SCALE).astype(BF16)
                    dsink[h] = dsink[h] - jnp.sum(psink * delta, axis=0, keepdims=True)
                    dq = dq + _dot(ds, kvar[g][p], NN)
                    which = 0 if p == g else 1
                    dk_acc[which] = dk_acc[which] + _dot(ds, qh, TN)
                    dv_acc[which] = dv_acc[which] + _dot(probs.astype(BF16), doh, TN)
                dq_ref[rows, hb * LANES:(hb + 1) * LANES] = dq
            dkb = dk_acc[0] + pltpu.roll(dk_acc[1], HEAD_DIM, 1)
            dvb = dv_acc[0] + pltpu.roll(dv_acc[1], HEAD_DIM, 1)
            start = pl.multiple_of((i * nb + b) * SWA_BLOCK, SWA_BLOCK)
            dk_ref[pl.ds(start, SWA_BLOCK), :] += dkb[SWA_BLOCK:]
            dv_ref[pl.ds(start, SWA_BLOCK), :] += dvb[SWA_BLOCK:]

            @pl.when(i * nb + b > 0)
            def _(dkb=dkb, dvb=dvb, start=start):
                before = pl.multiple_of(jnp.maximum(start - SWA_BLOCK, 0), SWA_BLOCK)
                dk_ref[pl.ds(before, SWA_BLOCK), :] += dkb[:SWA_BLOCK]
                dv_ref[pl.ds(before, SWA_BLOCK), :] += dvb[:SWA_BLOCK]

        for h in range(nheads):
            dsink_ref[0, h:h + 1, :] = jnp.broadcast_to(dsink[h], (1, LANES))

    cur, prev = _swa_specs(s, t)
    whole = pl.BlockSpec((s, LANES), lambda i: (0, 0))
    return pl.pallas_call(
        body, name="swa_bwd", grid=(s // t,),
        in_specs=[pl.BlockSpec(memory_space=pltpu.SMEM), _rows(t, SWA_Q_WIDTH), cur, prev, cur, prev,
                  _rows(t, SWA_Q_WIDTH)],
        out_specs=[_rows(t, SWA_Q_WIDTH), whole, whole, pl.BlockSpec((1, 8, LANES), lambda i: (i, 0, 0))],
        out_shape=[jax.ShapeDtypeStruct((s, SWA_Q_WIDTH), F32), jax.ShapeDtypeStruct((s, SWA_KV_WIDTH), F32),
                   jax.ShapeDtypeStruct((s, SWA_KV_WIDTH), F32), jax.ShapeDtypeStruct((s // t, 8, LANES), F32)],
        compiler_params=_cparams(("arbitrary",)),
    )(sinks, qn, kn, kn, v, v, do)


def _swa_post(raw, dqn, dkn, qg, kg, cos, sin):
    s = raw.shape[0]
    tm = min(512, s)
    nq = SWA_Q_WIDTH // LANES

    def body(raw_ref, dq_ref, dk_ref, qg_ref, kg_ref, cos_ref, sin_ref, out_ref, dg_ref):
        @pl.when(pl.program_id(0) == 0)
        def _():
            dg_ref[...] = jnp.zeros_like(dg_ref)

        cs, sn = cos_ref[...], sin_ref[...]
        dgq = jnp.zeros((1, LANES), F32)
        for b in range(nq):
            cols = slice(b * LANES, (b + 1) * LANES)
            dp, dg = _norm_rope_bwd(raw_ref[:, cols], dq_ref[:, cols], qg_ref[...], cs, sn)
            out_ref[:, cols] = dp.astype(BF16)
            dgq = dgq + dg
        cols = slice(SWA_Q_WIDTH, SWA_Q_WIDTH + LANES)
        dp, dgk = _norm_rope_bwd(raw_ref[:, cols], dk_ref[...], kg_ref[...], cs, sn)
        out_ref[:, cols] = dp.astype(BF16)
        dg_ref[0:1, :] += dgq
        dg_ref[1:2, :] += dgk

    return pl.pallas_call(
        body, name="swa_post", grid=(s // tm,),
        in_specs=[_rows(tm, 640), _rows(tm, SWA_Q_WIDTH), _rows(tm, LANES), _resident((1, LANES)),
                  _resident((1, LANES)), _rows(tm, LANES), _rows(tm, LANES)],
        out_specs=[_rows(tm, 640), pl.BlockSpec((8, LANES), lambda i: (0, 0))],
        out_shape=[jax.ShapeDtypeStruct((s, 640), BF16), jax.ShapeDtypeStruct((8, LANES), F32)],
        compiler_params=_cparams(("arbitrary",)),
    )(raw, dqn, dkn, qg, kg, cos, sin)


def _inproj_bwd(dsbq, dsbk, dsbv, dswqk, dswv, dgl, x, dx1, g, w):
    s = x.shape[0]
    tm = min(256, s)

    def body(dsbq_ref, dsbk_ref, dsbv_ref, dswqk_ref, dswv_ref, dgl_ref, x_ref, dx1_ref, g_ref, w_ref,
             dx_ref, dproj_ref, dg_ref):
        @pl.when(pl.program_id(0) == 0)
        def _():
            dg_ref[...] = jnp.zeros_like(dg_ref)

        dproj_ref[:, 0:512] = dsbq_ref[...]
        dproj_ref[:, 512:1024] = dsbk_ref[...].astype(BF16)
        dproj_ref[:, 1024:1536] = dsbv_ref[...].astype(BF16)
        dproj_ref[:, 1536:2176] = dswqk_ref[...]
        dproj_ref[:, 2176:2304] = dswv_ref[...].astype(BF16)
        dproj_ref[:, GATE_OFF:IN_WIDTH] = dgl_ref[...]
        dh = jnp.zeros((tm, D_MODEL), F32)
        for a in range(0, IN_WIDTH, 512):
            b = min(a + 512, IN_WIDTH)
            dh = dh + _dot(dproj_ref[:, a:b], w_ref[:, a:b], NT)
        gam = g_ref[...]
        xn, rstd = _rms(x_ref[...])
        dg_ref[...] += jnp.sum(dh * xn, axis=0, keepdims=True)
        dx_ref[...] = dx1_ref[...] + _rms_bwd(dh, xn, rstd, gam)

    return pl.pallas_call(
        body, name="inproj_bwd", grid=(s // tm,),
        in_specs=[_rows(tm, 512), _rows(tm, 512), _rows(tm, 512), _rows(tm, 640), _rows(tm, LANES),
                  _rows(tm, 2 * D_MODEL), _rows(tm, D_MODEL), _rows(tm, D_MODEL), _resident((1, D_MODEL)),
                  _resident((D_MODEL, IN_WIDTH))],
        out_specs=[_rows(tm, D_MODEL), _rows(tm, IN_WIDTH), pl.BlockSpec((1, D_MODEL), lambda i: (0, 0))],
        out_shape=[jax.ShapeDtypeStruct((s, D_MODEL), F32), jax.ShapeDtypeStruct((s, IN_WIDTH), BF16),
                   jax.ShapeDtypeStruct((1, D_MODEL), F32)],
        compiler_params=_cparams(("arbitrary",)),
    )(dsbq, dsbk, dsbv, dswqk, dswv, dgl, x, dx1, g, w)


def _gather_weights(shard):
    def body(src, out, ssem, rsem, lsem):
        once = pl.program_id(0) == 0
        _hosted_gather(src, out, ssem, rsem, lsem, once, once, once)

    return pl.pallas_call(
        body, name="gather_weights", grid=(1,), in_specs=[HBM], out_specs=HBM,
        out_shape=jax.ShapeDtypeStruct((N_CHIPS,) + shard.shape, shard.dtype),
        scratch_shapes=[pltpu.SemaphoreType.DMA((6,)), pltpu.SemaphoreType.DMA((6,)), pltpu.SemaphoreType.DMA],
        compiler_params=pltpu.CompilerParams(dimension_semantics=("arbitrary",), has_side_effects=True),
    )(shard)


def _exchange(send):
    ns = len(send)

    def body(*refs):
        once = pl.program_id(0) == 0
        _hosted_exchange(refs[:ns], refs[ns:2 * ns], *refs[2 * ns:], once, once)

    return pl.pallas_call(
        body, name="exchange_x%d" % ns, grid=(1,), in_specs=[HBM] * ns, out_specs=[HBM] * ns,
        out_shape=[jax.ShapeDtypeStruct(a.shape, a.dtype) for a in send],
        scratch_shapes=_exchange_scratch(ns),
        compiler_params=pltpu.CompilerParams(dimension_semantics=("arbitrary",), has_side_effects=True),
    )(*send)


def _pair_swap(arrs):
    n = len(arrs)

    def body(*refs):
        x, y, c, _ = _place()
        cps = [pltpu.make_async_remote_copy(refs[t], refs[n + t], refs[2 * n].at[t], refs[2 * n + 1].at[t],
                                            device_id=(x, y, 1 - c), device_id_type=MESH) for t in range(n)]
        for cp in cps:
            cp.start()
        for cp in cps:
            cp.wait()

    return pl.pallas_call(
        body, name="pair_swap", in_specs=[HBM] * n, out_specs=[HBM] * n,
        out_shape=[jax.ShapeDtypeStruct(a.shape, a.dtype) for a in arrs],
        scratch_shapes=[pltpu.SemaphoreType.DMA((n,)), pltpu.SemaphoreType.DMA((n,))],
        compiler_params=pltpu.CompilerParams(has_side_effects=True),
    )(*arrs)


def _allreduce_small(block):
    def body(src, out, buf, ssem, rsem):
        x, y, c, _ = _place()
        me = 4 * x + 2 * y + c
        buf[me] = src[...]
        cps = []
        for k in range(1, N_DEV):
            peer = (x ^ (k >> 2), y ^ ((k >> 1) & 1), c ^ (k & 1))
            cp = pltpu.make_async_remote_copy(src, buf.at[me], ssem.at[k - 1], rsem.at[k - 1], device_id=peer, device_id_type=MESH)
            cp.start()
            cps.append(cp)
        for k in range(1, N_DEV):
            got = buf.at[me ^ k]
            pltpu.make_async_remote_copy(got, got, ssem.at[k - 1], rsem.at[k - 1], device_id=(x, y, c), device_id_type=MESH).wait_recv()
        for cp in cps:
            cp.wait_send()
        tot = buf[0]
        for d in range(1, N_DEV):
            tot = tot + buf[d]
        out[...] = tot

    vm = pl.BlockSpec(memory_space=pltpu.VMEM)
    return pl.pallas_call(
        body, name="allreduce_small", in_specs=[vm], out_specs=vm,
        out_shape=jax.ShapeDtypeStruct(block.shape, F32),
        scratch_shapes=[pltpu.VMEM((N_DEV,) + block.shape, F32), pltpu.SemaphoreType.DMA((N_DEV - 1,)),
                        pltpu.SemaphoreType.DMA((N_DEV - 1,))],
        compiler_params=pltpu.CompilerParams(has_side_effects=True),
    )(block)


def _sum_chips(parts, name):
    nq, k, n = parts.shape
    tr = min(256, k)

    def body(p0, p1, p2, p3, o_ref):
        o_ref[...] = ((p0[0] + p1[0]) + p2[0]) + p3[0]

    spec = lambda q: pl.BlockSpec((1, tr, n), lambda i, q=q: (q, i, 0))
    return pl.pallas_call(
        body, name=name, grid=(k // tr,),
        in_specs=[spec(q) for q in range(nq)],
        out_specs=pl.BlockSpec((tr, n), lambda i: (i, 0)),
        out_shape=jax.ShapeDtypeStruct((k, n), F32),
        compiler_params=_cparams(("arbitrary",)),
    )(parts, parts, parts, parts)


def _adamw(w, g, g2, m, v, name):
    shape = w.shape
    cols = shape[-1]
    flat = lambda t: t.reshape(-1, cols)
    rows = flat(w).shape[0]
    tr = min(512, rows)
    pair = g2 is not None

    def body(*refs):
        if pair:
            w_ref, g_ref, g2_ref, m_ref, v_ref, go_ref, d_ref, nm_ref, nv_ref = refs
            gr = g_ref[...] + g2_ref[...]
        else:
            w_ref, g_ref, m_ref, v_ref, go_ref, d_ref, nm_ref, nv_ref = refs
            gr = g_ref[...]
        go_ref[...] = gr
        nm = ADAM_B1 * m_ref[...] + (1.0 - ADAM_B1) * gr
        nv = ADAM_B2 * v_ref[...] + (1.0 - ADAM_B2) * (gr * gr)
        m_hat = nm / (1.0 - ADAM_B1 ** ADAM_STEP)
        v_hat = nv / (1.0 - ADAM_B2 ** ADAM_STEP)
        d_ref[...] = -ADAM_LR * (m_hat / (jnp.sqrt(v_hat) + ADAM_EPS) + ADAM_WD * w_ref[...])
        nm_ref[...] = nm
        nv_ref[...] = nv

    spec = pl.BlockSpec((tr, cols), lambda i: (i, 0))
    ins = [w, g] + ([g2] if pair else []) + [m, v]
    outs = pl.pallas_call(
        body, name=name, grid=(rows // tr,),
        in_specs=[spec] * len(ins), out_specs=[spec] * 4,
        out_shape=[jax.ShapeDtypeStruct((rows, cols), F32)] * 4,
        compiler_params=_cparams(("arbitrary",)),
    )(*[flat(t) for t in ins])
    return [o.reshape(shape) for o in outs]


def _pack_layer(big, l):
    return jnp.concatenate([big[name][l].astype(BF16).reshape(rows, 1024) for name, rows in PACK_ROWS], axis=0)


def _whole_weights(gathered):
    out, at = {}, 0
    for name, rows in PACK_ROWS:
        t = gathered[:, at:at + rows, :].reshape((N_CHIPS,) + SHARD_SHAPES[name])
        at += rows
        if name in ROW_SHARDED:
            out[name] = t.reshape(N_CHIPS * t.shape[1], t.shape[2])
        elif name == "w_up":
            out[name] = t
        else:
            out[name] = jnp.moveaxis(t, 0, 1).reshape(t.shape[1], N_CHIPS * t.shape[2])
    return out


def _pack_small(d):
    flat = jnp.concatenate([d[n].reshape(-1) for n in SMALL_NAMES])
    return jnp.pad(flat, (0, SMALL_ROWS * LANES - flat.shape[0])).reshape(SMALL_ROWS, LANES)


def _unpack_small(block, like):
    flat, out, at = block.reshape(-1), {}, 0
    for n in SMALL_NAMES:
        size = like[n].size
        out[n] = flat[at:at + size].reshape(like[n].shape)
        at += size
    return out


def _rope_tables(s):
    inv_freq = 1.0 / (ROPE_THETA ** (jnp.arange(0, HEAD_DIM, 2, dtype=F32) / HEAD_DIM))
    ang = jnp.arange(s, dtype=F32)[:, None] * inv_freq[None, :]
    reps = LANES // (HEAD_DIM // 2)
    return jnp.tile(jnp.cos(ang), (1, reps)), jnp.tile(jnp.sin(ang), (1, reps))


def _forward_backward(x, target, big, small):
    s = x.shape[0]
    cos, sin = _rope_tables(s)
    two = lambda gvec: jnp.tile(gvec.reshape(1, HEAD_DIM), (1, 2))
    saved = []
    gathered = _gather_weights(_pack_layer(big, 0))
    for l in range(DEPTH):
        wts = _whole_weights(gathered)
        gm = small["mix_norm_g"][l].reshape(1, D_MODEL)
        gl = small["mlp_norm_g"][l].reshape(1, D_MODEL)
        qg, kg = two(small["q_norm_g"][l]), two(small["k_norm_g"][l])
        h, sbqkv, raw, qn, kn, v, gates = _inproj_fwd(x, gm, wts["w_in"], qg, kg, cos, sin)
        if l + 1 < DEPTH:
            osb, osbx, gathered = _sb_fwd(sbqkv, _pack_layer(big, l + 1))
        else:
            osb, osbx = _sb_fwd(sbqkv)
        oswa = _swa_fwd(qn, kn, v, small["sinks"][l])
        x1 = _merge_fwd(x, osb, oswa, gates, wts["w_branch_sb"], wts["w_branch_swa"], wts["w_out"])
        x2, u = _mlp_fwd(x1, gl, wts["w_up"], wts["w_down"])
        saved.append((x, h, sbqkv, raw, qn, kn, v, gates, osb, osbx, oswa, x1, u, gm, gl, qg, kg, wts))
        x = x2
    dx, loss = _loss_grad(x, target)

    got = {name: [None] * DEPTH for name in BIG}
    gsmall = {name: [None] * DEPTH for name in SMALL_NAMES}
    late = []
    for l in reversed(range(DEPTH)):
        x0, h, sbqkv, raw, qn, kn, v, gates, osb, osbx, oswa, x1, u, gm, gl, qg, kg, wts = saved[l]
        dx1, du, act, h2, dgl_mlp = _mlp_bwd(dx, x1, u, gl, wts["w_up"], wts["w_down"])
        dosb, doswa, dgate, merged, dysb, dyswa = _merge_bwd(
            dx1, osb, oswa, gates, wts["w_branch_sb"], wts["w_branch_swa"], wts["w_out"])
        ready = [("w_down", l, _wgrad(act, dx, "wgrad_down", shard_axis=0)),
                 ("w_up", l, _wgrad(h2, du, "wgrad_up", shard_axis=1)),
                 ("w_out", l, _wgrad(merged, dx1, "wgrad_out", shard_axis=0)),
                 ("w_branch_sb", l, _wgrad(osb, dysb, "wgrad_bsb", shard_axis=1)),
                 ("w_branch_swa", l, _wgrad(oswa, dyswa, "wgrad_bswa", shard_axis=1))] + late
        dsbq, dsbk, dsbv, *landed = _sb_bwd(sbqkv, osbx, dosb, [a for _, _, a in ready])
        for (name, layer, _), arr in zip(ready, landed):
            got[name][layer] = arr
        dqn, dkn, dswv, dsink = _swa_bwd(qn, kn, v, small["sinks"][l], doswa)
        dswqk, dgqk = _swa_post(raw, dqn, dkn, qg, kg, cos, sin)
        dx, dproj, dg_mix = _inproj_bwd(dsbq, dsbk, dsbv, dswqk, dswv, dgate, x0, dx1, gm, wts["w_in"])
        dwin = _wgrad(h, dproj, "wgrad_in")
        late = [("w_in", l, dwin.reshape(D_MODEL, N_CHIPS, IN_WIDTH // N_CHIPS).transpose(1, 0, 2))]
        gsmall["mix_norm_g"][l] = dg_mix[0]
        gsmall["mlp_norm_g"][l] = dgl_mlp[0]
        gsmall["q_norm_g"][l] = dgqk[0, :HEAD_DIM] + dgqk[0, HEAD_DIM:]
        gsmall["k_norm_g"][l] = dgqk[1, :HEAD_DIM] + dgqk[1, HEAD_DIM:]
        gsmall["sinks"][l] = jnp.sum(dsink[:, :, 0], axis=0)
    (got["w_in"][0],) = _exchange([late[0][2]])
    gsmall = {k: jnp.stack(vs) for k, vs in gsmall.items()}
    return loss, dx, got, gsmall


def kernel(x, mix_norm_g, w_in, q_norm_g, k_norm_g, sinks, w_branch_sb, w_branch_swa, w_out, mlp_norm_g, w_up, w_down, loss_target, m_mix_norm_g, m_w_in, m_q_norm_g, m_k_norm_g, m_sinks, m_w_branch_sb, m_w_branch_swa, m_w_out, m_mlp_norm_g, m_w_up, m_w_down, v_mix_norm_g, v_w_in, v_q_norm_g, v_k_norm_g, v_sinks, v_w_branch_sb, v_w_branch_swa, v_w_out, v_mlp_norm_g, v_w_up, v_w_down):
    big = dict(w_in=w_in, w_branch_sb=w_branch_sb, w_branch_swa=w_branch_swa, w_out=w_out, w_up=w_up, w_down=w_down)
    big_m = dict(w_in=m_w_in, w_branch_sb=m_w_branch_sb, w_branch_swa=m_w_branch_swa, w_out=m_w_out, w_up=m_w_up, w_down=m_w_down)
    big_v = dict(w_in=v_w_in, w_branch_sb=v_w_branch_sb, w_branch_swa=v_w_branch_swa, w_out=v_w_out, w_up=v_w_up, w_down=v_w_down)
    small = dict(mix_norm_g=mix_norm_g, q_norm_g=q_norm_g, k_norm_g=k_norm_g, sinks=sinks, mlp_norm_g=mlp_norm_g)
    small_m = dict(mix_norm_g=m_mix_norm_g, q_norm_g=m_q_norm_g, k_norm_g=m_k_norm_g, sinks=m_sinks, mlp_norm_g=m_mlp_norm_g)
    small_v = dict(mix_norm_g=v_mix_norm_g, q_norm_g=v_q_norm_g, k_norm_g=v_k_norm_g, sinks=v_sinks, mlp_norm_g=v_mlp_norm_g)

    loss_part, grad_x, got, gsmall = _forward_backward(x[0], loss_target[0], big, small)
    loss = lax.psum(loss_part, ("x", "y", "c"))

    mine = [jnp.stack([_sum_chips(got[name][l], "sum_" + name) for l in range(DEPTH)]) for name in BIG]
    theirs = _pair_swap(mine)
    upd = {name: _adamw(big[name], mine[i], theirs[i], big_m[name], big_v[name], "adamw_" + name)
           for i, name in enumerate(BIG)}
    g_small = _allreduce_small(_pack_small(gsmall))
    sm = _adamw(_pack_small(small), g_small, None, _pack_small(small_m), _pack_small(small_v), "adamw_small")
    upd_small = [_unpack_small(t, small) for t in sm]

    names = ("mix_norm_g", "w_in", "q_norm_g", "k_norm_g", "sinks", "w_branch_sb", "w_branch_swa", "w_out",
             "mlp_norm_g", "w_up", "w_down")
    pick = lambda n, i: upd[n][i] if n in upd else upd_small[i][n]
    return (loss, grad_x[None], *[pick(n, 0) for n in names], *[pick(n, 1) for n in names],
            *[pick(n, 2) for n in names], *[pick(n, 3) for n in names])
```

```python
import jax
import jax.numpy as jnp
from jax import lax
from jax.experimental import pallas as pl
from jax.experimental.pallas import tpu as pltpu

F32 = jnp.float32
BF16 = jnp.bfloat16

D_MODEL = 1024
DEPTH = 4
HEAD_DIM = 64
SB_WIDTH = 512
SWA_Q_WIDTH = 512
SWA_KV_WIDTH = 128
D_FF = 4096
IN_WIDTH = 4352
GATE_OFF = 2304
ROPE_THETA = 10000.0
NORM_EPS = 1e-6
SCALE = HEAD_DIM ** -0.5
N_CHIPS = 4
N_DEV = 8

ADAM_LR = 0.001
ADAM_B1 = 0.9
ADAM_B2 = 0.999
ADAM_EPS = 1e-08
ADAM_WD = 0.01
ADAM_STEP = 10

LANES = 128
SB_TILE = 256
SB_STEP_WIDTH = 256
SWA_BLOCK = 128
VMEM_LIMIT = 56 << 20

NN = (((1,), (0,)), ((), ()))
NT = (((1,), (1,)), ((), ()))
TN = (((0,), (0,)), ((), ()))
MESH = pl.DeviceIdType.MESH
HBM = pl.BlockSpec(memory_space=pl.ANY)

BIG = ("w_in", "w_branch_sb", "w_branch_swa", "w_out", "w_up", "w_down")
SMALL_NAMES = ("mix_norm_g", "q_norm_g", "k_norm_g", "sinks", "mlp_norm_g")
PACK_ROWS = (("w_in", 1088), ("w_branch_sb", 128), ("w_branch_swa", 128), ("w_out", 256), ("w_up", 1024), ("w_down", 1024))
SHARD_SHAPES = {"w_in": (1024, 1088), "w_branch_sb": (512, 256), "w_branch_swa": (512, 256), "w_out": (256, 1024),
                "w_up": (1024, 1024), "w_down": (1024, 1024)}
ROW_SHARDED = ("w_out", "w_down")
SMALL_ROWS = 72


def _dot(a, b, dims):
    return lax.dot_general(a, b, dims, preferred_element_type=F32)


def _cparams(sem):
    return pltpu.CompilerParams(dimension_semantics=sem, vmem_limit_bytes=VMEM_LIMIT)


def _resident(shape):
    nd = len(shape)
    return pl.BlockSpec(shape, lambda *_: (0,) * nd, pipeline_mode=pl.Buffered(1))


def _rows(tm, width):
    return pl.BlockSpec((tm, width), lambda i: (i, 0))


def _rms(xf):
    rstd = lax.rsqrt(jnp.mean(xf * xf, axis=-1, keepdims=True) + NORM_EPS)
    return xf * rstd, rstd


def _rms_bwd(dh, xn, rstd, g):
    dxn = dh * g
    return rstd * (dxn - xn * jnp.mean(dxn * xn, axis=-1, keepdims=True))


def _lane(shape):
    return lax.broadcasted_iota(jnp.int32, shape, len(shape) - 1)


def _head_mean(v, left):
    sl = jnp.sum(jnp.where(left, v, 0.0), axis=-1, keepdims=True)
    sr = jnp.sum(jnp.where(left, 0.0, v), axis=-1, keepdims=True)
    return jnp.where(left, sl, sr) * (1.0 / HEAD_DIM)


def _rope(y, cs, sn, first):
    up = pltpu.roll(y, 96, 1)
    dn = pltpu.roll(y, 32, 1)
    return y * cs + jnp.where(first, -up, dn) * sn


def _rope_t(d, cs, sn, first):
    t = d * jnp.where(first, -sn, sn)
    return d * cs + jnp.where(first, pltpu.roll(t, 96, 1), pltpu.roll(t, 32, 1))


def _norm_rope(p, g, cs, sn):
    lane = _lane((1, LANES))
    left = lane < HEAD_DIM
    first = (lane % HEAD_DIM) < (HEAD_DIM // 2)
    yn = p * lax.rsqrt(_head_mean(p * p, left) + NORM_EPS)
    return _rope(yn * g, cs, sn, first)


def _norm_rope_bwd(p, dout, g, cs, sn):
    lane = _lane((1, LANES))
    left = lane < HEAD_DIM
    first = (lane % HEAD_DIM) < (HEAD_DIM // 2)
    rstd = lax.rsqrt(_head_mean(p * p, left) + NORM_EPS)
    yn = p * rstd
    dyg = _rope_t(dout, cs, sn, first)
    dg = jnp.sum(dyg * yn, axis=0, keepdims=True)
    dyn = dyg * g
    dp = rstd * (dyn - yn * _head_mean(dyn * yn, left))
    return dp, dg


def _place():
    x, y, c = lax.axis_index("x"), lax.axis_index("y"), lax.axis_index("c")
    return x, y, c, [(1 - x, y), (x, 1 - y), (1 - x, 1 - y)]


def _hosted_gather(src, out, ssem, rsem, lsem, first, mid, last, slot=0):
    x, y, c, chips = _place()
    me = 2 * x + y
    sib = (x, y, 1 - c)
    r2 = src.shape[0] // 2
    base = 6 * slot

    def slab(chip, half):
        return out.at[chip, pl.ds(half * r2, r2)]

    def ici(j):
        cx, cy = chips[j]
        return pltpu.make_async_remote_copy(src.at[pl.ds(c * r2, r2)], slab(me, c), ssem.at[base + j], rsem.at[base + j],
                                            device_id=(cx, cy, c), device_id_type=MESH)

    def landed(j):
        got = slab(2 * chips[j][0] + chips[j][1], c)
        return pltpu.make_async_remote_copy(got, got, ssem.at[base + j], rsem.at[base + j], device_id=sib, device_id_type=MESH)

    def d2d(j, half):
        got = slab(2 * chips[j][0] + chips[j][1], half)
        return pltpu.make_async_remote_copy(got, got, ssem.at[base + 3 + j], rsem.at[base + 3 + j], device_id=sib,
                                            device_id_type=MESH)

    local = pltpu.make_async_copy(src, out.at[me], lsem.at[slot])

    @pl.when(first)
    def _():
        local.start()
        for j in range(3):
            ici(j).start()

    @pl.when(mid)
    def _():
        for j in range(3):
            landed(j).wait_recv()
            d2d(j, c).start()

    @pl.when(last)
    def _():
        for j in range(3):
            d2d(j, 1 - c).wait_recv()
        for j in range(3):
            ici(j).wait_send()
            d2d(j, c).wait_send()
        local.wait()


def _hosted_exchange(srcs, outs, ssem, rsem, lsem, first, last):
    x, y, c, chips = _place()
    me = 2 * x + y

    def send(t, j):
        cx, cy = chips[j]
        return pltpu.make_async_remote_copy(srcs[t].at[2 * cx + cy], outs[t].at[me], ssem.at[3 * t + j], rsem.at[3 * t + j],
                                            device_id=(cx, cy, c), device_id_type=MESH)

    def landed(t, j):
        cx, cy = chips[j]
        got = outs[t].at[2 * cx + cy]
        return pltpu.make_async_remote_copy(got, got, ssem.at[3 * t + j], rsem.at[3 * t + j],
                                            device_id=(cx, cy, c), device_id_type=MESH)

    def local(t):
        return pltpu.make_async_copy(srcs[t].at[me], outs[t].at[me], lsem.at[t])

    @pl.when(first)
    def _():
        for t in range(len(srcs)):
            local(t).start()
            for j in range(3):
                send(t, j).start()

    @pl.when(last)
    def _():
        for t in range(len(srcs)):
            for j in range(3):
                landed(t, j).wait_recv()
        for t in range(len(srcs)):
            for j in range(3):
                send(t, j).wait_send()
            local(t).wait()


def _gather_scratch(n):
    return [pltpu.SemaphoreType.DMA((6 * n,)), pltpu.SemaphoreType.DMA((6 * n,)), pltpu.SemaphoreType.DMA((n,))]


def _exchange_scratch(n):
    return [pltpu.SemaphoreType.DMA((3 * n,)), pltpu.SemaphoreType.DMA((3 * n,)), pltpu.SemaphoreType.DMA((n,))]


def _inproj_fwd(x, g, w, qg, kg, cos, sin):
    s = x.shape[0]
    tm = min(512, s)

    def body(x_ref, g_ref, w_ref, qg_ref, kg_ref, cos_ref, sin_ref,
             h_ref, sb_ref, raw_ref, qn_ref, kn_ref, v_ref, gate_ref):
        xn, _ = _rms(x_ref[...])
        h = (xn * g_ref[...]).astype(BF16)
        h_ref[...] = h
        for a in range(0, 3 * SB_WIDTH, 512):
            sb_ref[:, a:a + 512] = _dot(h, w_ref[:, a:a + 512], NN).astype(BF16)
        cs, sn = cos_ref[...], sin_ref[...]
        q0 = 3 * SB_WIDTH
        pq = _dot(h, w_ref[:, q0:q0 + SWA_Q_WIDTH], NN)
        raw_ref[:, 0:SWA_Q_WIDTH] = pq
        for b in range(SWA_Q_WIDTH // LANES):
            blk = pq[:, b * LANES:(b + 1) * LANES]
            qn_ref[:, b * LANES:(b + 1) * LANES] = _norm_rope(blk, qg_ref[...], cs, sn).astype(BF16)
        k0 = q0 + SWA_Q_WIDTH
        pk = _dot(h, w_ref[:, k0:k0 + 2 * SWA_KV_WIDTH], NN)
        raw_ref[:, SWA_Q_WIDTH:SWA_Q_WIDTH + SWA_KV_WIDTH] = pk[:, :SWA_KV_WIDTH]
        kn_ref[...] = _norm_rope(pk[:, :SWA_KV_WIDTH], kg_ref[...], cs, sn).astype(BF16)
        v_ref[...] = pk[:, SWA_KV_WIDTH:].astype(BF16)
        for a in range(GATE_OFF, IN_WIDTH, 512):
            gate_ref[:, a - GATE_OFF:a - GATE_OFF + 512] = jax.nn.sigmoid(_dot(h, w_ref[:, a:a + 512], NN))

    return pl.pallas_call(
        body, name="inproj_fwd", grid=(s // tm,),
        in_specs=[_rows(tm, D_MODEL), _resident((1, D_MODEL)), _resident((D_MODEL, IN_WIDTH)),
                  _resident((1, LANES)), _resident((1, LANES)), _rows(tm, LANES), _rows(tm, LANES)],
        out_specs=[_rows(tm, D_MODEL), _rows(tm, 3 * SB_WIDTH), _rows(tm, 640), _rows(tm, SWA_Q_WIDTH),
                   _rows(tm, SWA_KV_WIDTH), _rows(tm, SWA_KV_WIDTH), _rows(tm, 2 * D_MODEL)],
        out_shape=[jax.ShapeDtypeStruct((s, D_MODEL), BF16), jax.ShapeDtypeStruct((s, 3 * SB_WIDTH), BF16),
                   jax.ShapeDtypeStruct((s, 640), F32), jax.ShapeDtypeStruct((s, SWA_Q_WIDTH), BF16),
                   jax.ShapeDtypeStruct((s, SWA_KV_WIDTH), BF16), jax.ShapeDtypeStruct((s, SWA_KV_WIDTH), BF16),
                   jax.ShapeDtypeStruct((s, 2 * D_MODEL), F32)],
        compiler_params=_cparams(("arbitrary",)),
    )(x, g, w, qg, kg, cos, sin)


def _sb_tile_terms(qh, k, strict):
    z = _dot(qh, k, NT)
    nz = -z
    soft = jnp.log(1.0 + jnp.exp(jnp.minimum(z, nz)))
    log_keep = jnp.minimum(nz, 0.0) - soft
    log_beta = z + log_keep
    if strict is not None:
        log_keep = jnp.where(strict, log_keep, 0.0)
    return log_keep, log_beta


def _masked(strict, val):
    return val if strict is None else jnp.where(strict, val, 0.0)


def _tri_twice(t, inclusive):
    r = lax.broadcasted_iota(jnp.int32, (2 * t, t), 0)
    c = lax.broadcasted_iota(jnp.int32, (2 * t, t), 1)
    r = jnp.where(r >= t, r - t, r)
    return (r >= c if inclusive else r > c).astype(BF16)


def _split_dot(a, u2):
    hi = a.astype(BF16)
    lo = (a - hi.astype(F32)).astype(BF16)
    return _dot(jnp.concatenate([hi, lo], axis=1), u2, NN)


def _sb_fwd(sbqkv, shards=()):
    s = sbqkv.shape[0]
    t = min(SB_TILE, s)
    nq = s // t
    ng = SB_WIDTH // SB_STEP_WIDTH
    heads = range(2 * SB_STEP_WIDTH // LANES)
    nsh = len(shards)

    def body(*refs):
        q_ref, k_ref, v_ref = refs[:3]
        o_ref = refs[3 + nsh]
        j, i = pl.program_id(0), pl.program_id(1)
        for n in range(nsh):
            _hosted_gather(refs[3 + n], refs[4 + nsh + n], *refs[4 + 2 * nsh:], (j == 0) & (i == 0),
                           (j == ng // 2) & (i == 0), (j == ng - 1) & (i == nq - 1), slot=n)
        left = _lane((1, LANES)) < HEAD_DIM
        halves = (left, jnp.logical_not(left))
        pair = lambda a, p: a[:, (p // 2) * LANES:(p // 2 + 1) * LANES]
        only = lambda a, p: jnp.where(halves[p % 2], pair(a, p), jnp.zeros((), a.dtype))
        q_all = q_ref[...] * jnp.asarray(SCALE, BF16)
        qh = [only(q_all, p) for p in heads]
        u_after = _tri_twice(t, False)

        def tile(n, carry, strict):
            runs, accs = carry
            off = pl.multiple_of((i - n) * t, t)
            k = k_ref[pl.ds(off, t), :]
            v = v_ref[pl.ds(off, t), :]
            terms = [_sb_tile_terms(qh[p], pair(k, p), strict) for p in heads]
            tails = [runs[p] + _split_dot(terms[p][0], u_after) for p in heads]
            ws = [_masked(strict, jnp.exp(terms[p][1] + tails[p])).astype(BF16) for p in heads]
            accs = list(accs)
            for p in heads:
                accs[p // 2] = accs[p // 2] + _dot(ws[p], only(v, p), NN)
            return tuple(tails[p][:, 0:1] + terms[p][0][:, 0:1] for p in heads), tuple(accs)

        zero = jnp.zeros((t, 1), F32)
        wide = jnp.zeros((t, LANES), F32)
        r = lax.broadcasted_iota(jnp.int32, (t, t), 0)
        c = lax.broadcasted_iota(jnp.int32, (t, t), 1)
        carry = tile(0, (tuple(zero for _ in heads), tuple(wide for _ in heads[::2])), c < r)
        _, accs = lax.fori_loop(1, i + 1, lambda n, cr: tile(n, cr, None), carry)
        for g, acc in enumerate(accs):
            o_ref[:, g * LANES:(g + 1) * LANES] = acc

    w = SB_STEP_WIDTH
    blk = pl.BlockSpec((t, w), lambda j, i: (i, j))
    in_specs = [blk, pl.BlockSpec((s, w), lambda j, i: (0, ng + j)), pl.BlockSpec((s, w), lambda j, i: (0, 2 * ng + j))]
    out_specs = [blk]
    out_shape = [jax.ShapeDtypeStruct((s, SB_WIDTH), F32)]
    in_specs += [HBM] * nsh
    out_specs += [HBM] * nsh
    out_shape += [jax.ShapeDtypeStruct((N_CHIPS,) + a.shape, a.dtype) for a in shards]
    return pl.pallas_call(
        body, name="sb_fwd_g%d" % nsh, grid=(ng, nq),
        in_specs=in_specs, out_specs=out_specs, out_shape=out_shape,
        scratch_shapes=_gather_scratch(nsh) if nsh else [],
        compiler_params=_cparams(("arbitrary", "arbitrary")),
    )(sbqkv, sbqkv, sbqkv, *shards)


def _swa_band(cur_ref, prev_ref, b):
    lo = prev_ref[...] if b == 0 else cur_ref[(b - 1) * SWA_BLOCK:b * SWA_BLOCK, :]
    return jnp.concatenate([lo, cur_ref[b * SWA_BLOCK:(b + 1) * SWA_BLOCK, :]], axis=0)


def _swa_variants(band, left):
    f = band.astype(F32)
    sw = pltpu.roll(f, HEAD_DIM, 1)
    halves = (left, jnp.logical_not(left))
    return [[jnp.where(halves[p], f if p == g else sw, 0.0).astype(BF16) for p in range(2)] for g in range(2)]


def _swa_probs(qh, kv, sink, blk):
    sc = _dot(qh, kv, NT) * SCALE
    ii = lax.broadcasted_iota(jnp.int32, (SWA_BLOCK, 2 * SWA_BLOCK), 0)
    jj = lax.broadcasted_iota(jnp.int32, (SWA_BLOCK, 2 * SWA_BLOCK), 1)
    rel = jj - SWA_BLOCK - ii
    valid = (rel <= 0) & (rel > -SWA_BLOCK) & (jj + blk * SWA_BLOCK >= SWA_BLOCK)
    sc = jnp.where(valid, sc, -1e30)
    m = jnp.maximum(jnp.max(sc, axis=-1, keepdims=True), sink)
    e = jnp.exp(sc - m)
    es = jnp.exp(sink - m)
    inv = 1.0 / (jnp.sum(e, axis=-1, keepdims=True) + es)
    return e * inv, es * inv


def _swa_specs(s, t):
    nb = t // SWA_BLOCK
    cur = pl.BlockSpec((t, LANES), lambda i: (i, 0))
    prev = pl.BlockSpec((SWA_BLOCK, LANES), lambda i: (jnp.maximum(i * nb - 1, 0), 0))
    return cur, prev


def _swa_fwd(qn, kn, v, sinks):
    s = qn.shape[0]
    t = min(512, s)
    nb = t // SWA_BLOCK
    nheads = SWA_Q_WIDTH // HEAD_DIM

    def body(sink_ref, q_ref, kc_ref, kp_ref, vc_ref, vp_ref, o_ref):
        i = pl.program_id(0)
        left = _lane((1, LANES)) < HEAD_DIM
        halves = (left, jnp.logical_not(left))
        for b in range(nb):
            kvar = _swa_variants(_swa_band(kc_ref, kp_ref, b), left)
            vvar = _swa_variants(_swa_band(vc_ref, vp_ref, b), left)
            rows = slice(b * SWA_BLOCK, (b + 1) * SWA_BLOCK)
            for hb in range(nheads // 2):
                q2 = q_ref[rows, hb * LANES:(hb + 1) * LANES]
                acc = jnp.zeros((SWA_BLOCK, LANES), F32)
                for p in range(2):
                    h = 2 * hb + p
                    g = h // (nheads // 2)
                    qh = jnp.where(halves[p], q2, jnp.zeros_like(q2))
                    probs, _ = _swa_probs(qh, kvar[g][p], sink_ref[h], i * nb + b)
                    acc = acc + _dot(probs.astype(BF16), vvar[g][p], NN)
                o_ref[rows, hb * LANES:(hb + 1) * LANES] = acc.astype(BF16)

    cur, prev = _swa_specs(s, t)
    return pl.pallas_call(
        body, name="swa_fwd", grid=(s // t,),
        in_specs=[pl.BlockSpec(memory_space=pltpu.SMEM), _rows(t, SWA_Q_WIDTH), cur, prev, cur, prev],
        out_specs=_rows(t, SWA_Q_WIDTH),
        out_shape=jax.ShapeDtypeStruct((s, SWA_Q_WIDTH), BF16),
        compiler_params=_cparams(("arbitrary",)),
    )(sinks, qn, kn, kn, v, v)


def _merge_fwd(x, osb, oswa, gates, wsb, wswa, wout):
    s = x.shape[0]
    tm = min(512, s)

    def body(x_ref, osb_ref, oswa_ref, gate_ref, wsb_ref, wswa_ref, wout_ref, x1_ref):
        ysb = _dot(osb_ref[...].astype(BF16), wsb_ref[...], NN)
        yswa = _dot(oswa_ref[...], wswa_ref[...], NN)
        merged = gate_ref[:, :D_MODEL] * ysb + gate_ref[:, D_MODEL:] * yswa
        x1_ref[...] = x_ref[...] + _dot(merged.astype(BF16), wout_ref[...], NN)

    return pl.pallas_call(
        body, name="merge_fwd", grid=(s // tm,),
        in_specs=[_rows(tm, D_MODEL), _rows(tm, SB_WIDTH), _rows(tm, SWA_Q_WIDTH), _rows(tm, 2 * D_MODEL),
                  _resident((SB_WIDTH, D_MODEL)), _resident((SWA_Q_WIDTH, D_MODEL)), _resident((D_MODEL, D_MODEL))],
        out_specs=_rows(tm, D_MODEL),
        out_shape=jax.ShapeDtypeStruct((s, D_MODEL), F32),
        compiler_params=_cparams(("arbitrary",)),
    )(x, osb, oswa, gates, wsb, wswa, wout)


def _mlp_fwd(x1, g, wup, wdown):
    s = x1.shape[0]
    tm = min(512, s)
    fc = D_FF // N_CHIPS

    def body(x_ref, g_ref, wup_ref, wdown_ref, x2_ref, u_ref):
        xf = x_ref[...]
        xn, _ = _rms(xf)
        h2 = (xn * g_ref[...]).astype(BF16)
        acc = xf
        for q in range(N_CHIPS):
            u = _dot(h2, wup_ref[q], NN)
            u_ref[:, q * fc:(q + 1) * fc] = u
            r = jnp.maximum(u, 0.0)
            acc = acc + _dot((r * r).astype(BF16), wdown_ref[q * fc:(q + 1) * fc, :], NN)
        x2_ref[...] = acc

    return pl.pallas_call(
        body, name="mlp_fwd", grid=(s // tm,),
        in_specs=[_rows(tm, D_MODEL), _resident((1, D_MODEL)), _resident((N_CHIPS, D_MODEL, fc)), _resident((D_FF, D_MODEL))],
        out_specs=[_rows(tm, D_MODEL), _rows(tm, D_FF)],
        out_shape=[jax.ShapeDtypeStruct((s, D_MODEL), F32), jax.ShapeDtypeStruct((s, D_FF), F32)],
        compiler_params=_cparams(("arbitrary",)),
    )(x1, g, wup, wdown)


def _loss_grad(y, target):
    s = y.shape[0]
    tm = min(512, s)

    def body(y_ref, t_ref, dy_ref, part_ref):
        err = y_ref[...] - t_ref[...]
        dy_ref[...] = err * (1.0 / D_MODEL)
        tot = jnp.sum(jnp.sum(err * err, axis=-1, keepdims=True), axis=0, keepdims=True)
        part_ref[...] = jnp.broadcast_to(tot.reshape(1, 1, 1), (1, 8, LANES))

    dy, part = pl.pallas_call(
        body, name="loss_grad", grid=(s // tm,),
        in_specs=[_rows(tm, D_MODEL), _rows(tm, D_MODEL)],
        out_specs=[_rows(tm, D_MODEL), pl.BlockSpec((1, 8, LANES), lambda i: (i, 0, 0))],
        out_shape=[jax.ShapeDtypeStruct((s, D_MODEL), F32), jax.ShapeDtypeStruct((s // tm, 8, LANES), F32)],
        compiler_params=_cparams(("arbitrary",)),
    )(y, target)
    return dy, (0.5 / D_MODEL) * jnp.sum(part[:, 0, 0])


def _mlp_bwd(dx2, x1, u, g, wup, wdown):
    s = x1.shape[0]
    tm = min(256, s)
    fc = D_FF // N_CHIPS

    def body(dx2_ref, x_ref, u_ref, g_ref, wup_ref, wdown_ref, dx1_ref, du_ref, a_ref, h2_ref, dg_ref):
        @pl.when(pl.program_id(0) == 0)
        def _():
            dg_ref[...] = jnp.zeros_like(dg_ref)

        gam = g_ref[...]
        xn, rstd = _rms(x_ref[...])
        h2_ref[...] = (xn * gam).astype(BF16)
        dxf = dx2_ref[...]
        dxb = dxf.astype(BF16)
        dh2 = jnp.zeros((tm, D_MODEL), F32)
        for q in range(N_CHIPS):
            cols = slice(q * fc, (q + 1) * fc)
            da = _dot(dxb, wdown_ref[cols, :], NT)
            r = jnp.maximum(u_ref[:, cols], 0.0)
            a_ref[:, cols] = (r * r).astype(BF16)
            du = (da * (2.0 * r)).astype(BF16)
            du_ref[:, cols] = du
            dh2 = dh2 + _dot(du, wup_ref[q], NT)
        dg_ref[...] += jnp.sum(dh2 * xn, axis=0, keepdims=True)
        dx1_ref[...] = dxf + _rms_bwd(dh2, xn, rstd, gam)

    return pl.pallas_call(
        body, name="mlp_bwd", grid=(s // tm,),
        in_specs=[_rows(tm, D_MODEL), _rows(tm, D_MODEL), _rows(tm, D_FF), _resident((1, D_MODEL)),
                  _resident((N_CHIPS, D_MODEL, fc)), _resident((D_FF, D_MODEL))],
        out_specs=[_rows(tm, D_MODEL), _rows(tm, D_FF), _rows(tm, D_FF), _rows(tm, D_MODEL),
                   pl.BlockSpec((1, D_MODEL), lambda i: (0, 0))],
        out_shape=[jax.ShapeDtypeStruct((s, D_MODEL), F32), jax.ShapeDtypeStruct((s, D_FF), BF16),
                   jax.ShapeDtypeStruct((s, D_FF), BF16), jax.ShapeDtypeStruct((s, D_MODEL), BF16),
                   jax.ShapeDtypeStruct((1, D_MODEL), F32)],
        compiler_params=_cparams(("arbitrary",)),
    )(dx2, x1, u, g, wup, wdown)


def _wgrad(a, b, name, shard_axis=None):
    s, m = a.shape
    n = b.shape[1]
    tm = min(512, m if shard_axis != 0 else m // N_CHIPS)
    tn = min(512, n if shard_axis != 1 else n // N_CHIPS)
    if shard_axis is None and n % tn:
        tn = 256

    def body(a_ref, b_ref, o_ref):
        res = _dot(a_ref[...].astype(BF16), b_ref[...].astype(BF16), TN)
        o_ref[...] = res.reshape(o_ref.shape)

    if shard_axis is None:
        out_shape, out_spec = (m, n), pl.BlockSpec((tm, tn), lambda i, j: (i, j))
    elif shard_axis == 0:
        per = m // N_CHIPS // tm
        out_shape, out_spec = (N_CHIPS, m // N_CHIPS, n), pl.BlockSpec((1, tm, tn), lambda i, j: (i // per, i % per, j))
    else:
        per = n // N_CHIPS // tn
        out_shape, out_spec = (N_CHIPS, m, n // N_CHIPS), pl.BlockSpec((1, tm, tn), lambda i, j: (j // per, i, j % per))
    return pl.pallas_call(
        body, name=name, grid=(m // tm, n // tn),
        in_specs=[pl.BlockSpec((s, tm), lambda i, j: (0, i)), pl.BlockSpec((s, tn), lambda i, j: (0, j))],
        out_specs=out_spec,
        out_shape=jax.ShapeDtypeStruct(out_shape, F32),
        compiler_params=_cparams(("arbitrary", "arbitrary")),
    )(a, b)


def _merge_bwd(dx1, osb, oswa, gates, wsb, wswa, wout):
    s = dx1.shape[0]
    tm = min(512, s)

    def body(dx_ref, osb_ref, oswa_ref, gate_ref, wsb_ref, wswa_ref, wout_ref,
             dosb_ref, doswa_ref, dgl_ref, merged_ref, dysb_ref, dyswa_ref):
        dm = _dot(dx_ref[...].astype(BF16), wout_ref[...], NT)
        ysb = _dot(osb_ref[...].astype(BF16), wsb_ref[...], NN)
        yswa = _dot(oswa_ref[...], wswa_ref[...], NN)
        g0 = gate_ref[:, :D_MODEL]
        g1 = gate_ref[:, D_MODEL:]
        merged_ref[...] = (g0 * ysb + g1 * yswa).astype(BF16)
        dgl_ref[:, :D_MODEL] = (dm * ysb * (g0 * (1.0 - g0))).astype(BF16)
        dgl_ref[:, D_MODEL:] = (dm * yswa * (g1 * (1.0 - g1))).astype(BF16)
        dysb = (dm * g0).astype(BF16)
        dyswa = (dm * g1).astype(BF16)
        dysb_ref[...] = dysb
        dyswa_ref[...] = dyswa
        dosb_ref[...] = _dot(dysb, wsb_ref[...], NT)
        doswa_ref[...] = _dot(dyswa, wswa_ref[...], NT)

    return pl.pallas_call(
        body, name="merge_bwd", grid=(s // tm,),
        in_specs=[_rows(tm, D_MODEL), _rows(tm, SB_WIDTH), _rows(tm, SWA_Q_WIDTH), _rows(tm, 2 * D_MODEL),
                  _resident((SB_WIDTH, D_MODEL)), _resident((SWA_Q_WIDTH, D_MODEL)), _resident((D_MODEL, D_MODEL))],
        out_specs=[_rows(tm, SB_WIDTH), _rows(tm, SWA_Q_WIDTH), _rows(tm, 2 * D_MODEL), _rows(tm, D_MODEL),
                   _rows(tm, D_MODEL), _rows(tm, D_MODEL)],
        out_shape=[jax.ShapeDtypeStruct((s, SB_WIDTH), F32), jax.ShapeDtypeStruct((s, SWA_Q_WIDTH), F32),
                   jax.ShapeDtypeStruct((s, 2 * D_MODEL), BF16), jax.ShapeDtypeStruct((s, D_MODEL), BF16),
                   jax.ShapeDtypeStruct((s, D_MODEL), BF16), jax.ShapeDtypeStruct((s, D_MODEL), BF16)],
        compiler_params=_cparams(("arbitrary",)),
    )(dx1, osb, oswa, gates, wsb, wswa, wout)


def _sb_bwd(sbqkv, osb, dosb, send):
    s = sbqkv.shape[0]
    t = min(SB_TILE, s)
    nq = s // t
    ng = SB_WIDTH // SB_STEP_WIDTH
    heads = range(2 * SB_STEP_WIDTH // LANES)
    ns = len(send)

    def body(*refs):
        q_ref, k_ref, v_ref, o_ref, do_ref = refs[:5]
        srcs = refs[5:5 + ns]
        dq_ref, dk_ref, dv_ref = refs[5 + ns:8 + ns]
        outs = refs[8 + ns:8 + 2 * ns]
        j, i = pl.program_id(0), pl.program_id(1)
        if ns:
            ssem, rsem, lsem = refs[8 + 2 * ns:]
            _hosted_exchange(srcs, outs, ssem, rsem, lsem, (j == 0) & (i == 0), (j == ng - 1) & (i == nq - 1))

        @pl.when(i == 0)
        def _():
            dk_ref[...] = jnp.zeros_like(dk_ref)
            dv_ref[...] = jnp.zeros_like(dv_ref)

        left = _lane((1, LANES)) < HEAD_DIM
        halves = (left, jnp.logical_not(left))
        pair = lambda a, p: a[:, (p // 2) * LANES:(p // 2 + 1) * LANES]
        only = lambda a, p: jnp.where(halves[p % 2], pair(a, p), jnp.zeros((), a.dtype))
        scale = jnp.asarray(SCALE, BF16)
        q_all = q_ref[...] * scale
        do_all = do_ref[...].astype(BF16)
        prod = do_all.astype(F32) * o_ref[...]
        qh = [only(q_all, p) for p in heads]
        doh = [only(do_all, p) for p in heads]
        delta = [jnp.sum(only(prod, p), axis=-1, keepdims=True) for p in heads]
        u_after = _tri_twice(t, False)
        u_from = _tri_twice(t, True)

        def tile(n, carry, strict):
            runs, dqs = carry
            off = pl.multiple_of((i - n) * t, t)
            k = k_ref[pl.ds(off, t), :]
            v = v_ref[pl.ds(off, t), :]
            ks = k * scale
            terms = [_sb_tile_terms(qh[p], pair(k, p), strict) for p in heads]
            tails = [runs[p][0] + _split_dot(terms[p][0], u_after) for p in heads]
            dws = [_dot(doh[p], pair(v, p), NT) for p in heads]
            wbs = [_masked(strict, jnp.exp(terms[p][1] + tails[p])).astype(BF16) for p in heads]
            es = [dws[p] * wbs[p].astype(F32) for p in heads]
            rests = [runs[p][1] + _split_dot(es[p], u_from) for p in heads]
            dzs = [_masked(strict, es[p] - jnp.exp(terms[p][1]) * (es[p] + delta[p] - rests[p])).astype(BF16)
                   for p in heads]
            dqs = list(dqs)
            for g in range(len(heads) // 2):
                a, b = 2 * g, 2 * g + 1
                cols = slice(g * LANES, (g + 1) * LANES)
                dv_ref[pl.ds(off, t), cols] += _dot(wbs[a], doh[a], TN) + _dot(wbs[b], doh[b], TN)
                dk_ref[pl.ds(off, t), cols] += _dot(dzs[a], qh[a], TN) + _dot(dzs[b], qh[b], TN)
                dqs[g] = dqs[g] + _dot(dzs[a], only(ks, a), NN) + _dot(dzs[b], only(ks, b), NN)
            new_runs = tuple((tails[p][:, 0:1] + terms[p][0][:, 0:1], rests[p][:, 0:1]) for p in heads)
            return new_runs, tuple(dqs)

        zero = jnp.zeros((t, 1), F32)
        wide = jnp.zeros((t, LANES), F32)
        r = lax.broadcasted_iota(jnp.int32, (t, t), 0)
        c = lax.broadcasted_iota(jnp.int32, (t, t), 1)
        carry = tile(0, (tuple((zero, zero) for _ in heads), tuple(wide for _ in heads[::2])), c < r)
        _, dqs = lax.fori_loop(1, i + 1, lambda n, cr: tile(n, cr, None), carry)
        for g, dq in enumerate(dqs):
            dq_ref[:, g * LANES:(g + 1) * LANES] = dq.astype(BF16)

    w = SB_STEP_WIDTH
    blk = pl.BlockSpec((t, w), lambda j, i: (i, j))
    whole = pl.BlockSpec((s, w), lambda j, i: (0, j))
    return pl.pallas_call(
        body, name="sb_bwd_x%d" % ns, grid=(ng, nq),
        in_specs=[blk, pl.BlockSpec((s, w), lambda j, i: (0, ng + j)),
                  pl.BlockSpec((s, w), lambda j, i: (0, 2 * ng + j)), blk, blk] + [HBM] * ns,
        out_specs=[blk, whole, whole] + [HBM] * ns,
        out_shape=[jax.ShapeDtypeStruct((s, SB_WIDTH), BF16), jax.ShapeDtypeStruct((s, SB_WIDTH), F32),
                   jax.ShapeDtypeStruct((s, SB_WIDTH), F32)] + [jax.ShapeDtypeStruct(a.shape, a.dtype) for a in send],
        scratch_shapes=_exchange_scratch(ns) if ns else [],
        compiler_params=_cparams(("arbitrary", "arbitrary")),
    )(sbqkv, sbqkv, sbqkv, osb, dosb, *send)


def _swa_bwd(qn, kn, v, sinks, do):
    s = qn.shape[0]
    t = min(512, s)
    nb = t // SWA_BLOCK
    nheads = SWA_Q_WIDTH // HEAD_DIM

    def body(sink_ref, q_ref, kc_ref, kp_ref, vc_ref, vp_ref, do_ref, dq_ref, dk_ref, dv_ref, dsink_ref):
        i = pl.program_id(0)

        @pl.when(i == 0)
        def _():
            dk_ref[...] = jnp.zeros_like(dk_ref)
            dv_ref[...] = jnp.zeros_like(dv_ref)

        left = _lane((1, LANES)) < HEAD_DIM
        halves = (left, jnp.logical_not(left))
        dsink = [jnp.zeros((1, 1), F32) for _ in range(nheads)]
        for b in range(nb):
            kvar = _swa_variants(_swa_band(kc_ref, kp_ref, b), left)
            vvar = _swa_variants(_swa_band(vc_ref, vp_ref, b), left)
            rows = slice(b * SWA_BLOCK, (b + 1) * SWA_BLOCK)
            dk_acc = [jnp.zeros((2 * SWA_BLOCK, LANES), F32) for _ in range(2)]
            dv_acc = [jnp.zeros((2 * SWA_BLOCK, LANES), F32) for _ in range(2)]
            for hb in range(nheads // 2):
                q2 = q_ref[rows, hb * LANES:(hb + 1) * LANES]
                do2 = do_ref[rows, hb * LANES:(hb + 1) * LANES]
                dq = jnp.zeros((SWA_BLOCK, LANES), F32)
                for p in range(2):
                    h = 2 * hb + p
                    g = h // (nheads // 2)
                    qh = jnp.where(halves[p], q2, jnp.zeros_like(q2))
                    doh = jnp.where(halves[p], do2, 0.0).astype(BF16)
                    probs, psink = _swa_probs(qh, kvar[g][p], sink_ref[h], i * nb + b)
                    dp = _dot(doh, vvar[g][p], NT)
                    delta = jnp.sum(probs * dp, axis=-1, keepdims=True)
                    ds = (probs * (dp - delta) * SCALE).astype(BF16)
                    dsink[h] = dsink[h] - jnp.sum(psink * delta, axis=0, keepdims=True)
                    dq = dq + _dot(ds, kvar[g][p], NN)
                    which = 0 if p == g else 1
                    dk_acc[which] = dk_acc[which] + _dot(ds, qh, TN)
                    dv_acc[which] = dv_acc[which] + _dot(probs.astype(BF16), doh, TN)
                dq_ref[rows, hb * LANES:(hb + 1) * LANES] = dq
            dkb = dk_acc[0] + pltpu.roll(dk_acc[1], HEAD_DIM, 1)
            dvb = dv_acc[0] + pltpu.roll(dv_acc[1], HEAD_DIM, 1)
            start = pl.multiple_of((i * nb + b) * SWA_BLOCK, SWA_BLOCK)
            dk_ref[pl.ds(start, SWA_BLOCK), :] += dkb[SWA_BLOCK:]
            dv_ref[pl.ds(start, SWA_BLOCK), :] += dvb[SWA_BLOCK:]

            @pl.when(i * nb + b > 0)
            def _(dkb=dkb, dvb=dvb, start=start):
                before = pl.multiple_of(jnp.maximum(start - SWA_BLOCK, 0), SWA_BLOCK)
                dk_ref[pl.ds(before, SWA_BLOCK), :] += dkb[:SWA_BLOCK]
                dv_ref[pl.ds(before, SWA_BLOCK), :] += dvb[:SWA_BLOCK]

        for h in range(nheads):
            dsink_ref[0, h:h + 1, :] = jnp.broadcast_to(dsink[h], (1, LANES))

    cur, prev = _swa_specs(s, t)
    whole = pl.BlockSpec((s, LANES), lambda i: (0, 0))
    return pl.pallas_call(
        body, name="swa_bwd", grid=(s // t,),
        in_specs=[pl.BlockSpec(memory_space=pltpu.SMEM), _rows(t, SWA_Q_WIDTH), cur, prev, cur, prev,
                  _rows(t, SWA_Q_WIDTH)],
        out_specs=[_rows(t, SWA_Q_WIDTH), whole, whole, pl.BlockSpec((1, 8, LANES), lambda i: (i, 0, 0))],
        out_shape=[jax.ShapeDtypeStruct((s, SWA_Q_WIDTH), F32), jax.ShapeDtypeStruct((s, SWA_KV_WIDTH), F32),
                   jax.ShapeDtypeStruct((s, SWA_KV_WIDTH), F32), jax.ShapeDtypeStruct((s // t, 8, LANES), F32)],
        compiler_params=_cparams(("arbitrary",)),
    )(sinks, qn, kn, kn, v, v, do)


def _swa_post(raw, dqn, dkn, qg, kg, cos, sin):
    s = raw.shape[0]
    tm = min(512, s)
    nq = SWA_Q_WIDTH // LANES

    def body(raw_ref, dq_ref, dk_ref, qg_ref, kg_ref, cos_ref, sin_ref, out_ref, dg_ref):
        @pl.when(pl.program_id(0) == 0)
        def _():
            dg_ref[...] = jnp.zeros_like(dg_ref)

        cs, sn = cos_ref[...], sin_ref[...]
        dgq = jnp.zeros((1, LANES), F32)
        for b in range(nq):
            cols = slice(b * LANES, (b + 1) * LANES)
            dp, dg = _norm_rope_bwd(raw_ref[:, cols], dq_ref[:, cols], qg_ref[...], cs, sn)
            out_ref[:, cols] = dp.astype(BF16)
            dgq = dgq + dg
        cols = slice(SWA_Q_WIDTH, SWA_Q_WIDTH + LANES)
        dp, dgk = _norm_rope_bwd(raw_ref[:, cols], dk_ref[...], kg_ref[...], cs, sn)
        out_ref[:, cols] = dp.astype(BF16)
        dg_ref[0:1, :] += dgq
        dg_ref[1:2, :] += dgk

    return pl.pallas_call(
        body, name="swa_post", grid=(s // tm,),
        in_specs=[_rows(tm, 640), _rows(tm, SWA_Q_WIDTH), _rows(tm, LANES), _resident((1, LANES)),
                  _resident((1, LANES)), _rows(tm, LANES), _rows(tm, LANES)],
        out_specs=[_rows(tm, 640), pl.BlockSpec((8, LANES), lambda i: (0, 0))],
        out_shape=[jax.ShapeDtypeStruct((s, 640), BF16), jax.ShapeDtypeStruct((8, LANES), F32)],
        compiler_params=_cparams(("arbitrary",)),
    )(raw, dqn, dkn, qg, kg, cos, sin)


def _inproj_bwd(dsbq, dsbk, dsbv, dswqk, dswv, dgl, x, dx1, g, w):
    s = x.shape[0]
    tm = min(256, s)

    def body(dsbq_ref, dsbk_ref, dsbv_ref, dswqk_ref, dswv_ref, dgl_ref, x_ref, dx1_ref, g_ref, w_ref,
             dx_ref, dproj_ref, dg_ref):
        @pl.when(pl.program_id(0) == 0)
        def _():
            dg_ref[...] = jnp.zeros_like(dg_ref)

        dproj_ref[:, 0:512] = dsbq_ref[...]
        dproj_ref[:, 512:1024] = dsbk_ref[...].astype(BF16)
        dproj_ref[:, 1024:1536] = dsbv_ref[...].astype(BF16)
        dproj_ref[:, 1536:2176] = dswqk_ref[...]
        dproj_ref[:, 2176:2304] = dswv_ref[...].astype(BF16)
        dproj_ref[:, GATE_OFF:IN_WIDTH] = dgl_ref[...]
        dh = jnp.zeros((tm, D_MODEL), F32)
        for a in range(0, IN_WIDTH, 512):
            b = min(a + 512, IN_WIDTH)
            dh = dh + _dot(dproj_ref[:, a:b], w_ref[:, a:b], NT)
        gam = g_ref[...]
        xn, rstd = _rms(x_ref[...])
        dg_ref[...] += jnp.sum(dh * xn, axis=0, keepdims=True)
        dx_ref[...] = dx1_ref[...] + _rms_bwd(dh, xn, rstd, gam)

    return pl.pallas_call(
        body, name="inproj_bwd", grid=(s // tm,),
        in_specs=[_rows(tm, 512), _rows(tm, 512), _rows(tm, 512), _rows(tm, 640), _rows(tm, LANES),
                  _rows(tm, 2 * D_MODEL), _rows(tm, D_MODEL), _rows(tm, D_MODEL), _resident((1, D_MODEL)),
                  _resident((D_MODEL, IN_WIDTH))],
        out_specs=[_rows(tm, D_MODEL), _rows(tm, IN_WIDTH), pl.BlockSpec((1, D_MODEL), lambda i: (0, 0))],
        out_shape=[jax.ShapeDtypeStruct((s, D_MODEL), F32), jax.ShapeDtypeStruct((s, IN_WIDTH), BF16),
                   jax.ShapeDtypeStruct((1, D_MODEL), F32)],
        compiler_params=_cparams(("arbitrary",)),
    )(dsbq, dsbk, dsbv, dswqk, dswv, dgl, x, dx1, g, w)


def _gather_weights(shard):
    def body(src, out, ssem, rsem, lsem):
        once = pl.program_id(0) == 0
        _hosted_gather(src, out, ssem, rsem, lsem, once, once, once)

    return pl.pallas_call(
        body, name="gather_weights", grid=(1,), in_specs=[HBM], out_specs=HBM,
        out_shape=jax.ShapeDtypeStruct((N_CHIPS,) + shard.shape, shard.dtype),
        scratch_shapes=_gather_scratch(1),
        compiler_params=pltpu.CompilerParams(dimension_semantics=("arbitrary",), has_side_effects=True),
    )(shard)


def _exchange(send):
    ns = len(send)

    def body(*refs):
        once = pl.program_id(0) == 0
        _hosted_exchange(refs[:ns], refs[ns:2 * ns], *refs[2 * ns:], once, once)

    return pl.pallas_call(
        body, name="exchange_x%d" % ns, grid=(1,), in_specs=[HBM] * ns, out_specs=[HBM] * ns,
        out_shape=[jax.ShapeDtypeStruct(a.shape, a.dtype) for a in send],
        scratch_shapes=_exchange_scratch(ns),
        compiler_params=pltpu.CompilerParams(dimension_semantics=("arbitrary",), has_side_effects=True),
    )(*send)


def _pair_swap(arrs):
    n = len(arrs)

    def body(*refs):
        x, y, c, _ = _place()
        cps = [pltpu.make_async_remote_copy(refs[t], refs[n + t], refs[2 * n].at[t], refs[2 * n + 1].at[t],
                                            device_id=(x, y, 1 - c), device_id_type=MESH) for t in range(n)]
        for cp in cps:
            cp.start()
        for cp in cps:
            cp.wait()

    return pl.pallas_call(
        body, name="pair_swap", in_specs=[HBM] * n, out_specs=[HBM] * n,
        out_shape=[jax.ShapeDtypeStruct(a.shape, a.dtype) for a in arrs],
        scratch_shapes=[pltpu.SemaphoreType.DMA((n,)), pltpu.SemaphoreType.DMA((n,))],
        compiler_params=pltpu.CompilerParams(has_side_effects=True),
    )(*arrs)


def _allreduce_small(block):
    def body(src, out, buf, ssem, rsem):
        x, y, c, _ = _place()
        me = 4 * x + 2 * y + c
        buf[me] = src[...]
        cps = []
        for k in range(1, N_DEV):
            peer = (x ^ (k >> 2), y ^ ((k >> 1) & 1), c ^ (k & 1))
            cp = pltpu.make_async_remote_copy(src, buf.at[me], ssem.at[k - 1], rsem.at[k - 1], device_id=peer, device_id_type=MESH)
            cp.start()
            cps.append(cp)
        for k in range(1, N_DEV):
            got = buf.at[me ^ k]
            pltpu.make_async_remote_copy(got, got, ssem.at[k - 1], rsem.at[k - 1], device_id=(x, y, c), device_id_type=MESH).wait_recv()
        for cp in cps:
            cp.wait_send()
        tot = buf[0]
        for d in range(1, N_DEV):
            tot = tot + buf[d]
        out[...] = tot

    vm = pl.BlockSpec(memory_space=pltpu.VMEM)
    return pl.pallas_call(
        body, name="allreduce_small", in_specs=[vm], out_specs=vm,
        out_shape=jax.ShapeDtypeStruct(block.shape, F32),
        scratch_shapes=[pltpu.VMEM((N_DEV,) + block.shape, F32), pltpu.SemaphoreType.DMA((N_DEV - 1,)),
                        pltpu.SemaphoreType.DMA((N_DEV - 1,))],
        compiler_params=pltpu.CompilerParams(has_side_effects=True),
    )(block)


def _sum_chips(parts, name):
    nq, k, n = parts.shape
    tr = min(256, k)

    def body(p0, p1, p2, p3, o_ref):
        o_ref[...] = ((p0[0] + p1[0]) + p2[0]) + p3[0]

    spec = lambda q: pl.BlockSpec((1, tr, n), lambda i, q=q: (q, i, 0))
    return pl.pallas_call(
        body, name=name, grid=(k // tr,),
        in_specs=[spec(q) for q in range(nq)],
        out_specs=pl.BlockSpec((tr, n), lambda i: (i, 0)),
        out_shape=jax.ShapeDtypeStruct((k, n), F32),
        compiler_params=_cparams(("arbitrary",)),
    )(parts, parts, parts, parts)


def _adamw(w, g, g2, m, v, name):
    shape = w.shape
    cols = shape[-1]
    flat = lambda t: t.reshape(-1, cols)
    rows = flat(w).shape[0]
    tr = min(512, rows)
    pair = g2 is not None

    def body(*refs):
        if pair:
            w_ref, g_ref, g2_ref, m_ref, v_ref, go_ref, d_ref, nm_ref, nv_ref = refs
            gr = g_ref[...] + g2_ref[...]
        else:
            w_ref, g_ref, m_ref, v_ref, go_ref, d_ref, nm_ref, nv_ref = refs
            gr = g_ref[...]
        go_ref[...] = gr
        nm = ADAM_B1 * m_ref[...] + (1.0 - ADAM_B1) * gr
        nv = ADAM_B2 * v_ref[...] + (1.0 - ADAM_B2) * (gr * gr)
        m_hat = nm / (1.0 - ADAM_B1 ** ADAM_STEP)
        v_hat = nv / (1.0 - ADAM_B2 ** ADAM_STEP)
        d_ref[...] = -ADAM_LR * (m_hat / (jnp.sqrt(v_hat) + ADAM_EPS) + ADAM_WD * w_ref[...])
        nm_ref[...] = nm
        nv_ref[...] = nv

    spec = pl.BlockSpec((tr, cols), lambda i: (i, 0))
    ins = [w, g] + ([g2] if pair else []) + [m, v]
    outs = pl.pallas_call(
        body, name=name, grid=(rows // tr,),
        in_specs=[spec] * len(ins), out_specs=[spec] * 4,
        out_shape=[jax.ShapeDtypeStruct((rows, cols), F32)] * 4,
        compiler_params=_cparams(("arbitrary",)),
    )(*[flat(t) for t in ins])
    return [o.reshape(shape) for o in outs]


def _pack_layer(big, l, names=BIG):
    rows = dict(PACK_ROWS)
    return jnp.concatenate([big[name][l].astype(BF16).reshape(rows[name], 1024) for name in names], axis=0)


def _whole_weights(gathered, names=BIG):
    out, at = {}, 0
    for name in names:
        rows = dict(PACK_ROWS)[name]
        t = gathered[:, at:at + rows, :].reshape((N_CHIPS,) + SHARD_SHAPES[name])
        at += rows
        if name in ROW_SHARDED:
            out[name] = t.reshape(N_CHIPS * t.shape[1], t.shape[2])
        elif name == "w_up":
            out[name] = t
        else:
            out[name] = jnp.moveaxis(t, 0, 1).reshape(t.shape[1], N_CHIPS * t.shape[2])
    return out


def _pack_small(d):
    flat = jnp.concatenate([d[n].reshape(-1) for n in SMALL_NAMES])
    return jnp.pad(flat, (0, SMALL_ROWS * LANES - flat.shape[0])).reshape(SMALL_ROWS, LANES)


def _unpack_small(block, like):
    flat, out, at = block.reshape(-1), {}, 0
    for n in SMALL_NAMES:
        size = like[n].size
        out[n] = flat[at:at + size].reshape(like[n].shape)
        at += size
    return out


def _rope_tables(s):
    inv_freq = 1.0 / (ROPE_THETA ** (jnp.arange(0, HEAD_DIM, 2, dtype=F32) / HEAD_DIM))
    ang = jnp.arange(s, dtype=F32)[:, None] * inv_freq[None, :]
    reps = LANES // (HEAD_DIM // 2)
    return jnp.tile(jnp.cos(ang), (1, reps)), jnp.tile(jnp.sin(ang), (1, reps))


def _forward_backward(x, target, big, small):
    s = x.shape[0]
    cos, sin = _rope_tables(s)
    two = lambda gvec: jnp.tile(gvec.reshape(1, HEAD_DIM), (1, 2))
    saved = []
    wts = _whole_weights(_gather_weights(_pack_layer(big, 0, BIG[:1])), BIG[:1])
    for l in range(DEPTH):
        gm = small["mix_norm_g"][l].reshape(1, D_MODEL)
        gl = small["mlp_norm_g"][l].reshape(1, D_MODEL)
        qg, kg = two(small["q_norm_g"][l]), two(small["k_norm_g"][l])
        h, sbqkv, raw, qn, kn, v, gates = _inproj_fwd(x, gm, wts["w_in"], qg, kg, cos, sin)
        shards = ([_pack_layer(big, 0, BIG[1:])] if l == 0 else []) + ([_pack_layer(big, l + 1)] if l + 1 < DEPTH else [])
        osb, *landed = _sb_fwd(sbqkv, shards)
        if l == 0:
            wts = {**wts, **_whole_weights(landed.pop(0), BIG[1:])}
        nxt = _whole_weights(landed[0]) if landed else None
        oswa = _swa_fwd(qn, kn, v, small["sinks"][l])
        x1 = _merge_fwd(x, osb, oswa, gates, wts["w_branch_sb"], wts["w_branch_swa"], wts["w_out"])
        x2, u = _mlp_fwd(x1, gl, wts["w_up"], wts["w_down"])
        saved.append((x, h, sbqkv, raw, qn, kn, v, gates, osb, oswa, x1, u, gm, gl, qg, kg, wts))
        x, wts = x2, nxt
    dx, loss = _loss_grad(x, target)

    got = {name: [None] * DEPTH for name in BIG}
    gsmall = {name: [None] * DEPTH for name in SMALL_NAMES}
    late = []
    for l in reversed(range(DEPTH)):
        x0, h, sbqkv, raw, qn, kn, v, gates, osb, oswa, x1, u, gm, gl, qg, kg, wts = saved[l]
        dx1, du, act, h2, dgl_mlp = _mlp_bwd(dx, x1, u, gl, wts["w_up"], wts["w_down"])
        dosb, doswa, dgate, merged, dysb, dyswa = _merge_bwd(
            dx1, osb, oswa, gates, wts["w_branch_sb"], wts["w_branch_swa"], wts["w_out"])
        ready = [("w_down", l, _wgrad(act, dx, "wgrad_down", shard_axis=0)),
                 ("w_up", l, _wgrad(h2, du, "wgrad_up", shard_axis=1)),
                 ("w_out", l, _wgrad(merged, dx1, "wgrad_out", shard_axis=0)),
                 ("w_branch_sb", l, _wgrad(osb, dysb, "wgrad_bsb", shard_axis=1)),
                 ("w_branch_swa", l, _wgrad(oswa, dyswa, "wgrad_bswa", shard_axis=1))] + late
        dsbq, dsbk, dsbv, *landed = _sb_bwd(sbqkv, osb, dosb, [a for _, _, a in ready])
        for (name, layer, _), arr in zip(ready, landed):
            got[name][layer] = arr
        dqn, dkn, dswv, dsink = _swa_bwd(qn, kn, v, small["sinks"][l], doswa)
        dswqk, dgqk = _swa_post(raw, dqn, dkn, qg, kg, cos, sin)
        dx, dproj, dg_mix = _inproj_bwd(dsbq, dsbk, dsbv, dswqk, dswv, dgate, x0, dx1, gm, wts["w_in"])
        dwin = _wgrad(h, dproj, "wgrad_in")
        late = [("w_in", l, dwin.reshape(D_MODEL, N_CHIPS, IN_WIDTH // N_CHIPS).transpose(1, 0, 2))]
        gsmall["mix_norm_g"][l] = dg_mix[0]
        gsmall["mlp_norm_g"][l] = dgl_mlp[0]
        gsmall["q_norm_g"][l] = dgqk[0, :HEAD_DIM] + dgqk[0, HEAD_DIM:]
        gsmall["k_norm_g"][l] = dgqk[1, :HEAD_DIM] + dgqk[1, HEAD_DIM:]
        gsmall["sinks"][l] = jnp.sum(dsink[:, :, 0], axis=0)
    (got["w_in"][0],) = _exchange([late[0][2]])
    gsmall = {k: jnp.stack(vs) for k, vs in gsmall.items()}
    return loss, dx, got, gsmall


def kernel(x, mix_norm_g, w_in, q_norm_g, k_norm_g, sinks, w_branch_sb, w_branch_swa, w_out, mlp_norm_g, w_up, w_down, loss_target, m_mix_norm_g, m_w_in, m_q_norm_g, m_k_norm_g, m_sinks, m_w_branch_sb, m_w_branch_swa, m_w_out, m_mlp_norm_g, m_w_up, m_w_down, v_mix_norm_g, v_w_in, v_q_norm_g, v_k_norm_g, v_sinks, v_w_branch_sb, v_w_branch_swa, v_w_out, v_mlp_norm_g, v_w_up, v_w_down):
    big = dict(w_in=w_in, w_branch_sb=w_branch_sb, w_branch_swa=w_branch_swa, w_out=w_out, w_up=w_up, w_down=w_down)
    big_m = dict(w_in=m_w_in, w_branch_sb=m_w_branch_sb, w_branch_swa=m_w_branch_swa, w_out=m_w_out, w_up=m_w_up, w_down=m_w_down)
    big_v = dict(w_in=v_w_in, w_branch_sb=v_w_branch_sb, w_branch_swa=v_w_branch_swa, w_out=v_w_out, w_up=v_w_up, w_down=v_w_down)
    small = dict(mix_norm_g=mix_norm_g, q_norm_g=q_norm_g, k_norm_g=k_norm_g, sinks=sinks, mlp_norm_g=mlp_norm_g)
    small_m = dict(mix_norm_g=m_mix_norm_g, q_norm_g=m_q_norm_g, k_norm_g=m_k_norm_g, sinks=m_sinks, mlp_norm_g=m_mlp_norm_g)
    small_v = dict(mix_norm_g=v_mix_norm_g, q_norm_g=v_q_norm_g, k_norm_g=v_k_norm_g, sinks=v_sinks, mlp_norm_g=v_mlp_norm_g)

    loss_part, grad_x, got, gsmall = _forward_backward(x[0], loss_target[0], big, small)
    loss = lax.psum(loss_part, ("x", "y", "c"))

    mine = [jnp.stack([_sum_chips(got[name][l], "sum_" + name) for l in range(DEPTH)]) for name in BIG]
    theirs = _pair_swap(mine)
    upd = {name: _adamw(big[name], mine[i], theirs[i], big_m[name], big_v[name], "adamw_" + name)
           for i, name in enumerate(BIG)}
    g_small = _allreduce_small(_pack_small(gsmall))
    sm = _adamw(_pack_small(small), g_small, None, _pack_small(small_m), _pack_small(small_v), "adamw_small")
    upd_small = [_unpack_small(t, small) for t in sm]

    names = ("mix_norm_g", "w_in", "q_norm_g", "k_norm_g", "sinks", "w_branch_sb", "w_branch_swa", "w_out",
             "mlp_norm_g", "w_up", "w_down")
    pick = lambda n, i: upd[n][i] if n in upd else upd_small[i][n]
    return (loss, grad_x[None], *[pick(n, 0) for n in names], *[pick(n, 1) for n in names],
            *[pick(n, 2) for n in names], *[pick(n, 3) for n in names])
```

```python
import jax
import jax.numpy as jnp
from jax import lax
from jax.experimental import pallas as pl
from jax.experimental.pallas import tpu as pltpu

F32 = jnp.float32
BF16 = jnp.bfloat16

D_MODEL = 1024
DEPTH = 4
HEAD_DIM = 64
SB_WIDTH = 512
SWA_Q_WIDTH = 512
SWA_KV_WIDTH = 128
D_FF = 4096
IN_WIDTH = 4352
GATE_OFF = 2304
ROPE_THETA = 10000.0
NORM_EPS = 1e-6
SCALE = HEAD_DIM ** -0.5
N_CHIPS = 4
N_DEV = 8

ADAM_LR = 0.001
ADAM_B1 = 0.9
ADAM_B2 = 0.999
ADAM_EPS = 1e-08
ADAM_WD = 0.01
ADAM_STEP = 10

LANES = 128
SB_TILE = 256
SB_STEP_WIDTH = 256
SWA_BLOCK = 128
VMEM_LIMIT = 56 << 20

NN = (((1,), (0,)), ((), ()))
NT = (((1,), (1,)), ((), ()))
TN = (((0,), (0,)), ((), ()))
MESH = pl.DeviceIdType.MESH
HBM = pl.BlockSpec(memory_space=pl.ANY)

BIG = ("w_in", "w_branch_sb", "w_branch_swa", "w_out", "w_up", "w_down")
SMALL_NAMES = ("mix_norm_g", "q_norm_g", "k_norm_g", "sinks", "mlp_norm_g")
PACK_ROWS = (("w_in", 1088), ("w_branch_sb", 128), ("w_branch_swa", 128), ("w_out", 256), ("w_up", 1024), ("w_down", 1024))
SHARD_SHAPES = {"w_in": (1024, 1088), "w_branch_sb": (512, 256), "w_branch_swa": (512, 256), "w_out": (256, 1024),
                "w_up": (1024, 1024), "w_down": (1024, 1024)}
ROW_SHARDED = ("w_out", "w_down")
SMALL_ROWS = 72


def _dot(a, b, dims):
    return lax.dot_general(a, b, dims, preferred_element_type=F32)


def _cparams(sem):
    return pltpu.CompilerParams(dimension_semantics=sem, vmem_limit_bytes=VMEM_LIMIT)


def _resident(shape):
    nd = len(shape)
    return pl.BlockSpec(shape, lambda *_: (0,) * nd, pipeline_mode=pl.Buffered(1))


def _rows(tm, width):
    return pl.BlockSpec((tm, width), lambda i: (i, 0))


def _rms(xf):
    rstd = lax.rsqrt(jnp.mean(xf * xf, axis=-1, keepdims=True) + NORM_EPS)
    return xf * rstd, rstd


def _rms_bwd(dh, xn, rstd, g):
    dxn = dh * g
    return rstd * (dxn - xn * jnp.mean(dxn * xn, axis=-1, keepdims=True))


def _lane(shape):
    return lax.broadcasted_iota(jnp.int32, shape, len(shape) - 1)


def _head_mean(v, left):
    sl = jnp.sum(jnp.where(left, v, 0.0), axis=-1, keepdims=True)
    sr = jnp.sum(jnp.where(left, 0.0, v), axis=-1, keepdims=True)
    return jnp.where(left, sl, sr) * (1.0 / HEAD_DIM)


def _rope(y, cs, sn, first):
    up = pltpu.roll(y, 96, 1)
    dn = pltpu.roll(y, 32, 1)
    return y * cs + jnp.where(first, -up, dn) * sn


def _rope_t(d, cs, sn, first):
    t = d * jnp.where(first, -sn, sn)
    return d * cs + jnp.where(first, pltpu.roll(t, 96, 1), pltpu.roll(t, 32, 1))


def _norm_rope(p, g, cs, sn):
    lane = _lane((1, LANES))
    left = lane < HEAD_DIM
    first = (lane % HEAD_DIM) < (HEAD_DIM // 2)
    yn = p * lax.rsqrt(_head_mean(p * p, left) + NORM_EPS)
    return _rope(yn * g, cs, sn, first)


def _norm_rope_bwd(p, dout, g, cs, sn):
    lane = _lane((1, LANES))
    left = lane < HEAD_DIM
    first = (lane % HEAD_DIM) < (HEAD_DIM // 2)
    rstd = lax.rsqrt(_head_mean(p * p, left) + NORM_EPS)
    yn = p * rstd
    dyg = _rope_t(dout, cs, sn, first)
    dg = jnp.sum(dyg * yn, axis=0, keepdims=True)
    dyn = dyg * g
    dp = rstd * (dyn - yn * _head_mean(dyn * yn, left))
    return dp, dg


def _place():
    x, y, c = lax.axis_index("x"), lax.axis_index("y"), lax.axis_index("c")
    return x, y, c, [(1 - x, y), (x, 1 - y), (1 - x, 1 - y)]


def _hosted_gather(src, out, ssem, rsem, lsem, first, mid, last, slot=0):
    x, y, c, chips = _place()
    me = 2 * x + y
    sib = (x, y, 1 - c)
    r2 = src.shape[0] // 2
    base = 6 * slot

    def slab(chip, half):
        return out.at[chip, pl.ds(half * r2, r2)]

    def ici(j):
        cx, cy = chips[j]
        return pltpu.make_async_remote_copy(src.at[pl.ds(c * r2, r2)], slab(me, c), ssem.at[base + j], rsem.at[base + j],
                                            device_id=(cx, cy, c), device_id_type=MESH)

    def landed(j):
        got = slab(2 * chips[j][0] + chips[j][1], c)
        return pltpu.make_async_remote_copy(got, got, ssem.at[base + j], rsem.at[base + j], device_id=sib, device_id_type=MESH)

    def d2d(j, half):
        got = slab(2 * chips[j][0] + chips[j][1], half)
        return pltpu.make_async_remote_copy(got, got, ssem.at[base + 3 + j], rsem.at[base + 3 + j], device_id=sib,
                                            device_id_type=MESH)

    local = pltpu.make_async_copy(src, out.at[me], lsem.at[slot])

    @pl.when(first)
    def _():
        local.start()
        for j in range(3):
            ici(j).start()

    @pl.when(mid)
    def _():
        for j in range(3):
            landed(j).wait_recv()
            d2d(j, c).start()

    @pl.when(last)
    def _():
        for j in range(3):
            d2d(j, 1 - c).wait_recv()
        for j in range(3):
            ici(j).wait_send()
            d2d(j, c).wait_send()
        local.wait()


def _hosted_exchange(srcs, outs, ssem, rsem, first, last):
    x, y, c, chips = _place()
    me = 2 * x + y

    def send(t, j):
        cx, cy = chips[j]
        return pltpu.make_async_remote_copy(srcs[t].at[2 * cx + cy], outs[t].at[me], ssem.at[3 * t + j], rsem.at[3 * t + j],
                                            device_id=(cx, cy, c), device_id_type=MESH)

    def landed(t, j):
        cx, cy = chips[j]
        got = outs[t].at[2 * cx + cy]
        return pltpu.make_async_remote_copy(got, got, ssem.at[3 * t + j], rsem.at[3 * t + j],
                                            device_id=(cx, cy, c), device_id_type=MESH)

    @pl.when(first)
    def _():
        for t in range(len(srcs)):
            for j in range(3):
                send(t, j).start()

    @pl.when(last)
    def _():
        for t in range(len(srcs)):
            for j in range(3):
                landed(t, j).wait_recv()
        for t in range(len(srcs)):
            for j in range(3):
                send(t, j).wait_send()


def _gather_scratch(n):
    return [pltpu.SemaphoreType.DMA((6 * n,)), pltpu.SemaphoreType.DMA((6 * n,)), pltpu.SemaphoreType.DMA((n,))]


def _exchange_scratch(n):
    return [pltpu.SemaphoreType.DMA((3 * n,)), pltpu.SemaphoreType.DMA((3 * n,))]


def _inproj_fwd(x, g, w, qg, kg, cos, sin):
    s = x.shape[0]
    tm = min(512, s)

    def body(x_ref, g_ref, w_ref, qg_ref, kg_ref, cos_ref, sin_ref,
             h_ref, sb_ref, raw_ref, qn_ref, kn_ref, v_ref, gate_ref):
        xn, _ = _rms(x_ref[...])
        h = (xn * g_ref[...]).astype(BF16)
        h_ref[...] = h
        for a in range(0, 3 * SB_WIDTH, 512):
            sb_ref[:, a:a + 512] = _dot(h, w_ref[:, a:a + 512], NN).astype(BF16)
        cs, sn = cos_ref[...], sin_ref[...]
        q0 = 3 * SB_WIDTH
        pq = _dot(h, w_ref[:, q0:q0 + SWA_Q_WIDTH], NN)
        raw_ref[:, 0:SWA_Q_WIDTH] = pq
        for b in range(SWA_Q_WIDTH // LANES):
            blk = pq[:, b * LANES:(b + 1) * LANES]
            qn_ref[:, b * LANES:(b + 1) * LANES] = _norm_rope(blk, qg_ref[...], cs, sn).astype(BF16)
        k0 = q0 + SWA_Q_WIDTH
        pk = _dot(h, w_ref[:, k0:k0 + 2 * SWA_KV_WIDTH], NN)
        raw_ref[:, SWA_Q_WIDTH:SWA_Q_WIDTH + SWA_KV_WIDTH] = pk[:, :SWA_KV_WIDTH]
        kn_ref[...] = _norm_rope(pk[:, :SWA_KV_WIDTH], kg_ref[...], cs, sn).astype(BF16)
        v_ref[...] = pk[:, SWA_KV_WIDTH:].astype(BF16)
        for a in range(GATE_OFF, IN_WIDTH, 512):
            gate_ref[:, a - GATE_OFF:a - GATE_OFF + 512] = jax.nn.sigmoid(_dot(h, w_ref[:, a:a + 512], NN))

    return pl.pallas_call(
        body, name="inproj_fwd", grid=(s // tm,),
        in_specs=[_rows(tm, D_MODEL), _resident((1, D_MODEL)), _resident((D_MODEL, IN_WIDTH)),
                  _resident((1, LANES)), _resident((1, LANES)), _rows(tm, LANES), _rows(tm, LANES)],
        out_specs=[_rows(tm, D_MODEL), _rows(tm, 3 * SB_WIDTH), _rows(tm, 640), _rows(tm, SWA_Q_WIDTH),
                   _rows(tm, SWA_KV_WIDTH), _rows(tm, SWA_KV_WIDTH), _rows(tm, 2 * D_MODEL)],
        out_shape=[jax.ShapeDtypeStruct((s, D_MODEL), BF16), jax.ShapeDtypeStruct((s, 3 * SB_WIDTH), BF16),
                   jax.ShapeDtypeStruct((s, 640), F32), jax.ShapeDtypeStruct((s, SWA_Q_WIDTH), BF16),
                   jax.ShapeDtypeStruct((s, SWA_KV_WIDTH), BF16), jax.ShapeDtypeStruct((s, SWA_KV_WIDTH), BF16),
                   jax.ShapeDtypeStruct((s, 2 * D_MODEL), F32)],
        compiler_params=_cparams(("arbitrary",)),
    )(x, g, w, qg, kg, cos, sin)


def _sb_tile_terms(qh, k, strict):
    z = _dot(qh, k, NT)
    nz = -z
    soft = jnp.log(1.0 + jnp.exp(jnp.minimum(z, nz)))
    log_keep = jnp.minimum(nz, 0.0) - soft
    if strict is not None:
        log_keep = jnp.where(strict, log_keep, 0.0)
    return z, log_keep


def _log_weights(z, log_keep, run, u_from):
    tail = run + _dot(log_keep.astype(BF16), u_from, NN)
    return z + tail, tail[:, 0:1]


def _masked(strict, val):
    return val if strict is None else jnp.where(strict, val, 0.0)


def _tri_twice(t):
    r = lax.broadcasted_iota(jnp.int32, (2 * t, t), 0)
    c = lax.broadcasted_iota(jnp.int32, (2 * t, t), 1)
    return (jnp.where(r >= t, r - t, r) >= c).astype(BF16)


def _split_dot(a, u2):
    hi = a.astype(BF16)
    lo = (a - hi.astype(F32)).astype(BF16)
    return _dot(jnp.concatenate([hi, lo], axis=1), u2, NN)


def _sb_fwd(sbqkv, shards=()):
    s = sbqkv.shape[0]
    t = min(SB_TILE, s)
    nq = s // t
    ng = SB_WIDTH // SB_STEP_WIDTH
    heads = range(2 * SB_STEP_WIDTH // LANES)
    nsh = len(shards)

    def body(*refs):
        q_ref, k_ref, v_ref = refs[:3]
        o_ref = refs[3 + nsh]
        j, i = pl.program_id(0), pl.program_id(1)
        for n in range(nsh):
            _hosted_gather(refs[3 + n], refs[4 + nsh + n], *refs[4 + 2 * nsh:], (j == 0) & (i == 0),
                           (j == ng - 1) & (i == (3 * nq) // 4), (j == ng - 1) & (i == nq - 1), slot=n)
        left = _lane((1, LANES)) < HEAD_DIM
        halves = (left, jnp.logical_not(left))
        pair = lambda a, p: a[:, (p // 2) * LANES:(p // 2 + 1) * LANES]
        only = lambda a, p: jnp.where(halves[p % 2], pair(a, p), jnp.zeros((), a.dtype))
        q_all = q_ref[...] * jnp.asarray(SCALE, BF16)
        qh = [only(q_all, p) for p in heads]
        r = lax.broadcasted_iota(jnp.int32, (t, t), 0)
        c = lax.broadcasted_iota(jnp.int32, (t, t), 1)
        u_from = (r >= c).astype(BF16)

        def tile(n, carry, strict):
            runs, accs = carry
            off = pl.multiple_of((i - n) * t, t)
            k = k_ref[pl.ds(off, t), :]
            v = v_ref[pl.ds(off, t), :]
            terms = [_sb_tile_terms(qh[p], pair(k, p), strict) for p in heads]
            logw = [_log_weights(*terms[p], runs[p], u_from) for p in heads]
            ws = [_masked(strict, jnp.exp(logw[p][0])).astype(BF16) for p in heads]
            accs = list(accs)
            for p in heads:
                accs[p // 2] = accs[p // 2] + _dot(ws[p], only(v, p), NN)
            return tuple(logw[p][1] for p in heads), tuple(accs)

        zero = jnp.zeros((t, 1), F32)
        wide = jnp.zeros((t, LANES), F32)
        carry = tile(0, (tuple(zero for _ in heads), tuple(wide for _ in heads[::2])), c < r)
        _, accs = lax.fori_loop(1, i + 1, lambda n, cr: tile(n, cr, None), carry)
        for g, acc in enumerate(accs):
            o_ref[:, g * LANES:(g + 1) * LANES] = acc

    w = SB_STEP_WIDTH
    blk = pl.BlockSpec((t, w), lambda j, i: (i, j))
    in_specs = [blk, pl.BlockSpec((s, w), lambda j, i: (0, ng + j)), pl.BlockSpec((s, w), lambda j, i: (0, 2 * ng + j))]
    out_specs = [blk]
    out_shape = [jax.ShapeDtypeStruct((s, SB_WIDTH), F32)]
    in_specs += [HBM] * nsh
    out_specs += [HBM] * nsh
    out_shape += [jax.ShapeDtypeStruct((N_CHIPS,) + a.shape, a.dtype) for a in shards]
    return pl.pallas_call(
        body, name="sb_fwd_g%d" % nsh, grid=(ng, nq),
        in_specs=in_specs, out_specs=out_specs, out_shape=out_shape,
        scratch_shapes=_gather_scratch(nsh) if nsh else [],
        compiler_params=_cparams(("arbitrary", "arbitrary")),
    )(sbqkv, sbqkv, sbqkv, *shards)


def _swa_band(cur_ref, prev_ref, b):
    lo = prev_ref[...] if b == 0 else cur_ref[(b - 1) * SWA_BLOCK:b * SWA_BLOCK, :]
    return jnp.concatenate([lo, cur_ref[b * SWA_BLOCK:(b + 1) * SWA_BLOCK, :]], axis=0)


def _swa_variants(band, left):
    f = band.astype(F32)
    sw = pltpu.roll(f, HEAD_DIM, 1)
    halves = (left, jnp.logical_not(left))
    return [[jnp.where(halves[p], f if p == g else sw, 0.0).astype(BF16) for p in range(2)] for g in range(2)]


def _swa_probs(qh, kv, sink, blk):
    sc = _dot(qh, kv, NT) * SCALE
    ii = lax.broadcasted_iota(jnp.int32, (SWA_BLOCK, 2 * SWA_BLOCK), 0)
    jj = lax.broadcasted_iota(jnp.int32, (SWA_BLOCK, 2 * SWA_BLOCK), 1)
    rel = jj - SWA_BLOCK - ii
    valid = (rel <= 0) & (rel > -SWA_BLOCK) & (jj + blk * SWA_BLOCK >= SWA_BLOCK)
    sc = jnp.where(valid, sc, -1e30)
    m = jnp.maximum(jnp.max(sc, axis=-1, keepdims=True), sink)
    e = jnp.exp(sc - m)
    es = jnp.exp(sink - m)
    inv = 1.0 / (jnp.sum(e, axis=-1, keepdims=True) + es)
    return e * inv, es * inv


def _swa_specs(s, t):
    nb = t // SWA_BLOCK
    cur = pl.BlockSpec((t, LANES), lambda i: (i, 0))
    prev = pl.BlockSpec((SWA_BLOCK, LANES), lambda i: (jnp.maximum(i * nb - 1, 0), 0))
    return cur, prev


def _swa_fwd(qn, kn, v, sinks):
    s = qn.shape[0]
    t = min(512, s)
    nb = t // SWA_BLOCK
    nheads = SWA_Q_WIDTH // HEAD_DIM

    def body(sink_ref, q_ref, kc_ref, kp_ref, vc_ref, vp_ref, o_ref):
        i = pl.program_id(0)
        left = _lane((1, LANES)) < HEAD_DIM
        halves = (left, jnp.logical_not(left))
        for b in range(nb):
            kvar = _swa_variants(_swa_band(kc_ref, kp_ref, b), left)
            vvar = _swa_variants(_swa_band(vc_ref, vp_ref, b), left)
            rows = slice(b * SWA_BLOCK, (b + 1) * SWA_BLOCK)
            for hb in range(nheads // 2):
                q2 = q_ref[rows, hb * LANES:(hb + 1) * LANES]
                acc = jnp.zeros((SWA_BLOCK, LANES), F32)
                for p in range(2):
                    h = 2 * hb + p
                    g = h // (nheads // 2)
                    qh = jnp.where(halves[p], q2, jnp.zeros_like(q2))
                    probs, _ = _swa_probs(qh, kvar[g][p], sink_ref[h], i * nb + b)
                    acc = acc + _dot(probs.astype(BF16), vvar[g][p], NN)
                o_ref[rows, hb * LANES:(hb + 1) * LANES] = acc.astype(BF16)

    cur, prev = _swa_specs(s, t)
    return pl.pallas_call(
        body, name="swa_fwd", grid=(s // t,),
        in_specs=[pl.BlockSpec(memory_space=pltpu.SMEM), _rows(t, SWA_Q_WIDTH), cur, prev, cur, prev],
        out_specs=_rows(t, SWA_Q_WIDTH),
        out_shape=jax.ShapeDtypeStruct((s, SWA_Q_WIDTH), BF16),
        compiler_params=_cparams(("arbitrary",)),
    )(sinks, qn, kn, kn, v, v)


def _merge_fwd(x, osb, oswa, gates, wsb, wswa, wout):
    s = x.shape[0]
    tm = min(512, s)

    def body(x_ref, osb_ref, oswa_ref, gate_ref, wsb_ref, wswa_ref, wout_ref, x1_ref):
        ysb = _dot(osb_ref[...].astype(BF16), wsb_ref[...], NN)
        yswa = _dot(oswa_ref[...], wswa_ref[...], NN)
        merged = gate_ref[:, :D_MODEL] * ysb + gate_ref[:, D_MODEL:] * yswa
        x1_ref[...] = x_ref[...] + _dot(merged.astype(BF16), wout_ref[...], NN)

    return pl.pallas_call(
        body, name="merge_fwd", grid=(s // tm,),
        in_specs=[_rows(tm, D_MODEL), _rows(tm, SB_WIDTH), _rows(tm, SWA_Q_WIDTH), _rows(tm, 2 * D_MODEL),
                  _resident((SB_WIDTH, D_MODEL)), _resident((SWA_Q_WIDTH, D_MODEL)), _resident((D_MODEL, D_MODEL))],
        out_specs=_rows(tm, D_MODEL),
        out_shape=jax.ShapeDtypeStruct((s, D_MODEL), F32),
        compiler_params=_cparams(("arbitrary",)),
    )(x, osb, oswa, gates, wsb, wswa, wout)


def _mlp_fwd(x1, g, wup, wdown):
    s = x1.shape[0]
    tm = min(512, s)
    fc = D_FF // N_CHIPS

    def body(x_ref, g_ref, wup_ref, wdown_ref, x2_ref, u_ref):
        xf = x_ref[...]
        xn, _ = _rms(xf)
        h2 = (xn * g_ref[...]).astype(BF16)
        acc = xf
        for q in range(N_CHIPS):
            u = _dot(h2, wup_ref[q], NN)
            u_ref[:, q * fc:(q + 1) * fc] = u
            r = jnp.maximum(u, 0.0)
            acc = acc + _dot((r * r).astype(BF16), wdown_ref[q * fc:(q + 1) * fc, :], NN)
        x2_ref[...] = acc

    return pl.pallas_call(
        body, name="mlp_fwd", grid=(s // tm,),
        in_specs=[_rows(tm, D_MODEL), _resident((1, D_MODEL)), _resident((N_CHIPS, D_MODEL, fc)), _resident((D_FF, D_MODEL))],
        out_specs=[_rows(tm, D_MODEL), _rows(tm, D_FF)],
        out_shape=[jax.ShapeDtypeStruct((s, D_MODEL), F32), jax.ShapeDtypeStruct((s, D_FF), F32)],
        compiler_params=_cparams(("arbitrary",)),
    )(x1, g, wup, wdown)


def _loss_grad(y, target):
    s = y.shape[0]
    tm = min(512, s)

    def body(y_ref, t_ref, dy_ref, part_ref):
        err = y_ref[...] - t_ref[...]
        dy_ref[...] = err * (1.0 / D_MODEL)
        tot = jnp.sum(jnp.sum(err * err, axis=-1, keepdims=True), axis=0, keepdims=True)
        part_ref[...] = jnp.broadcast_to(tot.reshape(1, 1, 1), (1, 8, LANES))

    dy, part = pl.pallas_call(
        body, name="loss_grad", grid=(s // tm,),
        in_specs=[_rows(tm, D_MODEL), _rows(tm, D_MODEL)],
        out_specs=[_rows(tm, D_MODEL), pl.BlockSpec((1, 8, LANES), lambda i: (i, 0, 0))],
        out_shape=[jax.ShapeDtypeStruct((s, D_MODEL), F32), jax.ShapeDtypeStruct((s // tm, 8, LANES), F32)],
        compiler_params=_cparams(("arbitrary",)),
    )(y, target)
    return dy, (0.5 / D_MODEL) * jnp.sum(part[:, 0, 0])


def _mlp_bwd(dx2, x1, u, g, wup, wdown):
    s = x1.shape[0]
    tm = min(256, s)
    fc = D_FF // N_CHIPS

    def body(dx2_ref, x_ref, u_ref, g_ref, wup_ref, wdown_ref, dx1_ref, du_ref, a_ref, h2_ref, dg_ref):
        @pl.when(pl.program_id(0) == 0)
        def _():
            dg_ref[...] = jnp.zeros_like(dg_ref)

        gam = g_ref[...]
        xn, rstd = _rms(x_ref[...])
        h2_ref[...] = (xn * gam).astype(BF16)
        dxf = dx2_ref[...]
        dxb = dxf.astype(BF16)
        dh2 = jnp.zeros((tm, D_MODEL), F32)
        for q in range(N_CHIPS):
            cols = slice(q * fc, (q + 1) * fc)
            da = _dot(dxb, wdown_ref[cols, :], NT)
            r = jnp.maximum(u_ref[:, cols], 0.0)
            a_ref[:, cols] = (r * r).astype(BF16)
            du = (da * (2.0 * r)).astype(BF16)
            du_ref[:, cols] = du
            dh2 = dh2 + _dot(du, wup_ref[q], NT)
        dg_ref[...] += jnp.sum(dh2 * xn, axis=0, keepdims=True)
        dx1_ref[...] = dxf + _rms_bwd(dh2, xn, rstd, gam)

    return pl.pallas_call(
        body, name="mlp_bwd", grid=(s // tm,),
        in_specs=[_rows(tm, D_MODEL), _rows(tm, D_MODEL), _rows(tm, D_FF), _resident((1, D_MODEL)),
                  _resident((N_CHIPS, D_MODEL, fc)), _resident((D_FF, D_MODEL))],
        out_specs=[_rows(tm, D_MODEL), _rows(tm, D_FF), _rows(tm, D_FF), _rows(tm, D_MODEL),
                   pl.BlockSpec((1, D_MODEL), lambda i: (0, 0))],
        out_shape=[jax.ShapeDtypeStruct((s, D_MODEL), F32), jax.ShapeDtypeStruct((s, D_FF), BF16),
                   jax.ShapeDtypeStruct((s, D_FF), BF16), jax.ShapeDtypeStruct((s, D_MODEL), BF16),
                   jax.ShapeDtypeStruct((1, D_MODEL), F32)],
        compiler_params=_cparams(("arbitrary",)),
    )(dx2, x1, u, g, wup, wdown)


def _wgrad(a, b, name, shard_axis=None):
    s, m = a.shape
    n = b.shape[1]
    tm = min(512, m if shard_axis != 0 else m // N_CHIPS)
    tn = min(512, n if shard_axis != 1 else n // N_CHIPS)
    if shard_axis is None and n % tn:
        tn = 256

    def body(a_ref, b_ref, o_ref, *narrow):
        res = _dot(a_ref[...].astype(BF16), b_ref[...].astype(BF16), TN)
        o_ref[...] = res.reshape(o_ref.shape)
        for n_ref in narrow:
            n_ref[...] = res.astype(BF16).reshape(n_ref.shape)

    if shard_axis is None:
        out_shape, out_spec = (m, n), pl.BlockSpec((tm, tn), lambda i, j: (i, j))
    elif shard_axis == 0:
        per = m // N_CHIPS // tm
        out_shape, out_spec = (N_CHIPS, m // N_CHIPS, n), pl.BlockSpec((1, tm, tn), lambda i, j: (i // per, i % per, j))
    else:
        per = n // N_CHIPS // tn
        out_shape, out_spec = (N_CHIPS, m, n // N_CHIPS), pl.BlockSpec((1, tm, tn), lambda i, j: (j // per, i, j % per))
    both = shard_axis is not None
    return pl.pallas_call(
        body, name=name, grid=(m // tm, n // tn),
        in_specs=[pl.BlockSpec((s, tm), lambda i, j: (0, i)), pl.BlockSpec((s, tn), lambda i, j: (0, j))],
        out_specs=[out_spec, out_spec] if both else out_spec,
        out_shape=[jax.ShapeDtypeStruct(out_shape, F32), jax.ShapeDtypeStruct(out_shape, BF16)] if both
        else jax.ShapeDtypeStruct(out_shape, F32),
        compiler_params=_cparams(("arbitrary", "arbitrary")),
    )(a, b)


def _merge_bwd(dx1, osb, oswa, gates, wsb, wswa, wout):
    s = dx1.shape[0]
    tm = min(512, s)

    def body(dx_ref, osb_ref, oswa_ref, gate_ref, wsb_ref, wswa_ref, wout_ref,
             dosb_ref, doswa_ref, dgl_ref, merged_ref, dysb_ref, dyswa_ref):
        dm = _dot(dx_ref[...].astype(BF16), wout_ref[...], NT)
        ysb = _dot(osb_ref[...].astype(BF16), wsb_ref[...], NN)
        yswa = _dot(oswa_ref[...], wswa_ref[...], NN)
        g0 = gate_ref[:, :D_MODEL]
        g1 = gate_ref[:, D_MODEL:]
        merged_ref[...] = (g0 * ysb + g1 * yswa).astype(BF16)
        dgl_ref[:, :D_MODEL] = (dm * ysb * (g0 * (1.0 - g0))).astype(BF16)
        dgl_ref[:, D_MODEL:] = (dm * yswa * (g1 * (1.0 - g1))).astype(BF16)
        dysb = (dm * g0).astype(BF16)
        dyswa = (dm * g1).astype(BF16)
        dysb_ref[...] = dysb
        dyswa_ref[...] = dyswa
        dosb_ref[...] = _dot(dysb, wsb_ref[...], NT)
        doswa_ref[...] = _dot(dyswa, wswa_ref[...], NT)

    return pl.pallas_call(
        body, name="merge_bwd", grid=(s // tm,),
        in_specs=[_rows(tm, D_MODEL), _rows(tm, SB_WIDTH), _rows(tm, SWA_Q_WIDTH), _rows(tm, 2 * D_MODEL),
                  _resident((SB_WIDTH, D_MODEL)), _resident((SWA_Q_WIDTH, D_MODEL)), _resident((D_MODEL, D_MODEL))],
        out_specs=[_rows(tm, SB_WIDTH), _rows(tm, SWA_Q_WIDTH), _rows(tm, 2 * D_MODEL), _rows(tm, D_MODEL),
                   _rows(tm, D_MODEL), _rows(tm, D_MODEL)],
        out_shape=[jax.ShapeDtypeStruct((s, SB_WIDTH), F32), jax.ShapeDtypeStruct((s, SWA_Q_WIDTH), F32),
                   jax.ShapeDtypeStruct((s, 2 * D_MODEL), BF16), jax.ShapeDtypeStruct((s, D_MODEL), BF16),
                   jax.ShapeDtypeStruct((s, D_MODEL), BF16), jax.ShapeDtypeStruct((s, D_MODEL), BF16)],
        compiler_params=_cparams(("arbitrary",)),
    )(dx1, osb, oswa, gates, wsb, wswa, wout)


def _sb_bwd(sbqkv, osb, dosb, send):
    s = sbqkv.shape[0]
    t = min(SB_TILE, s)
    nq = s // t
    ng = SB_WIDTH // SB_STEP_WIDTH
    heads = range(2 * SB_STEP_WIDTH // LANES)
    ns = len(send)

    def body(*refs):
        q_ref, k_ref, v_ref, o_ref, do_ref = refs[:5]
        srcs = refs[5:5 + ns]
        dq_ref, dk_ref, dv_ref = refs[5 + ns:8 + ns]
        outs = refs[8 + ns:8 + 2 * ns]
        j, i = pl.program_id(0), pl.program_id(1)
        if ns:
            ssem, rsem = refs[8 + 2 * ns:]
            _hosted_exchange(srcs, outs, ssem, rsem, (j == 0) & (i == 0), (j == ng - 1) & (i == nq - 1))

        @pl.when(i == 0)
        def _():
            dk_ref[...] = jnp.zeros_like(dk_ref)
            dv_ref[...] = jnp.zeros_like(dv_ref)

        left = _lane((1, LANES)) < HEAD_DIM
        halves = (left, jnp.logical_not(left))
        pair = lambda a, p: a[:, (p // 2) * LANES:(p // 2 + 1) * LANES]
        only = lambda a, p: jnp.where(halves[p % 2], pair(a, p), jnp.zeros((), a.dtype))
        scale = jnp.asarray(SCALE, BF16)
        q_all = q_ref[...] * scale
        do_all = do_ref[...].astype(BF16)
        prod = do_all.astype(F32) * o_ref[...]
        qh = [only(q_all, p) for p in heads]
        doh = [only(do_all, p) for p in heads]
        delta = [jnp.sum(only(prod, p), axis=-1, keepdims=True) for p in heads]
        r = lax.broadcasted_iota(jnp.int32, (t, t), 0)
        c = lax.broadcasted_iota(jnp.int32, (t, t), 1)
        u_from = (r >= c).astype(BF16)
        u_from2 = _tri_twice(t)

        def tile(n, carry, strict):
            runs, dqs = carry
            off = pl.multiple_of((i - n) * t, t)
            k = k_ref[pl.ds(off, t), :]
            v = v_ref[pl.ds(off, t), :]
            ks = k * scale
            terms = [_sb_tile_terms(qh[p], pair(k, p), strict) for p in heads]
            logw = [_log_weights(*terms[p], runs[p][0], u_from) for p in heads]
            dws = [_dot(doh[p], pair(v, p), NT) for p in heads]
            wbs = [_masked(strict, jnp.exp(logw[p][0])).astype(BF16) for p in heads]
            es = [dws[p] * wbs[p].astype(F32) for p in heads]
            rests = [runs[p][1] + _split_dot(es[p], u_from2) for p in heads]
            betas = [jnp.exp(terms[p][0] + terms[p][1]) for p in heads]
            dzs = [_masked(strict, es[p] - betas[p] * (es[p] + delta[p] - rests[p])).astype(BF16) for p in heads]
            dqs = list(dqs)
            for g in range(len(heads) // 2):
                a, b = 2 * g, 2 * g + 1
                cols = slice(g * LANES, (g + 1) * LANES)
                dv_ref[pl.ds(off, t), cols] += _dot(wbs[a], doh[a], TN) + _dot(wbs[b], doh[b], TN)
                dk_ref[pl.ds(off, t), cols] += _dot(dzs[a], qh[a], TN) + _dot(dzs[b], qh[b], TN)
                dqs[g] = dqs[g] + _dot(dzs[a], only(ks, a), NN) + _dot(dzs[b], only(ks, b), NN)
            new_runs = tuple((logw[p][1], rests[p][:, 0:1]) for p in heads)
            return new_runs, tuple(dqs)

        zero = jnp.zeros((t, 1), F32)
        wide = jnp.zeros((t, LANES), F32)
        carry = tile(0, (tuple((zero, zero) for _ in heads), tuple(wide for _ in heads[::2])), c < r)
        _, dqs = lax.fori_loop(1, i + 1, lambda n, cr: tile(n, cr, None), carry)
        for g, dq in enumerate(dqs):
            dq_ref[:, g * LANES:(g + 1) * LANES] = dq.astype(BF16)

    w = SB_STEP_WIDTH
    blk = pl.BlockSpec((t, w), lambda j, i: (i, j))
    whole = pl.BlockSpec((s, w), lambda j, i: (0, j))
    return pl.pallas_call(
        body, name="sb_bwd_x%d" % ns, grid=(ng, nq),
        in_specs=[blk, pl.BlockSpec((s, w), lambda j, i: (0, ng + j)),
                  pl.BlockSpec((s, w), lambda j, i: (0, 2 * ng + j)), blk, blk] + [HBM] * ns,
        out_specs=[blk, whole, whole] + [HBM] * ns,
        out_shape=[jax.ShapeDtypeStruct((s, SB_WIDTH), BF16), jax.ShapeDtypeStruct((s, SB_WIDTH), F32),
                   jax.ShapeDtypeStruct((s, SB_WIDTH), F32)] + [jax.ShapeDtypeStruct(a.shape, a.dtype) for a in send],
        scratch_shapes=_exchange_scratch(ns) if ns else [],
        compiler_params=_cparams(("arbitrary", "arbitrary")),
    )(sbqkv, sbqkv, sbqkv, osb, dosb, *send)


def _swa_bwd(qn, kn, v, sinks, do):
    s = qn.shape[0]
    t = min(512, s)
    nb = t // SWA_BLOCK
    nheads = SWA_Q_WIDTH // HEAD_DIM

    def body(sink_ref, q_ref, kc_ref, kp_ref, vc_ref, vp_ref, do_ref, dq_ref, dk_ref, dv_ref, dsink_ref):
        i = pl.program_id(0)

        @pl.when(i == 0)
        def _():
            dk_ref[...] = jnp.zeros_like(dk_ref)
            dv_ref[...] = jnp.zeros_like(dv_ref)

        left = _lane((1, LANES)) < HEAD_DIM
        halves = (left, jnp.logical_not(left))
        dsink = [jnp.zeros((1, 1), F32) for _ in range(nheads)]
        for b in range(nb):
            kvar = _swa_variants(_swa_band(kc_ref, kp_ref, b), left)
            vvar = _swa_variants(_swa_band(vc_ref, vp_ref, b), left)
            rows = slice(b * SWA_BLOCK, (b + 1) * SWA_BLOCK)
            dk_acc = [jnp.zeros((2 * SWA_BLOCK, LANES), F32) for _ in range(2)]
            dv_acc = [jnp.zeros((2 * SWA_BLOCK, LANES), F32) for _ in range(2)]
            for hb in range(nheads // 2):
                q2 = q_ref[rows, hb * LANES:(hb + 1) * LANES]
                do2 = do_ref[rows, hb * LANES:(hb + 1) * LANES]
                dq = jnp.zeros((SWA_BLOCK, LANES), F32)
                for p in range(2):
                    h = 2 * hb + p
                    g = h // (nheads // 2)
                    qh = jnp.where(halves[p], q2, jnp.zeros_like(q2))
                    doh = jnp.where(halves[p], do2, 0.0).astype(BF16)
                    probs, psink = _swa_probs(qh, kvar[g][p], sink_ref[h], i * nb + b)
                    dp = _dot(doh, vvar[g][p], NT)
                    delta = jnp.sum(probs * dp, axis=-1, keepdims=True)
                    ds = (probs * (dp - delta) * SCALE).astype(BF16)
                    dsink[h] = dsink[h] - jnp.sum(psink * delta, axis=0, keepdims=True)
                    dq = dq + _dot(ds, kvar[g][p], NN)
                    which = 0 if p == g else 1
                    dk_acc[which] = dk_acc[which] + _dot(ds, qh, TN)
                    dv_acc[which] = dv_acc[which] + _dot(probs.astype(BF16), doh, TN)
                dq_ref[rows, hb * LANES:(hb + 1) * LANES] = dq
            dkb = dk_acc[0] + pltpu.roll(dk_acc[1], HEAD_DIM, 1)
            dvb = dv_acc[0] + pltpu.roll(dv_acc[1], HEAD_DIM, 1)
            start = pl.multiple_of((i * nb + b) * SWA_BLOCK, SWA_BLOCK)
            dk_ref[pl.ds(start, SWA_BLOCK), :] += dkb[SWA_BLOCK:]
            dv_ref[pl.ds(start, SWA_BLOCK), :] += dvb[SWA_BLOCK:]

            @pl.when(i * nb + b > 0)
            def _(dkb=dkb, dvb=dvb, start=start):
                before = pl.multiple_of(jnp.maximum(start - SWA_BLOCK, 0), SWA_BLOCK)
                dk_ref[pl.ds(before, SWA_BLOCK), :] += dkb[:SWA_BLOCK]
                dv_ref[pl.ds(before, SWA_BLOCK), :] += dvb[:SWA_BLOCK]

        for h in range(nheads):
            dsink_ref[0, h:h + 1, :] = jnp.broadcast_to(dsink[h], (1, LANES))

    cur, prev = _swa_specs(s, t)
    whole = pl.BlockSpec((s, LANES), lambda i: (0, 0))
    return pl.pallas_call(
        body, name="swa_bwd", grid=(s // t,),
        in_specs=[pl.BlockSpec(memory_space=pltpu.SMEM), _rows(t, SWA_Q_WIDTH), cur, prev, cur, prev,
                  _rows(t, SWA_Q_WIDTH)],
        out_specs=[_rows(t, SWA_Q_WIDTH), whole, whole, pl.BlockSpec((1, 8, LANES), lambda i: (i, 0, 0))],
        out_shape=[jax.ShapeDtypeStruct((s, SWA_Q_WIDTH), F32), jax.ShapeDtypeStruct((s, SWA_KV_WIDTH), F32),
                   jax.ShapeDtypeStruct((s, SWA_KV_WIDTH), F32), jax.ShapeDtypeStruct((s // t, 8, LANES), F32)],
        compiler_params=_cparams(("arbitrary",)),
    )(sinks, qn, kn, kn, v, v, do)


def _swa_post(raw, dqn, dkn, qg, kg, cos, sin):
    s = raw.shape[0]
    tm = min(512, s)
    nq = SWA_Q_WIDTH // LANES

    def body(raw_ref, dq_ref, dk_ref, qg_ref, kg_ref, cos_ref, sin_ref, out_ref, dg_ref):
        @pl.when(pl.program_id(0) == 0)
        def _():
            dg_ref[...] = jnp.zeros_like(dg_ref)

        cs, sn = cos_ref[...], sin_ref[...]
        dgq = jnp.zeros((1, LANES), F32)
        for b in range(nq):
            cols = slice(b * LANES, (b + 1) * LANES)
            dp, dg = _norm_rope_bwd(raw_ref[:, cols], dq_ref[:, cols], qg_ref[...], cs, sn)
            out_ref[:, cols] = dp.astype(BF16)
            dgq = dgq + dg
        cols = slice(SWA_Q_WIDTH, SWA_Q_WIDTH + LANES)
        dp, dgk = _norm_rope_bwd(raw_ref[:, cols], dk_ref[...], kg_ref[...], cs, sn)
        out_ref[:, cols] = dp.astype(BF16)
        dg_ref[0:1, :] += dgq
        dg_ref[1:2, :] += dgk

    return pl.pallas_call(
        body, name="swa_post", grid=(s // tm,),
        in_specs=[_rows(tm, 640), _rows(tm, SWA_Q_WIDTH), _rows(tm, LANES), _resident((1, LANES)),
                  _resident((1, LANES)), _rows(tm, LANES), _rows(tm, LANES)],
        out_specs=[_rows(tm, 640), pl.BlockSpec((8, LANES), lambda i: (0, 0))],
        out_shape=[jax.ShapeDtypeStruct((s, 640), BF16), jax.ShapeDtypeStruct((8, LANES), F32)],
        compiler_params=_cparams(("arbitrary",)),
    )(raw, dqn, dkn, qg, kg, cos, sin)


def _inproj_bwd(dsbq, dsbk, dsbv, dswqk, dswv, dgl, x, dx1, g, w):
    s = x.shape[0]
    tm = min(256, s)

    def body(dsbq_ref, dsbk_ref, dsbv_ref, dswqk_ref, dswv_ref, dgl_ref, x_ref, dx1_ref, g_ref, w_ref,
             dx_ref, dproj_ref, dg_ref):
        @pl.when(pl.program_id(0) == 0)
        def _():
            dg_ref[...] = jnp.zeros_like(dg_ref)

        dproj_ref[:, 0:512] = dsbq_ref[...]
        dproj_ref[:, 512:1024] = dsbk_ref[...].astype(BF16)
        dproj_ref[:, 1024:1536] = dsbv_ref[...].astype(BF16)
        dproj_ref[:, 1536:2176] = dswqk_ref[...]
        dproj_ref[:, 2176:2304] = dswv_ref[...].astype(BF16)
        dproj_ref[:, GATE_OFF:IN_WIDTH] = dgl_ref[...]
        dh = jnp.zeros((tm, D_MODEL), F32)
        for a in range(0, IN_WIDTH, 512):
            b = min(a + 512, IN_WIDTH)
            dh = dh + _dot(dproj_ref[:, a:b], w_ref[:, a:b], NT)
        gam = g_ref[...]
        xn, rstd = _rms(x_ref[...])
        dg_ref[...] += jnp.sum(dh * xn, axis=0, keepdims=True)
        dx_ref[...] = dx1_ref[...] + _rms_bwd(dh, xn, rstd, gam)

    return pl.pallas_call(
        body, name="inproj_bwd", grid=(s // tm,),
        in_specs=[_rows(tm, 512), _rows(tm, 512), _rows(tm, 512), _rows(tm, 640), _rows(tm, LANES),
                  _rows(tm, 2 * D_MODEL), _rows(tm, D_MODEL), _rows(tm, D_MODEL), _resident((1, D_MODEL)),
                  _resident((D_MODEL, IN_WIDTH))],
        out_specs=[_rows(tm, D_MODEL), _rows(tm, IN_WIDTH), pl.BlockSpec((1, D_MODEL), lambda i: (0, 0))],
        out_shape=[jax.ShapeDtypeStruct((s, D_MODEL), F32), jax.ShapeDtypeStruct((s, IN_WIDTH), BF16),
                   jax.ShapeDtypeStruct((1, D_MODEL), F32)],
        compiler_params=_cparams(("arbitrary",)),
    )(dsbq, dsbk, dsbv, dswqk, dswv, dgl, x, dx1, g, w)


def _gather_weights(shard):
    def body(src, out, ssem, rsem, lsem):
        once = pl.program_id(0) == 0
        _hosted_gather(src, out, ssem, rsem, lsem, once, once, once)

    return pl.pallas_call(
        body, name="gather_weights", grid=(1,), in_specs=[HBM], out_specs=HBM,
        out_shape=jax.ShapeDtypeStruct((N_CHIPS,) + shard.shape, shard.dtype),
        scratch_shapes=_gather_scratch(1),
        compiler_params=pltpu.CompilerParams(dimension_semantics=("arbitrary",), has_side_effects=True),
    )(shard)


def _exchange(send):
    ns = len(send)

    def body(*refs):
        once = pl.program_id(0) == 0
        _hosted_exchange(refs[:ns], refs[ns:2 * ns], *refs[2 * ns:], once, once)

    return pl.pallas_call(
        body, name="exchange_x%d" % ns, grid=(1,), in_specs=[HBM] * ns, out_specs=[HBM] * ns,
        out_shape=[jax.ShapeDtypeStruct(a.shape, a.dtype) for a in send],
        scratch_shapes=_exchange_scratch(ns),
        compiler_params=pltpu.CompilerParams(dimension_semantics=("arbitrary",), has_side_effects=True),
    )(*send)


def _pair_swap(arrs):
    n = len(arrs)

    def body(*refs):
        x, y, c, _ = _place()
        cps = [pltpu.make_async_remote_copy(refs[t], refs[n + t], refs[2 * n].at[t], refs[2 * n + 1].at[t],
                                            device_id=(x, y, 1 - c), device_id_type=MESH) for t in range(n)]
        for cp in cps:
            cp.start()
        for cp in cps:
            cp.wait()

    return pl.pallas_call(
        body, name="pair_swap", in_specs=[HBM] * n, out_specs=[HBM] * n,
        out_shape=[jax.ShapeDtypeStruct(a.shape, a.dtype) for a in arrs],
        scratch_shapes=[pltpu.SemaphoreType.DMA((n,)), pltpu.SemaphoreType.DMA((n,))],
        compiler_params=pltpu.CompilerParams(has_side_effects=True),
    )(*arrs)


def _allreduce_small(block):
    def body(src, out, buf, ssem, rsem):
        x, y, c, _ = _place()
        me = 4 * x + 2 * y + c
        buf[me] = src[...]
        cps = []
        for k in range(1, N_DEV):
            peer = (x ^ (k >> 2), y ^ ((k >> 1) & 1), c ^ (k & 1))
            cp = pltpu.make_async_remote_copy(src, buf.at[me], ssem.at[k - 1], rsem.at[k - 1], device_id=peer, device_id_type=MESH)
            cp.start()
            cps.append(cp)
        for k in range(1, N_DEV):
            got = buf.at[me ^ k]
            pltpu.make_async_remote_copy(got, got, ssem.at[k - 1], rsem.at[k - 1], device_id=(x, y, c), device_id_type=MESH).wait_recv()
        for cp in cps:
            cp.wait_send()
        tot = buf[0]
        for d in range(1, N_DEV):
            tot = tot + buf[d]
        out[...] = tot

    vm = pl.BlockSpec(memory_space=pltpu.VMEM)
    return pl.pallas_call(
        body, name="allreduce_small", in_specs=[vm], out_specs=vm,
        out_shape=jax.ShapeDtypeStruct(block.shape, F32),
        scratch_shapes=[pltpu.VMEM((N_DEV,) + block.shape, F32), pltpu.SemaphoreType.DMA((N_DEV - 1,)),
                        pltpu.SemaphoreType.DMA((N_DEV - 1,))],
        compiler_params=pltpu.CompilerParams(has_side_effects=True),
    )(block)


def _sum_chips(landed, own, name):
    nq, k, n = landed.shape
    tr = min(256, k)
    me = (2 * lax.axis_index("x") + lax.axis_index("y")).astype(jnp.int32).reshape(1)

    def body(me_ref, p0, p1, p2, p3, own_ref, o_ref):
        mine = own_ref[0]
        terms = [jnp.where(me_ref[0] == q, mine, p[0].astype(F32)) for q, p in enumerate((p0, p1, p2, p3))]
        o_ref[...] = ((terms[0] + terms[1]) + terms[2]) + terms[3]

    spec = lambda q: pl.BlockSpec((1, tr, n), lambda i, m, q=q: (q, i, 0))
    return pl.pallas_call(
        body, name=name,
        grid_spec=pltpu.PrefetchScalarGridSpec(
            num_scalar_prefetch=1, grid=(k // tr,),
            in_specs=[spec(q) for q in range(nq)] + [pl.BlockSpec((1, tr, n), lambda i, m: (m[0], i, 0))],
            out_specs=pl.BlockSpec((tr, n), lambda i, m: (i, 0))),
        out_shape=jax.ShapeDtypeStruct((k, n), F32),
        compiler_params=_cparams(("arbitrary",)),
    )(me, landed, landed, landed, landed, own)


def _adamw(w, g, g2, m, v, name):
    shape = w.shape
    cols = shape[-1]
    flat = lambda t: t.reshape(-1, cols)
    rows = flat(w).shape[0]
    tr = min(512, rows)
    pair = g2 is not None

    def body(*refs):
        if pair:
            w_ref, g_ref, g2_ref, m_ref, v_ref, go_ref, d_ref, nm_ref, nv_ref = refs
            gr = g_ref[...] + g2_ref[...]
        else:
            w_ref, g_ref, m_ref, v_ref, go_ref, d_ref, nm_ref, nv_ref = refs
            gr = g_ref[...]
        go_ref[...] = gr
        nm = ADAM_B1 * m_ref[...] + (1.0 - ADAM_B1) * gr
        nv = ADAM_B2 * v_ref[...] + (1.0 - ADAM_B2) * (gr * gr)
        m_hat = nm / (1.0 - ADAM_B1 ** ADAM_STEP)
        v_hat = nv / (1.0 - ADAM_B2 ** ADAM_STEP)
        d_ref[...] = -ADAM_LR * (m_hat / (jnp.sqrt(v_hat) + ADAM_EPS) + ADAM_WD * w_ref[...])
        nm_ref[...] = nm
        nv_ref[...] = nv

    spec = pl.BlockSpec((tr, cols), lambda i: (i, 0))
    ins = [w, g] + ([g2] if pair else []) + [m, v]
    outs = pl.pallas_call(
        body, name=name, grid=(rows // tr,),
        in_specs=[spec] * len(ins), out_specs=[spec] * 4,
        out_shape=[jax.ShapeDtypeStruct((rows, cols), F32)] * 4,
        compiler_params=_cparams(("arbitrary",)),
    )(*[flat(t) for t in ins])
    return [o.reshape(shape) for o in outs]


def _pack_layer(big, l, names=BIG):
    rows = dict(PACK_ROWS)
    return jnp.concatenate([big[name][l].astype(BF16).reshape(rows[name], 1024) for name in names], axis=0)


def _whole_weights(gathered, names=BIG):
    out, at = {}, 0
    for name in names:
        rows = dict(PACK_ROWS)[name]
        t = gathered[:, at:at + rows, :].reshape((N_CHIPS,) + SHARD_SHAPES[name])
        at += rows
        if name in ROW_SHARDED:
            out[name] = t.reshape(N_CHIPS * t.shape[1], t.shape[2])
        elif name == "w_up":
            out[name] = t
        else:
            out[name] = jnp.moveaxis(t, 0, 1).reshape(t.shape[1], N_CHIPS * t.shape[2])
    return out


def _pack_small(d):
    flat = jnp.concatenate([d[n].reshape(-1) for n in SMALL_NAMES])
    return jnp.pad(flat, (0, SMALL_ROWS * LANES - flat.shape[0])).reshape(SMALL_ROWS, LANES)


def _unpack_small(block, like):
    flat, out, at = block.reshape(-1), {}, 0
    for n in SMALL_NAMES:
        size = like[n].size
        out[n] = flat[at:at + size].reshape(like[n].shape)
        at += size
    return out


def _rope_tables(s):
    inv_freq = 1.0 / (ROPE_THETA ** (jnp.arange(0, HEAD_DIM, 2, dtype=F32) / HEAD_DIM))
    ang = jnp.arange(s, dtype=F32)[:, None] * inv_freq[None, :]
    reps = LANES // (HEAD_DIM // 2)
    return jnp.tile(jnp.cos(ang), (1, reps)), jnp.tile(jnp.sin(ang), (1, reps))


def _forward_backward(x, target, big, small):
    s = x.shape[0]
    cos, sin = _rope_tables(s)
    two = lambda gvec: jnp.tile(gvec.reshape(1, HEAD_DIM), (1, 2))
    saved = []
    wts = _whole_weights(_gather_weights(_pack_layer(big, 0, BIG[:1])), BIG[:1])
    for l in range(DEPTH):
        gm = small["mix_norm_g"][l].reshape(1, D_MODEL)
        gl = small["mlp_norm_g"][l].reshape(1, D_MODEL)
        qg, kg = two(small["q_norm_g"][l]), two(small["k_norm_g"][l])
        h, sbqkv, raw, qn, kn, v, gates = _inproj_fwd(x, gm, wts["w_in"], qg, kg, cos, sin)
        shards = ([_pack_layer(big, 0, BIG[1:])] if l == 0 else []) + ([_pack_layer(big, l + 1)] if l + 1 < DEPTH else [])
        osb, *landed = _sb_fwd(sbqkv, shards)
        if l == 0:
            wts = {**wts, **_whole_weights(landed.pop(0), BIG[1:])}
        nxt = _whole_weights(landed[0]) if landed else None
        oswa = _swa_fwd(qn, kn, v, small["sinks"][l])
        x1 = _merge_fwd(x, osb, oswa, gates, wts["w_branch_sb"], wts["w_branch_swa"], wts["w_out"])
        x2, u = _mlp_fwd(x1, gl, wts["w_up"], wts["w_down"])
        saved.append((x, h, sbqkv, raw, qn, kn, v, gates, osb, oswa, x1, u, gm, gl, qg, kg, wts))
        x, wts = x2, nxt
    dx, loss = _loss_grad(x, target)

    got = {name: [None] * DEPTH for name in BIG}
    gsmall = {name: [None] * DEPTH for name in SMALL_NAMES}
    late = []
    for l in reversed(range(DEPTH)):
        x0, h, sbqkv, raw, qn, kn, v, gates, osb, oswa, x1, u, gm, gl, qg, kg, wts = saved[l]
        dx1, du, act, h2, dgl_mlp = _mlp_bwd(dx, x1, u, gl, wts["w_up"], wts["w_down"])
        dosb, doswa, dgate, merged, dysb, dyswa = _merge_bwd(
            dx1, osb, oswa, gates, wts["w_branch_sb"], wts["w_branch_swa"], wts["w_out"])
        ready = [("w_down", l, *_wgrad(act, dx, "wgrad_down", shard_axis=0)),
                 ("w_up", l, *_wgrad(h2, du, "wgrad_up", shard_axis=1)),
                 ("w_out", l, *_wgrad(merged, dx1, "wgrad_out", shard_axis=0)),
                 ("w_branch_sb", l, *_wgrad(osb, dysb, "wgrad_bsb", shard_axis=1)),
                 ("w_branch_swa", l, *_wgrad(oswa, dyswa, "wgrad_bswa", shard_axis=1))] + late
        dsbq, dsbk, dsbv, *landed = _sb_bwd(sbqkv, osb, dosb, [narrow for _, _, _, narrow in ready])
        for (name, layer, own, _), arr in zip(ready, landed):
            got[name][layer] = (arr, own)
        dqn, dkn, dswv, dsink = _swa_bwd(qn, kn, v, small["sinks"][l], doswa)
        dswqk, dgqk = _swa_post(raw, dqn, dkn, qg, kg, cos, sin)
        dx, dproj, dg_mix = _inproj_bwd(dsbq, dsbk, dsbv, dswqk, dswv, dgate, x0, dx1, gm, wts["w_in"])
        dwin = _wgrad(h, dproj, "wgrad_in")
        dwin = dwin.reshape(D_MODEL, N_CHIPS, IN_WIDTH // N_CHIPS).transpose(1, 0, 2)
        late = [("w_in", l, dwin, dwin.astype(BF16))]
        gsmall["mix_norm_g"][l] = dg_mix[0]
        gsmall["mlp_norm_g"][l] = dgl_mlp[0]
        gsmall["q_norm_g"][l] = dgqk[0, :HEAD_DIM] + dgqk[0, HEAD_DIM:]
        gsmall["k_norm_g"][l] = dgqk[1, :HEAD_DIM] + dgqk[1, HEAD_DIM:]
        gsmall["sinks"][l] = jnp.sum(dsink[:, :, 0], axis=0)
    got["w_in"][0] = (_exchange([late[0][3]])[0], late[0][2])
    gsmall = {k: jnp.stack(vs) for k, vs in gsmall.items()}
    return loss, dx, got, gsmall


def kernel(x, mix_norm_g, w_in, q_norm_g, k_norm_g, sinks, w_branch_sb, w_branch_swa, w_out, mlp_norm_g, w_up, w_down, loss_target, m_mix_norm_g, m_w_in, m_q_norm_g, m_k_norm_g, m_sinks, m_w_branch_sb, m_w_branch_swa, m_w_out, m_mlp_norm_g, m_w_up, m_w_down, v_mix_norm_g, v_w_in, v_q_norm_g, v_k_norm_g, v_sinks, v_w_branch_sb, v_w_branch_swa, v_w_out, v_mlp_norm_g, v_w_up, v_w_down):
    big = dict(w_in=w_in, w_branch_sb=w_branch_sb, w_branch_swa=w_branch_swa, w_out=w_out, w_up=w_up, w_down=w_down)
    big_m = dict(w_in=m_w_in, w_branch_sb=m_w_branch_sb, w_branch_swa=m_w_branch_swa, w_out=m_w_out, w_up=m_w_up, w_down=m_w_down)
    big_v = dict(w_in=v_w_in, w_branch_sb=v_w_branch_sb, w_branch_swa=v_w_branch_swa, w_out=v_w_out, w_up=v_w_up, w_down=v_w_down)
    small = dict(mix_norm_g=mix_norm_g, q_norm_g=q_norm_g, k_norm_g=k_norm_g, sinks=sinks, mlp_norm_g=mlp_norm_g)
    small_m = dict(mix_norm_g=m_mix_norm_g, q_norm_g=m_q_norm_g, k_norm_g=m_k_norm_g, sinks=m_sinks, mlp_norm_g=m_mlp_norm_g)
    small_v = dict(mix_norm_g=v_mix_norm_g, q_norm_g=v_q_norm_g, k_norm_g=v_k_norm_g, sinks=v_sinks, mlp_norm_g=v_mlp_norm_g)

    loss_part, grad_x, got, gsmall = _forward_backward(x[0], loss_target[0], big, small)
    loss = lax.psum(loss_part, ("x", "y", "c"))

    mine = [jnp.stack([_sum_chips(*got[name][l], "sum_" + name) for l in range(DEPTH)]) for name in BIG]
    theirs = _pair_swap(mine)
    upd = {name: _adamw(big[name], mine[i], theirs[i], big_m[name], big_v[name], "adamw_" + name)
           for i, name in enumerate(BIG)}
    g_small = _allreduce_small(_pack_small(gsmall))
    sm = _adamw(_pack_small(small), g_small, None, _pack_small(small_m), _pack_small(small_v), "adamw_small")
    upd_small = [_unpack_small(t, small) for t in sm]

    names = ("mix_norm_g", "w_in", "q_norm_g", "k_norm_g", "sinks", "w_branch_sb", "w_branch_swa", "w_out",
             "mlp_norm_g", "w_up", "w_down")
    pick = lambda n, i: upd[n][i] if n in upd else upd_small[i][n]
    return (loss, grad_x[None], *[pick(n, 0) for n in names], *[pick(n, 1) for n in names],
            *[pick(n, 2) for n in names], *[pick(n, 3) for n in names])
```

```python
import jax
import jax.numpy as jnp
from jax import lax
from jax.experimental import pallas as pl
from jax.experimental.pallas import tpu as pltpu

F32 = jnp.float32
BF16 = jnp.bfloat16

D_MODEL = 1024
DEPTH = 4
HEAD_DIM = 64
SB_WIDTH = 512
SWA_Q_WIDTH = 512
SWA_KV_WIDTH = 128
D_FF = 4096
IN_WIDTH = 4352
GATE_OFF = 2304
ROPE_THETA = 10000.0
NORM_EPS = 1e-6
SCALE = HEAD_DIM ** -0.5
N_CHIPS = 4
N_DEV = 8

ADAM_LR = 0.001
ADAM_B1 = 0.9
ADAM_B2 = 0.999
ADAM_EPS = 1e-08
ADAM_WD = 0.01
ADAM_STEP = 10

LANES = 128
SB_TILE = 256
SB_STEP_WIDTH = 256
SWA_BLOCK = 128
VMEM_LIMIT = 56 << 20

NN = (((1,), (0,)), ((), ()))
NT = (((1,), (1,)), ((), ()))
TN = (((0,), (0,)), ((), ()))
MESH = pl.DeviceIdType.MESH
HBM = pl.BlockSpec(memory_space=pl.ANY)

BIG = ("w_in", "w_branch_sb", "w_branch_swa", "w_out", "w_up", "w_down")
SMALL_NAMES = ("mix_norm_g", "q_norm_g", "k_norm_g", "sinks", "mlp_norm_g")
PACK_ROWS = (("w_in", 1088), ("w_branch_sb", 128), ("w_branch_swa", 128), ("w_out", 256), ("w_up", 1024), ("w_down", 1024))
SHARD_SHAPES = {"w_in": (1024, 1088), "w_branch_sb": (512, 256), "w_branch_swa": (512, 256), "w_out": (256, 1024),
                "w_up": (1024, 1024), "w_down": (1024, 1024)}
ROW_SHARDED = ("w_out", "w_down")
SMALL_ROWS = 72


def _dot(a, b, dims):
    return lax.dot_general(a, b, dims, preferred_element_type=F32)


def _cparams(sem):
    return pltpu.CompilerParams(dimension_semantics=sem, vmem_limit_bytes=VMEM_LIMIT)


def _resident(shape):
    nd = len(shape)
    return pl.BlockSpec(shape, lambda *_: (0,) * nd, pipeline_mode=pl.Buffered(1))


def _rows(tm, width):
    return pl.BlockSpec((tm, width), lambda i: (i, 0))


def _rms(xf):
    rstd = lax.rsqrt(jnp.mean(xf * xf, axis=-1, keepdims=True) + NORM_EPS)
    return xf * rstd, rstd


def _rms_bwd(dh, xn, rstd, g):
    dxn = dh * g
    return rstd * (dxn - xn * jnp.mean(dxn * xn, axis=-1, keepdims=True))


def _lane(shape):
    return lax.broadcasted_iota(jnp.int32, shape, len(shape) - 1)


def _head_mean(v, left):
    sl = jnp.sum(jnp.where(left, v, 0.0), axis=-1, keepdims=True)
    sr = jnp.sum(jnp.where(left, 0.0, v), axis=-1, keepdims=True)
    return jnp.where(left, sl, sr) * (1.0 / HEAD_DIM)


def _rope(y, cs, sn, first):
    up = pltpu.roll(y, 96, 1)
    dn = pltpu.roll(y, 32, 1)
    return y * cs + jnp.where(first, -up, dn) * sn


def _rope_t(d, cs, sn, first):
    t = d * jnp.where(first, -sn, sn)
    return d * cs + jnp.where(first, pltpu.roll(t, 96, 1), pltpu.roll(t, 32, 1))


def _norm_rope(p, g, cs, sn):
    lane = _lane((1, LANES))
    left = lane < HEAD_DIM
    first = (lane % HEAD_DIM) < (HEAD_DIM // 2)
    yn = p * lax.rsqrt(_head_mean(p * p, left) + NORM_EPS)
    return _rope(yn * g, cs, sn, first)


def _norm_rope_bwd(p, dout, g, cs, sn):
    lane = _lane((1, LANES))
    left = lane < HEAD_DIM
    first = (lane % HEAD_DIM) < (HEAD_DIM // 2)
    rstd = lax.rsqrt(_head_mean(p * p, left) + NORM_EPS)
    yn = p * rstd
    dyg = _rope_t(dout, cs, sn, first)
    dg = jnp.sum(dyg * yn, axis=0, keepdims=True)
    dyn = dyg * g
    dp = rstd * (dyn - yn * _head_mean(dyn * yn, left))
    return dp, dg


def _place():
    x, y, c = lax.axis_index("x"), lax.axis_index("y"), lax.axis_index("c")
    return x, y, c, [(1 - x, y), (x, 1 - y), (1 - x, 1 - y)]


def _hosted_gather(src, out, ssem, rsem, lsem, first, mid, last, slot=0):
    x, y, c, chips = _place()
    me = 2 * x + y
    sib = (x, y, 1 - c)
    r2 = src.shape[0] // 2
    base = 6 * slot

    def slab(chip, half):
        return out.at[chip, pl.ds(half * r2, r2)]

    def ici(j):
        cx, cy = chips[j]
        return pltpu.make_async_remote_copy(src.at[pl.ds(c * r2, r2)], slab(me, c), ssem.at[base + j], rsem.at[base + j],
                                            device_id=(cx, cy, c), device_id_type=MESH)

    def landed(j):
        got = slab(2 * chips[j][0] + chips[j][1], c)
        return pltpu.make_async_remote_copy(got, got, ssem.at[base + j], rsem.at[base + j], device_id=sib, device_id_type=MESH)

    def d2d(j, half):
        got = slab(2 * chips[j][0] + chips[j][1], half)
        return pltpu.make_async_remote_copy(got, got, ssem.at[base + 3 + j], rsem.at[base + 3 + j], device_id=sib,
                                            device_id_type=MESH)

    local = pltpu.make_async_copy(src, out.at[me], lsem.at[slot])

    @pl.when(first)
    def _():
        local.start()
        for j in range(3):
            ici(j).start()

    @pl.when(mid)
    def _():
        for j in range(3):
            landed(j).wait_recv()
            d2d(j, c).start()

    @pl.when(last)
    def _():
        for j in range(3):
            d2d(j, 1 - c).wait_recv()
        for j in range(3):
            ici(j).wait_send()
            d2d(j, c).wait_send()
        local.wait()


def _hosted_exchange(srcs, outs, ssem, rsem, first, last):
    x, y, c, chips = _place()
    me = 2 * x + y

    def send(t, j):
        cx, cy = chips[j]
        return pltpu.make_async_remote_copy(srcs[t].at[2 * cx + cy], outs[t].at[me], ssem.at[3 * t + j], rsem.at[3 * t + j],
                                            device_id=(cx, cy, c), device_id_type=MESH)

    def landed(t, j):
        cx, cy = chips[j]
        got = outs[t].at[2 * cx + cy]
        return pltpu.make_async_remote_copy(got, got, ssem.at[3 * t + j], rsem.at[3 * t + j],
                                            device_id=(cx, cy, c), device_id_type=MESH)

    @pl.when(first)
    def _():
        for t in range(len(srcs)):
            for j in range(3):
                send(t, j).start()

    @pl.when(last)
    def _():
        for t in range(len(srcs)):
            for j in range(3):
                landed(t, j).wait_recv()
        for t in range(len(srcs)):
            for j in range(3):
                send(t, j).wait_send()


def _gather_scratch(n):
    return [pltpu.SemaphoreType.DMA((6 * n,)), pltpu.SemaphoreType.DMA((6 * n,)), pltpu.SemaphoreType.DMA((n,))]


def _exchange_scratch(n):
    return [pltpu.SemaphoreType.DMA((3 * n,)), pltpu.SemaphoreType.DMA((3 * n,))]


def _inproj_fwd(x, g, w, qg, kg, cos, sin):
    s = x.shape[0]
    tm = min(512, s)

    def body(x_ref, g_ref, w_ref, qg_ref, kg_ref, cos_ref, sin_ref,
             h_ref, sb_ref, raw_ref, qn_ref, kn_ref, v_ref, gate_ref):
        xn, _ = _rms(x_ref[...])
        h = (xn * g_ref[...]).astype(BF16)
        h_ref[...] = h
        for a in range(0, 3 * SB_WIDTH, 512):
            sb_ref[:, a:a + 512] = _dot(h, w_ref[:, a:a + 512], NN).astype(BF16)
        cs, sn = cos_ref[...], sin_ref[...]
        q0 = 3 * SB_WIDTH
        pq = _dot(h, w_ref[:, q0:q0 + SWA_Q_WIDTH], NN)
        raw_ref[:, 0:SWA_Q_WIDTH] = pq
        for b in range(SWA_Q_WIDTH // LANES):
            blk = pq[:, b * LANES:(b + 1) * LANES]
            qn_ref[:, b * LANES:(b + 1) * LANES] = _norm_rope(blk, qg_ref[...], cs, sn).astype(BF16)
        k0 = q0 + SWA_Q_WIDTH
        pk = _dot(h, w_ref[:, k0:k0 + 2 * SWA_KV_WIDTH], NN)
        raw_ref[:, SWA_Q_WIDTH:SWA_Q_WIDTH + SWA_KV_WIDTH] = pk[:, :SWA_KV_WIDTH]
        kn_ref[...] = _norm_rope(pk[:, :SWA_KV_WIDTH], kg_ref[...], cs, sn).astype(BF16)
        v_ref[...] = pk[:, SWA_KV_WIDTH:].astype(BF16)
        for a in range(GATE_OFF, IN_WIDTH, 512):
            gate_ref[:, a - GATE_OFF:a - GATE_OFF + 512] = jax.nn.sigmoid(_dot(h, w_ref[:, a:a + 512], NN))

    return pl.pallas_call(
        body, name="inproj_fwd", grid=(s // tm,),
        in_specs=[_rows(tm, D_MODEL), _resident((1, D_MODEL)), _resident((D_MODEL, IN_WIDTH)),
                  _resident((1, LANES)), _resident((1, LANES)), _rows(tm, LANES), _rows(tm, LANES)],
        out_specs=[_rows(tm, D_MODEL), _rows(tm, 3 * SB_WIDTH), _rows(tm, 640), _rows(tm, SWA_Q_WIDTH),
                   _rows(tm, SWA_KV_WIDTH), _rows(tm, SWA_KV_WIDTH), _rows(tm, 2 * D_MODEL)],
        out_shape=[jax.ShapeDtypeStruct((s, D_MODEL), BF16), jax.ShapeDtypeStruct((s, 3 * SB_WIDTH), BF16),
                   jax.ShapeDtypeStruct((s, 640), F32), jax.ShapeDtypeStruct((s, SWA_Q_WIDTH), BF16),
                   jax.ShapeDtypeStruct((s, SWA_KV_WIDTH), BF16), jax.ShapeDtypeStruct((s, SWA_KV_WIDTH), BF16),
                   jax.ShapeDtypeStruct((s, 2 * D_MODEL), F32)],
        compiler_params=_cparams(("arbitrary",)),
    )(x, g, w, qg, kg, cos, sin)


def _sb_tile_terms(qh, k, strict):
    z = _dot(qh, k, NT)
    nz = -z
    soft = jnp.log(1.0 + jnp.exp(jnp.minimum(z, nz)))
    log_keep = jnp.minimum(nz, 0.0) - soft
    if strict is not None:
        log_keep = jnp.where(strict, log_keep, 0.0)
    return z, log_keep


def _log_weights(z, log_keep, run, u_from):
    tail = run + _dot(log_keep.astype(BF16), u_from, NN)
    return z + tail, tail[:, 0:1]


def _masked(strict, val):
    return val if strict is None else jnp.where(strict, val, 0.0)


def _tri_twice(t):
    r = lax.broadcasted_iota(jnp.int32, (2 * t, t), 0)
    c = lax.broadcasted_iota(jnp.int32, (2 * t, t), 1)
    return (jnp.where(r >= t, r - t, r) >= c).astype(BF16)


def _split_dot(a, u2):
    hi = a.astype(BF16)
    lo = (a - hi.astype(F32)).astype(BF16)
    return _dot(jnp.concatenate([hi, lo], axis=1), u2, NN)


def _sb_fwd(sbqkv, shards=()):
    s = sbqkv.shape[0]
    t = min(SB_TILE, s)
    nq = s // t
    ng = SB_WIDTH // SB_STEP_WIDTH
    heads = range(2 * SB_STEP_WIDTH // LANES)
    nsh = len(shards)

    def body(*refs):
        q_ref, k_ref, v_ref = refs[:3]
        o_ref = refs[3 + nsh]
        j, i = pl.program_id(0), pl.program_id(1)
        for n in range(nsh):
            _hosted_gather(refs[3 + n], refs[4 + nsh + n], *refs[4 + 2 * nsh:], (j == 0) & (i == 0),
                           (j == ng - 1) & (i == (3 * nq) // 4), (j == ng - 1) & (i == nq - 1), slot=n)
        left = _lane((1, LANES)) < HEAD_DIM
        halves = (left, jnp.logical_not(left))
        pair = lambda a, p: a[:, (p // 2) * LANES:(p // 2 + 1) * LANES]
        only = lambda a, p: jnp.where(halves[p % 2], pair(a, p), jnp.zeros((), a.dtype))
        q_all = q_ref[...] * jnp.asarray(SCALE, BF16)
        qh = [only(q_all, p) for p in heads]
        r = lax.broadcasted_iota(jnp.int32, (t, t), 0)
        c = lax.broadcasted_iota(jnp.int32, (t, t), 1)
        u_from = (r >= c).astype(BF16)

        def tile(n, carry, strict):
            runs, accs = carry
            off = pl.multiple_of((i - n) * t, t)
            k = k_ref[pl.ds(off, t), :]
            v = v_ref[pl.ds(off, t), :]
            terms = [_sb_tile_terms(qh[p], pair(k, p), strict) for p in heads]
            logw = [_log_weights(*terms[p], runs[p], u_from) for p in heads]
            ws = [_masked(strict, jnp.exp(logw[p][0])).astype(BF16) for p in heads]
            accs = list(accs)
            for p in heads:
                accs[p // 2] = accs[p // 2] + _dot(ws[p], only(v, p), NN)
            return tuple(logw[p][1] for p in heads), tuple(accs)

        zero = jnp.zeros((t, 1), F32)
        wide = jnp.zeros((t, LANES), F32)
        carry = tile(0, (tuple(zero for _ in heads), tuple(wide for _ in heads[::2])), c < r)
        _, accs = lax.fori_loop(1, i + 1, lambda n, cr: tile(n, cr, None), carry)
        for g, acc in enumerate(accs):
            o_ref[:, g * LANES:(g + 1) * LANES] = acc

    w = SB_STEP_WIDTH
    blk = pl.BlockSpec((t, w), lambda j, i: (i, j))
    in_specs = [blk, pl.BlockSpec((s, w), lambda j, i: (0, ng + j)), pl.BlockSpec((s, w), lambda j, i: (0, 2 * ng + j))]
    out_specs = [blk]
    out_shape = [jax.ShapeDtypeStruct((s, SB_WIDTH), F32)]
    in_specs += [HBM] * nsh
    out_specs += [HBM] * nsh
    out_shape += [jax.ShapeDtypeStruct((N_CHIPS,) + a.shape, a.dtype) for a in shards]
    return pl.pallas_call(
        body, name="sb_fwd_g%d" % nsh, grid=(ng, nq),
        in_specs=in_specs, out_specs=out_specs, out_shape=out_shape,
        scratch_shapes=_gather_scratch(nsh) if nsh else [],
        compiler_params=_cparams(("arbitrary", "arbitrary")),
    )(sbqkv, sbqkv, sbqkv, *shards)


def _swa_band(cur_ref, prev_ref, b):
    lo = prev_ref[...] if b == 0 else cur_ref[(b - 1) * SWA_BLOCK:b * SWA_BLOCK, :]
    return jnp.concatenate([lo, cur_ref[b * SWA_BLOCK:(b + 1) * SWA_BLOCK, :]], axis=0)


def _swa_variants(band, left):
    f = band.astype(F32)
    sw = pltpu.roll(f, HEAD_DIM, 1)
    halves = (left, jnp.logical_not(left))
    return [[jnp.where(halves[p], f if p == g else sw, 0.0).astype(BF16) for p in range(2)] for g in range(2)]


def _swa_valid(blk):
    ii = lax.broadcasted_iota(jnp.int32, (SWA_BLOCK, 2 * SWA_BLOCK), 0)
    jj = lax.broadcasted_iota(jnp.int32, (SWA_BLOCK, 2 * SWA_BLOCK), 1)
    rel = jj - SWA_BLOCK - ii
    return (rel <= 0) & (rel > -SWA_BLOCK) & (jj + blk * SWA_BLOCK >= SWA_BLOCK)


def _swa_softmax(dots, sink, valid):
    sc = jnp.where(valid, dots * SCALE, -1e30)
    m = jnp.maximum(jnp.max(sc, axis=-1, keepdims=True), sink)
    e = jnp.exp(sc - m)
    es = jnp.exp(sink - m)
    inv = 1.0 / (jnp.sum(e, axis=-1, keepdims=True) + es)
    return e * inv, es * inv


def _swa_heads(q_ref, rows, halves):
    nheads = SWA_Q_WIDTH // HEAD_DIM
    out = []
    for h in range(nheads):
        hb, p = h // 2, h % 2
        q2 = q_ref[rows, hb * LANES:(hb + 1) * LANES]
        out.append((hb, p, h // (nheads // 2), jnp.where(halves[p], q2, jnp.zeros_like(q2))))
    return out


def _swa_specs(s, t):
    nb = t // SWA_BLOCK
    cur = pl.BlockSpec((t, LANES), lambda i: (i, 0))
    prev = pl.BlockSpec((SWA_BLOCK, LANES), lambda i: (jnp.maximum(i * nb - 1, 0), 0))
    return cur, prev


def _swa_fwd(qn, kn, v, sinks):
    s = qn.shape[0]
    t = min(512, s)
    nb = t // SWA_BLOCK
    nheads = SWA_Q_WIDTH // HEAD_DIM

    def body(sink_ref, q_ref, kc_ref, kp_ref, vc_ref, vp_ref, o_ref):
        i = pl.program_id(0)
        left = _lane((1, LANES)) < HEAD_DIM
        halves = (left, jnp.logical_not(left))
        for b in range(nb):
            kvar = _swa_variants(_swa_band(kc_ref, kp_ref, b), left)
            vvar = _swa_variants(_swa_band(vc_ref, vp_ref, b), left)
            rows = slice(b * SWA_BLOCK, (b + 1) * SWA_BLOCK)
            valid = _swa_valid(i * nb + b)
            heads = _swa_heads(q_ref, rows, halves)
            dots = [_dot(qh, kvar[g][p], NT) for _, p, g, qh in heads]
            probs = [_swa_softmax(dots[h], sink_ref[h], valid)[0].astype(BF16) for h in range(nheads)]
            outs = [_dot(probs[h], vvar[g][p], NN) for h, (_, p, g, _) in enumerate(heads)]
            for hb in range(nheads // 2):
                o_ref[rows, hb * LANES:(hb + 1) * LANES] = (outs[2 * hb] + outs[2 * hb + 1]).astype(BF16)

    cur, prev = _swa_specs(s, t)
    return pl.pallas_call(
        body, name="swa_fwd", grid=(s // t,),
        in_specs=[pl.BlockSpec(memory_space=pltpu.SMEM), _rows(t, SWA_Q_WIDTH), cur, prev, cur, prev],
        out_specs=_rows(t, SWA_Q_WIDTH),
        out_shape=jax.ShapeDtypeStruct((s, SWA_Q_WIDTH), BF16),
        compiler_params=_cparams(("arbitrary",)),
    )(sinks, qn, kn, kn, v, v)


def _merge_fwd(x, osb, oswa, gates, wsb, wswa, wout):
    s = x.shape[0]
    tm = min(512, s)

    def body(x_ref, osb_ref, oswa_ref, gate_ref, wsb_ref, wswa_ref, wout_ref, x1_ref):
        ysb = _dot(osb_ref[...].astype(BF16), wsb_ref[...], NN)
        yswa = _dot(oswa_ref[...], wswa_ref[...], NN)
        merged = gate_ref[:, :D_MODEL] * ysb + gate_ref[:, D_MODEL:] * yswa
        x1_ref[...] = x_ref[...] + _dot(merged.astype(BF16), wout_ref[...], NN)

    return pl.pallas_call(
        body, name="merge_fwd", grid=(s // tm,),
        in_specs=[_rows(tm, D_MODEL), _rows(tm, SB_WIDTH), _rows(tm, SWA_Q_WIDTH), _rows(tm, 2 * D_MODEL),
                  _resident((SB_WIDTH, D_MODEL)), _resident((SWA_Q_WIDTH, D_MODEL)), _resident((D_MODEL, D_MODEL))],
        out_specs=_rows(tm, D_MODEL),
        out_shape=jax.ShapeDtypeStruct((s, D_MODEL), F32),
        compiler_params=_cparams(("arbitrary",)),
    )(x, osb, oswa, gates, wsb, wswa, wout)


def _mlp_fwd(x1, g, wup, wdown):
    s = x1.shape[0]
    tm = min(512, s)
    fc = D_FF // N_CHIPS

    def body(x_ref, g_ref, wup_ref, wdown_ref, x2_ref, u_ref):
        xf = x_ref[...]
        xn, _ = _rms(xf)
        h2 = (xn * g_ref[...]).astype(BF16)
        acc = xf
        for q in range(N_CHIPS):
            u = _dot(h2, wup_ref[q], NN)
            u_ref[:, q * fc:(q + 1) * fc] = u
            r = jnp.maximum(u, 0.0)
            acc = acc + _dot((r * r).astype(BF16), wdown_ref[q * fc:(q + 1) * fc, :], NN)
        x2_ref[...] = acc

    return pl.pallas_call(
        body, name="mlp_fwd", grid=(s // tm,),
        in_specs=[_rows(tm, D_MODEL), _resident((1, D_MODEL)), _resident((N_CHIPS, D_MODEL, fc)), _resident((D_FF, D_MODEL))],
        out_specs=[_rows(tm, D_MODEL), _rows(tm, D_FF)],
        out_shape=[jax.ShapeDtypeStruct((s, D_MODEL), F32), jax.ShapeDtypeStruct((s, D_FF), F32)],
        compiler_params=_cparams(("arbitrary",)),
    )(x1, g, wup, wdown)


def _loss_grad(y, target):
    s = y.shape[0]
    tm = min(512, s)

    def body(y_ref, t_ref, dy_ref, part_ref):
        err = y_ref[...] - t_ref[...]
        dy_ref[...] = err * (1.0 / D_MODEL)
        tot = jnp.sum(jnp.sum(err * err, axis=-1, keepdims=True), axis=0, keepdims=True)
        part_ref[...] = jnp.broadcast_to(tot.reshape(1, 1, 1), (1, 8, LANES))

    dy, part = pl.pallas_call(
        body, name="loss_grad", grid=(s // tm,),
        in_specs=[_rows(tm, D_MODEL), _rows(tm, D_MODEL)],
        out_specs=[_rows(tm, D_MODEL), pl.BlockSpec((1, 8, LANES), lambda i: (i, 0, 0))],
        out_shape=[jax.ShapeDtypeStruct((s, D_MODEL), F32), jax.ShapeDtypeStruct((s // tm, 8, LANES), F32)],
        compiler_params=_cparams(("arbitrary",)),
    )(y, target)
    return dy, (0.5 / D_MODEL) * jnp.sum(part[:, 0, 0])


def _mlp_bwd(dx2, x1, u, g, wup, wdown):
    s = x1.shape[0]
    tm = min(256, s)
    fc = D_FF // N_CHIPS

    def body(dx2_ref, x_ref, u_ref, g_ref, wup_ref, wdown_ref, dx1_ref, du_ref, a_ref, h2_ref, dg_ref, dxb_ref):
        @pl.when(pl.program_id(0) == 0)
        def _():
            dg_ref[...] = jnp.zeros_like(dg_ref)

        gam = g_ref[...]
        xn, rstd = _rms(x_ref[...])
        h2_ref[...] = (xn * gam).astype(BF16)
        dxf = dx2_ref[...]
        dxb = dxf.astype(BF16)
        dxb_ref[...] = dxb
        dh2 = jnp.zeros((tm, D_MODEL), F32)
        for q in range(N_CHIPS):
            cols = slice(q * fc, (q + 1) * fc)
            da = _dot(dxb, wdown_ref[cols, :], NT)
            r = jnp.maximum(u_ref[:, cols], 0.0)
            a_ref[:, cols] = (r * r).astype(BF16)
            du = (da * (2.0 * r)).astype(BF16)
            du_ref[:, cols] = du
            dh2 = dh2 + _dot(du, wup_ref[q], NT)
        dg_ref[...] += jnp.sum(dh2 * xn, axis=0, keepdims=True)
        dx1_ref[...] = dxf + _rms_bwd(dh2, xn, rstd, gam)

    return pl.pallas_call(
        body, name="mlp_bwd", grid=(s // tm,),
        in_specs=[_rows(tm, D_MODEL), _rows(tm, D_MODEL), _rows(tm, D_FF), _resident((1, D_MODEL)),
                  _resident((N_CHIPS, D_MODEL, fc)), _resident((D_FF, D_MODEL))],
        out_specs=[_rows(tm, D_MODEL), _rows(tm, D_FF), _rows(tm, D_FF), _rows(tm, D_MODEL),
                   pl.BlockSpec((1, D_MODEL), lambda i: (0, 0)), _rows(tm, D_MODEL)],
        out_shape=[jax.ShapeDtypeStruct((s, D_MODEL), F32), jax.ShapeDtypeStruct((s, D_FF), BF16),
                   jax.ShapeDtypeStruct((s, D_FF), BF16), jax.ShapeDtypeStruct((s, D_MODEL), BF16),
                   jax.ShapeDtypeStruct((1, D_MODEL), F32), jax.ShapeDtypeStruct((s, D_MODEL), BF16)],
        compiler_params=_cparams(("arbitrary",)),
    )(dx2, x1, u, g, wup, wdown)


def _wgrad(a, b, name, shard_axis=None):
    s, m = a.shape
    n = b.shape[1]
    tm = min(512, m if shard_axis != 0 else m // N_CHIPS)
    tn = min(512, n if shard_axis != 1 else n // N_CHIPS)
    if shard_axis is None and n % tn:
        tn = 256

    def body(a_ref, b_ref, o_ref, *narrow):
        res = _dot(a_ref[...].astype(BF16), b_ref[...].astype(BF16), TN)
        o_ref[...] = res.reshape(o_ref.shape)
        for n_ref in narrow:
            n_ref[...] = res.astype(BF16).reshape(n_ref.shape)

    if shard_axis is None:
        out_shape, out_spec = (m, n), pl.BlockSpec((tm, tn), lambda i, j: (i, j))
    elif shard_axis == 0:
        per = m // N_CHIPS // tm
        out_shape, out_spec = (N_CHIPS, m // N_CHIPS, n), pl.BlockSpec((1, tm, tn), lambda i, j: (i // per, i % per, j))
    else:
        per = n // N_CHIPS // tn
        out_shape, out_spec = (N_CHIPS, m, n // N_CHIPS), pl.BlockSpec((1, tm, tn), lambda i, j: (j // per, i, j % per))
    both = shard_axis is not None
    return pl.pallas_call(
        body, name=name, grid=(m // tm, n // tn),
        in_specs=[pl.BlockSpec((s, tm), lambda i, j: (0, i)), pl.BlockSpec((s, tn), lambda i, j: (0, j))],
        out_specs=[out_spec, out_spec] if both else out_spec,
        out_shape=[jax.ShapeDtypeStruct(out_shape, F32), jax.ShapeDtypeStruct(out_shape, BF16)] if both
        else jax.ShapeDtypeStruct(out_shape, F32),
        compiler_params=_cparams(("arbitrary", "arbitrary")),
    )(a, b)


def _merge_bwd(dx1, osb, oswa, gates, wsb, wswa, wout):
    s = dx1.shape[0]
    tm = min(512, s)

    def body(dx_ref, osb_ref, oswa_ref, gate_ref, wsb_ref, wswa_ref, wout_ref,
             dosb_ref, doswa_ref, dgl_ref, merged_ref, dysb_ref, dyswa_ref, dxb_ref):
        dxb = dx_ref[...].astype(BF16)
        dxb_ref[...] = dxb
        dm = _dot(dxb, wout_ref[...], NT)
        ysb = _dot(osb_ref[...].astype(BF16), wsb_ref[...], NN)
        yswa = _dot(oswa_ref[...], wswa_ref[...], NN)
        g0 = gate_ref[:, :D_MODEL]
        g1 = gate_ref[:, D_MODEL:]
        merged_ref[...] = (g0 * ysb + g1 * yswa).astype(BF16)
        dgl_ref[:, :D_MODEL] = (dm * ysb * (g0 * (1.0 - g0))).astype(BF16)
        dgl_ref[:, D_MODEL:] = (dm * yswa * (g1 * (1.0 - g1))).astype(BF16)
        dysb = (dm * g0).astype(BF16)
        dyswa = (dm * g1).astype(BF16)
        dysb_ref[...] = dysb
        dyswa_ref[...] = dyswa
        dosb_ref[...] = _dot(dysb, wsb_ref[...], NT)
        doswa_ref[...] = _dot(dyswa, wswa_ref[...], NT)

    return pl.pallas_call(
        body, name="merge_bwd", grid=(s // tm,),
        in_specs=[_rows(tm, D_MODEL), _rows(tm, SB_WIDTH), _rows(tm, SWA_Q_WIDTH), _rows(tm, 2 * D_MODEL),
                  _resident((SB_WIDTH, D_MODEL)), _resident((SWA_Q_WIDTH, D_MODEL)), _resident((D_MODEL, D_MODEL))],
        out_specs=[_rows(tm, SB_WIDTH), _rows(tm, SWA_Q_WIDTH), _rows(tm, 2 * D_MODEL), _rows(tm, D_MODEL),
                   _rows(tm, D_MODEL), _rows(tm, D_MODEL), _rows(tm, D_MODEL)],
        out_shape=[jax.ShapeDtypeStruct((s, SB_WIDTH), F32), jax.ShapeDtypeStruct((s, SWA_Q_WIDTH), F32),
                   jax.ShapeDtypeStruct((s, 2 * D_MODEL), BF16), jax.ShapeDtypeStruct((s, D_MODEL), BF16),
                   jax.ShapeDtypeStruct((s, D_MODEL), BF16), jax.ShapeDtypeStruct((s, D_MODEL), BF16),
                   jax.ShapeDtypeStruct((s, D_MODEL), BF16)],
        compiler_params=_cparams(("arbitrary",)),
    )(dx1, osb, oswa, gates, wsb, wswa, wout)


def _sb_bwd(sbqkv, osb, dosb, send):
    s = sbqkv.shape[0]
    t = min(SB_TILE, s)
    nq = s // t
    ng = SB_WIDTH // SB_STEP_WIDTH
    heads = range(2 * SB_STEP_WIDTH // LANES)
    ns = len(send)

    def body(*refs):
        q_ref, k_ref, v_ref, o_ref, do_ref = refs[:5]
        srcs = refs[5:5 + ns]
        dq_ref, dk_ref, dv_ref = refs[5 + ns:8 + ns]
        outs = refs[8 + ns:8 + 2 * ns]
        j, i = pl.program_id(0), pl.program_id(1)
        if ns:
            ssem, rsem = refs[8 + 2 * ns:]
            _hosted_exchange(srcs, outs, ssem, rsem, (j == 0) & (i == 0), (j == ng - 1) & (i == nq - 1))

        @pl.when(i == 0)
        def _():
            dk_ref[...] = jnp.zeros_like(dk_ref)
            dv_ref[...] = jnp.zeros_like(dv_ref)

        left = _lane((1, LANES)) < HEAD_DIM
        halves = (left, jnp.logical_not(left))
        pair = lambda a, p: a[:, (p // 2) * LANES:(p // 2 + 1) * LANES]
        only = lambda a, p: jnp.where(halves[p % 2], pair(a, p), jnp.zeros((), a.dtype))
        scale = jnp.asarray(SCALE, BF16)
        q_all = q_ref[...] * scale
        do_all = do_ref[...].astype(BF16)
        prod = do_all.astype(F32) * o_ref[...]
        qh = [only(q_all, p) for p in heads]
        doh = [only(do_all, p) for p in heads]
        delta = [jnp.sum(only(prod, p), axis=-1, keepdims=True) for p in heads]
        r = lax.broadcasted_iota(jnp.int32, (t, t), 0)
        c = lax.broadcasted_iota(jnp.int32, (t, t), 1)
        u_from = (r >= c).astype(BF16)
        u_from2 = _tri_twice(t)

        def tile(n, carry, strict):
            runs, dqs = carry
            off = pl.multiple_of((i - n) * t, t)
            k = k_ref[pl.ds(off, t), :]
            v = v_ref[pl.ds(off, t), :]
            ks = k * scale
            terms = [_sb_tile_terms(qh[p], pair(k, p), strict) for p in heads]
            logw = [_log_weights(*terms[p], runs[p][0], u_from) for p in heads]
            dws = [_dot(doh[p], pair(v, p), NT) for p in heads]
            wbs = [_masked(strict, jnp.exp(logw[p][0])).astype(BF16) for p in heads]
            es = [dws[p] * wbs[p].astype(F32) for p in heads]
            rests = [runs[p][1] + _split_dot(es[p], u_from2) for p in heads]
            betas = [jnp.exp(terms[p][0] + terms[p][1]) for p in heads]
            dzs = [_masked(strict, es[p] - betas[p] * (es[p] + delta[p] - rests[p])).astype(BF16) for p in heads]
            dqs = list(dqs)
            for g in range(len(heads) // 2):
                a, b = 2 * g, 2 * g + 1
                cols = slice(g * LANES, (g + 1) * LANES)
                dv_ref[pl.ds(off, t), cols] += _dot(wbs[a], doh[a], TN) + _dot(wbs[b], doh[b], TN)
                dk_ref[pl.ds(off, t), cols] += _dot(dzs[a], qh[a], TN) + _dot(dzs[b], qh[b], TN)
                dqs[g] = dqs[g] + _dot(dzs[a], only(ks, a), NN) + _dot(dzs[b], only(ks, b), NN)
            new_runs = tuple((logw[p][1], rests[p][:, 0:1]) for p in heads)
            return new_runs, tuple(dqs)

        zero = jnp.zeros((t, 1), F32)
        wide = jnp.zeros((t, LANES), F32)
        carry = tile(0, (tuple((zero, zero) for _ in heads), tuple(wide for _ in heads[::2])), c < r)
        _, dqs = lax.fori_loop(1, i + 1, lambda n, cr: tile(n, cr, None), carry)
        for g, dq in enumerate(dqs):
            dq_ref[:, g * LANES:(g + 1) * LANES] = dq.astype(BF16)

    w = SB_STEP_WIDTH
    blk = pl.BlockSpec((t, w), lambda j, i: (i, j))
    whole = pl.BlockSpec((s, w), lambda j, i: (0, j))
    return pl.pallas_call(
        body, name="sb_bwd_x%d" % ns, grid=(ng, nq),
        in_specs=[blk, pl.BlockSpec((s, w), lambda j, i: (0, ng + j)),
                  pl.BlockSpec((s, w), lambda j, i: (0, 2 * ng + j)), blk, blk] + [HBM] * ns,
        out_specs=[blk, whole, whole] + [HBM] * ns,
        out_shape=[jax.ShapeDtypeStruct((s, SB_WIDTH), BF16), jax.ShapeDtypeStruct((s, SB_WIDTH), F32),
                   jax.ShapeDtypeStruct((s, SB_WIDTH), F32)] + [jax.ShapeDtypeStruct(a.shape, a.dtype) for a in send],
        scratch_shapes=_exchange_scratch(ns) if ns else [],
        compiler_params=_cparams(("arbitrary", "arbitrary")),
    )(sbqkv, sbqkv, sbqkv, osb, dosb, *send)


def _swa_bwd(qn, kn, v, sinks, do):
    s = qn.shape[0]
    t = min(512, s)
    nb = t // SWA_BLOCK
    nheads = SWA_Q_WIDTH // HEAD_DIM

    def body(sink_ref, q_ref, kc_ref, kp_ref, vc_ref, vp_ref, do_ref, dq_ref, dk_ref, dv_ref, dsink_ref):
        i = pl.program_id(0)

        @pl.when(i == 0)
        def _():
            dk_ref[...] = jnp.zeros_like(dk_ref)
            dv_ref[...] = jnp.zeros_like(dv_ref)

        left = _lane((1, LANES)) < HEAD_DIM
        halves = (left, jnp.logical_not(left))
        dsink = [jnp.zeros((1, 1), F32) for _ in range(nheads)]
        for b in range(nb):
            kvar = _swa_variants(_swa_band(kc_ref, kp_ref, b), left)
            vvar = _swa_variants(_swa_band(vc_ref, vp_ref, b), left)
            rows = slice(b * SWA_BLOCK, (b + 1) * SWA_BLOCK)
            valid = _swa_valid(i * nb + b)
            heads = _swa_heads(q_ref, rows, halves)
            doh = []
            for hb, p, _, _ in heads:
                do2 = do_ref[rows, hb * LANES:(hb + 1) * LANES]
                doh.append(jnp.where(halves[p], do2, 0.0).astype(BF16))
            dots = [_dot(qh, kvar[g][p], NT) for _, p, g, qh in heads]
            dps = [_dot(doh[h], vvar[g][p], NT) for h, (_, p, g, _) in enumerate(heads)]
            dss, pbs = [], []
            for h in range(nheads):
                probs, psink = _swa_softmax(dots[h], sink_ref[h], valid)
                delta = jnp.sum(probs * dps[h], axis=-1, keepdims=True)
                dss.append((probs * (dps[h] - delta) * SCALE).astype(BF16))
                pbs.append(probs.astype(BF16))
                dsink[h] = dsink[h] - jnp.sum(psink * delta, axis=0, keepdims=True)
            dk_acc = [jnp.zeros((2 * SWA_BLOCK, LANES), F32) for _ in range(2)]
            dv_acc = [jnp.zeros((2 * SWA_BLOCK, LANES), F32) for _ in range(2)]
            dqs = [_dot(dss[h], kvar[g][p], NN) for h, (_, p, g, _) in enumerate(heads)]
            for h, (_, p, g, qh) in enumerate(heads):
                which = 0 if p == g else 1
                dk_acc[which] = dk_acc[which] + _dot(dss[h], qh, TN)
                dv_acc[which] = dv_acc[which] + _dot(pbs[h], doh[h], TN)
            for hb in range(nheads // 2):
                dq_ref[rows, hb * LANES:(hb + 1) * LANES] = dqs[2 * hb] + dqs[2 * hb + 1]
            dkb = dk_acc[0] + pltpu.roll(dk_acc[1], HEAD_DIM, 1)
            dvb = dv_acc[0] + pltpu.roll(dv_acc[1], HEAD_DIM, 1)
            start = pl.multiple_of((i * nb + b) * SWA_BLOCK, SWA_BLOCK)
            dk_ref[pl.ds(start, SWA_BLOCK), :] += dkb[SWA_BLOCK:]
            dv_ref[pl.ds(start, SWA_BLOCK), :] += dvb[SWA_BLOCK:]

            @pl.when(i * nb + b > 0)
            def _(dkb=dkb, dvb=dvb, start=start):
                before = pl.multiple_of(jnp.maximum(start - SWA_BLOCK, 0), SWA_BLOCK)
                dk_ref[pl.ds(before, SWA_BLOCK), :] += dkb[:SWA_BLOCK]
                dv_ref[pl.ds(before, SWA_BLOCK), :] += dvb[:SWA_BLOCK]

        for h in range(nheads):
            dsink_ref[0, h:h + 1, :] = jnp.broadcast_to(dsink[h], (1, LANES))

    cur, prev = _swa_specs(s, t)
    whole = pl.BlockSpec((s, LANES), lambda i: (0, 0))
    return pl.pallas_call(
        body, name="swa_bwd", grid=(s // t,),
        in_specs=[pl.BlockSpec(memory_space=pltpu.SMEM), _rows(t, SWA_Q_WIDTH), cur, prev, cur, prev,
                  _rows(t, SWA_Q_WIDTH)],
        out_specs=[_rows(t, SWA_Q_WIDTH), whole, whole, pl.BlockSpec((1, 8, LANES), lambda i: (i, 0, 0))],
        out_shape=[jax.ShapeDtypeStruct((s, SWA_Q_WIDTH), F32), jax.ShapeDtypeStruct((s, SWA_KV_WIDTH), F32),
                   jax.ShapeDtypeStruct((s, SWA_KV_WIDTH), F32), jax.ShapeDtypeStruct((s // t, 8, LANES), F32)],
        compiler_params=_cparams(("arbitrary",)),
    )(sinks, qn, kn, kn, v, v, do)


def _swa_post(raw, dqn, dkn, qg, kg, cos, sin):
    s = raw.shape[0]
    tm = min(512, s)
    nq = SWA_Q_WIDTH // LANES

    def body(raw_ref, dq_ref, dk_ref, qg_ref, kg_ref, cos_ref, sin_ref, out_ref, dg_ref):
        @pl.when(pl.program_id(0) == 0)
        def _():
            dg_ref[...] = jnp.zeros_like(dg_ref)

        cs, sn = cos_ref[...], sin_ref[...]
        dgq = jnp.zeros((1, LANES), F32)
        for b in range(nq):
            cols = slice(b * LANES, (b + 1) * LANES)
            dp, dg = _norm_rope_bwd(raw_ref[:, cols], dq_ref[:, cols], qg_ref[...], cs, sn)
            out_ref[:, cols] = dp.astype(BF16)
            dgq = dgq + dg
        cols = slice(SWA_Q_WIDTH, SWA_Q_WIDTH + LANES)
        dp, dgk = _norm_rope_bwd(raw_ref[:, cols], dk_ref[...], kg_ref[...], cs, sn)
        out_ref[:, cols] = dp.astype(BF16)
        dg_ref[0:1, :] += dgq
        dg_ref[1:2, :] += dgk

    return pl.pallas_call(
        body, name="swa_post", grid=(s // tm,),
        in_specs=[_rows(tm, 640), _rows(tm, SWA_Q_WIDTH), _rows(tm, LANES), _resident((1, LANES)),
                  _resident((1, LANES)), _rows(tm, LANES), _rows(tm, LANES)],
        out_specs=[_rows(tm, 640), pl.BlockSpec((8, LANES), lambda i: (0, 0))],
        out_shape=[jax.ShapeDtypeStruct((s, 640), BF16), jax.ShapeDtypeStruct((8, LANES), F32)],
        compiler_params=_cparams(("arbitrary",)),
    )(raw, dqn, dkn, qg, kg, cos, sin)


def _inproj_bwd(dsbq, dsbk, dsbv, dswqk, dswv, dgl, x, dx1, g, w):
    s = x.shape[0]
    tm = min(256, s)

    def body(dsbq_ref, dsbk_ref, dsbv_ref, dswqk_ref, dswv_ref, dgl_ref, x_ref, dx1_ref, g_ref, w_ref,
             dx_ref, dproj_ref, dg_ref):
        @pl.when(pl.program_id(0) == 0)
        def _():
            dg_ref[...] = jnp.zeros_like(dg_ref)

        dproj_ref[:, 0:512] = dsbq_ref[...]
        dproj_ref[:, 512:1024] = dsbk_ref[...].astype(BF16)
        dproj_ref[:, 1024:1536] = dsbv_ref[...].astype(BF16)
        dproj_ref[:, 1536:2176] = dswqk_ref[...]
        dproj_ref[:, 2176:2304] = dswv_ref[...].astype(BF16)
        dproj_ref[:, GATE_OFF:IN_WIDTH] = dgl_ref[...]
        dh = jnp.zeros((tm, D_MODEL), F32)
        for a in range(0, IN_WIDTH, 512):
            b = min(a + 512, IN_WIDTH)
            dh = dh + _dot(dproj_ref[:, a:b], w_ref[:, a:b], NT)
        gam = g_ref[...]
        xn, rstd = _rms(x_ref[...])
        dg_ref[...] += jnp.sum(dh * xn, axis=0, keepdims=True)
        dx_ref[...] = dx1_ref[...] + _rms_bwd(dh, xn, rstd, gam)

    return pl.pallas_call(
        body, name="inproj_bwd", grid=(s // tm,),
        in_specs=[_rows(tm, 512), _rows(tm, 512), _rows(tm, 512), _rows(tm, 640), _rows(tm, LANES),
                  _rows(tm, 2 * D_MODEL), _rows(tm, D_MODEL), _rows(tm, D_MODEL), _resident((1, D_MODEL)),
                  _resident((D_MODEL, IN_WIDTH))],
        out_specs=[_rows(tm, D_MODEL), _rows(tm, IN_WIDTH), pl.BlockSpec((1, D_MODEL), lambda i: (0, 0))],
        out_shape=[jax.ShapeDtypeStruct((s, D_MODEL), F32), jax.ShapeDtypeStruct((s, IN_WIDTH), BF16),
                   jax.ShapeDtypeStruct((1, D_MODEL), F32)],
        compiler_params=_cparams(("arbitrary",)),
    )(dsbq, dsbk, dsbv, dswqk, dswv, dgl, x, dx1, g, w)


def _gather_weights(shard):
    def body(src, out, ssem, rsem, lsem):
        once = pl.program_id(0) == 0
        _hosted_gather(src, out, ssem, rsem, lsem, once, once, once)

    return pl.pallas_call(
        body, name="gather_weights", grid=(1,), in_specs=[HBM], out_specs=HBM,
        out_shape=jax.ShapeDtypeStruct((N_CHIPS,) + shard.shape, shard.dtype),
        scratch_shapes=_gather_scratch(1),
        compiler_params=pltpu.CompilerParams(dimension_semantics=("arbitrary",), has_side_effects=True),
    )(shard)


def _exchange(send):
    ns = len(send)

    def body(*refs):
        once = pl.program_id(0) == 0
        _hosted_exchange(refs[:ns], refs[ns:2 * ns], *refs[2 * ns:], once, once)

    return pl.pallas_call(
        body, name="exchange_x%d" % ns, grid=(1,), in_specs=[HBM] * ns, out_specs=[HBM] * ns,
        out_shape=[jax.ShapeDtypeStruct(a.shape, a.dtype) for a in send],
        scratch_shapes=_exchange_scratch(ns),
        compiler_params=pltpu.CompilerParams(dimension_semantics=("arbitrary",), has_side_effects=True),
    )(*send)


def _pair_swap(arrs):
    n = len(arrs)

    def body(*refs):
        x, y, c, _ = _place()
        cps = [pltpu.make_async_remote_copy(refs[t], refs[n + t], refs[2 * n].at[t], refs[2 * n + 1].at[t],
                                            device_id=(x, y, 1 - c), device_id_type=MESH) for t in range(n)]
        for cp in cps:
            cp.start()
        for cp in cps:
            cp.wait()

    return pl.pallas_call(
        body, name="pair_swap", in_specs=[HBM] * n, out_specs=[HBM] * n,
        out_shape=[jax.ShapeDtypeStruct(a.shape, a.dtype) for a in arrs],
        scratch_shapes=[pltpu.SemaphoreType.DMA((n,)), pltpu.SemaphoreType.DMA((n,))],
        compiler_params=pltpu.CompilerParams(has_side_effects=True),
    )(*arrs)


def _allreduce_small(block):
    def body(src, out, buf, ssem, rsem):
        x, y, c, _ = _place()
        me = 4 * x + 2 * y + c
        buf[me] = src[...]
        cps = []
        for k in range(1, N_DEV):
            peer = (x ^ (k >> 2), y ^ ((k >> 1) & 1), c ^ (k & 1))
            cp = pltpu.make_async_remote_copy(src, buf.at[me], ssem.at[k - 1], rsem.at[k - 1], device_id=peer, device_id_type=MESH)
            cp.start()
            cps.append(cp)
        for k in range(1, N_DEV):
            got = buf.at[me ^ k]
            pltpu.make_async_remote_copy(got, got, ssem.at[k - 1], rsem.at[k - 1], device_id=(x, y, c), device_id_type=MESH).wait_recv()
        for cp in cps:
            cp.wait_send()
        tot = buf[0]
        for d in range(1, N_DEV):
            tot = tot + buf[d]
        out[...] = tot

    vm = pl.BlockSpec(memory_space=pltpu.VMEM)
    return pl.pallas_call(
        body, name="allreduce_small", in_specs=[vm], out_specs=vm,
        out_shape=jax.ShapeDtypeStruct(block.shape, F32),
        scratch_shapes=[pltpu.VMEM((N_DEV,) + block.shape, F32), pltpu.SemaphoreType.DMA((N_DEV - 1,)),
                        pltpu.SemaphoreType.DMA((N_DEV - 1,))],
        compiler_params=pltpu.CompilerParams(has_side_effects=True),
    )(block)


def _sum_chips(landed, own, name):
    nq, k, n = landed.shape
    tr = min(256, k)
    me = (2 * lax.axis_index("x") + lax.axis_index("y")).astype(jnp.int32).reshape(1)

    def body(me_ref, p0, p1, p2, p3, own_ref, o_ref):
        mine = own_ref[0]
        terms = [jnp.where(me_ref[0] == q, mine, p[0].astype(F32)) for q, p in enumerate((p0, p1, p2, p3))]
        o_ref[...] = ((terms[0] + terms[1]) + terms[2]) + terms[3]

    spec = lambda q: pl.BlockSpec((1, tr, n), lambda i, m, q=q: (q, i, 0))
    return pl.pallas_call(
        body, name=name,
        grid_spec=pltpu.PrefetchScalarGridSpec(
            num_scalar_prefetch=1, grid=(k // tr,),
            in_specs=[spec(q) for q in range(nq)] + [pl.BlockSpec((1, tr, n), lambda i, m: (m[0], i, 0))],
            out_specs=pl.BlockSpec((tr, n), lambda i, m: (i, 0))),
        out_shape=jax.ShapeDtypeStruct((k, n), F32),
        compiler_params=_cparams(("arbitrary",)),
    )(me, landed, landed, landed, landed, own)


def _adamw(w, g, g2, m, v, name):
    shape = w.shape
    cols = shape[-1]
    flat = lambda t: t.reshape(-1, cols)
    rows = flat(w).shape[0]
    tr = min(512, rows)
    pair = g2 is not None

    def body(*refs):
        if pair:
            w_ref, g_ref, g2_ref, m_ref, v_ref, go_ref, d_ref, nm_ref, nv_ref = refs
            gr = g_ref[...] + g2_ref[...]
        else:
            w_ref, g_ref, m_ref, v_ref, go_ref, d_ref, nm_ref, nv_ref = refs
            gr = g_ref[...]
        go_ref[...] = gr
        nm = ADAM_B1 * m_ref[...] + (1.0 - ADAM_B1) * gr
        nv = ADAM_B2 * v_ref[...] + (1.0 - ADAM_B2) * (gr * gr)
        m_hat = nm / (1.0 - ADAM_B1 ** ADAM_STEP)
        v_hat = nv / (1.0 - ADAM_B2 ** ADAM_STEP)
        d_ref[...] = -ADAM_LR * (m_hat / (jnp.sqrt(v_hat) + ADAM_EPS) + ADAM_WD * w_ref[...])
        nm_ref[...] = nm
        nv_ref[...] = nv

    spec = pl.BlockSpec((tr, cols), lambda i: (i, 0))
    ins = [w, g] + ([g2] if pair else []) + [m, v]
    outs = pl.pallas_call(
        body, name=name, grid=(rows // tr,),
        in_specs=[spec] * len(ins), out_specs=[spec] * 4,
        out_shape=[jax.ShapeDtypeStruct((rows, cols), F32)] * 4,
        compiler_params=_cparams(("arbitrary",)),
    )(*[flat(t) for t in ins])
    return [o.reshape(shape) for o in outs]


def _pack_layer(big, l, names=BIG):
    rows = dict(PACK_ROWS)
    return jnp.concatenate([big[name][l].astype(BF16).reshape(rows[name], 1024) for name in names], axis=0)


def _whole_weights(gathered, names=BIG):
    out, at = {}, 0
    for name in names:
        rows = dict(PACK_ROWS)[name]
        t = gathered[:, at:at + rows, :].reshape((N_CHIPS,) + SHARD_SHAPES[name])
        at += rows
        if name in ROW_SHARDED:
            out[name] = t.reshape(N_CHIPS * t.shape[1], t.shape[2])
        elif name == "w_up":
            out[name] = t
        else:
            out[name] = jnp.moveaxis(t, 0, 1).reshape(t.shape[1], N_CHIPS * t.shape[2])
    return out


def _pack_small(d):
    flat = jnp.concatenate([d[n].reshape(-1) for n in SMALL_NAMES])
    return jnp.pad(flat, (0, SMALL_ROWS * LANES - flat.shape[0])).reshape(SMALL_ROWS, LANES)


def _unpack_small(block, like):
    flat, out, at = block.reshape(-1), {}, 0
    for n in SMALL_NAMES:
        size = like[n].size
        out[n] = flat[at:at + size].reshape(like[n].shape)
        at += size
    return out


def _rope_tables(s):
    inv_freq = 1.0 / (ROPE_THETA ** (jnp.arange(0, HEAD_DIM, 2, dtype=F32) / HEAD_DIM))
    ang = jnp.arange(s, dtype=F32)[:, None] * inv_freq[None, :]
    reps = LANES // (HEAD_DIM // 2)
    return jnp.tile(jnp.cos(ang), (1, reps)), jnp.tile(jnp.sin(ang), (1, reps))


def _forward_backward(x, target, big, small):
    s = x.shape[0]
    cos, sin = _rope_tables(s)
    two = lambda gvec: jnp.tile(gvec.reshape(1, HEAD_DIM), (1, 2))
    saved = []
    first, rest = BIG[:1], BIG[1:]
    win = _whole_weights(_gather_weights(_pack_layer(big, 0, first)), first)
    for l in range(DEPTH):
        gm = small["mix_norm_g"][l].reshape(1, D_MODEL)
        gl = small["mlp_norm_g"][l].reshape(1, D_MODEL)
        qg, kg = two(small["q_norm_g"][l]), two(small["k_norm_g"][l])
        h, sbqkv, raw, qn, kn, v, gates = _inproj_fwd(x, gm, win["w_in"], qg, kg, cos, sin)
        shards = [_pack_layer(big, l, rest)] + ([_pack_layer(big, l + 1, first)] if l + 1 < DEPTH else [])
        osb, *landed = _sb_fwd(sbqkv, shards)
        wts = {**win, **_whole_weights(landed[0], rest)}
        nxt = _whole_weights(landed[1], first) if len(landed) > 1 else None
        oswa = _swa_fwd(qn, kn, v, small["sinks"][l])
        x1 = _merge_fwd(x, osb, oswa, gates, wts["w_branch_sb"], wts["w_branch_swa"], wts["w_out"])
        x2, u = _mlp_fwd(x1, gl, wts["w_up"], wts["w_down"])
        saved.append((x, h, sbqkv, raw, qn, kn, v, gates, osb, oswa, x1, u, gm, gl, qg, kg, wts))
        x, win = x2, nxt
    dx, loss = _loss_grad(x, target)

    got = {name: [None] * DEPTH for name in BIG}
    gsmall = {name: [None] * DEPTH for name in SMALL_NAMES}
    late = []
    for l in reversed(range(DEPTH)):
        x0, h, sbqkv, raw, qn, kn, v, gates, osb, oswa, x1, u, gm, gl, qg, kg, wts = saved[l]
        dx1, du, act, h2, dgl_mlp, dxb = _mlp_bwd(dx, x1, u, gl, wts["w_up"], wts["w_down"])
        dosb, doswa, dgate, merged, dysb, dyswa, dx1b = _merge_bwd(
            dx1, osb, oswa, gates, wts["w_branch_sb"], wts["w_branch_swa"], wts["w_out"])
        ready = [("w_down", l, *_wgrad(act, dxb, "wgrad_down", shard_axis=0)),
                 ("w_up", l, *_wgrad(h2, du, "wgrad_up", shard_axis=1)),
                 ("w_out", l, *_wgrad(merged, dx1b, "wgrad_out", shard_axis=0)),
                 ("w_branch_sb", l, *_wgrad(osb, dysb, "wgrad_bsb", shard_axis=1)),
                 ("w_branch_swa", l, *_wgrad(oswa, dyswa, "wgrad_bswa", shard_axis=1))] + late
        dsbq, dsbk, dsbv, *landed = _sb_bwd(sbqkv, osb, dosb, [narrow for _, _, _, narrow in ready])
        for (name, layer, own, _), arr in zip(ready, landed):
            got[name][layer] = (arr, own)
        dqn, dkn, dswv, dsink = _swa_bwd(qn, kn, v, small["sinks"][l], doswa)
        dswqk, dgqk = _swa_post(raw, dqn, dkn, qg, kg, cos, sin)
        dx, dproj, dg_mix = _inproj_bwd(dsbq, dsbk, dsbv, dswqk, dswv, dgate, x0, dx1, gm, wts["w_in"])
        dwin = _wgrad(h, dproj, "wgrad_in")
        dwin = dwin.reshape(D_MODEL, N_CHIPS, IN_WIDTH // N_CHIPS).transpose(1, 0, 2)
        late = [("w_in", l, dwin, dwin.astype(BF16))]
        gsmall["mix_norm_g"][l] = dg_mix[0]
        gsmall["mlp_norm_g"][l] = dgl_mlp[0]
        gsmall["q_norm_g"][l] = dgqk[0, :HEAD_DIM] + dgqk[0, HEAD_DIM:]
        gsmall["k_norm_g"][l] = dgqk[1, :HEAD_DIM] + dgqk[1, HEAD_DIM:]
        gsmall["sinks"][l] = jnp.sum(dsink[:, :, 0], axis=0)
    got["w_in"][0] = (_exchange([late[0][3]])[0], late[0][2])
    gsmall = {k: jnp.stack(vs) for k, vs in gsmall.items()}
    return loss, dx, got, gsmall


def kernel(x, mix_norm_g, w_in, q_norm_g, k_norm_g, sinks, w_branch_sb, w_branch_swa, w_out, mlp_norm_g, w_up, w_down, loss_target, m_mix_norm_g, m_w_in, m_q_norm_g, m_k_norm_g, m_sinks, m_w_branch_sb, m_w_branch_swa, m_w_out, m_mlp_norm_g, m_w_up, m_w_down, v_mix_norm_g, v_w_in, v_q_norm_g, v_k_norm_g, v_sinks, v_w_branch_sb, v_w_branch_swa, v_w_out, v_mlp_norm_g, v_w_up, v_w_down):
    big = dict(w_in=w_in, w_branch_sb=w_branch_sb, w_branch_swa=w_branch_swa, w_out=w_out, w_up=w_up, w_down=w_down)
    big_m = dict(w_in=m_w_in, w_branch_sb=m_w_branch_sb, w_branch_swa=m_w_branch_swa, w_out=m_w_out, w_up=m_w_up, w_down=m_w_down)
    big_v = dict(w_in=v_w_in, w_branch_sb=v_w_branch_sb, w_branch_swa=v_w_branch_swa, w_out=v_w_out, w_up=v_w_up, w_down=v_w_down)
    small = dict(mix_norm_g=mix_norm_g, q_norm_g=q_norm_g, k_norm_g=k_norm_g, sinks=sinks, mlp_norm_g=mlp_norm_g)
    small_m = dict(mix_norm_g=m_mix_norm_g, q_norm_g=m_q_norm_g, k_norm_g=m_k_norm_g, sinks=m_sinks, mlp_norm_g=m_mlp_norm_g)
    small_v = dict(mix_norm_g=v_mix_norm_g, q_norm_g=v_q_norm_g, k_norm_g=v_k_norm_g, sinks=v_sinks, mlp_norm_g=v_mlp_norm_g)

    loss_part, grad_x, got, gsmall = _forward_backward(x[0], loss_target[0], big, small)
    loss = lax.psum(loss_part, ("x", "y", "c"))

    mine = [jnp.stack([_sum_chips(*got[name][l], "sum_" + name) for l in range(DEPTH)]) for name in BIG]
    theirs = _pair_swap(mine)
    upd = {name: _adamw(big[name], mine[i], theirs[i], big_m[name], big_v[name], "adamw_" + name)
           for i, name in enumerate(BIG)}
    g_small = _allreduce_small(_pack_small(gsmall))
    sm = _adamw(_pack_small(small), g_small, None, _pack_small(small_m), _pack_small(small_v), "adamw_small")
    upd_small = [_unpack_small(t, small) for t in sm]

    names = ("mix_norm_g", "w_in", "q_norm_g", "k_norm_g", "sinks", "w_branch_sb", "w_branch_swa", "w_out",
             "mlp_norm_g", "w_up", "w_down")
    pick = lambda n, i: upd[n][i] if n in upd else upd_small[i][n]
    return (loss, grad_x[None], *[pick(n, 0) for n in names], *[pick(n, 1) for n in names],
            *[pick(n, 2) for n in names], *[pick(n, 3) for n in names])
```

```python
import jax
import jax.numpy as jnp
from jax import lax
from jax.experimental import pallas as pl
from jax.experimental.pallas import tpu as pltpu

F32 = jnp.float32
BF16 = jnp.bfloat16

D_MODEL = 1024
DEPTH = 4
HEAD_DIM = 64
SB_WIDTH = 512
SWA_Q_WIDTH = 512
SWA_KV_WIDTH = 128
D_FF = 4096
IN_WIDTH = 4352
GATE_OFF = 2304
ROPE_THETA = 10000.0
NORM_EPS = 1e-6
SCALE = HEAD_DIM ** -0.5
N_CHIPS = 4
N_DEV = 8

ADAM_LR = 0.001
ADAM_B1 = 0.9
ADAM_B2 = 0.999
ADAM_EPS = 1e-08
ADAM_WD = 0.01
ADAM_STEP = 10

LANES = 128
SB_TILE = 256
SB_STEP_WIDTH = 256
SB_FWD_STEP_WIDTH = 512
SWA_BLOCK = 128
VMEM_LIMIT = 56 << 20

NN = (((1,), (0,)), ((), ()))
NT = (((1,), (1,)), ((), ()))
TN = (((0,), (0,)), ((), ()))
MESH = pl.DeviceIdType.MESH
HBM = pl.BlockSpec(memory_space=pl.ANY)

BIG = ("w_in", "w_branch_sb", "w_branch_swa", "w_out", "w_up", "w_down")
SMALL_NAMES = ("mix_norm_g", "q_norm_g", "k_norm_g", "sinks", "mlp_norm_g")
PACK_ROWS = (("w_in", 1088), ("w_branch_sb", 128), ("w_branch_swa", 128), ("w_out", 256), ("w_up", 1024), ("w_down", 1024))
SHARD_SHAPES = {"w_in": (1024, 1088), "w_branch_sb": (512, 256), "w_branch_swa": (512, 256), "w_out": (256, 1024),
                "w_up": (1024, 1024), "w_down": (1024, 1024)}
ROW_SHARDED = ("w_out", "w_down")
SMALL_ROWS = 72


def _dot(a, b, dims):
    return lax.dot_general(a, b, dims, preferred_element_type=F32)


def _cparams(sem):
    return pltpu.CompilerParams(dimension_semantics=sem, vmem_limit_bytes=VMEM_LIMIT)


def _resident(shape):
    nd = len(shape)
    return pl.BlockSpec(shape, lambda *_: (0,) * nd, pipeline_mode=pl.Buffered(1))


def _rows(tm, width):
    return pl.BlockSpec((tm, width), lambda i: (i, 0))


def _rms(xf):
    rstd = lax.rsqrt(jnp.mean(xf * xf, axis=-1, keepdims=True) + NORM_EPS)
    return xf * rstd, rstd


def _rms_bwd(dh, xn, rstd, g):
    dxn = dh * g
    return rstd * (dxn - xn * jnp.mean(dxn * xn, axis=-1, keepdims=True))


def _lane(shape):
    return lax.broadcasted_iota(jnp.int32, shape, len(shape) - 1)


def _head_mean(v, left):
    sl = jnp.sum(jnp.where(left, v, 0.0), axis=-1, keepdims=True)
    sr = jnp.sum(jnp.where(left, 0.0, v), axis=-1, keepdims=True)
    return jnp.where(left, sl, sr) * (1.0 / HEAD_DIM)


def _rope(y, cs, sn, first):
    up = pltpu.roll(y, 96, 1)
    dn = pltpu.roll(y, 32, 1)
    return y * cs + jnp.where(first, -up, dn) * sn


def _rope_t(d, cs, sn, first):
    t = d * jnp.where(first, -sn, sn)
    return d * cs + jnp.where(first, pltpu.roll(t, 96, 1), pltpu.roll(t, 32, 1))


def _norm_rope(p, g, cs, sn):
    lane = _lane((1, LANES))
    left = lane < HEAD_DIM
    first = (lane % HEAD_DIM) < (HEAD_DIM // 2)
    yn = p * lax.rsqrt(_head_mean(p * p, left) + NORM_EPS)
    return _rope(yn * g, cs, sn, first)


def _norm_rope_bwd(p, dout, g, cs, sn):
    lane = _lane((1, LANES))
    left = lane < HEAD_DIM
    first = (lane % HEAD_DIM) < (HEAD_DIM // 2)
    rstd = lax.rsqrt(_head_mean(p * p, left) + NORM_EPS)
    yn = p * rstd
    dyg = _rope_t(dout, cs, sn, first)
    dg = jnp.sum(dyg * yn, axis=0, keepdims=True)
    dyn = dyg * g
    dp = rstd * (dyn - yn * _head_mean(dyn * yn, left))
    return dp, dg


def _place():
    x, y, c = lax.axis_index("x"), lax.axis_index("y"), lax.axis_index("c")
    return x, y, c, [(1 - x, y), (x, 1 - y), (1 - x, 1 - y)]


def _hosted_gather(src, out, ssem, rsem, lsem, first, mid, last, slot=0):
    x, y, c, chips = _place()
    me = 2 * x + y
    sib = (x, y, 1 - c)
    r2 = src.shape[0] // 2
    base = 6 * slot

    def slab(chip, half):
        return out.at[chip, pl.ds(half * r2, r2)]

    def ici(j):
        cx, cy = chips[j]
        return pltpu.make_async_remote_copy(src.at[pl.ds(c * r2, r2)], slab(me, c), ssem.at[base + j], rsem.at[base + j],
                                            device_id=(cx, cy, c), device_id_type=MESH)

    def landed(j):
        got = slab(2 * chips[j][0] + chips[j][1], c)
        return pltpu.make_async_remote_copy(got, got, ssem.at[base + j], rsem.at[base + j], device_id=sib, device_id_type=MESH)

    def d2d(j, half):
        got = slab(2 * chips[j][0] + chips[j][1], half)
        return pltpu.make_async_remote_copy(got, got, ssem.at[base + 3 + j], rsem.at[base + 3 + j], device_id=sib,
                                            device_id_type=MESH)

    local = pltpu.make_async_copy(src, out.at[me], lsem.at[slot])

    @pl.when(first)
    def _():
        local.start()
        for j in range(3):
            ici(j).start()

    @pl.when(mid)
    def _():
        for j in range(3):
            landed(j).wait_recv()
            d2d(j, c).start()

    @pl.when(last)
    def _():
        for j in range(3):
            d2d(j, 1 - c).wait_recv()
        for j in range(3):
            ici(j).wait_send()
            d2d(j, c).wait_send()
        local.wait()


def _hosted_exchange(srcs, outs, ssem, rsem, first, last):
    x, y, c, chips = _place()
    me = 2 * x + y

    def send(t, j):
        cx, cy = chips[j]
        return pltpu.make_async_remote_copy(srcs[t].at[2 * cx + cy], outs[t].at[me], ssem.at[3 * t + j], rsem.at[3 * t + j],
                                            device_id=(cx, cy, c), device_id_type=MESH)

    def landed(t, j):
        cx, cy = chips[j]
        got = outs[t].at[2 * cx + cy]
        return pltpu.make_async_remote_copy(got, got, ssem.at[3 * t + j], rsem.at[3 * t + j],
                                            device_id=(cx, cy, c), device_id_type=MESH)

    @pl.when(first)
    def _():
        for t in range(len(srcs)):
            for j in range(3):
                send(t, j).start()

    @pl.when(last)
    def _():
        for t in range(len(srcs)):
            for j in range(3):
                landed(t, j).wait_recv()
        for t in range(len(srcs)):
            for j in range(3):
                send(t, j).wait_send()


def _gather_scratch(n):
    return [pltpu.SemaphoreType.DMA((6 * n,)), pltpu.SemaphoreType.DMA((6 * n,)), pltpu.SemaphoreType.DMA((n,))]


def _exchange_scratch(n):
    return [pltpu.SemaphoreType.DMA((3 * n,)), pltpu.SemaphoreType.DMA((3 * n,))]


def _inproj_fwd(x, g, w, qg, kg, cos, sin):
    s = x.shape[0]
    tm = min(512, s)

    def body(x_ref, g_ref, w_ref, qg_ref, kg_ref, cos_ref, sin_ref,
             h_ref, sb_ref, raw_ref, qn_ref, kn_ref, v_ref, gate_ref):
        xn, _ = _rms(x_ref[...])
        h = (xn * g_ref[...]).astype(BF16)
        h_ref[...] = h
        for a in range(0, 3 * SB_WIDTH, 512):
            sb_ref[:, a:a + 512] = _dot(h, w_ref[:, a:a + 512], NN).astype(BF16)
        cs, sn = cos_ref[...], sin_ref[...]
        q0 = 3 * SB_WIDTH
        pq = _dot(h, w_ref[:, q0:q0 + SWA_Q_WIDTH], NN)
        raw_ref[:, 0:SWA_Q_WIDTH] = pq
        for b in range(SWA_Q_WIDTH // LANES):
            blk = pq[:, b * LANES:(b + 1) * LANES]
            qn_ref[:, b * LANES:(b + 1) * LANES] = _norm_rope(blk, qg_ref[...], cs, sn).astype(BF16)
        k0 = q0 + SWA_Q_WIDTH
        pk = _dot(h, w_ref[:, k0:k0 + 2 * SWA_KV_WIDTH], NN)
        raw_ref[:, SWA_Q_WIDTH:SWA_Q_WIDTH + SWA_KV_WIDTH] = pk[:, :SWA_KV_WIDTH]
        kn_ref[...] = _norm_rope(pk[:, :SWA_KV_WIDTH], kg_ref[...], cs, sn).astype(BF16)
        v_ref[...] = pk[:, SWA_KV_WIDTH:].astype(BF16)
        for a in range(GATE_OFF, IN_WIDTH, 512):
            gate_ref[:, a - GATE_OFF:a - GATE_OFF + 512] = jax.nn.sigmoid(_dot(h, w_ref[:, a:a + 512], NN))

    return pl.pallas_call(
        body, name="inproj_fwd", grid=(s // tm,),
        in_specs=[_rows(tm, D_MODEL), _resident((1, D_MODEL)), _resident((D_MODEL, IN_WIDTH)),
                  _resident((1, LANES)), _resident((1, LANES)), _rows(tm, LANES), _rows(tm, LANES)],
        out_specs=[_rows(tm, D_MODEL), _rows(tm, 3 * SB_WIDTH), _rows(tm, 640), _rows(tm, SWA_Q_WIDTH),
                   _rows(tm, SWA_KV_WIDTH), _rows(tm, SWA_KV_WIDTH), _rows(tm, 2 * D_MODEL)],
        out_shape=[jax.ShapeDtypeStruct((s, D_MODEL), BF16), jax.ShapeDtypeStruct((s, 3 * SB_WIDTH), BF16),
                   jax.ShapeDtypeStruct((s, 640), F32), jax.ShapeDtypeStruct((s, SWA_Q_WIDTH), BF16),
                   jax.ShapeDtypeStruct((s, SWA_KV_WIDTH), BF16), jax.ShapeDtypeStruct((s, SWA_KV_WIDTH), BF16),
                   jax.ShapeDtypeStruct((s, 2 * D_MODEL), F32)],
        compiler_params=_cparams(("arbitrary",)),
    )(x, g, w, qg, kg, cos, sin)


def _sb_tile_terms(qh, k, strict):
    z = _dot(qh, k, NT)
    nz = -z
    soft = jnp.log(1.0 + jnp.exp(jnp.minimum(z, nz)))
    log_keep = jnp.minimum(nz, 0.0) - soft
    if strict is not None:
        log_keep = jnp.where(strict, log_keep, 0.0)
    return z, log_keep


def _log_weights(z, log_keep, run, u_from):
    tail = run + _dot(log_keep.astype(BF16), u_from, NN)
    return z + tail, tail[:, 0:1]


def _masked(strict, val):
    return val if strict is None else jnp.where(strict, val, 0.0)


def _tri_twice(t):
    r = lax.broadcasted_iota(jnp.int32, (2 * t, t), 0)
    c = lax.broadcasted_iota(jnp.int32, (2 * t, t), 1)
    return (jnp.where(r >= t, r - t, r) >= c).astype(BF16)


def _split_dot(a, u2):
    hi = a.astype(BF16)
    lo = (a - hi.astype(F32)).astype(BF16)
    return _dot(jnp.concatenate([hi, lo], axis=1), u2, NN)


def _sb_fwd(sbqkv, shards=()):
    s = sbqkv.shape[0]
    t = min(SB_TILE, s)
    nq = s // t
    ng = SB_WIDTH // SB_FWD_STEP_WIDTH
    heads = range(2 * SB_FWD_STEP_WIDTH // LANES)
    nsh = len(shards)

    def body(*refs):
        q_ref, k_ref, v_ref = refs[:3]
        o_ref = refs[3 + nsh]
        j, i = pl.program_id(0), pl.program_id(1)
        for n in range(nsh):
            _hosted_gather(refs[3 + n], refs[4 + nsh + n], *refs[4 + 2 * nsh:], (j == 0) & (i == 0),
                           (j == ng - 1) & (i == (3 * nq) // 4), (j == ng - 1) & (i == nq - 1), slot=n)
        left = _lane((1, LANES)) < HEAD_DIM
        halves = (left, jnp.logical_not(left))
        pair = lambda a, p: a[:, (p // 2) * LANES:(p // 2 + 1) * LANES]
        only = lambda a, p: jnp.where(halves[p % 2], pair(a, p), jnp.zeros((), a.dtype))
        q_all = q_ref[...] * jnp.asarray(SCALE, BF16)
        qh = [only(q_all, p) for p in heads]
        r = lax.broadcasted_iota(jnp.int32, (t, t), 0)
        c = lax.broadcasted_iota(jnp.int32, (t, t), 1)
        u_from = (r >= c).astype(BF16)

        def tile(n, carry, strict):
            runs, accs = carry
            off = pl.multiple_of((i - n) * t, t)
            k = k_ref[pl.ds(off, t), :]
            v = v_ref[pl.ds(off, t), :]
            terms = [_sb_tile_terms(qh[p], pair(k, p), strict) for p in heads]
            logw = [_log_weights(*terms[p], runs[p], u_from) for p in heads]
            ws = [_masked(strict, jnp.exp(logw[p][0])).astype(BF16) for p in heads]
            accs = list(accs)
            for p in heads:
                accs[p // 2] = accs[p // 2] + _dot(ws[p], only(v, p), NN)
            return tuple(logw[p][1] for p in heads), tuple(accs)

        zero = jnp.zeros((t, 1), F32)
        wide = jnp.zeros((t, LANES), F32)
        carry = tile(0, (tuple(zero for _ in heads), tuple(wide for _ in heads[::2])), c < r)
        _, accs = lax.fori_loop(1, i + 1, lambda n, cr: tile(n, cr, None), carry)
        for g, acc in enumerate(accs):
            o_ref[:, g * LANES:(g + 1) * LANES] = acc

    w = SB_FWD_STEP_WIDTH
    blk = pl.BlockSpec((t, w), lambda j, i: (i, j))
    in_specs = [blk, pl.BlockSpec((s, w), lambda j, i: (0, ng + j)), pl.BlockSpec((s, w), lambda j, i: (0, 2 * ng + j))]
    out_specs = [blk]
    out_shape = [jax.ShapeDtypeStruct((s, SB_WIDTH), F32)]
    in_specs += [HBM] * nsh
    out_specs += [HBM] * nsh
    out_shape += [jax.ShapeDtypeStruct((N_CHIPS,) + a.shape, a.dtype) for a in shards]
    return pl.pallas_call(
        body, name="sb_fwd_g%d" % nsh, grid=(ng, nq),
        in_specs=in_specs, out_specs=out_specs, out_shape=out_shape,
        scratch_shapes=_gather_scratch(nsh) if nsh else [],
        compiler_params=_cparams(("arbitrary", "arbitrary")),
    )(sbqkv, sbqkv, sbqkv, *shards)


def _swa_band(cur_ref, prev_ref, b):
    lo = prev_ref[...] if b == 0 else cur_ref[(b - 1) * SWA_BLOCK:b * SWA_BLOCK, :]
    return jnp.concatenate([lo, cur_ref[b * SWA_BLOCK:(b + 1) * SWA_BLOCK, :]], axis=0)


def _swa_variants(band, left):
    f = band.astype(F32)
    sw = pltpu.roll(f, HEAD_DIM, 1)
    halves = (left, jnp.logical_not(left))
    return [[jnp.where(halves[p], f if p == g else sw, 0.0).astype(BF16) for p in range(2)] for g in range(2)]


def _swa_valid(blk):
    ii = lax.broadcasted_iota(jnp.int32, (SWA_BLOCK, 2 * SWA_BLOCK), 0)
    jj = lax.broadcasted_iota(jnp.int32, (SWA_BLOCK, 2 * SWA_BLOCK), 1)
    rel = jj - SWA_BLOCK - ii
    return (rel <= 0) & (rel > -SWA_BLOCK) & (jj + blk * SWA_BLOCK >= SWA_BLOCK)


def _swa_softmax(dots, sink, valid):
    sc = jnp.where(valid, dots * SCALE, -1e30)
    m = jnp.maximum(jnp.max(sc, axis=-1, keepdims=True), sink)
    e = jnp.exp(sc - m)
    es = jnp.exp(sink - m)
    inv = 1.0 / (jnp.sum(e, axis=-1, keepdims=True) + es)
    return e * inv, es * inv


def _swa_heads(q_ref, rows, halves):
    nheads = SWA_Q_WIDTH // HEAD_DIM
    out = []
    for h in range(nheads):
        hb, p = h // 2, h % 2
        q2 = q_ref[rows, hb * LANES:(hb + 1) * LANES]
        out.append((hb, p, h // (nheads // 2), jnp.where(halves[p], q2, jnp.zeros_like(q2))))
    return out


def _swa_specs(s, t):
    nb = t // SWA_BLOCK
    cur = pl.BlockSpec((t, LANES), lambda i: (i, 0))
    prev = pl.BlockSpec((SWA_BLOCK, LANES), lambda i: (jnp.maximum(i * nb - 1, 0), 0))
    return cur, prev


def _swa_fwd(qn, kn, v, sinks):
    s = qn.shape[0]
    t = min(512, s)
    nb = t // SWA_BLOCK
    nheads = SWA_Q_WIDTH // HEAD_DIM

    def body(sink_ref, q_ref, kc_ref, kp_ref, vc_ref, vp_ref, o_ref):
        i = pl.program_id(0)
        left = _lane((1, LANES)) < HEAD_DIM
        halves = (left, jnp.logical_not(left))
        for b in range(nb):
            kvar = _swa_variants(_swa_band(kc_ref, kp_ref, b), left)
            vvar = _swa_variants(_swa_band(vc_ref, vp_ref, b), left)
            rows = slice(b * SWA_BLOCK, (b + 1) * SWA_BLOCK)
            valid = _swa_valid(i * nb + b)
            heads = _swa_heads(q_ref, rows, halves)
            dots = [_dot(qh, kvar[g][p], NT) for _, p, g, qh in heads]
            probs = [_swa_softmax(dots[h], sink_ref[h], valid)[0].astype(BF16) for h in range(nheads)]
            outs = [_dot(probs[h], vvar[g][p], NN) for h, (_, p, g, _) in enumerate(heads)]
            for hb in range(nheads // 2):
                o_ref[rows, hb * LANES:(hb + 1) * LANES] = (outs[2 * hb] + outs[2 * hb + 1]).astype(BF16)

    cur, prev = _swa_specs(s, t)
    return pl.pallas_call(
        body, name="swa_fwd", grid=(s // t,),
        in_specs=[pl.BlockSpec(memory_space=pltpu.SMEM), _rows(t, SWA_Q_WIDTH), cur, prev, cur, prev],
        out_specs=_rows(t, SWA_Q_WIDTH),
        out_shape=jax.ShapeDtypeStruct((s, SWA_Q_WIDTH), BF16),
        compiler_params=_cparams(("arbitrary",)),
    )(sinks, qn, kn, kn, v, v)


def _merge_fwd(x, osb, oswa, gates, wsb, wswa, wout):
    s = x.shape[0]
    tm = min(512, s)

    def body(x_ref, osb_ref, oswa_ref, gate_ref, wsb_ref, wswa_ref, wout_ref, x1_ref):
        ysb = _dot(osb_ref[...].astype(BF16), wsb_ref[...], NN)
        yswa = _dot(oswa_ref[...], wswa_ref[...], NN)
        merged = gate_ref[:, :D_MODEL] * ysb + gate_ref[:, D_MODEL:] * yswa
        x1_ref[...] = x_ref[...] + _dot(merged.astype(BF16), wout_ref[...], NN)

    return pl.pallas_call(
        body, name="merge_fwd", grid=(s // tm,),
        in_specs=[_rows(tm, D_MODEL), _rows(tm, SB_WIDTH), _rows(tm, SWA_Q_WIDTH), _rows(tm, 2 * D_MODEL),
                  _resident((SB_WIDTH, D_MODEL)), _resident((SWA_Q_WIDTH, D_MODEL)), _resident((D_MODEL, D_MODEL))],
        out_specs=_rows(tm, D_MODEL),
        out_shape=jax.ShapeDtypeStruct((s, D_MODEL), F32),
        compiler_params=_cparams(("arbitrary",)),
    )(x, osb, oswa, gates, wsb, wswa, wout)


def _mlp_fwd(x1, g, wup, wdown):
    s = x1.shape[0]
    tm = min(512, s)
    fc = D_FF // N_CHIPS

    def body(x_ref, g_ref, wup_ref, wdown_ref, x2_ref, u_ref):
        xf = x_ref[...]
        xn, _ = _rms(xf)
        h2 = (xn * g_ref[...]).astype(BF16)
        acc = xf
        for q in range(N_CHIPS):
            u = _dot(h2, wup_ref[q], NN)
            u_ref[:, q * fc:(q + 1) * fc] = u
            r = jnp.maximum(u, 0.0)
            acc = acc + _dot((r * r).astype(BF16), wdown_ref[q * fc:(q + 1) * fc, :], NN)
        x2_ref[...] = acc

    return pl.pallas_call(
        body, name="mlp_fwd", grid=(s // tm,),
        in_specs=[_rows(tm, D_MODEL), _resident((1, D_MODEL)), _resident((N_CHIPS, D_MODEL, fc)), _resident((D_FF, D_MODEL))],
        out_specs=[_rows(tm, D_MODEL), _rows(tm, D_FF)],
        out_shape=[jax.ShapeDtypeStruct((s, D_MODEL), F32), jax.ShapeDtypeStruct((s, D_FF), F32)],
        compiler_params=_cparams(("arbitrary",)),
    )(x1, g, wup, wdown)


def _loss_grad(y, target):
    s = y.shape[0]
    tm = min(512, s)

    def body(y_ref, t_ref, dy_ref, part_ref):
        err = y_ref[...] - t_ref[...]
        dy_ref[...] = err * (1.0 / D_MODEL)
        tot = jnp.sum(jnp.sum(err * err, axis=-1, keepdims=True), axis=0, keepdims=True)
        part_ref[...] = jnp.broadcast_to(tot.reshape(1, 1, 1), (1, 8, LANES))

    dy, part = pl.pallas_call(
        body, name="loss_grad", grid=(s // tm,),
        in_specs=[_rows(tm, D_MODEL), _rows(tm, D_MODEL)],
        out_specs=[_rows(tm, D_MODEL), pl.BlockSpec((1, 8, LANES), lambda i: (i, 0, 0))],
        out_shape=[jax.ShapeDtypeStruct((s, D_MODEL), F32), jax.ShapeDtypeStruct((s // tm, 8, LANES), F32)],
        compiler_params=_cparams(("arbitrary",)),
    )(y, target)
    return dy, (0.5 / D_MODEL) * jnp.sum(part[:, 0, 0])


def _mlp_bwd(dx2, x1, u, g, wup, wdown):
    s = x1.shape[0]
    tm = min(256, s)
    fc = D_FF // N_CHIPS

    def body(dx2_ref, x_ref, u_ref, g_ref, wup_ref, wdown_ref, dx1_ref, du_ref, a_ref, h2_ref, dg_ref, dxb_ref):
        @pl.when(pl.program_id(0) == 0)
        def _():
            dg_ref[...] = jnp.zeros_like(dg_ref)

        gam = g_ref[...]
        xn, rstd = _rms(x_ref[...])
        h2_ref[...] = (xn * gam).astype(BF16)
        dxf = dx2_ref[...]
        dxb = dxf.astype(BF16)
        dxb_ref[...] = dxb
        dh2 = jnp.zeros((tm, D_MODEL), F32)
        for q in range(N_CHIPS):
            cols = slice(q * fc, (q + 1) * fc)
            da = _dot(dxb, wdown_ref[cols, :], NT)
            r = jnp.maximum(u_ref[:, cols], 0.0)
            a_ref[:, cols] = (r * r).astype(BF16)
            du = (da * (2.0 * r)).astype(BF16)
            du_ref[:, cols] = du
            dh2 = dh2 + _dot(du, wup_ref[q], NT)
        dg_ref[...] += jnp.sum(dh2 * xn, axis=0, keepdims=True)
        dx1_ref[...] = dxf + _rms_bwd(dh2, xn, rstd, gam)

    return pl.pallas_call(
        body, name="mlp_bwd", grid=(s // tm,),
        in_specs=[_rows(tm, D_MODEL), _rows(tm, D_MODEL), _rows(tm, D_FF), _resident((1, D_MODEL)),
                  _resident((N_CHIPS, D_MODEL, fc)), _resident((D_FF, D_MODEL))],
        out_specs=[_rows(tm, D_MODEL), _rows(tm, D_FF), _rows(tm, D_FF), _rows(tm, D_MODEL),
                   pl.BlockSpec((1, D_MODEL), lambda i: (0, 0)), _rows(tm, D_MODEL)],
        out_shape=[jax.ShapeDtypeStruct((s, D_MODEL), F32), jax.ShapeDtypeStruct((s, D_FF), BF16),
                   jax.ShapeDtypeStruct((s, D_FF), BF16), jax.ShapeDtypeStruct((s, D_MODEL), BF16),
                   jax.ShapeDtypeStruct((1, D_MODEL), F32), jax.ShapeDtypeStruct((s, D_MODEL), BF16)],
        compiler_params=_cparams(("arbitrary",)),
    )(dx2, x1, u, g, wup, wdown)


def _wgrad(a, b, name, shard_axis=None):
    s, m = a.shape
    n = b.shape[1]
    tm = min(512, m if shard_axis != 0 else m // N_CHIPS)
    tn = min(512, n if shard_axis != 1 else n // N_CHIPS)
    if shard_axis is None and n % tn:
        tn = 256

    def body(a_ref, b_ref, o_ref, *narrow):
        res = _dot(a_ref[...].astype(BF16), b_ref[...].astype(BF16), TN)
        o_ref[...] = res.reshape(o_ref.shape)
        for n_ref in narrow:
            n_ref[...] = res.astype(BF16).reshape(n_ref.shape)

    if shard_axis is None:
        out_shape, out_spec = (m, n), pl.BlockSpec((tm, tn), lambda i, j: (i, j))
    elif shard_axis == 0:
        per = m // N_CHIPS // tm
        out_shape, out_spec = (N_CHIPS, m // N_CHIPS, n), pl.BlockSpec((1, tm, tn), lambda i, j: (i // per, i % per, j))
    else:
        per = n // N_CHIPS // tn
        out_shape, out_spec = (N_CHIPS, m, n // N_CHIPS), pl.BlockSpec((1, tm, tn), lambda i, j: (j // per, i, j % per))
    both = shard_axis is not None
    return pl.pallas_call(
        body, name=name, grid=(m // tm, n // tn),
        in_specs=[pl.BlockSpec((s, tm), lambda i, j: (0, i)), pl.BlockSpec((s, tn), lambda i, j: (0, j))],
        out_specs=[out_spec, out_spec] if both else out_spec,
        out_shape=[jax.ShapeDtypeStruct(out_shape, F32), jax.ShapeDtypeStruct(out_shape, BF16)] if both
        else jax.ShapeDtypeStruct(out_shape, F32),
        compiler_params=_cparams(("arbitrary", "arbitrary")),
    )(a, b)


def _merge_bwd(dx1, osb, oswa, gates, wsb, wswa, wout):
    s = dx1.shape[0]
    tm = min(512, s)

    def body(dx_ref, osb_ref, oswa_ref, gate_ref, wsb_ref, wswa_ref, wout_ref,
             dosb_ref, doswa_ref, dgl_ref, merged_ref, dysb_ref, dyswa_ref, dxb_ref):
        dxb = dx_ref[...].astype(BF16)
        dxb_ref[...] = dxb
        dm = _dot(dxb, wout_ref[...], NT)
        ysb = _dot(osb_ref[...].astype(BF16), wsb_ref[...], NN)
        yswa = _dot(oswa_ref[...], wswa_ref[...], NN)
        g0 = gate_ref[:, :D_MODEL]
        g1 = gate_ref[:, D_MODEL:]
        merged_ref[...] = (g0 * ysb + g1 * yswa).astype(BF16)
        dgl_ref[:, :D_MODEL] = (dm * ysb * (g0 * (1.0 - g0))).astype(BF16)
        dgl_ref[:, D_MODEL:] = (dm * yswa * (g1 * (1.0 - g1))).astype(BF16)
        dysb = (dm * g0).astype(BF16)
        dyswa = (dm * g1).astype(BF16)
        dysb_ref[...] = dysb
        dyswa_ref[...] = dyswa
        dosb_ref[...] = _dot(dysb, wsb_ref[...], NT)
        doswa_ref[...] = _dot(dyswa, wswa_ref[...], NT)

    return pl.pallas_call(
        body, name="merge_bwd", grid=(s // tm,),
        in_specs=[_rows(tm, D_MODEL), _rows(tm, SB_WIDTH), _rows(tm, SWA_Q_WIDTH), _rows(tm, 2 * D_MODEL),
                  _resident((SB_WIDTH, D_MODEL)), _resident((SWA_Q_WIDTH, D_MODEL)), _resident((D_MODEL, D_MODEL))],
        out_specs=[_rows(tm, SB_WIDTH), _rows(tm, SWA_Q_WIDTH), _rows(tm, 2 * D_MODEL), _rows(tm, D_MODEL),
                   _rows(tm, D_MODEL), _rows(tm, D_MODEL), _rows(tm, D_MODEL)],
        out_shape=[jax.ShapeDtypeStruct((s, SB_WIDTH), F32), jax.ShapeDtypeStruct((s, SWA_Q_WIDTH), F32),
                   jax.ShapeDtypeStruct((s, 2 * D_MODEL), BF16), jax.ShapeDtypeStruct((s, D_MODEL), BF16),
                   jax.ShapeDtypeStruct((s, D_MODEL), BF16), jax.ShapeDtypeStruct((s, D_MODEL), BF16),
                   jax.ShapeDtypeStruct((s, D_MODEL), BF16)],
        compiler_params=_cparams(("arbitrary",)),
    )(dx1, osb, oswa, gates, wsb, wswa, wout)


def _sb_bwd(sbqkv, osb, dosb, send):
    s = sbqkv.shape[0]
    t = min(SB_TILE, s)
    nq = s // t
    ng = SB_WIDTH // SB_STEP_WIDTH
    heads = range(2 * SB_STEP_WIDTH // LANES)
    ns = len(send)

    def body(*refs):
        q_ref, k_ref, v_ref, o_ref, do_ref = refs[:5]
        srcs = refs[5:5 + ns]
        dq_ref, dk_ref, dv_ref = refs[5 + ns:8 + ns]
        outs = refs[8 + ns:8 + 2 * ns]
        j, i = pl.program_id(0), pl.program_id(1)
        if ns:
            ssem, rsem = refs[8 + 2 * ns:]
            _hosted_exchange(srcs, outs, ssem, rsem, (j == 0) & (i == 0), (j == ng - 1) & (i == nq - 1))

        @pl.when(i == 0)
        def _():
            dk_ref[...] = jnp.zeros_like(dk_ref)
            dv_ref[...] = jnp.zeros_like(dv_ref)

        left = _lane((1, LANES)) < HEAD_DIM
        halves = (left, jnp.logical_not(left))
        pair = lambda a, p: a[:, (p // 2) * LANES:(p // 2 + 1) * LANES]
        only = lambda a, p: jnp.where(halves[p % 2], pair(a, p), jnp.zeros((), a.dtype))
        scale = jnp.asarray(SCALE, BF16)
        q_all = q_ref[...] * scale
        do_all = do_ref[...].astype(BF16)
        prod = do_all.astype(F32) * o_ref[...]
        qh = [only(q_all, p) for p in heads]
        doh = [only(do_all, p) for p in heads]
        delta = [jnp.sum(only(prod, p), axis=-1, keepdims=True) for p in heads]
        r = lax.broadcasted_iota(jnp.int32, (t, t), 0)
        c = lax.broadcasted_iota(jnp.int32, (t, t), 1)
        u_from = (r >= c).astype(BF16)
        u_from2 = _tri_twice(t)

        def tile(n, carry, strict):
            runs, dqs = carry
            off = pl.multiple_of((i - n) * t, t)
            k = k_ref[pl.ds(off, t), :]
            v = v_ref[pl.ds(off, t), :]
            ks = k * scale
            terms = [_sb_tile_terms(qh[p], pair(k, p), strict) for p in heads]
            logw = [_log_weights(*terms[p], runs[p][0], u_from) for p in heads]
            dws = [_dot(doh[p], pair(v, p), NT) for p in heads]
            wbs = [_masked(strict, jnp.exp(logw[p][0])).astype(BF16) for p in heads]
            es = [dws[p] * wbs[p].astype(F32) for p in heads]
            rests = [runs[p][1] + _split_dot(es[p], u_from2) for p in heads]
            betas = [jnp.exp(terms[p][0] + terms[p][1]) for p in heads]
            dzs = [_masked(strict, es[p] - betas[p] * (es[p] + delta[p] - rests[p])).astype(BF16) for p in heads]
            dqs = list(dqs)
            for g in range(len(heads) // 2):
                a, b = 2 * g, 2 * g + 1
                cols = slice(g * LANES, (g + 1) * LANES)
                dv_ref[pl.ds(off, t), cols] += _dot(wbs[a], doh[a], TN) + _dot(wbs[b], doh[b], TN)
                dk_ref[pl.ds(off, t), cols] += _dot(dzs[a], qh[a], TN) + _dot(dzs[b], qh[b], TN)
                dqs[g] = dqs[g] + _dot(dzs[a], only(ks, a), NN) + _dot(dzs[b], only(ks, b), NN)
            new_runs = tuple((logw[p][1], rests[p][:, 0:1]) for p in heads)
            return new_runs, tuple(dqs)

        zero = jnp.zeros((t, 1), F32)
        wide = jnp.zeros((t, LANES), F32)
        carry = tile(0, (tuple((zero, zero) for _ in heads), tuple(wide for _ in heads[::2])), c < r)
        _, dqs = lax.fori_loop(1, i + 1, lambda n, cr: tile(n, cr, None), carry)
        for g, dq in enumerate(dqs):
            dq_ref[:, g * LANES:(g + 1) * LANES] = dq.astype(BF16)

    w = SB_STEP_WIDTH
    blk = pl.BlockSpec((t, w), lambda j, i: (i, j))
    whole = pl.BlockSpec((s, w), lambda j, i: (0, j))
    return pl.pallas_call(
        body, name="sb_bwd_x%d" % ns, grid=(ng, nq),
        in_specs=[blk, pl.BlockSpec((s, w), lambda j, i: (0, ng + j)),
                  pl.BlockSpec((s, w), lambda j, i: (0, 2 * ng + j)), blk, blk] + [HBM] * ns,
        out_specs=[blk, whole, whole] + [HBM] * ns,
        out_shape=[jax.ShapeDtypeStruct((s, SB_WIDTH), BF16), jax.ShapeDtypeStruct((s, SB_WIDTH), F32),
                   jax.ShapeDtypeStruct((s, SB_WIDTH), F32)] + [jax.ShapeDtypeStruct(a.shape, a.dtype) for a in send],
        scratch_shapes=_exchange_scratch(ns) if ns else [],
        compiler_params=_cparams(("arbitrary", "arbitrary")),
    )(sbqkv, sbqkv, sbqkv, osb, dosb, *send)


def _swa_bwd(qn, kn, v, sinks, do):
    s = qn.shape[0]
    t = min(512, s)
    nb = t // SWA_BLOCK
    nheads = SWA_Q_WIDTH // HEAD_DIM

    def body(sink_ref, q_ref, kc_ref, kp_ref, vc_ref, vp_ref, do_ref, dq_ref, dk_ref, dv_ref, dsink_ref):
        i = pl.program_id(0)

        @pl.when(i == 0)
        def _():
            dk_ref[...] = jnp.zeros_like(dk_ref)
            dv_ref[...] = jnp.zeros_like(dv_ref)

        left = _lane((1, LANES)) < HEAD_DIM
        halves = (left, jnp.logical_not(left))
        dsink = [jnp.zeros((1, 1), F32) for _ in range(nheads)]
        for b in range(nb):
            kvar = _swa_variants(_swa_band(kc_ref, kp_ref, b), left)
            vvar = _swa_variants(_swa_band(vc_ref, vp_ref, b), left)
            rows = slice(b * SWA_BLOCK, (b + 1) * SWA_BLOCK)
            valid = _swa_valid(i * nb + b)
            heads = _swa_heads(q_ref, rows, halves)
            doh = []
            for hb, p, _, _ in heads:
                do2 = do_ref[rows, hb * LANES:(hb + 1) * LANES]
                doh.append(jnp.where(halves[p], do2, 0.0).astype(BF16))
            dots = [_dot(qh, kvar[g][p], NT) for _, p, g, qh in heads]
            dps = [_dot(doh[h], vvar[g][p], NT) for h, (_, p, g, _) in enumerate(heads)]
            dss, pbs = [], []
            for h in range(nheads):
                probs, psink = _swa_softmax(dots[h], sink_ref[h], valid)
                delta = jnp.sum(probs * dps[h], axis=-1, keepdims=True)
                dss.append((probs * (dps[h] - delta) * SCALE).astype(BF16))
                pbs.append(probs.astype(BF16))
                dsink[h] = dsink[h] - jnp.sum(psink * delta, axis=0, keepdims=True)
            dk_acc = [jnp.zeros((2 * SWA_BLOCK, LANES), F32) for _ in range(2)]
            dv_acc = [jnp.zeros((2 * SWA_BLOCK, LANES), F32) for _ in range(2)]
            dqs = [_dot(dss[h], kvar[g][p], NN) for h, (_, p, g, _) in enumerate(heads)]
            for h, (_, p, g, qh) in enumerate(heads):
                which = 0 if p == g else 1
                dk_acc[which] = dk_acc[which] + _dot(dss[h], qh, TN)
                dv_acc[which] = dv_acc[which] + _dot(pbs[h], doh[h], TN)
            for hb in range(nheads // 2):
                dq_ref[rows, hb * LANES:(hb + 1) * LANES] = dqs[2 * hb] + dqs[2 * hb + 1]
            dkb = dk_acc[0] + pltpu.roll(dk_acc[1], HEAD_DIM, 1)
            dvb = dv_acc[0] + pltpu.roll(dv_acc[1], HEAD_DIM, 1)
            start = pl.multiple_of((i * nb + b) * SWA_BLOCK, SWA_BLOCK)
            dk_ref[pl.ds(start, SWA_BLOCK), :] += dkb[SWA_BLOCK:]
            dv_ref[pl.ds(start, SWA_BLOCK), :] += dvb[SWA_BLOCK:]

            @pl.when(i * nb + b > 0)
            def _(dkb=dkb, dvb=dvb, start=start):
                before = pl.multiple_of(jnp.maximum(start - SWA_BLOCK, 0), SWA_BLOCK)
                dk_ref[pl.ds(before, SWA_BLOCK), :] += dkb[:SWA_BLOCK]
                dv_ref[pl.ds(before, SWA_BLOCK), :] += dvb[:SWA_BLOCK]

        for h in range(nheads):
            dsink_ref[0, h:h + 1, :] = jnp.broadcast_to(dsink[h], (1, LANES))

    cur, prev = _swa_specs(s, t)
    whole = pl.BlockSpec((s, LANES), lambda i: (0, 0))
    return pl.pallas_call(
        body, name="swa_bwd", grid=(s // t,),
        in_specs=[pl.BlockSpec(memory_space=pltpu.SMEM), _rows(t, SWA_Q_WIDTH), cur, prev, cur, prev,
                  _rows(t, SWA_Q_WIDTH)],
        out_specs=[_rows(t, SWA_Q_WIDTH), whole, whole, pl.BlockSpec((1, 8, LANES), lambda i: (i, 0, 0))],
        out_shape=[jax.ShapeDtypeStruct((s, SWA_Q_WIDTH), F32), jax.ShapeDtypeStruct((s, SWA_KV_WIDTH), F32),
                   jax.ShapeDtypeStruct((s, SWA_KV_WIDTH), F32), jax.ShapeDtypeStruct((s // t, 8, LANES), F32)],
        compiler_params=_cparams(("arbitrary",)),
    )(sinks, qn, kn, kn, v, v, do)


def _swa_post(raw, dqn, dkn, qg, kg, cos, sin):
    s = raw.shape[0]
    tm = min(512, s)
    nq = SWA_Q_WIDTH // LANES

    def body(raw_ref, dq_ref, dk_ref, qg_ref, kg_ref, cos_ref, sin_ref, out_ref, dg_ref):
        @pl.when(pl.program_id(0) == 0)
        def _():
            dg_ref[...] = jnp.zeros_like(dg_ref)

        cs, sn = cos_ref[...], sin_ref[...]
        dgq = jnp.zeros((1, LANES), F32)
        for b in range(nq):
            cols = slice(b * LANES, (b + 1) * LANES)
            dp, dg = _norm_rope_bwd(raw_ref[:, cols], dq_ref[:, cols], qg_ref[...], cs, sn)
            out_ref[:, cols] = dp.astype(BF16)
            dgq = dgq + dg
        cols = slice(SWA_Q_WIDTH, SWA_Q_WIDTH + LANES)
        dp, dgk = _norm_rope_bwd(raw_ref[:, cols], dk_ref[...], kg_ref[...], cs, sn)
        out_ref[:, cols] = dp.astype(BF16)
        dg_ref[0:1, :] += dgq
        dg_ref[1:2, :] += dgk

    return pl.pallas_call(
        body, name="swa_post", grid=(s // tm,),
        in_specs=[_rows(tm, 640), _rows(tm, SWA_Q_WIDTH), _rows(tm, LANES), _resident((1, LANES)),
                  _resident((1, LANES)), _rows(tm, LANES), _rows(tm, LANES)],
        out_specs=[_rows(tm, 640), pl.BlockSpec((8, LANES), lambda i: (0, 0))],
        out_shape=[jax.ShapeDtypeStruct((s, 640), BF16), jax.ShapeDtypeStruct((8, LANES), F32)],
        compiler_params=_cparams(("arbitrary",)),
    )(raw, dqn, dkn, qg, kg, cos, sin)


def _inproj_bwd(dsbq, dsbk, dsbv, dswqk, dswv, dgl, x, dx1, g, w):
    s = x.shape[0]
    tm = min(256, s)

    def body(dsbq_ref, dsbk_ref, dsbv_ref, dswqk_ref, dswv_ref, dgl_ref, x_ref, dx1_ref, g_ref, w_ref,
             dx_ref, dproj_ref, dg_ref):
        @pl.when(pl.program_id(0) == 0)
        def _():
            dg_ref[...] = jnp.zeros_like(dg_ref)

        dproj_ref[:, 0:512] = dsbq_ref[...]
        dproj_ref[:, 512:1024] = dsbk_ref[...].astype(BF16)
        dproj_ref[:, 1024:1536] = dsbv_ref[...].astype(BF16)
        dproj_ref[:, 1536:2176] = dswqk_ref[...]
        dproj_ref[:, 2176:2304] = dswv_ref[...].astype(BF16)
        dproj_ref[:, GATE_OFF:IN_WIDTH] = dgl_ref[...]
        dh = jnp.zeros((tm, D_MODEL), F32)
        for a in range(0, IN_WIDTH, 512):
            b = min(a + 512, IN_WIDTH)
            dh = dh + _dot(dproj_ref[:, a:b], w_ref[:, a:b], NT)
        gam = g_ref[...]
        xn, rstd = _rms(x_ref[...])
        dg_ref[...] += jnp.sum(dh * xn, axis=0, keepdims=True)
        dx_ref[...] = dx1_ref[...] + _rms_bwd(dh, xn, rstd, gam)

    return pl.pallas_call(
        body, name="inproj_bwd", grid=(s // tm,),
        in_specs=[_rows(tm, 512), _rows(tm, 512), _rows(tm, 512), _rows(tm, 640), _rows(tm, LANES),
                  _rows(tm, 2 * D_MODEL), _rows(tm, D_MODEL), _rows(tm, D_MODEL), _resident((1, D_MODEL)),
                  _resident((D_MODEL, IN_WIDTH))],
        out_specs=[_rows(tm, D_MODEL), _rows(tm, IN_WIDTH), pl.BlockSpec((1, D_MODEL), lambda i: (0, 0))],
        out_shape=[jax.ShapeDtypeStruct((s, D_MODEL), F32), jax.ShapeDtypeStruct((s, IN_WIDTH), BF16),
                   jax.ShapeDtypeStruct((1, D_MODEL), F32)],
        compiler_params=_cparams(("arbitrary",)),
    )(dsbq, dsbk, dsbv, dswqk, dswv, dgl, x, dx1, g, w)


def _gather_weights(shard):
    def body(src, out, ssem, rsem, lsem):
        once = pl.program_id(0) == 0
        _hosted_gather(src, out, ssem, rsem, lsem, once, once, once)

    return pl.pallas_call(
        body, name="gather_weights", grid=(1,), in_specs=[HBM], out_specs=HBM,
        out_shape=jax.ShapeDtypeStruct((N_CHIPS,) + shard.shape, shard.dtype),
        scratch_shapes=_gather_scratch(1),
        compiler_params=pltpu.CompilerParams(dimension_semantics=("arbitrary",), has_side_effects=True),
    )(shard)


def _exchange(send):
    ns = len(send)

    def body(*refs):
        once = pl.program_id(0) == 0
        _hosted_exchange(refs[:ns], refs[ns:2 * ns], *refs[2 * ns:], once, once)

    return pl.pallas_call(
        body, name="exchange_x%d" % ns, grid=(1,), in_specs=[HBM] * ns, out_specs=[HBM] * ns,
        out_shape=[jax.ShapeDtypeStruct(a.shape, a.dtype) for a in send],
        scratch_shapes=_exchange_scratch(ns),
        compiler_params=pltpu.CompilerParams(dimension_semantics=("arbitrary",), has_side_effects=True),
    )(*send)


def _pair_swap(arrs):
    n = len(arrs)

    def body(*refs):
        x, y, c, _ = _place()
        cps = [pltpu.make_async_remote_copy(refs[t], refs[n + t], refs[2 * n].at[t], refs[2 * n + 1].at[t],
                                            device_id=(x, y, 1 - c), device_id_type=MESH) for t in range(n)]
        for cp in cps:
            cp.start()
        for cp in cps:
            cp.wait()

    return pl.pallas_call(
        body, name="pair_swap", in_specs=[HBM] * n, out_specs=[HBM] * n,
        out_shape=[jax.ShapeDtypeStruct(a.shape, a.dtype) for a in arrs],
        scratch_shapes=[pltpu.SemaphoreType.DMA((n,)), pltpu.SemaphoreType.DMA((n,))],
        compiler_params=pltpu.CompilerParams(has_side_effects=True),
    )(*arrs)


def _allreduce_small(block):
    def body(src, out, buf, ssem, rsem):
        x, y, c, _ = _place()
        me = 4 * x + 2 * y + c
        buf[me] = src[...]
        cps = []
        for k in range(1, N_DEV):
            peer = (x ^ (k >> 2), y ^ ((k >> 1) & 1), c ^ (k & 1))
            cp = pltpu.make_async_remote_copy(src, buf.at[me], ssem.at[k - 1], rsem.at[k - 1], device_id=peer, device_id_type=MESH)
            cp.start()
            cps.append(cp)
        for k in range(1, N_DEV):
            got = buf.at[me ^ k]
            pltpu.make_async_remote_copy(got, got, ssem.at[k - 1], rsem.at[k - 1], device_id=(x, y, c), device_id_type=MESH).wait_recv()
        for cp in cps:
            cp.wait_send()
        tot = buf[0]
        for d in range(1, N_DEV):
            tot = tot + buf[d]
        out[...] = tot

    vm = pl.BlockSpec(memory_space=pltpu.VMEM)
    return pl.pallas_call(
        body, name="allreduce_small", in_specs=[vm], out_specs=vm,
        out_shape=jax.ShapeDtypeStruct(block.shape, F32),
        scratch_shapes=[pltpu.VMEM((N_DEV,) + block.shape, F32), pltpu.SemaphoreType.DMA((N_DEV - 1,)),
                        pltpu.SemaphoreType.DMA((N_DEV - 1,))],
        compiler_params=pltpu.CompilerParams(has_side_effects=True),
    )(block)


def _sum_chips(landed, own, into, layer, name):
    nq, k, n = landed.shape
    tr = min(256, k)
    me = (2 * lax.axis_index("x") + lax.axis_index("y")).astype(jnp.int32).reshape(1)

    def body(me_ref, p0, p1, p2, p3, own_ref, into_ref, o_ref):
        mine = own_ref[0]
        terms = [jnp.where(me_ref[0] == q, mine, p[0].astype(F32)) for q, p in enumerate((p0, p1, p2, p3))]
        o_ref[0] = ((terms[0] + terms[1]) + terms[2]) + terms[3]

    spec = lambda q: pl.BlockSpec((1, tr, n), lambda i, m, q=q: (q, i, 0))
    return pl.pallas_call(
        body, name=name,
        grid_spec=pltpu.PrefetchScalarGridSpec(
            num_scalar_prefetch=1, grid=(k // tr,),
            in_specs=[spec(q) for q in range(nq)] + [pl.BlockSpec((1, tr, n), lambda i, m: (m[0], i, 0)), HBM],
            out_specs=pl.BlockSpec((1, tr, n), lambda i, m: (layer, i, 0))),
        out_shape=jax.ShapeDtypeStruct(into.shape, F32),
        input_output_aliases={6: 0},
        compiler_params=_cparams(("arbitrary",)),
    )(me, landed, landed, landed, landed, own, into)


def _adamw(w, g, g2, m, v, name):
    shape = w.shape
    cols = shape[-1]
    flat = lambda t: t.reshape(-1, cols)
    rows = flat(w).shape[0]
    tr = min(512, rows)
    pair = g2 is not None

    def body(*refs):
        if pair:
            w_ref, g_ref, g2_ref, m_ref, v_ref, go_ref, d_ref, nm_ref, nv_ref = refs
            gr = g_ref[...] + g2_ref[...]
        else:
            w_ref, g_ref, m_ref, v_ref, go_ref, d_ref, nm_ref, nv_ref = refs
            gr = g_ref[...]
        go_ref[...] = gr
        nm = ADAM_B1 * m_ref[...] + (1.0 - ADAM_B1) * gr
        nv = ADAM_B2 * v_ref[...] + (1.0 - ADAM_B2) * (gr * gr)
        m_hat = nm / (1.0 - ADAM_B1 ** ADAM_STEP)
        v_hat = nv / (1.0 - ADAM_B2 ** ADAM_STEP)
        d_ref[...] = -ADAM_LR * (m_hat / (jnp.sqrt(v_hat) + ADAM_EPS) + ADAM_WD * w_ref[...])
        nm_ref[...] = nm
        nv_ref[...] = nv

    spec = pl.BlockSpec((tr, cols), lambda i: (i, 0))
    ins = [w, g] + ([g2] if pair else []) + [m, v]
    outs = pl.pallas_call(
        body, name=name, grid=(rows // tr,),
        in_specs=[spec] * len(ins), out_specs=[spec] * 4,
        out_shape=[jax.ShapeDtypeStruct((rows, cols), F32)] * 4,
        compiler_params=_cparams(("arbitrary",)),
    )(*[flat(t) for t in ins])
    return [o.reshape(shape) for o in outs]


def _pack_layer(big, l, names=BIG):
    rows = dict(PACK_ROWS)
    return jnp.concatenate([big[name][l].astype(BF16).reshape(rows[name], 1024) for name in names], axis=0)


def _whole_weights(gathered, names=BIG):
    out, at = {}, 0
    for name in names:
        rows = dict(PACK_ROWS)[name]
        t = gathered[:, at:at + rows, :].reshape((N_CHIPS,) + SHARD_SHAPES[name])
        at += rows
        if name in ROW_SHARDED:
            out[name] = t.reshape(N_CHIPS * t.shape[1], t.shape[2])
        elif name == "w_up":
            out[name] = t
        else:
            out[name] = jnp.moveaxis(t, 0, 1).reshape(t.shape[1], N_CHIPS * t.shape[2])
    return out


def _pack_small(d):
    flat = jnp.concatenate([d[n].reshape(-1) for n in SMALL_NAMES])
    return jnp.pad(flat, (0, SMALL_ROWS * LANES - flat.shape[0])).reshape(SMALL_ROWS, LANES)


def _unpack_small(block, like):
    flat, out, at = block.reshape(-1), {}, 0
    for n in SMALL_NAMES:
        size = like[n].size
        out[n] = flat[at:at + size].reshape(like[n].shape)
        at += size
    return out


def _rope_tables(s):
    inv_freq = 1.0 / (ROPE_THETA ** (jnp.arange(0, HEAD_DIM, 2, dtype=F32) / HEAD_DIM))
    ang = jnp.arange(s, dtype=F32)[:, None] * inv_freq[None, :]
    reps = LANES // (HEAD_DIM // 2)
    return jnp.tile(jnp.cos(ang), (1, reps)), jnp.tile(jnp.sin(ang), (1, reps))


def _forward_backward(x, target, big, small):
    s = x.shape[0]
    cos, sin = _rope_tables(s)
    two = lambda gvec: jnp.tile(gvec.reshape(1, HEAD_DIM), (1, 2))
    saved = []
    first, rest = BIG[:1], BIG[1:]
    win = _whole_weights(_gather_weights(_pack_layer(big, 0, first)), first)
    for l in range(DEPTH):
        gm = small["mix_norm_g"][l].reshape(1, D_MODEL)
        gl = small["mlp_norm_g"][l].reshape(1, D_MODEL)
        qg, kg = two(small["q_norm_g"][l]), two(small["k_norm_g"][l])
        h, sbqkv, raw, qn, kn, v, gates = _inproj_fwd(x, gm, win["w_in"], qg, kg, cos, sin)
        shards = [_pack_layer(big, l, rest)] + ([_pack_layer(big, l + 1, first)] if l + 1 < DEPTH else [])
        osb, *landed = _sb_fwd(sbqkv, shards)
        wts = {**win, **_whole_weights(landed[0], rest)}
        nxt = _whole_weights(landed[1], first) if len(landed) > 1 else None
        oswa = _swa_fwd(qn, kn, v, small["sinks"][l])
        x1 = _merge_fwd(x, osb, oswa, gates, wts["w_branch_sb"], wts["w_branch_swa"], wts["w_out"])
        x2, u = _mlp_fwd(x1, gl, wts["w_up"], wts["w_down"])
        saved.append((x, h, sbqkv, raw, qn, kn, v, gates, osb, oswa, x1, u, gm, gl, qg, kg, wts))
        x, win = x2, nxt
    dx, loss = _loss_grad(x, target)

    got = {name: [None] * DEPTH for name in BIG}
    gsmall = {name: [None] * DEPTH for name in SMALL_NAMES}
    late = []
    for l in reversed(range(DEPTH)):
        x0, h, sbqkv, raw, qn, kn, v, gates, osb, oswa, x1, u, gm, gl, qg, kg, wts = saved[l]
        dx1, du, act, h2, dgl_mlp, dxb = _mlp_bwd(dx, x1, u, gl, wts["w_up"], wts["w_down"])
        dosb, doswa, dgate, merged, dysb, dyswa, dx1b = _merge_bwd(
            dx1, osb, oswa, gates, wts["w_branch_sb"], wts["w_branch_swa"], wts["w_out"])
        ready = [("w_down", l, *_wgrad(act, dxb, "wgrad_down", shard_axis=0)),
                 ("w_up", l, *_wgrad(h2, du, "wgrad_up", shard_axis=1)),
                 ("w_out", l, *_wgrad(merged, dx1b, "wgrad_out", shard_axis=0)),
                 ("w_branch_sb", l, *_wgrad(osb, dysb, "wgrad_bsb", shard_axis=1)),
                 ("w_branch_swa", l, *_wgrad(oswa, dyswa, "wgrad_bswa", shard_axis=1))] + late
        dsbq, dsbk, dsbv, *landed = _sb_bwd(sbqkv, osb, dosb, [narrow for _, _, _, narrow in ready])
        for (name, layer, own, _), arr in zip(ready, landed):
            got[name][layer] = (arr, own)
        dqn, dkn, dswv, dsink = _swa_bwd(qn, kn, v, small["sinks"][l], doswa)
        dswqk, dgqk = _swa_post(raw, dqn, dkn, qg, kg, cos, sin)
        dx, dproj, dg_mix = _inproj_bwd(dsbq, dsbk, dsbv, dswqk, dswv, dgate, x0, dx1, gm, wts["w_in"])
        dwin = _wgrad(h, dproj, "wgrad_in")
        dwin = dwin.reshape(D_MODEL, N_CHIPS, IN_WIDTH // N_CHIPS).transpose(1, 0, 2)
        late = [("w_in", l, dwin, dwin.astype(BF16))]
        gsmall["mix_norm_g"][l] = dg_mix[0]
        gsmall["mlp_norm_g"][l] = dgl_mlp[0]
        gsmall["q_norm_g"][l] = dgqk[0, :HEAD_DIM] + dgqk[0, HEAD_DIM:]
        gsmall["k_norm_g"][l] = dgqk[1, :HEAD_DIM] + dgqk[1, HEAD_DIM:]
        gsmall["sinks"][l] = jnp.sum(dsink[:, :, 0], axis=0)
    got["w_in"][0] = (_exchange([late[0][3]])[0], late[0][2])
    gsmall = {k: jnp.stack(vs) for k, vs in gsmall.items()}
    return loss, dx, got, gsmall


def kernel(x, mix_norm_g, w_in, q_norm_g, k_norm_g, sinks, w_branch_sb, w_branch_swa, w_out, mlp_norm_g, w_up, w_down, loss_target, m_mix_norm_g, m_w_in, m_q_norm_g, m_k_norm_g, m_sinks, m_w_branch_sb, m_w_branch_swa, m_w_out, m_mlp_norm_g, m_w_up, m_w_down, v_mix_norm_g, v_w_in, v_q_norm_g, v_k_norm_g, v_sinks, v_w_branch_sb, v_w_branch_swa, v_w_out, v_mlp_norm_g, v_w_up, v_w_down):
    big = dict(w_in=w_in, w_branch_sb=w_branch_sb, w_branch_swa=w_branch_swa, w_out=w_out, w_up=w_up, w_down=w_down)
    big_m = dict(w_in=m_w_in, w_branch_sb=m_w_branch_sb, w_branch_swa=m_w_branch_swa, w_out=m_w_out, w_up=m_w_up, w_down=m_w_down)
    big_v = dict(w_in=v_w_in, w_branch_sb=v_w_branch_sb, w_branch_swa=v_w_branch_swa, w_out=v_w_out, w_up=v_w_up, w_down=v_w_down)
    small = dict(mix_norm_g=mix_norm_g, q_norm_g=q_norm_g, k_norm_g=k_norm_g, sinks=sinks, mlp_norm_g=mlp_norm_g)
    small_m = dict(mix_norm_g=m_mix_norm_g, q_norm_g=m_q_norm_g, k_norm_g=m_k_norm_g, sinks=m_sinks, mlp_norm_g=m_mlp_norm_g)
    small_v = dict(mix_norm_g=v_mix_norm_g, q_norm_g=v_q_norm_g, k_norm_g=v_k_norm_g, sinks=v_sinks, mlp_norm_g=v_mlp_norm_g)

    loss_part, grad_x, got, gsmall = _forward_backward(x[0], loss_target[0], big, small)
    loss = lax.psum(loss_part, ("x", "y", "c"))

    mine = []
    for name in BIG:
        tot = lax.empty((DEPTH,) + SHARD_SHAPES[name], F32)
        for l in range(DEPTH):
            tot = _sum_chips(*got[name][l], tot, l, "sum_" + name)
        mine.append(tot)
    theirs = _pair_swap(mine)
    upd = {name: _adamw(big[name], mine[i], theirs[i], big_m[name], big_v[name], "adamw_" + name)
           for i, name in enumerate(BIG)}
    g_small = _allreduce_small(_pack_small(gsmall))
    sm = _adamw(_pack_small(small), g_small, None, _pack_small(small_m), _pack_small(small_v), "adamw_small")
    upd_small = [_unpack_small(t, small) for t in sm]

    names = ("mix_norm_g", "w_in", "q_norm_g", "k_norm_g", "sinks", "w_branch_sb", "w_branch_swa", "w_out",
             "mlp_norm_g", "w_up", "w_down")
    pick = lambda n, i: upd[n][i] if n in upd else upd_small[i][n]
    return (loss, grad_x[None], *[pick(n, 0) for n in names], *[pick(n, 1) for n in names],
            *[pick(n, 2) for n in names], *[pick(n, 3) for n in names])
```

```python
import jax
import jax.numpy as jnp
from jax import lax
from jax.experimental import pallas as pl
from jax.experimental.pallas import tpu as pltpu

F32 = jnp.float32
BF16 = jnp.bfloat16

D_MODEL = 1024
DEPTH = 4
HEAD_DIM = 64
SB_WIDTH = 512
SWA_Q_WIDTH = 512
SWA_KV_WIDTH = 128
D_FF = 4096
IN_WIDTH = 4352
GATE_OFF = 2304
ROPE_THETA = 10000.0
NORM_EPS = 1e-6
SCALE = HEAD_DIM ** -0.5
N_CHIPS = 4
N_DEV = 8

ADAM_LR = 0.001
ADAM_B1 = 0.9
ADAM_B2 = 0.999
ADAM_EPS = 1e-08
ADAM_WD = 0.01
ADAM_STEP = 10

LANES = 128
SB_TILE = 256
SB_STEP_WIDTH = 256
SB_FWD_STEP_WIDTH = 512
SWA_BLOCK = 128
VMEM_LIMIT = 56 << 20

NN = (((1,), (0,)), ((), ()))
NT = (((1,), (1,)), ((), ()))
TN = (((0,), (0,)), ((), ()))
MESH = pl.DeviceIdType.MESH
HBM = pl.BlockSpec(memory_space=pl.ANY)

BIG = ("w_in", "w_branch_sb", "w_branch_swa", "w_out", "w_up", "w_down")
SMALL_NAMES = ("mix_norm_g", "q_norm_g", "k_norm_g", "sinks", "mlp_norm_g")
PACK_ROWS = (("w_in", 1088), ("w_branch_sb", 128), ("w_branch_swa", 128), ("w_out", 256), ("w_up", 1024), ("w_down", 1024))
SHARD_SHAPES = {"w_in": (1024, 1088), "w_branch_sb": (512, 256), "w_branch_swa": (512, 256), "w_out": (256, 1024),
                "w_up": (1024, 1024), "w_down": (1024, 1024)}
ROW_SHARDED = ("w_out", "w_down")
SMALL_ROWS = 72


def _dot(a, b, dims):
    return lax.dot_general(a, b, dims, preferred_element_type=F32)


def _cparams(sem):
    return pltpu.CompilerParams(dimension_semantics=sem, vmem_limit_bytes=VMEM_LIMIT)


def _resident(shape, index=None):
    index = (0,) * len(shape) if index is None else index
    return pl.BlockSpec(shape, lambda *_: index, pipeline_mode=pl.Buffered(1))


REST = ("w_up", "w_down", "w_out", "w_branch_sb", "w_branch_swa")
REST_ROWS = 2560
W_UP_BLOCK = ((N_CHIPS, 1024, 1024), (0, 0, 0))
W_DOWN_BLOCK = ((N_CHIPS, 1024, 1024), (0, 1, 0))
W_OUT_BLOCK = ((N_CHIPS, 256, 1024), (0, 8, 0))


def _rows(tm, width):
    return pl.BlockSpec((tm, width), lambda i: (i, 0))


def _rms(xf):
    rstd = lax.rsqrt(jnp.mean(xf * xf, axis=-1, keepdims=True) + NORM_EPS)
    return xf * rstd, rstd


def _rms_bwd(dh, xn, rstd, g):
    dxn = dh * g
    return rstd * (dxn - xn * jnp.mean(dxn * xn, axis=-1, keepdims=True))


def _lane(shape):
    return lax.broadcasted_iota(jnp.int32, shape, len(shape) - 1)


def _head_mean(v, left):
    sl = jnp.sum(jnp.where(left, v, 0.0), axis=-1, keepdims=True)
    sr = jnp.sum(jnp.where(left, 0.0, v), axis=-1, keepdims=True)
    return jnp.where(left, sl, sr) * (1.0 / HEAD_DIM)


def _rope(y, cs, sn, first):
    up = pltpu.roll(y, 96, 1)
    dn = pltpu.roll(y, 32, 1)
    return y * cs + jnp.where(first, -up, dn) * sn


def _rope_t(d, cs, sn, first):
    t = d * jnp.where(first, -sn, sn)
    return d * cs + jnp.where(first, pltpu.roll(t, 96, 1), pltpu.roll(t, 32, 1))


def _norm_rope(p, g, cs, sn):
    lane = _lane((1, LANES))
    left = lane < HEAD_DIM
    first = (lane % HEAD_DIM) < (HEAD_DIM // 2)
    yn = p * lax.rsqrt(_head_mean(p * p, left) + NORM_EPS)
    return _rope(yn * g, cs, sn, first)


def _norm_rope_bwd(p, dout, g, cs, sn):
    lane = _lane((1, LANES))
    left = lane < HEAD_DIM
    first = (lane % HEAD_DIM) < (HEAD_DIM // 2)
    rstd = lax.rsqrt(_head_mean(p * p, left) + NORM_EPS)
    yn = p * rstd
    dyg = _rope_t(dout, cs, sn, first)
    dg = jnp.sum(dyg * yn, axis=0, keepdims=True)
    dyn = dyg * g
    dp = rstd * (dyn - yn * _head_mean(dyn * yn, left))
    return dp, dg


def _place():
    x, y, c = lax.axis_index("x"), lax.axis_index("y"), lax.axis_index("c")
    return x, y, c, [(1 - x, y), (x, 1 - y), (1 - x, 1 - y)]


def _hosted_gather(src, out, ssem, rsem, lsem, first, mid, last, slot=0):
    x, y, c, chips = _place()
    me = 2 * x + y
    sib = (x, y, 1 - c)
    r2 = src.shape[0] // 2
    base = 6 * slot

    def slab(chip, half):
        return out.at[chip, pl.ds(half * r2, r2)]

    def ici(j):
        cx, cy = chips[j]
        return pltpu.make_async_remote_copy(src.at[pl.ds(c * r2, r2)], slab(me, c), ssem.at[base + j], rsem.at[base + j],
                                            device_id=(cx, cy, c), device_id_type=MESH)

    def landed(j):
        got = slab(2 * chips[j][0] + chips[j][1], c)
        return pltpu.make_async_remote_copy(got, got, ssem.at[base + j], rsem.at[base + j], device_id=sib, device_id_type=MESH)

    def d2d(j, half):
        got = slab(2 * chips[j][0] + chips[j][1], half)
        return pltpu.make_async_remote_copy(got, got, ssem.at[base + 3 + j], rsem.at[base + 3 + j], device_id=sib,
                                            device_id_type=MESH)

    local = pltpu.make_async_copy(src, out.at[me], lsem.at[slot])

    @pl.when(first)
    def _():
        local.start()
        for j in range(3):
            ici(j).start()

    @pl.when(mid)
    def _():
        for j in range(3):
            landed(j).wait_recv()
            d2d(j, c).start()

    @pl.when(last)
    def _():
        for j in range(3):
            d2d(j, 1 - c).wait_recv()
        for j in range(3):
            ici(j).wait_send()
            d2d(j, c).wait_send()
        local.wait()


def _hosted_exchange(srcs, outs, ssem, rsem, first, last):
    x, y, c, chips = _place()
    me = 2 * x + y

    def send(t, j):
        cx, cy = chips[j]
        return pltpu.make_async_remote_copy(srcs[t].at[2 * cx + cy], outs[t].at[me], ssem.at[3 * t + j], rsem.at[3 * t + j],
                                            device_id=(cx, cy, c), device_id_type=MESH)

    def landed(t, j):
        cx, cy = chips[j]
        got = outs[t].at[2 * cx + cy]
        return pltpu.make_async_remote_copy(got, got, ssem.at[3 * t + j], rsem.at[3 * t + j],
                                            device_id=(cx, cy, c), device_id_type=MESH)

    @pl.when(first)
    def _():
        for t in range(len(srcs)):
            for j in range(3):
                send(t, j).start()

    @pl.when(last)
    def _():
        for t in range(len(srcs)):
            for j in range(3):
                landed(t, j).wait_recv()
        for t in range(len(srcs)):
            for j in range(3):
                send(t, j).wait_send()


def _gather_scratch(n):
    return [pltpu.SemaphoreType.DMA((6 * n,)), pltpu.SemaphoreType.DMA((6 * n,)), pltpu.SemaphoreType.DMA((n,))]


def _exchange_scratch(n):
    return [pltpu.SemaphoreType.DMA((3 * n,)), pltpu.SemaphoreType.DMA((3 * n,))]


def _inproj_fwd(x, g, w, qg, kg, cos, sin):
    s = x.shape[0]
    tm = min(512, s)

    def body(x_ref, g_ref, w_ref, qg_ref, kg_ref, cos_ref, sin_ref,
             h_ref, sb_ref, raw_ref, qn_ref, kn_ref, v_ref, gate_ref):
        xn, _ = _rms(x_ref[...])
        h = (xn * g_ref[...]).astype(BF16)
        h_ref[...] = h
        for a in range(0, 3 * SB_WIDTH, 512):
            sb_ref[:, a:a + 512] = _dot(h, w_ref[:, a:a + 512], NN).astype(BF16)
        cs, sn = cos_ref[...], sin_ref[...]
        q0 = 3 * SB_WIDTH
        pq = _dot(h, w_ref[:, q0:q0 + SWA_Q_WIDTH], NN)
        raw_ref[:, 0:SWA_Q_WIDTH] = pq
        for b in range(SWA_Q_WIDTH // LANES):
            blk = pq[:, b * LANES:(b + 1) * LANES]
            qn_ref[:, b * LANES:(b + 1) * LANES] = _norm_rope(blk, qg_ref[...], cs, sn).astype(BF16)
        k0 = q0 + SWA_Q_WIDTH
        pk = _dot(h, w_ref[:, k0:k0 + 2 * SWA_KV_WIDTH], NN)
        raw_ref[:, SWA_Q_WIDTH:SWA_Q_WIDTH + SWA_KV_WIDTH] = pk[:, :SWA_KV_WIDTH]
        kn_ref[...] = _norm_rope(pk[:, :SWA_KV_WIDTH], kg_ref[...], cs, sn).astype(BF16)
        v_ref[...] = pk[:, SWA_KV_WIDTH:].astype(BF16)
        for a in range(GATE_OFF, IN_WIDTH, 512):
            gate_ref[:, a - GATE_OFF:a - GATE_OFF + 512] = jax.nn.sigmoid(_dot(h, w_ref[:, a:a + 512], NN))

    return pl.pallas_call(
        body, name="inproj_fwd", grid=(s // tm,),
        in_specs=[_rows(tm, D_MODEL), _resident((1, D_MODEL)), _resident((D_MODEL, IN_WIDTH)),
                  _resident((1, LANES)), _resident((1, LANES)), _rows(tm, LANES), _rows(tm, LANES)],
        out_specs=[_rows(tm, D_MODEL), _rows(tm, 3 * SB_WIDTH), _rows(tm, 640), _rows(tm, SWA_Q_WIDTH),
                   _rows(tm, SWA_KV_WIDTH), _rows(tm, SWA_KV_WIDTH), _rows(tm, 2 * D_MODEL)],
        out_shape=[jax.ShapeDtypeStruct((s, D_MODEL), BF16), jax.ShapeDtypeStruct((s, 3 * SB_WIDTH), BF16),
                   jax.ShapeDtypeStruct((s, 640), F32), jax.ShapeDtypeStruct((s, SWA_Q_WIDTH), BF16),
                   jax.ShapeDtypeStruct((s, SWA_KV_WIDTH), BF16), jax.ShapeDtypeStruct((s, SWA_KV_WIDTH), BF16),
                   jax.ShapeDtypeStruct((s, 2 * D_MODEL), F32)],
        compiler_params=_cparams(("arbitrary",)),
    )(x, g, w, qg, kg, cos, sin)


def _sb_tile_terms(qh, k, strict):
    z = _dot(qh, k, NT)
    nz = -z
    soft = jnp.log(1.0 + jnp.exp(jnp.minimum(z, nz)))
    log_keep = jnp.minimum(nz, 0.0) - soft
    if strict is not None:
        log_keep = jnp.where(strict, log_keep, 0.0)
    return z, log_keep


def _log_weights(z, log_keep, run, u_from):
    tail = run + _dot(log_keep.astype(BF16), u_from, NN)
    return z + tail, tail[:, 0:1]


def _masked(strict, val):
    return val if strict is None else jnp.where(strict, val, 0.0)


def _tri_twice(t):
    r = lax.broadcasted_iota(jnp.int32, (2 * t, t), 0)
    c = lax.broadcasted_iota(jnp.int32, (2 * t, t), 1)
    return (jnp.where(r >= t, r - t, r) >= c).astype(BF16)


def _split_dot(a, u2):
    hi = a.astype(BF16)
    lo = (a - hi.astype(F32)).astype(BF16)
    return _dot(jnp.concatenate([hi, lo], axis=1), u2, NN)


def _sb_fwd(sbqkv, shards=()):
    s = sbqkv.shape[0]
    t = min(SB_TILE, s)
    nq = s // t
    ng = SB_WIDTH // SB_FWD_STEP_WIDTH
    heads = range(2 * SB_FWD_STEP_WIDTH // LANES)
    nsh = len(shards)

    def body(*refs):
        q_ref, k_ref, v_ref = refs[:3]
        o_ref = refs[3 + nsh]
        j, i = pl.program_id(0), pl.program_id(1)
        for n in range(nsh):
            _hosted_gather(refs[3 + n], refs[4 + nsh + n], *refs[4 + 2 * nsh:], (j == 0) & (i == 0),
                           (j == ng - 1) & (i == (3 * nq) // 4), (j == ng - 1) & (i == nq - 1), slot=n)
        left = _lane((1, LANES)) < HEAD_DIM
        halves = (left, jnp.logical_not(left))
        pair = lambda a, p: a[:, (p // 2) * LANES:(p // 2 + 1) * LANES]
        only = lambda a, p: jnp.where(halves[p % 2], pair(a, p), jnp.zeros((), a.dtype))
        q_all = q_ref[...] * jnp.asarray(SCALE, BF16)
        qh = [only(q_all, p) for p in heads]
        r = lax.broadcasted_iota(jnp.int32, (t, t), 0)
        c = lax.broadcasted_iota(jnp.int32, (t, t), 1)
        u_from = (r >= c).astype(BF16)

        def tile(n, carry, strict):
            runs, accs = carry
            off = pl.multiple_of((i - n) * t, t)
            k = k_ref[pl.ds(off, t), :]
            v = v_ref[pl.ds(off, t), :]
            terms = [_sb_tile_terms(qh[p], pair(k, p), strict) for p in heads]
            logw = [_log_weights(*terms[p], runs[p], u_from) for p in heads]
            ws = [_masked(strict, jnp.exp(logw[p][0])).astype(BF16) for p in heads]
            accs = list(accs)
            for p in heads:
                accs[p // 2] = accs[p // 2] + _dot(ws[p], only(v, p), NN)
            return tuple(logw[p][1] for p in heads), tuple(accs)

        zero = jnp.zeros((t, 1), F32)
        wide = jnp.zeros((t, LANES), F32)
        carry = tile(0, (tuple(zero for _ in heads), tuple(wide for _ in heads[::2])), c < r)
        _, accs = lax.fori_loop(1, i + 1, lambda n, cr: tile(n, cr, None), carry)
        for g, acc in enumerate(accs):
            o_ref[:, g * LANES:(g + 1) * LANES] = acc

    w = SB_FWD_STEP_WIDTH
    blk = pl.BlockSpec((t, w), lambda j, i: (i, j))
    in_specs = [blk, pl.BlockSpec((s, w), lambda j, i: (0, ng + j)), pl.BlockSpec((s, w), lambda j, i: (0, 2 * ng + j))]
    out_specs = [blk]
    out_shape = [jax.ShapeDtypeStruct((s, SB_WIDTH), F32)]
    in_specs += [HBM] * nsh
    out_specs += [HBM] * nsh
    out_shape += [jax.ShapeDtypeStruct((N_CHIPS,) + a.shape, a.dtype) for a in shards]
    return pl.pallas_call(
        body, name="sb_fwd_g%d" % nsh, grid=(ng, nq),
        in_specs=in_specs, out_specs=out_specs, out_shape=out_shape,
        scratch_shapes=_gather_scratch(nsh) if nsh else [],
        compiler_params=_cparams(("arbitrary", "arbitrary")),
    )(sbqkv, sbqkv, sbqkv, *shards)


def _swa_band(cur_ref, prev_ref, b):
    lo = prev_ref[...] if b == 0 else cur_ref[(b - 1) * SWA_BLOCK:b * SWA_BLOCK, :]
    return jnp.concatenate([lo, cur_ref[b * SWA_BLOCK:(b + 1) * SWA_BLOCK, :]], axis=0)


def _swa_variants(band, left):
    f = band.astype(F32)
    sw = pltpu.roll(f, HEAD_DIM, 1)
    halves = (left, jnp.logical_not(left))
    return [[jnp.where(halves[p], f if p == g else sw, 0.0).astype(BF16) for p in range(2)] for g in range(2)]


def _swa_valid(blk):
    ii = lax.broadcasted_iota(jnp.int32, (SWA_BLOCK, 2 * SWA_BLOCK), 0)
    jj = lax.broadcasted_iota(jnp.int32, (SWA_BLOCK, 2 * SWA_BLOCK), 1)
    rel = jj - SWA_BLOCK - ii
    return (rel <= 0) & (rel > -SWA_BLOCK) & (jj + blk * SWA_BLOCK >= SWA_BLOCK)


def _swa_softmax(dots, sink, valid):
    sc = jnp.where(valid, dots * SCALE, -1e30)
    m = jnp.maximum(jnp.max(sc, axis=-1, keepdims=True), sink)
    e = jnp.exp(sc - m)
    es = jnp.exp(sink - m)
    inv = 1.0 / (jnp.sum(e, axis=-1, keepdims=True) + es)
    return e * inv, es * inv


def _swa_heads(q_ref, rows, halves):
    nheads = SWA_Q_WIDTH // HEAD_DIM
    out = []
    for h in range(nheads):
        hb, p = h // 2, h % 2
        q2 = q_ref[rows, hb * LANES:(hb + 1) * LANES]
        out.append((hb, p, h // (nheads // 2), jnp.where(halves[p], q2, jnp.zeros_like(q2))))
    return out


def _swa_specs(s, t):
    nb = t // SWA_BLOCK
    cur = pl.BlockSpec((t, LANES), lambda i: (i, 0))
    prev = pl.BlockSpec((SWA_BLOCK, LANES), lambda i: (jnp.maximum(i * nb - 1, 0), 0))
    return cur, prev


def _swa_fwd(qn, kn, v, sinks):
    s = qn.shape[0]
    t = min(512, s)
    nb = t // SWA_BLOCK
    nheads = SWA_Q_WIDTH // HEAD_DIM

    def body(sink_ref, q_ref, kc_ref, kp_ref, vc_ref, vp_ref, o_ref):
        i = pl.program_id(0)
        left = _lane((1, LANES)) < HEAD_DIM
        halves = (left, jnp.logical_not(left))
        for b in range(nb):
            kvar = _swa_variants(_swa_band(kc_ref, kp_ref, b), left)
            vvar = _swa_variants(_swa_band(vc_ref, vp_ref, b), left)
            rows = slice(b * SWA_BLOCK, (b + 1) * SWA_BLOCK)
            valid = _swa_valid(i * nb + b)
            heads = _swa_heads(q_ref, rows, halves)
            dots = [_dot(qh, kvar[g][p], NT) for _, p, g, qh in heads]
            probs = [_swa_softmax(dots[h], sink_ref[h], valid)[0].astype(BF16) for h in range(nheads)]
            outs = [_dot(probs[h], vvar[g][p], NN) for h, (_, p, g, _) in enumerate(heads)]
            for hb in range(nheads // 2):
                o_ref[rows, hb * LANES:(hb + 1) * LANES] = (outs[2 * hb] + outs[2 * hb + 1]).astype(BF16)

    cur, prev = _swa_specs(s, t)
    return pl.pallas_call(
        body, name="swa_fwd", grid=(s // t,),
        in_specs=[pl.BlockSpec(memory_space=pltpu.SMEM), _rows(t, SWA_Q_WIDTH), cur, prev, cur, prev],
        out_specs=_rows(t, SWA_Q_WIDTH),
        out_shape=jax.ShapeDtypeStruct((s, SWA_Q_WIDTH), BF16),
        compiler_params=_cparams(("arbitrary",)),
    )(sinks, qn, kn, kn, v, v)


def _merge_fwd(x, osb, oswa, gates, wsb, wswa, rest):
    s = x.shape[0]
    tm = min(512, s)
    rc = D_MODEL // N_CHIPS

    def body(x_ref, osb_ref, oswa_ref, gate_ref, wsb_ref, wswa_ref, wout_ref, x1_ref):
        ysb = _dot(osb_ref[...].astype(BF16), wsb_ref[...], NN)
        yswa = _dot(oswa_ref[...], wswa_ref[...], NN)
        merged = (gate_ref[:, :D_MODEL] * ysb + gate_ref[:, D_MODEL:] * yswa).astype(BF16)
        acc = x_ref[...]
        for q in range(N_CHIPS):
            acc = acc + _dot(merged[:, q * rc:(q + 1) * rc], wout_ref[q], NN)
        x1_ref[...] = acc

    return pl.pallas_call(
        body, name="merge_fwd", grid=(s // tm,),
        in_specs=[_rows(tm, D_MODEL), _rows(tm, SB_WIDTH), _rows(tm, SWA_Q_WIDTH), _rows(tm, 2 * D_MODEL),
                  _resident((SB_WIDTH, D_MODEL)), _resident((SWA_Q_WIDTH, D_MODEL)), _resident(*W_OUT_BLOCK)],
        out_specs=_rows(tm, D_MODEL),
        out_shape=jax.ShapeDtypeStruct((s, D_MODEL), F32),
        compiler_params=_cparams(("arbitrary",)),
    )(x, osb, oswa, gates, wsb, wswa, rest)


def _mlp_fwd(x1, g, rest):
    s = x1.shape[0]
    tm = min(512, s)
    fc = D_FF // N_CHIPS

    def body(x_ref, g_ref, wup_ref, wdown_ref, x2_ref, u_ref):
        xf = x_ref[...]
        xn, _ = _rms(xf)
        h2 = (xn * g_ref[...]).astype(BF16)
        acc = xf
        for q in range(N_CHIPS):
            u = _dot(h2, wup_ref[q], NN)
            u_ref[:, q * fc:(q + 1) * fc] = u
            r = jnp.maximum(u, 0.0)
            acc = acc + _dot((r * r).astype(BF16), wdown_ref[q], NN)
        x2_ref[...] = acc

    return pl.pallas_call(
        body, name="mlp_fwd", grid=(s // tm,),
        in_specs=[_rows(tm, D_MODEL), _resident((1, D_MODEL)), _resident(*W_UP_BLOCK), _resident(*W_DOWN_BLOCK)],
        out_specs=[_rows(tm, D_MODEL), _rows(tm, D_FF)],
        out_shape=[jax.ShapeDtypeStruct((s, D_MODEL), F32), jax.ShapeDtypeStruct((s, D_FF), F32)],
        compiler_params=_cparams(("arbitrary",)),
    )(x1, g, rest, rest)


def _loss_grad(y, target):
    s = y.shape[0]
    tm = min(512, s)

    def body(y_ref, t_ref, dy_ref, part_ref):
        err = y_ref[...] - t_ref[...]
        dy_ref[...] = err * (1.0 / D_MODEL)
        tot = jnp.sum(jnp.sum(err * err, axis=-1, keepdims=True), axis=0, keepdims=True)
        part_ref[...] = jnp.broadcast_to(tot.reshape(1, 1, 1), (1, 8, LANES))

    dy, part = pl.pallas_call(
        body, name="loss_grad", grid=(s // tm,),
        in_specs=[_rows(tm, D_MODEL), _rows(tm, D_MODEL)],
        out_specs=[_rows(tm, D_MODEL), pl.BlockSpec((1, 8, LANES), lambda i: (i, 0, 0))],
        out_shape=[jax.ShapeDtypeStruct((s, D_MODEL), F32), jax.ShapeDtypeStruct((s // tm, 8, LANES), F32)],
        compiler_params=_cparams(("arbitrary",)),
    )(y, target)
    return dy, (0.5 / D_MODEL) * jnp.sum(part[:, 0, 0])


def _mlp_bwd(dx2, x1, u, g, rest):
    s = x1.shape[0]
    tm = min(256, s)
    fc = D_FF // N_CHIPS

    def body(dx2_ref, x_ref, u_ref, g_ref, wup_ref, wdown_ref, dx1_ref, du_ref, a_ref, h2_ref, dg_ref, dxb_ref):
        @pl.when(pl.program_id(0) == 0)
        def _():
            dg_ref[...] = jnp.zeros_like(dg_ref)

        gam = g_ref[...]
        xn, rstd = _rms(x_ref[...])
        h2_ref[...] = (xn * gam).astype(BF16)
        dxf = dx2_ref[...]
        dxb = dxf.astype(BF16)
        dxb_ref[...] = dxb
        dh2 = jnp.zeros((tm, D_MODEL), F32)
        for q in range(N_CHIPS):
            cols = slice(q * fc, (q + 1) * fc)
            da = _dot(dxb, wdown_ref[q], NT)
            r = jnp.maximum(u_ref[:, cols], 0.0)
            a_ref[:, cols] = (r * r).astype(BF16)
            du = (da * (2.0 * r)).astype(BF16)
            du_ref[:, cols] = du
            dh2 = dh2 + _dot(du, wup_ref[q], NT)
        dg_ref[...] += jnp.sum(dh2 * xn, axis=0, keepdims=True)
        dx1_ref[...] = dxf + _rms_bwd(dh2, xn, rstd, gam)

    return pl.pallas_call(
        body, name="mlp_bwd", grid=(s // tm,),
        in_specs=[_rows(tm, D_MODEL), _rows(tm, D_MODEL), _rows(tm, D_FF), _resident((1, D_MODEL)),
                  _resident(*W_UP_BLOCK), _resident(*W_DOWN_BLOCK)],
        out_specs=[_rows(tm, D_MODEL), _rows(tm, D_FF), _rows(tm, D_FF), _rows(tm, D_MODEL),
                   pl.BlockSpec((1, D_MODEL), lambda i: (0, 0)), _rows(tm, D_MODEL)],
        out_shape=[jax.ShapeDtypeStruct((s, D_MODEL), F32), jax.ShapeDtypeStruct((s, D_FF), BF16),
                   jax.ShapeDtypeStruct((s, D_FF), BF16), jax.ShapeDtypeStruct((s, D_MODEL), BF16),
                   jax.ShapeDtypeStruct((1, D_MODEL), F32), jax.ShapeDtypeStruct((s, D_MODEL), BF16)],
        compiler_params=_cparams(("arbitrary",)),
    )(dx2, x1, u, g, rest, rest)


def _wgrad(a, b, name, shard_axis=None):
    s, m = a.shape
    n = b.shape[1]
    tm = min(512, m if shard_axis != 0 else m // N_CHIPS)
    tn = min(512, n if shard_axis != 1 else n // N_CHIPS)
    if shard_axis is None and n % tn:
        tn = 256

    def body(a_ref, b_ref, o_ref, *narrow):
        res = _dot(a_ref[...].astype(BF16), b_ref[...].astype(BF16), TN)
        o_ref[...] = res.reshape(o_ref.shape)
        for n_ref in narrow:
            n_ref[...] = res.astype(BF16).reshape(n_ref.shape)

    if shard_axis is None:
        out_shape, out_spec = (m, n), pl.BlockSpec((tm, tn), lambda i, j: (i, j))
    elif shard_axis == 0:
        per = m // N_CHIPS // tm
        out_shape, out_spec = (N_CHIPS, m // N_CHIPS, n), pl.BlockSpec((1, tm, tn), lambda i, j: (i // per, i % per, j))
    else:
        per = n // N_CHIPS // tn
        out_shape, out_spec = (N_CHIPS, m, n // N_CHIPS), pl.BlockSpec((1, tm, tn), lambda i, j: (j // per, i, j % per))
    both = shard_axis is not None
    return pl.pallas_call(
        body, name=name, grid=(m // tm, n // tn),
        in_specs=[pl.BlockSpec((s, tm), lambda i, j: (0, i)), pl.BlockSpec((s, tn), lambda i, j: (0, j))],
        out_specs=[out_spec, out_spec] if both else out_spec,
        out_shape=[jax.ShapeDtypeStruct(out_shape, F32), jax.ShapeDtypeStruct(out_shape, BF16)] if both
        else jax.ShapeDtypeStruct(out_shape, F32),
        compiler_params=_cparams(("arbitrary", "arbitrary")),
    )(a, b)


def _merge_bwd(dx1, osb, oswa, gates, wsb, wswa, rest):
    s = dx1.shape[0]
    tm = min(512, s)

    def body(dx_ref, osb_ref, oswa_ref, gate_ref, wsb_ref, wswa_ref, wout_ref,
             dosb_ref, doswa_ref, dgl_ref, merged_ref, dysb_ref, dyswa_ref, dxb_ref):
        dxb = dx_ref[...].astype(BF16)
        dxb_ref[...] = dxb
        dm = jnp.concatenate([_dot(dxb, wout_ref[q], NT) for q in range(N_CHIPS)], axis=1)
        ysb = _dot(osb_ref[...].astype(BF16), wsb_ref[...], NN)
        yswa = _dot(oswa_ref[...], wswa_ref[...], NN)
        g0 = gate_ref[:, :D_MODEL]
        g1 = gate_ref[:, D_MODEL:]
        merged_ref[...] = (g0 * ysb + g1 * yswa).astype(BF16)
        dgl_ref[:, :D_MODEL] = (dm * ysb * (g0 * (1.0 - g0))).astype(BF16)
        dgl_ref[:, D_MODEL:] = (dm * yswa * (g1 * (1.0 - g1))).astype(BF16)
        dysb = (dm * g0).astype(BF16)
        dyswa = (dm * g1).astype(BF16)
        dysb_ref[...] = dysb
        dyswa_ref[...] = dyswa
        dosb_ref[...] = _dot(dysb, wsb_ref[...], NT)
        doswa_ref[...] = _dot(dyswa, wswa_ref[...], NT)

    return pl.pallas_call(
        body, name="merge_bwd", grid=(s // tm,),
        in_specs=[_rows(tm, D_MODEL), _rows(tm, SB_WIDTH), _rows(tm, SWA_Q_WIDTH), _rows(tm, 2 * D_MODEL),
                  _resident((SB_WIDTH, D_MODEL)), _resident((SWA_Q_WIDTH, D_MODEL)), _resident(*W_OUT_BLOCK)],
        out_specs=[_rows(tm, SB_WIDTH), _rows(tm, SWA_Q_WIDTH), _rows(tm, 2 * D_MODEL), _rows(tm, D_MODEL),
                   _rows(tm, D_MODEL), _rows(tm, D_MODEL), _rows(tm, D_MODEL)],
        out_shape=[jax.ShapeDtypeStruct((s, SB_WIDTH), F32), jax.ShapeDtypeStruct((s, SWA_Q_WIDTH), F32),
                   jax.ShapeDtypeStruct((s, 2 * D_MODEL), BF16), jax.ShapeDtypeStruct((s, D_MODEL), BF16),
                   jax.ShapeDtypeStruct((s, D_MODEL), BF16), jax.ShapeDtypeStruct((s, D_MODEL), BF16),
                   jax.ShapeDtypeStruct((s, D_MODEL), BF16)],
        compiler_params=_cparams(("arbitrary",)),
    )(dx1, osb, oswa, gates, wsb, wswa, rest)


def _sb_bwd(sbqkv, osb, dosb, send):
    s = sbqkv.shape[0]
    t = min(SB_TILE, s)
    nq = s // t
    ng = SB_WIDTH // SB_STEP_WIDTH
    heads = range(2 * SB_STEP_WIDTH // LANES)
    ns = len(send)

    def body(*refs):
        q_ref, k_ref, v_ref, o_ref, do_ref = refs[:5]
        srcs = refs[5:5 + ns]
        dq_ref, dk_ref, dv_ref = refs[5 + ns:8 + ns]
        outs = refs[8 + ns:8 + 2 * ns]
        j, i = pl.program_id(0), pl.program_id(1)
        if ns:
            ssem, rsem = refs[8 + 2 * ns:]
            _hosted_exchange(srcs, outs, ssem, rsem, (j == 0) & (i == 0), (j == ng - 1) & (i == nq - 1))

        @pl.when(i == 0)
        def _():
            dk_ref[...] = jnp.zeros_like(dk_ref)
            dv_ref[...] = jnp.zeros_like(dv_ref)

        left = _lane((1, LANES)) < HEAD_DIM
        halves = (left, jnp.logical_not(left))
        pair = lambda a, p: a[:, (p // 2) * LANES:(p // 2 + 1) * LANES]
        only = lambda a, p: jnp.where(halves[p % 2], pair(a, p), jnp.zeros((), a.dtype))
        scale = jnp.asarray(SCALE, BF16)
        q_all = q_ref[...] * scale
        do_all = do_ref[...].astype(BF16)
        prod = do_all.astype(F32) * o_ref[...]
        qh = [only(q_all, p) for p in heads]
        doh = [only(do_all, p) for p in heads]
        delta = [jnp.sum(only(prod, p), axis=-1, keepdims=True) for p in heads]
        r = lax.broadcasted_iota(jnp.int32, (t, t), 0)
        c = lax.broadcasted_iota(jnp.int32, (t, t), 1)
        u_from = (r >= c).astype(BF16)
        u_from2 = _tri_twice(t)

        def tile(n, carry, strict):
            runs, dqs = carry
            off = pl.multiple_of((i - n) * t, t)
            k = k_ref[pl.ds(off, t), :]
            v = v_ref[pl.ds(off, t), :]
            ks = k * scale
            terms = [_sb_tile_terms(qh[p], pair(k, p), strict) for p in heads]
            logw = [_log_weights(*terms[p], runs[p][0], u_from) for p in heads]
            dws = [_dot(doh[p], pair(v, p), NT) for p in heads]
            wbs = [_masked(strict, jnp.exp(logw[p][0])).astype(BF16) for p in heads]
            es = [dws[p] * wbs[p].astype(F32) for p in heads]
            rests = [runs[p][1] + _split_dot(es[p], u_from2) for p in heads]
            betas = [jnp.exp(terms[p][0] + terms[p][1]) for p in heads]
            dzs = [_masked(strict, es[p] - betas[p] * (es[p] - rests[p])).astype(BF16) for p in heads]
            dqs = list(dqs)
            for g in range(len(heads) // 2):
                a, b = 2 * g, 2 * g + 1
                cols = slice(g * LANES, (g + 1) * LANES)
                dv_ref[pl.ds(off, t), cols] += _dot(wbs[a], doh[a], TN) + _dot(wbs[b], doh[b], TN)
                dk_ref[pl.ds(off, t), cols] += _dot(dzs[a], qh[a], TN) + _dot(dzs[b], qh[b], TN)
                dqs[g] = dqs[g] + _dot(dzs[a], only(ks, a), NN) + _dot(dzs[b], only(ks, b), NN)
            new_runs = tuple((logw[p][1], rests[p][:, 0:1]) for p in heads)
            return new_runs, tuple(dqs)

        zero = jnp.zeros((t, 1), F32)
        wide = jnp.zeros((t, LANES), F32)
        carry = tile(0, (tuple((zero, -delta[p]) for p in heads), tuple(wide for _ in heads[::2])), c < r)
        _, dqs = lax.fori_loop(1, i + 1, lambda n, cr: tile(n, cr, None), carry)
        for g, dq in enumerate(dqs):
            dq_ref[:, g * LANES:(g + 1) * LANES] = dq.astype(BF16)

    w = SB_STEP_WIDTH
    blk = pl.BlockSpec((t, w), lambda j, i: (i, j))
    whole = pl.BlockSpec((s, w), lambda j, i: (0, j))
    return pl.pallas_call(
        body, name="sb_bwd_x%d" % ns, grid=(ng, nq),
        in_specs=[blk, pl.BlockSpec((s, w), lambda j, i: (0, ng + j)),
                  pl.BlockSpec((s, w), lambda j, i: (0, 2 * ng + j)), blk, blk] + [HBM] * ns,
        out_specs=[blk, whole, whole] + [HBM] * ns,
        out_shape=[jax.ShapeDtypeStruct((s, SB_WIDTH), BF16), jax.ShapeDtypeStruct((s, SB_WIDTH), F32),
                   jax.ShapeDtypeStruct((s, SB_WIDTH), F32)] + [jax.ShapeDtypeStruct(a.shape, a.dtype) for a in send],
        scratch_shapes=_exchange_scratch(ns) if ns else [],
        compiler_params=_cparams(("arbitrary", "arbitrary")),
    )(sbqkv, sbqkv, sbqkv, osb, dosb, *send)


def _swa_bwd(qn, kn, v, sinks, do):
    s = qn.shape[0]
    t = min(512, s)
    nb = t // SWA_BLOCK
    nheads = SWA_Q_WIDTH // HEAD_DIM

    def body(sink_ref, q_ref, kc_ref, kp_ref, vc_ref, vp_ref, do_ref, dq_ref, dk_ref, dv_ref, dsink_ref):
        i = pl.program_id(0)

        @pl.when(i == 0)
        def _():
            dk_ref[...] = jnp.zeros_like(dk_ref)
            dv_ref[...] = jnp.zeros_like(dv_ref)

        left = _lane((1, LANES)) < HEAD_DIM
        halves = (left, jnp.logical_not(left))
        dsink = [jnp.zeros((1, 1), F32) for _ in range(nheads)]
        for b in range(nb):
            kvar = _swa_variants(_swa_band(kc_ref, kp_ref, b), left)
            vvar = _swa_variants(_swa_band(vc_ref, vp_ref, b), left)
            rows = slice(b * SWA_BLOCK, (b + 1) * SWA_BLOCK)
            valid = _swa_valid(i * nb + b)
            heads = _swa_heads(q_ref, rows, halves)
            doh = []
            for hb, p, _, _ in heads:
                do2 = do_ref[rows, hb * LANES:(hb + 1) * LANES]
                doh.append(jnp.where(halves[p], do2, 0.0).astype(BF16))
            dots = [_dot(qh, kvar[g][p], NT) for _, p, g, qh in heads]
            dps = [_dot(doh[h], vvar[g][p], NT) for h, (_, p, g, _) in enumerate(heads)]
            dss, pbs = [], []
            for h in range(nheads):
                probs, psink = _swa_softmax(dots[h], sink_ref[h], valid)
                delta = jnp.sum(probs * dps[h], axis=-1, keepdims=True)
                dss.append((probs * (dps[h] - delta) * SCALE).astype(BF16))
                pbs.append(probs.astype(BF16))
                dsink[h] = dsink[h] - jnp.sum(psink * delta, axis=0, keepdims=True)
            dk_acc = [jnp.zeros((2 * SWA_BLOCK, LANES), F32) for _ in range(2)]
            dv_acc = [jnp.zeros((2 * SWA_BLOCK, LANES), F32) for _ in range(2)]
            dqs = [_dot(dss[h], kvar[g][p], NN) for h, (_, p, g, _) in enumerate(heads)]
            for h, (_, p, g, qh) in enumerate(heads):
                which = 0 if p == g else 1
                dk_acc[which] = dk_acc[which] + _dot(dss[h], qh, TN)
                dv_acc[which] = dv_acc[which] + _dot(pbs[h], doh[h], TN)
            for hb in range(nheads // 2):
                dq_ref[rows, hb * LANES:(hb + 1) * LANES] = dqs[2 * hb] + dqs[2 * hb + 1]
            dkb = dk_acc[0] + pltpu.roll(dk_acc[1], HEAD_DIM, 1)
            dvb = dv_acc[0] + pltpu.roll(dv_acc[1], HEAD_DIM, 1)
            start = pl.multiple_of((i * nb + b) * SWA_BLOCK, SWA_BLOCK)
            dk_ref[pl.ds(start, SWA_BLOCK), :] += dkb[SWA_BLOCK:]
            dv_ref[pl.ds(start, SWA_BLOCK), :] += dvb[SWA_BLOCK:]

            @pl.when(i * nb + b > 0)
            def _(dkb=dkb, dvb=dvb, start=start):
                before = pl.multiple_of(jnp.maximum(start - SWA_BLOCK, 0), SWA_BLOCK)
                dk_ref[pl.ds(before, SWA_BLOCK), :] += dkb[:SWA_BLOCK]
                dv_ref[pl.ds(before, SWA_BLOCK), :] += dvb[:SWA_BLOCK]

        for h in range(nheads):
            dsink_ref[0, h:h + 1, :] = jnp.broadcast_to(dsink[h], (1, LANES))

    cur, prev = _swa_specs(s, t)
    whole = pl.BlockSpec((s, LANES), lambda i: (0, 0))
    return pl.pallas_call(
        body, name="swa_bwd", grid=(s // t,),
        in_specs=[pl.BlockSpec(memory_space=pltpu.SMEM), _rows(t, SWA_Q_WIDTH), cur, prev, cur, prev,
                  _rows(t, SWA_Q_WIDTH)],
        out_specs=[_rows(t, SWA_Q_WIDTH), whole, whole, pl.BlockSpec((1, 8, LANES), lambda i: (i, 0, 0))],
        out_shape=[jax.ShapeDtypeStruct((s, SWA_Q_WIDTH), F32), jax.ShapeDtypeStruct((s, SWA_KV_WIDTH), F32),
                   jax.ShapeDtypeStruct((s, SWA_KV_WIDTH), F32), jax.ShapeDtypeStruct((s // t, 8, LANES), F32)],
        compiler_params=_cparams(("arbitrary",)),
    )(sinks, qn, kn, kn, v, v, do)


def _swa_post(raw, dqn, dkn, qg, kg, cos, sin):
    s = raw.shape[0]
    tm = min(512, s)
    nq = SWA_Q_WIDTH // LANES

    def body(raw_ref, dq_ref, dk_ref, qg_ref, kg_ref, cos_ref, sin_ref, out_ref, dg_ref):
        @pl.when(pl.program_id(0) == 0)
        def _():
            dg_ref[...] = jnp.zeros_like(dg_ref)

        cs, sn = cos_ref[...], sin_ref[...]
        dgq = jnp.zeros((1, LANES), F32)
        for b in range(nq):
            cols = slice(b * LANES, (b + 1) * LANES)
            dp, dg = _norm_rope_bwd(raw_ref[:, cols], dq_ref[:, cols], qg_ref[...], cs, sn)
            out_ref[:, cols] = dp.astype(BF16)
            dgq = dgq + dg
        cols = slice(SWA_Q_WIDTH, SWA_Q_WIDTH + LANES)
        dp, dgk = _norm_rope_bwd(raw_ref[:, cols], dk_ref[...], kg_ref[...], cs, sn)
        out_ref[:, cols] = dp.astype(BF16)
        dg_ref[0:1, :] += dgq
        dg_ref[1:2, :] += dgk

    return pl.pallas_call(
        body, name="swa_post", grid=(s // tm,),
        in_specs=[_rows(tm, 640), _rows(tm, SWA_Q_WIDTH), _rows(tm, LANES), _resident((1, LANES)),
                  _resident((1, LANES)), _rows(tm, LANES), _rows(tm, LANES)],
        out_specs=[_rows(tm, 640), pl.BlockSpec((8, LANES), lambda i: (0, 0))],
        out_shape=[jax.ShapeDtypeStruct((s, 640), BF16), jax.ShapeDtypeStruct((8, LANES), F32)],
        compiler_params=_cparams(("arbitrary",)),
    )(raw, dqn, dkn, qg, kg, cos, sin)


def _inproj_bwd(dsbq, dsbk, dsbv, dswqk, dswv, dgl, x, dx1, g, w):
    s = x.shape[0]
    tm = min(256, s)

    def body(dsbq_ref, dsbk_ref, dsbv_ref, dswqk_ref, dswv_ref, dgl_ref, x_ref, dx1_ref, g_ref, w_ref,
             dx_ref, dproj_ref, dg_ref):
        @pl.when(pl.program_id(0) == 0)
        def _():
            dg_ref[...] = jnp.zeros_like(dg_ref)

        dproj_ref[:, 0:512] = dsbq_ref[...]
        dproj_ref[:, 512:1024] = dsbk_ref[...].astype(BF16)
        dproj_ref[:, 1024:1536] = dsbv_ref[...].astype(BF16)
        dproj_ref[:, 1536:2176] = dswqk_ref[...]
        dproj_ref[:, 2176:2304] = dswv_ref[...].astype(BF16)
        dproj_ref[:, GATE_OFF:IN_WIDTH] = dgl_ref[...]
        dh = jnp.zeros((tm, D_MODEL), F32)
        for a in range(0, IN_WIDTH, 512):
            b = min(a + 512, IN_WIDTH)
            dh = dh + _dot(dproj_ref[:, a:b], w_ref[:, a:b], NT)
        gam = g_ref[...]
        xn, rstd = _rms(x_ref[...])
        dg_ref[...] += jnp.sum(dh * xn, axis=0, keepdims=True)
        dx_ref[...] = dx1_ref[...] + _rms_bwd(dh, xn, rstd, gam)

    return pl.pallas_call(
        body, name="inproj_bwd", grid=(s // tm,),
        in_specs=[_rows(tm, 512), _rows(tm, 512), _rows(tm, 512), _rows(tm, 640), _rows(tm, LANES),
                  _rows(tm, 2 * D_MODEL), _rows(tm, D_MODEL), _rows(tm, D_MODEL), _resident((1, D_MODEL)),
                  _resident((D_MODEL, IN_WIDTH))],
        out_specs=[_rows(tm, D_MODEL), _rows(tm, IN_WIDTH), pl.BlockSpec((1, D_MODEL), lambda i: (0, 0))],
        out_shape=[jax.ShapeDtypeStruct((s, D_MODEL), F32), jax.ShapeDtypeStruct((s, IN_WIDTH), BF16),
                   jax.ShapeDtypeStruct((1, D_MODEL), F32)],
        compiler_params=_cparams(("arbitrary",)),
    )(dsbq, dsbk, dsbv, dswqk, dswv, dgl, x, dx1, g, w)


def _gather_weights(shard):
    def body(src, out, ssem, rsem, lsem):
        once = pl.program_id(0) == 0
        _hosted_gather(src, out, ssem, rsem, lsem, once, once, once)

    return pl.pallas_call(
        body, name="gather_weights", grid=(1,), in_specs=[HBM], out_specs=HBM,
        out_shape=jax.ShapeDtypeStruct((N_CHIPS,) + shard.shape, shard.dtype),
        scratch_shapes=_gather_scratch(1),
        compiler_params=pltpu.CompilerParams(dimension_semantics=("arbitrary",), has_side_effects=True),
    )(shard)


def _exchange(send):
    ns = len(send)

    def body(*refs):
        once = pl.program_id(0) == 0
        _hosted_exchange(refs[:ns], refs[ns:2 * ns], *refs[2 * ns:], once, once)

    return pl.pallas_call(
        body, name="exchange_x%d" % ns, grid=(1,), in_specs=[HBM] * ns, out_specs=[HBM] * ns,
        out_shape=[jax.ShapeDtypeStruct(a.shape, a.dtype) for a in send],
        scratch_shapes=_exchange_scratch(ns),
        compiler_params=pltpu.CompilerParams(dimension_semantics=("arbitrary",), has_side_effects=True),
    )(*send)


def _pair_swap(arrs):
    n = len(arrs)

    def body(*refs):
        x, y, c, _ = _place()
        cps = [pltpu.make_async_remote_copy(refs[t], refs[n + t], refs[2 * n].at[t], refs[2 * n + 1].at[t],
                                            device_id=(x, y, 1 - c), device_id_type=MESH) for t in range(n)]
        for cp in cps:
            cp.start()
        for cp in cps:
            cp.wait()

    return pl.pallas_call(
        body, name="pair_swap", in_specs=[HBM] * n, out_specs=[HBM] * n,
        out_shape=[jax.ShapeDtypeStruct(a.shape, a.dtype) for a in arrs],
        scratch_shapes=[pltpu.SemaphoreType.DMA((n,)), pltpu.SemaphoreType.DMA((n,))],
        compiler_params=pltpu.CompilerParams(has_side_effects=True),
    )(*arrs)


def _allreduce_small(block):
    def body(src, out, buf, ssem, rsem):
        x, y, c, _ = _place()
        me = 4 * x + 2 * y + c
        buf[me] = src[...]
        cps = []
        for k in range(1, N_DEV):
            peer = (x ^ (k >> 2), y ^ ((k >> 1) & 1), c ^ (k & 1))
            cp = pltpu.make_async_remote_copy(src, buf.at[me], ssem.at[k - 1], rsem.at[k - 1], device_id=peer, device_id_type=MESH)
            cp.start()
            cps.append(cp)
        for k in range(1, N_DEV):
            got = buf.at[me ^ k]
            pltpu.make_async_remote_copy(got, got, ssem.at[k - 1], rsem.at[k - 1], device_id=(x, y, c), device_id_type=MESH).wait_recv()
        for cp in cps:
            cp.wait_send()
        tot = buf[0]
        for d in range(1, N_DEV):
            tot = tot + buf[d]
        out[...] = tot

    vm = pl.BlockSpec(memory_space=pltpu.VMEM)
    return pl.pallas_call(
        body, name="allreduce_small", in_specs=[vm], out_specs=vm,
        out_shape=jax.ShapeDtypeStruct(block.shape, F32),
        scratch_shapes=[pltpu.VMEM((N_DEV,) + block.shape, F32), pltpu.SemaphoreType.DMA((N_DEV - 1,)),
                        pltpu.SemaphoreType.DMA((N_DEV - 1,))],
        compiler_params=pltpu.CompilerParams(has_side_effects=True),
    )(block)


def _sum_chips(landed, own, into, layer, name):
    nq, k, n = landed.shape
    tr = min(256, k)
    me = (2 * lax.axis_index("x") + lax.axis_index("y")).astype(jnp.int32).reshape(1)

    def body(me_ref, p0, p1, p2, p3, own_ref, into_ref, o_ref):
        mine = own_ref[0]
        terms = [jnp.where(me_ref[0] == q, mine, p[0].astype(F32)) for q, p in enumerate((p0, p1, p2, p3))]
        o_ref[0] = ((terms[0] + terms[1]) + terms[2]) + terms[3]

    spec = lambda q: pl.BlockSpec((1, tr, n), lambda i, m, q=q: (q, i, 0))
    return pl.pallas_call(
        body, name=name,
        grid_spec=pltpu.PrefetchScalarGridSpec(
            num_scalar_prefetch=1, grid=(k // tr,),
            in_specs=[spec(q) for q in range(nq)] + [pl.BlockSpec((1, tr, n), lambda i, m: (m[0], i, 0)), HBM],
            out_specs=pl.BlockSpec((1, tr, n), lambda i, m: (layer, i, 0))),
        out_shape=jax.ShapeDtypeStruct(into.shape, F32),
        input_output_aliases={6: 0},
        compiler_params=_cparams(("arbitrary",)),
    )(me, landed, landed, landed, landed, own, into)


def _adamw(w, g, g2, m, v, name):
    shape = w.shape
    cols = shape[-1]
    flat = lambda t: t.reshape(-1, cols)
    rows = flat(w).shape[0]
    tr = min(512, rows)
    pair = g2 is not None

    def body(*refs):
        if pair:
            w_ref, g_ref, g2_ref, m_ref, v_ref, go_ref, d_ref, nm_ref, nv_ref = refs
            gr = g_ref[...] + g2_ref[...]
        else:
            w_ref, g_ref, m_ref, v_ref, go_ref, d_ref, nm_ref, nv_ref = refs
            gr = g_ref[...]
        go_ref[...] = gr
        nm = ADAM_B1 * m_ref[...] + (1.0 - ADAM_B1) * gr
        nv = ADAM_B2 * v_ref[...] + (1.0 - ADAM_B2) * (gr * gr)
        m_hat = nm / (1.0 - ADAM_B1 ** ADAM_STEP)
        v_hat = nv / (1.0 - ADAM_B2 ** ADAM_STEP)
        d_ref[...] = -ADAM_LR * (m_hat / (jnp.sqrt(v_hat) + ADAM_EPS) + ADAM_WD * w_ref[...])
        nm_ref[...] = nm
        nv_ref[...] = nv

    spec = pl.BlockSpec((tr, cols), lambda i: (i, 0))
    ins = [w, g] + ([g2] if pair else []) + [m, v]
    outs = pl.pallas_call(
        body, name=name, grid=(rows // tr,),
        in_specs=[spec] * len(ins), out_specs=[spec] * 4,
        out_shape=[jax.ShapeDtypeStruct((rows, cols), F32)] * 4,
        compiler_params=_cparams(("arbitrary",)),
    )(*[flat(t) for t in ins])
    return [o.reshape(shape) for o in outs]


def _pack_rest(big, l):
    rows = dict(PACK_ROWS)
    return jnp.concatenate([big[name][l].astype(BF16).reshape(rows[name], 1024) for name in REST], axis=0)


def _whole_columns(t):
    return jnp.moveaxis(t, 0, 1).reshape(t.shape[1], N_CHIPS * t.shape[2])


def _branch_weights(rest):
    out, at = [], REST_ROWS - 256
    for _ in range(2):
        out.append(_whole_columns(rest[:, at:at + 128, :].reshape((N_CHIPS,) + SHARD_SHAPES["w_branch_sb"])))
        at += 128
    return out


def _pack_small(d):
    flat = jnp.concatenate([d[n].reshape(-1) for n in SMALL_NAMES])
    return jnp.pad(flat, (0, SMALL_ROWS * LANES - flat.shape[0])).reshape(SMALL_ROWS, LANES)


def _unpack_small(block, like):
    flat, out, at = block.reshape(-1), {}, 0
    for n in SMALL_NAMES:
        size = like[n].size
        out[n] = flat[at:at + size].reshape(like[n].shape)
        at += size
    return out


def _rope_tables(s):
    inv_freq = 1.0 / (ROPE_THETA ** (jnp.arange(0, HEAD_DIM, 2, dtype=F32) / HEAD_DIM))
    ang = jnp.arange(s, dtype=F32)[:, None] * inv_freq[None, :]
    reps = LANES // (HEAD_DIM // 2)
    return jnp.tile(jnp.cos(ang), (1, reps)), jnp.tile(jnp.sin(ang), (1, reps))


def _forward_backward(x, target, big, small):
    s = x.shape[0]
    cos, sin = _rope_tables(s)
    two = lambda gvec: jnp.tile(gvec.reshape(1, HEAD_DIM), (1, 2))
    saved = []
    win_shard = lambda l: big["w_in"][l].astype(BF16)
    win = _whole_columns(_gather_weights(win_shard(0)))
    for l in range(DEPTH):
        gm = small["mix_norm_g"][l].reshape(1, D_MODEL)
        gl = small["mlp_norm_g"][l].reshape(1, D_MODEL)
        qg, kg = two(small["q_norm_g"][l]), two(small["k_norm_g"][l])
        h, sbqkv, raw, qn, kn, v, gates = _inproj_fwd(x, gm, win, qg, kg, cos, sin)
        osb, rest, *nxt = _sb_fwd(sbqkv, [_pack_rest(big, l)] + ([win_shard(l + 1)] if l + 1 < DEPTH else []))
        wsb, wswa = _branch_weights(rest)
        oswa = _swa_fwd(qn, kn, v, small["sinks"][l])
        x1 = _merge_fwd(x, osb, oswa, gates, wsb, wswa, rest)
        x2, u = _mlp_fwd(x1, gl, rest)
        saved.append((x, h, sbqkv, raw, qn, kn, v, gates, osb, oswa, x1, u, gm, gl, qg, kg, win, wsb, wswa, rest))
        x, win = x2, (_whole_columns(nxt[0]) if nxt else None)
    dx, loss = _loss_grad(x, target)

    got = {name: [None] * DEPTH for name in BIG}
    gsmall = {name: [None] * DEPTH for name in SMALL_NAMES}
    late = []
    for l in reversed(range(DEPTH)):
        x0, h, sbqkv, raw, qn, kn, v, gates, osb, oswa, x1, u, gm, gl, qg, kg, win, wsb, wswa, rest = saved[l]
        dx1, du, act, h2, dgl_mlp, dxb = _mlp_bwd(dx, x1, u, gl, rest)
        dosb, doswa, dgate, merged, dysb, dyswa, dx1b = _merge_bwd(dx1, osb, oswa, gates, wsb, wswa, rest)
        ready = [("w_down", l, *_wgrad(act, dxb, "wgrad_down", shard_axis=0)),
                 ("w_up", l, *_wgrad(h2, du, "wgrad_up", shard_axis=1)),
                 ("w_out", l, *_wgrad(merged, dx1b, "wgrad_out", shard_axis=0)),
                 ("w_branch_sb", l, *_wgrad(osb, dysb, "wgrad_bsb", shard_axis=1)),
                 ("w_branch_swa", l, *_wgrad(oswa, dyswa, "wgrad_bswa", shard_axis=1))] + late
        dsbq, dsbk, dsbv, *landed = _sb_bwd(sbqkv, osb, dosb, [narrow for _, _, _, narrow in ready])
        for (name, layer, own, _), arr in zip(ready, landed):
            got[name][layer] = (arr, own)
        dqn, dkn, dswv, dsink = _swa_bwd(qn, kn, v, small["sinks"][l], doswa)
        dswqk, dgqk = _swa_post(raw, dqn, dkn, qg, kg, cos, sin)
        dx, dproj, dg_mix = _inproj_bwd(dsbq, dsbk, dsbv, dswqk, dswv, dgate, x0, dx1, gm, win)
        dwin = _wgrad(h, dproj, "wgrad_in")
        dwin = dwin.reshape(D_MODEL, N_CHIPS, IN_WIDTH // N_CHIPS).transpose(1, 0, 2)
        late = [("w_in", l, dwin, dwin.astype(BF16))]
        gsmall["mix_norm_g"][l] = dg_mix[0]
        gsmall["mlp_norm_g"][l] = dgl_mlp[0]
        gsmall["q_norm_g"][l] = dgqk[0, :HEAD_DIM] + dgqk[0, HEAD_DIM:]
        gsmall["k_norm_g"][l] = dgqk[1, :HEAD_DIM] + dgqk[1, HEAD_DIM:]
        gsmall["sinks"][l] = jnp.sum(dsink[:, :, 0], axis=0)
    got["w_in"][0] = (_exchange([late[0][3]])[0], late[0][2])
    gsmall = {k: jnp.stack(vs) for k, vs in gsmall.items()}
    return loss, dx, got, gsmall


def kernel(x, mix_norm_g, w_in, q_norm_g, k_norm_g, sinks, w_branch_sb, w_branch_swa, w_out, mlp_norm_g, w_up, w_down, loss_target, m_mix_norm_g, m_w_in, m_q_norm_g, m_k_norm_g, m_sinks, m_w_branch_sb, m_w_branch_swa, m_w_out, m_mlp_norm_g, m_w_up, m_w_down, v_mix_norm_g, v_w_in, v_q_norm_g, v_k_norm_g, v_sinks, v_w_branch_sb, v_w_branch_swa, v_w_out, v_mlp_norm_g, v_w_up, v_w_down):
    big = dict(w_in=w_in, w_branch_sb=w_branch_sb, w_branch_swa=w_branch_swa, w_out=w_out, w_up=w_up, w_down=w_down)
    big_m = dict(w_in=m_w_in, w_branch_sb=m_w_branch_sb, w_branch_swa=m_w_branch_swa, w_out=m_w_out, w_up=m_w_up, w_down=m_w_down)
    big_v = dict(w_in=v_w_in, w_branch_sb=v_w_branch_sb, w_branch_swa=v_w_branch_swa, w_out=v_w_out, w_up=v_w_up, w_down=v_w_down)
    small = dict(mix_norm_g=mix_norm_g, q_norm_g=q_norm_g, k_norm_g=k_norm_g, sinks=sinks, mlp_norm_g=mlp_norm_g)
    small_m = dict(mix_norm_g=m_mix_norm_g, q_norm_g=m_q_norm_g, k_norm_g=m_k_norm_g, sinks=m_sinks, mlp_norm_g=m_mlp_norm_g)
    small_v = dict(mix_norm_g=v_mix_norm_g, q_norm_g=v_q_norm_g, k_norm_g=v_k_norm_g, sinks=v_sinks, mlp_norm_g=v_mlp_norm_g)

    loss_part, grad_x, got, gsmall = _forward_backward(x[0], loss_target[0], big, small)
    loss = lax.psum(loss_part, ("x", "y", "c"))

    mine = []
    for name in BIG:
        tot = lax.empty((DEPTH,) + SHARD_SHAPES[name], F32)
        for l in range(DEPTH):
            tot = _sum_chips(*got[name][l], tot, l, "sum_" + name)
        mine.append(tot)
    theirs = _pair_swap(mine)
    upd = {name: _adamw(big[name], mine[i], theirs[i], big_m[name], big_v[name], "adamw_" + name)
           for i, name in enumerate(BIG)}
    g_small = _allreduce_small(_pack_small(gsmall))
    sm = _adamw(_pack_small(small), g_small, None, _pack_small(small_m), _pack_small(small_v), "adamw_small")
    upd_small = [_unpack_small(t, small) for t in sm]

    names = ("mix_norm_g", "w_in", "q_norm_g", "k_norm_g", "sinks", "w_branch_sb", "w_branch_swa", "w_out",
             "mlp_norm_g", "w_up", "w_down")
    pick = lambda n, i: upd[n][i] if n in upd else upd_small[i][n]
    return (loss, grad_x[None], *[pick(n, 0) for n in names], *[pick(n, 1) for n in names],
            *[pick(n, 2) for n in names], *[pick(n, 3) for n in names])
```

```python
import jax
import jax.numpy as jnp
from jax import lax
from jax.experimental import pallas as pl
from jax.experimental.pallas import tpu as pltpu

F32 = jnp.float32
BF16 = jnp.bfloat16

D_MODEL = 1024
DEPTH = 4
HEAD_DIM = 64
SB_WIDTH = 512
SWA_Q_WIDTH = 512
SWA_KV_WIDTH = 128
D_FF = 4096
IN_WIDTH = 4352
GATE_OFF = 2304
ROPE_THETA = 10000.0
NORM_EPS = 1e-6
SCALE = HEAD_DIM ** -0.5
N_CHIPS = 4
N_DEV = 8

ADAM_LR = 0.001
ADAM_B1 = 0.9
ADAM_B2 = 0.999
ADAM_EPS = 1e-08
ADAM_WD = 0.01
ADAM_STEP = 10

LANES = 128
SB_TILE = 256
SB_STEP_WIDTH = 512
SB_FWD_STEP_WIDTH = 512
SWA_BLOCK = 128
VMEM_LIMIT = 56 << 20

NN = (((1,), (0,)), ((), ()))
NT = (((1,), (1,)), ((), ()))
TN = (((0,), (0,)), ((), ()))
MESH = pl.DeviceIdType.MESH
HBM = pl.BlockSpec(memory_space=pl.ANY)

BIG = ("w_in", "w_branch_sb", "w_branch_swa", "w_out", "w_up", "w_down")
SMALL_NAMES = ("mix_norm_g", "q_norm_g", "k_norm_g", "sinks", "mlp_norm_g")
PACK_ROWS = (("w_in", 1088), ("w_branch_sb", 128), ("w_branch_swa", 128), ("w_out", 256), ("w_up", 1024), ("w_down", 1024))
SHARD_SHAPES = {"w_in": (1024, 1088), "w_branch_sb": (512, 256), "w_branch_swa": (512, 256), "w_out": (256, 1024),
                "w_up": (1024, 1024), "w_down": (1024, 1024)}
ROW_SHARDED = ("w_out", "w_down")
SMALL_ROWS = 72


def _dot(a, b, dims):
    return lax.dot_general(a, b, dims, preferred_element_type=F32)


def _cparams(sem):
    return pltpu.CompilerParams(dimension_semantics=sem, vmem_limit_bytes=VMEM_LIMIT)


def _resident(shape, index=None):
    index = (0,) * len(shape) if index is None else index
    return pl.BlockSpec(shape, lambda *_: index, pipeline_mode=pl.Buffered(1))


REST = ("w_up", "w_down", "w_out", "w_branch_sb", "w_branch_swa")
REST_ROWS = 2560
W_UP_BLOCK = ((N_CHIPS, 1024, 1024), (0, 0, 0))
W_DOWN_BLOCK = ((N_CHIPS, 1024, 1024), (0, 1, 0))
W_OUT_BLOCK = ((N_CHIPS, 256, 1024), (0, 8, 0))


def _rows(tm, width):
    return pl.BlockSpec((tm, width), lambda i: (i, 0))


def _rms(xf):
    rstd = lax.rsqrt(jnp.mean(xf * xf, axis=-1, keepdims=True) + NORM_EPS)
    return xf * rstd, rstd


def _rms_bwd(dh, xn, rstd, g):
    dxn = dh * g
    return rstd * (dxn - xn * jnp.mean(dxn * xn, axis=-1, keepdims=True))


def _lane(shape):
    return lax.broadcasted_iota(jnp.int32, shape, len(shape) - 1)


def _head_mean(v, left):
    sl = jnp.sum(jnp.where(left, v, 0.0), axis=-1, keepdims=True)
    sr = jnp.sum(jnp.where(left, 0.0, v), axis=-1, keepdims=True)
    return jnp.where(left, sl, sr) * (1.0 / HEAD_DIM)


def _rope(y, cs, sn, first):
    up = pltpu.roll(y, 96, 1)
    dn = pltpu.roll(y, 32, 1)
    return y * cs + jnp.where(first, -up, dn) * sn


def _rope_t(d, cs, sn, first):
    t = d * jnp.where(first, -sn, sn)
    return d * cs + jnp.where(first, pltpu.roll(t, 96, 1), pltpu.roll(t, 32, 1))


def _norm_rope(p, g, cs, sn):
    lane = _lane((1, LANES))
    left = lane < HEAD_DIM
    first = (lane % HEAD_DIM) < (HEAD_DIM // 2)
    yn = p * lax.rsqrt(_head_mean(p * p, left) + NORM_EPS)
    return _rope(yn * g, cs, sn, first)


def _norm_rope_bwd(p, dout, g, cs, sn):
    lane = _lane((1, LANES))
    left = lane < HEAD_DIM
    first = (lane % HEAD_DIM) < (HEAD_DIM // 2)
    rstd = lax.rsqrt(_head_mean(p * p, left) + NORM_EPS)
    yn = p * rstd
    dyg = _rope_t(dout, cs, sn, first)
    dg = jnp.sum(dyg * yn, axis=0, keepdims=True)
    dyn = dyg * g
    dp = rstd * (dyn - yn * _head_mean(dyn * yn, left))
    return dp, dg


def _place():
    x, y, c = lax.axis_index("x"), lax.axis_index("y"), lax.axis_index("c")
    return x, y, c, [(1 - x, y), (x, 1 - y), (1 - x, 1 - y)]


def _hosted_gather(src, out, ssem, rsem, lsem, first, mid, last, slot=0):
    x, y, c, chips = _place()
    me = 2 * x + y
    sib = (x, y, 1 - c)
    r2 = src.shape[0] // 2
    base = 6 * slot

    def slab(chip, half):
        return out.at[chip, pl.ds(half * r2, r2)]

    def ici(j):
        cx, cy = chips[j]
        return pltpu.make_async_remote_copy(src.at[pl.ds(c * r2, r2)], slab(me, c), ssem.at[base + j], rsem.at[base + j],
                                            device_id=(cx, cy, c), device_id_type=MESH)

    def landed(j):
        got = slab(2 * chips[j][0] + chips[j][1], c)
        return pltpu.make_async_remote_copy(got, got, ssem.at[base + j], rsem.at[base + j], device_id=sib, device_id_type=MESH)

    def d2d(j, half):
        got = slab(2 * chips[j][0] + chips[j][1], half)
        return pltpu.make_async_remote_copy(got, got, ssem.at[base + 3 + j], rsem.at[base + 3 + j], device_id=sib,
                                            device_id_type=MESH)

    local = pltpu.make_async_copy(src, out.at[me], lsem.at[slot])

    @pl.when(first)
    def _():
        local.start()
        for j in range(3):
            ici(j).start()

    @pl.when(mid)
    def _():
        for j in range(3):
            landed(j).wait_recv()
            d2d(j, c).start()

    @pl.when(last)
    def _():
        for j in range(3):
            d2d(j, 1 - c).wait_recv()
        for j in range(3):
            ici(j).wait_send()
            d2d(j, c).wait_send()
        local.wait()


def _hosted_exchange(srcs, outs, ssem, rsem, first, last):
    x, y, c, chips = _place()
    me = 2 * x + y

    def send(t, j):
        cx, cy = chips[j]
        return pltpu.make_async_remote_copy(srcs[t].at[2 * cx + cy], outs[t].at[me], ssem.at[3 * t + j], rsem.at[3 * t + j],
                                            device_id=(cx, cy, c), device_id_type=MESH)

    def landed(t, j):
        cx, cy = chips[j]
        got = outs[t].at[2 * cx + cy]
        return pltpu.make_async_remote_copy(got, got, ssem.at[3 * t + j], rsem.at[3 * t + j],
                                            device_id=(cx, cy, c), device_id_type=MESH)

    @pl.when(first)
    def _():
        for t in range(len(srcs)):
            for j in range(3):
                send(t, j).start()

    @pl.when(last)
    def _():
        for t in range(len(srcs)):
            for j in range(3):
                landed(t, j).wait_recv()
        for t in range(len(srcs)):
            for j in range(3):
                send(t, j).wait_send()


def _gather_scratch(n):
    return [pltpu.SemaphoreType.DMA((6 * n,)), pltpu.SemaphoreType.DMA((6 * n,)), pltpu.SemaphoreType.DMA((n,))]


def _exchange_scratch(n):
    return [pltpu.SemaphoreType.DMA((3 * n,)), pltpu.SemaphoreType.DMA((3 * n,))]


def _inproj_fwd(x, g, w, qg, kg, cos, sin):
    s = x.shape[0]
    tm = min(512, s)

    def body(x_ref, g_ref, w_ref, qg_ref, kg_ref, cos_ref, sin_ref,
             h_ref, sb_ref, raw_ref, qn_ref, kn_ref, v_ref, gate_ref):
        xn, _ = _rms(x_ref[...])
        h = (xn * g_ref[...]).astype(BF16)
        h_ref[...] = h
        for a in range(0, 3 * SB_WIDTH, 512):
            sb_ref[:, a:a + 512] = _dot(h, w_ref[:, a:a + 512], NN).astype(BF16)
        cs, sn = cos_ref[...], sin_ref[...]
        q0 = 3 * SB_WIDTH
        pq = _dot(h, w_ref[:, q0:q0 + SWA_Q_WIDTH], NN)
        raw_ref[:, 0:SWA_Q_WIDTH] = pq
        for b in range(SWA_Q_WIDTH // LANES):
            blk = pq[:, b * LANES:(b + 1) * LANES]
            qn_ref[:, b * LANES:(b + 1) * LANES] = _norm_rope(blk, qg_ref[...], cs, sn).astype(BF16)
        k0 = q0 + SWA_Q_WIDTH
        pk = _dot(h, w_ref[:, k0:k0 + 2 * SWA_KV_WIDTH], NN)
        raw_ref[:, SWA_Q_WIDTH:SWA_Q_WIDTH + SWA_KV_WIDTH] = pk[:, :SWA_KV_WIDTH]
        kn_ref[...] = _norm_rope(pk[:, :SWA_KV_WIDTH], kg_ref[...], cs, sn).astype(BF16)
        v_ref[...] = pk[:, SWA_KV_WIDTH:].astype(BF16)
        for a in range(GATE_OFF, IN_WIDTH, 512):
            gate_ref[:, a - GATE_OFF:a - GATE_OFF + 512] = jax.nn.sigmoid(_dot(h, w_ref[:, a:a + 512], NN))

    return pl.pallas_call(
        body, name="inproj_fwd", grid=(s // tm,),
        in_specs=[_rows(tm, D_MODEL), _resident((1, D_MODEL)), _resident((D_MODEL, IN_WIDTH)),
                  _resident((1, LANES)), _resident((1, LANES)), _rows(tm, LANES), _rows(tm, LANES)],
        out_specs=[_rows(tm, D_MODEL), _rows(tm, 3 * SB_WIDTH), _rows(tm, 640), _rows(tm, SWA_Q_WIDTH),
                   _rows(tm, SWA_KV_WIDTH), _rows(tm, SWA_KV_WIDTH), _rows(tm, 2 * D_MODEL)],
        out_shape=[jax.ShapeDtypeStruct((s, D_MODEL), BF16), jax.ShapeDtypeStruct((s, 3 * SB_WIDTH), BF16),
                   jax.ShapeDtypeStruct((s, 640), F32), jax.ShapeDtypeStruct((s, SWA_Q_WIDTH), BF16),
                   jax.ShapeDtypeStruct((s, SWA_KV_WIDTH), BF16), jax.ShapeDtypeStruct((s, SWA_KV_WIDTH), BF16),
                   jax.ShapeDtypeStruct((s, 2 * D_MODEL), F32)],
        compiler_params=_cparams(("arbitrary",)),
    )(x, g, w, qg, kg, cos, sin)


def _sb_tile_terms(qh, k, strict):
    z = _dot(qh, k, NT)
    nz = -z
    soft = jnp.log(1.0 + jnp.exp(jnp.minimum(z, nz)))
    log_keep = jnp.minimum(nz, 0.0) - soft
    if strict is not None:
        log_keep = jnp.where(strict, log_keep, 0.0)
    return z, log_keep


def _log_weights(z, log_keep, run, u_from):
    tail = run + _dot(log_keep.astype(BF16), u_from, NN)
    return z + tail, tail[:, 0:1]


def _masked(strict, val):
    return val if strict is None else jnp.where(strict, val, 0.0)


def _tri_twice(t):
    r = lax.broadcasted_iota(jnp.int32, (2 * t, t), 0)
    c = lax.broadcasted_iota(jnp.int32, (2 * t, t), 1)
    return (jnp.where(r >= t, r - t, r) >= c).astype(BF16)


def _split_dot(a, u2):
    hi = a.astype(BF16)
    lo = (a - hi.astype(F32)).astype(BF16)
    return _dot(jnp.concatenate([hi, lo], axis=1), u2, NN)


def _sb_fwd(sbqkv, shards=()):
    s = sbqkv.shape[0]
    t = min(SB_TILE, s)
    nq = s // t
    ng = SB_WIDTH // SB_FWD_STEP_WIDTH
    heads = range(2 * SB_FWD_STEP_WIDTH // LANES)
    nsh = len(shards)

    def body(*refs):
        q_ref, k_ref, v_ref = refs[:3]
        o_ref = refs[3 + nsh]
        j, i = pl.program_id(0), pl.program_id(1)
        for n in range(nsh):
            _hosted_gather(refs[3 + n], refs[4 + nsh + n], *refs[4 + 2 * nsh:], (j == 0) & (i == 0),
                           (j == ng - 1) & (i == (3 * nq) // 4), (j == ng - 1) & (i == nq - 1), slot=n)
        left = _lane((1, LANES)) < HEAD_DIM
        halves = (left, jnp.logical_not(left))
        pair = lambda a, p: a[:, (p // 2) * LANES:(p // 2 + 1) * LANES]
        only = lambda a, p: jnp.where(halves[p % 2], pair(a, p), jnp.zeros((), a.dtype))
        q_all = q_ref[...] * jnp.asarray(SCALE, BF16)
        qh = [only(q_all, p) for p in heads]
        r = lax.broadcasted_iota(jnp.int32, (t, t), 0)
        c = lax.broadcasted_iota(jnp.int32, (t, t), 1)
        u_from = (r >= c).astype(BF16)

        def tile(n, carry, strict):
            runs, accs = carry
            off = pl.multiple_of((i - n) * t, t)
            k = k_ref[pl.ds(off, t), :]
            v = v_ref[pl.ds(off, t), :]
            terms = [_sb_tile_terms(qh[p], pair(k, p), strict) for p in heads]
            logw = [_log_weights(*terms[p], runs[p], u_from) for p in heads]
            ws = [_masked(strict, jnp.exp(logw[p][0])).astype(BF16) for p in heads]
            accs = list(accs)
            for p in heads:
                accs[p // 2] = accs[p // 2] + _dot(ws[p], only(v, p), NN)
            return tuple(logw[p][1] for p in heads), tuple(accs)

        zero = jnp.zeros((t, 1), F32)
        wide = jnp.zeros((t, LANES), F32)
        carry = tile(0, (tuple(zero for _ in heads), tuple(wide for _ in heads[::2])), c < r)
        _, accs = lax.fori_loop(1, i + 1, lambda n, cr: tile(n, cr, None), carry)
        for g, acc in enumerate(accs):
            o_ref[:, g * LANES:(g + 1) * LANES] = acc

    w = SB_FWD_STEP_WIDTH
    blk = pl.BlockSpec((t, w), lambda j, i: (i, j))
    in_specs = [blk, pl.BlockSpec((s, w), lambda j, i: (0, ng + j)), pl.BlockSpec((s, w), lambda j, i: (0, 2 * ng + j))]
    out_specs = [blk]
    out_shape = [jax.ShapeDtypeStruct((s, SB_WIDTH), F32)]
    in_specs += [HBM] * nsh
    out_specs += [HBM] * nsh
    out_shape += [jax.ShapeDtypeStruct((N_CHIPS,) + a.shape, a.dtype) for a in shards]
    return pl.pallas_call(
        body, name="sb_fwd_g%d" % nsh, grid=(ng, nq),
        in_specs=in_specs, out_specs=out_specs, out_shape=out_shape,
        scratch_shapes=_gather_scratch(nsh) if nsh else [],
        compiler_params=_cparams(("arbitrary", "arbitrary")),
    )(sbqkv, sbqkv, sbqkv, *shards)


def _swa_band(cur_ref, prev_ref, b):
    lo = prev_ref[...] if b == 0 else cur_ref[(b - 1) * SWA_BLOCK:b * SWA_BLOCK, :]
    return jnp.concatenate([lo, cur_ref[b * SWA_BLOCK:(b + 1) * SWA_BLOCK, :]], axis=0)


def _swa_variants(band, left):
    f = band.astype(F32)
    sw = pltpu.roll(f, HEAD_DIM, 1)
    halves = (left, jnp.logical_not(left))
    return [[jnp.where(halves[p], f if p == g else sw, 0.0).astype(BF16) for p in range(2)] for g in range(2)]


def _swa_valid(blk):
    ii = lax.broadcasted_iota(jnp.int32, (SWA_BLOCK, 2 * SWA_BLOCK), 0)
    jj = lax.broadcasted_iota(jnp.int32, (SWA_BLOCK, 2 * SWA_BLOCK), 1)
    rel = jj - SWA_BLOCK - ii
    return (rel <= 0) & (rel > -SWA_BLOCK) & (jj + blk * SWA_BLOCK >= SWA_BLOCK)


def _swa_softmax(dots, sink, valid):
    sc = jnp.where(valid, dots * SCALE, -1e30)
    m = jnp.maximum(jnp.max(sc, axis=-1, keepdims=True), sink)
    e = jnp.exp(sc - m)
    es = jnp.exp(sink - m)
    inv = 1.0 / (jnp.sum(e, axis=-1, keepdims=True) + es)
    return e * inv, es * inv


def _swa_heads(q_ref, rows, halves):
    nheads = SWA_Q_WIDTH // HEAD_DIM
    out = []
    for h in range(nheads):
        hb, p = h // 2, h % 2
        q2 = q_ref[rows, hb * LANES:(hb + 1) * LANES]
        out.append((hb, p, h // (nheads // 2), jnp.where(halves[p], q2, jnp.zeros_like(q2))))
    return out


def _swa_specs(s, t):
    nb = t // SWA_BLOCK
    cur = pl.BlockSpec((t, LANES), lambda i: (i, 0))
    prev = pl.BlockSpec((SWA_BLOCK, LANES), lambda i: (jnp.maximum(i * nb - 1, 0), 0))
    return cur, prev


def _swa_fwd(qn, kn, v, sinks):
    s = qn.shape[0]
    t = min(512, s)
    nb = t // SWA_BLOCK
    nheads = SWA_Q_WIDTH // HEAD_DIM

    def body(sink_ref, q_ref, kc_ref, kp_ref, vc_ref, vp_ref, o_ref):
        i = pl.program_id(0)
        left = _lane((1, LANES)) < HEAD_DIM
        halves = (left, jnp.logical_not(left))
        for b in range(nb):
            kvar = _swa_variants(_swa_band(kc_ref, kp_ref, b), left)
            vvar = _swa_variants(_swa_band(vc_ref, vp_ref, b), left)
            rows = slice(b * SWA_BLOCK, (b + 1) * SWA_BLOCK)
            valid = _swa_valid(i * nb + b)
            heads = _swa_heads(q_ref, rows, halves)
            dots = [_dot(qh, kvar[g][p], NT) for _, p, g, qh in heads]
            probs = [_swa_softmax(dots[h], sink_ref[h], valid)[0].astype(BF16) for h in range(nheads)]
            outs = [_dot(probs[h], vvar[g][p], NN) for h, (_, p, g, _) in enumerate(heads)]
            for hb in range(nheads // 2):
                o_ref[rows, hb * LANES:(hb + 1) * LANES] = (outs[2 * hb] + outs[2 * hb + 1]).astype(BF16)

    cur, prev = _swa_specs(s, t)
    return pl.pallas_call(
        body, name="swa_fwd", grid=(s // t,),
        in_specs=[pl.BlockSpec(memory_space=pltpu.SMEM), _rows(t, SWA_Q_WIDTH), cur, prev, cur, prev],
        out_specs=_rows(t, SWA_Q_WIDTH),
        out_shape=jax.ShapeDtypeStruct((s, SWA_Q_WIDTH), BF16),
        compiler_params=_cparams(("arbitrary",)),
    )(sinks, qn, kn, kn, v, v)


def _merge_fwd(x, osb, oswa, gates, wsb, wswa, rest):
    s = x.shape[0]
    tm = min(512, s)
    rc = D_MODEL // N_CHIPS

    def body(x_ref, osb_ref, oswa_ref, gate_ref, wsb_ref, wswa_ref, wout_ref, x1_ref):
        ysb = _dot(osb_ref[...].astype(BF16), wsb_ref[...], NN)
        yswa = _dot(oswa_ref[...], wswa_ref[...], NN)
        merged = (gate_ref[:, :D_MODEL] * ysb + gate_ref[:, D_MODEL:] * yswa).astype(BF16)
        acc = x_ref[...]
        for q in range(N_CHIPS):
            acc = acc + _dot(merged[:, q * rc:(q + 1) * rc], wout_ref[q], NN)
        x1_ref[...] = acc

    return pl.pallas_call(
        body, name="merge_fwd", grid=(s // tm,),
        in_specs=[_rows(tm, D_MODEL), _rows(tm, SB_WIDTH), _rows(tm, SWA_Q_WIDTH), _rows(tm, 2 * D_MODEL),
                  _resident((SB_WIDTH, D_MODEL)), _resident((SWA_Q_WIDTH, D_MODEL)), _resident(*W_OUT_BLOCK)],
        out_specs=_rows(tm, D_MODEL),
        out_shape=jax.ShapeDtypeStruct((s, D_MODEL), F32),
        compiler_params=_cparams(("arbitrary",)),
    )(x, osb, oswa, gates, wsb, wswa, rest)


def _mlp_fwd(x1, g, rest):
    s = x1.shape[0]
    tm = min(512, s)
    fc = D_FF // N_CHIPS

    def body(x_ref, g_ref, wup_ref, wdown_ref, x2_ref, u_ref):
        xf = x_ref[...]
        xn, _ = _rms(xf)
        h2 = (xn * g_ref[...]).astype(BF16)
        acc = xf
        for q in range(N_CHIPS):
            u = _dot(h2, wup_ref[q], NN)
            u_ref[:, q * fc:(q + 1) * fc] = u
            r = jnp.maximum(u, 0.0)
            acc = acc + _dot((r * r).astype(BF16), wdown_ref[q], NN)
        x2_ref[...] = acc

    return pl.pallas_call(
        body, name="mlp_fwd", grid=(s // tm,),
        in_specs=[_rows(tm, D_MODEL), _resident((1, D_MODEL)), _resident(*W_UP_BLOCK), _resident(*W_DOWN_BLOCK)],
        out_specs=[_rows(tm, D_MODEL), _rows(tm, D_FF)],
        out_shape=[jax.ShapeDtypeStruct((s, D_MODEL), F32), jax.ShapeDtypeStruct((s, D_FF), F32)],
        compiler_params=_cparams(("arbitrary",)),
    )(x1, g, rest, rest)


def _loss_grad(y, target):
    s = y.shape[0]
    tm = min(512, s)

    def body(y_ref, t_ref, dy_ref, part_ref):
        err = y_ref[...] - t_ref[...]
        dy_ref[...] = err * (1.0 / D_MODEL)
        tot = jnp.sum(jnp.sum(err * err, axis=-1, keepdims=True), axis=0, keepdims=True)
        part_ref[...] = jnp.broadcast_to(tot.reshape(1, 1, 1), (1, 8, LANES))

    dy, part = pl.pallas_call(
        body, name="loss_grad", grid=(s // tm,),
        in_specs=[_rows(tm, D_MODEL), _rows(tm, D_MODEL)],
        out_specs=[_rows(tm, D_MODEL), pl.BlockSpec((1, 8, LANES), lambda i: (i, 0, 0))],
        out_shape=[jax.ShapeDtypeStruct((s, D_MODEL), F32), jax.ShapeDtypeStruct((s // tm, 8, LANES), F32)],
        compiler_params=_cparams(("arbitrary",)),
    )(y, target)
    return dy, (0.5 / D_MODEL) * jnp.sum(part[:, 0, 0])


def _mlp_bwd(dx2, x1, u, g, rest):
    s = x1.shape[0]
    tm = min(256, s)
    fc = D_FF // N_CHIPS

    def body(dx2_ref, x_ref, u_ref, g_ref, wup_ref, wdown_ref, dx1_ref, du_ref, a_ref, h2_ref, dg_ref, dxb_ref):
        @pl.when(pl.program_id(0) == 0)
        def _():
            dg_ref[...] = jnp.zeros_like(dg_ref)

        gam = g_ref[...]
        xn, rstd = _rms(x_ref[...])
        h2_ref[...] = (xn * gam).astype(BF16)
        dxf = dx2_ref[...]
        dxb = dxf.astype(BF16)
        dxb_ref[...] = dxb
        dh2 = jnp.zeros((tm, D_MODEL), F32)
        for q in range(N_CHIPS):
            cols = slice(q * fc, (q + 1) * fc)
            da = _dot(dxb, wdown_ref[q], NT)
            r = jnp.maximum(u_ref[:, cols], 0.0)
            a_ref[:, cols] = (r * r).astype(BF16)
            du = (da * (2.0 * r)).astype(BF16)
            du_ref[:, cols] = du
            dh2 = dh2 + _dot(du, wup_ref[q], NT)
        dg_ref[...] += jnp.sum(dh2 * xn, axis=0, keepdims=True)
        dx1_ref[...] = dxf + _rms_bwd(dh2, xn, rstd, gam)

    return pl.pallas_call(
        body, name="mlp_bwd", grid=(s // tm,),
        in_specs=[_rows(tm, D_MODEL), _rows(tm, D_MODEL), _rows(tm, D_FF), _resident((1, D_MODEL)),
                  _resident(*W_UP_BLOCK), _resident(*W_DOWN_BLOCK)],
        out_specs=[_rows(tm, D_MODEL), _rows(tm, D_FF), _rows(tm, D_FF), _rows(tm, D_MODEL),
                   pl.BlockSpec((1, D_MODEL), lambda i: (0, 0)), _rows(tm, D_MODEL)],
        out_shape=[jax.ShapeDtypeStruct((s, D_MODEL), F32), jax.ShapeDtypeStruct((s, D_FF), BF16),
                   jax.ShapeDtypeStruct((s, D_FF), BF16), jax.ShapeDtypeStruct((s, D_MODEL), BF16),
                   jax.ShapeDtypeStruct((1, D_MODEL), F32), jax.ShapeDtypeStruct((s, D_MODEL), BF16)],
        compiler_params=_cparams(("arbitrary",)),
    )(dx2, x1, u, g, rest, rest)


def _wgrad(a, b, name, tm, tn, shard_axis=None):
    s, m = a.shape
    n = b.shape[1]

    def body(a_ref, b_ref, o_ref, *narrow):
        res = _dot(a_ref[...].astype(BF16), b_ref[...].astype(BF16), TN)
        o_ref[...] = res.reshape(o_ref.shape)
        for n_ref in narrow:
            n_ref[...] = res.astype(BF16).reshape(n_ref.shape)

    if shard_axis is None:
        out_shape, out_spec = (m, n), pl.BlockSpec((tm, tn), lambda i, j: (i, j))
    elif shard_axis == 0:
        per = m // N_CHIPS // tm
        out_shape, out_spec = (N_CHIPS, m // N_CHIPS, n), pl.BlockSpec((1, tm, tn), lambda i, j: (i // per, i % per, j))
    else:
        per = n // N_CHIPS // tn
        out_shape, out_spec = (N_CHIPS, m, n // N_CHIPS), pl.BlockSpec((1, tm, tn), lambda i, j: (j // per, i, j % per))
    both = shard_axis is not None
    a_spec = _resident((s, tm)) if tm == m else pl.BlockSpec((s, tm), lambda i, j: (0, i))
    b_spec = _resident((s, tn)) if tn == n else pl.BlockSpec((s, tn), lambda i, j: (0, j))
    return pl.pallas_call(
        body, name=name, grid=(m // tm, n // tn),
        in_specs=[a_spec, b_spec],
        out_specs=[out_spec, out_spec] if both else out_spec,
        out_shape=[jax.ShapeDtypeStruct(out_shape, F32), jax.ShapeDtypeStruct(out_shape, BF16)] if both
        else jax.ShapeDtypeStruct(out_shape, F32),
        compiler_params=_cparams(("arbitrary", "arbitrary")),
    )(a, b)


def _merge_bwd(dx1, osb, oswa, gates, wsb, wswa, rest):
    s = dx1.shape[0]
    tm = min(512, s)

    def body(dx_ref, osb_ref, oswa_ref, gate_ref, wsb_ref, wswa_ref, wout_ref,
             dosb_ref, doswa_ref, dgl_ref, merged_ref, dysb_ref, dyswa_ref, dxb_ref):
        dxb = dx_ref[...].astype(BF16)
        dxb_ref[...] = dxb
        dm = jnp.concatenate([_dot(dxb, wout_ref[q], NT) for q in range(N_CHIPS)], axis=1)
        ysb = _dot(osb_ref[...].astype(BF16), wsb_ref[...], NN)
        yswa = _dot(oswa_ref[...], wswa_ref[...], NN)
        g0 = gate_ref[:, :D_MODEL]
        g1 = gate_ref[:, D_MODEL:]
        merged_ref[...] = (g0 * ysb + g1 * yswa).astype(BF16)
        dgl_ref[:, :D_MODEL] = (dm * ysb * (g0 * (1.0 - g0))).astype(BF16)
        dgl_ref[:, D_MODEL:] = (dm * yswa * (g1 * (1.0 - g1))).astype(BF16)
        dysb = (dm * g0).astype(BF16)
        dyswa = (dm * g1).astype(BF16)
        dysb_ref[...] = dysb
        dyswa_ref[...] = dyswa
        dosb_ref[...] = _dot(dysb, wsb_ref[...], NT)
        doswa_ref[...] = _dot(dyswa, wswa_ref[...], NT)

    return pl.pallas_call(
        body, name="merge_bwd", grid=(s // tm,),
        in_specs=[_rows(tm, D_MODEL), _rows(tm, SB_WIDTH), _rows(tm, SWA_Q_WIDTH), _rows(tm, 2 * D_MODEL),
                  _resident((SB_WIDTH, D_MODEL)), _resident((SWA_Q_WIDTH, D_MODEL)), _resident(*W_OUT_BLOCK)],
        out_specs=[_rows(tm, SB_WIDTH), _rows(tm, SWA_Q_WIDTH), _rows(tm, 2 * D_MODEL), _rows(tm, D_MODEL),
                   _rows(tm, D_MODEL), _rows(tm, D_MODEL), _rows(tm, D_MODEL)],
        out_shape=[jax.ShapeDtypeStruct((s, SB_WIDTH), F32), jax.ShapeDtypeStruct((s, SWA_Q_WIDTH), F32),
                   jax.ShapeDtypeStruct((s, 2 * D_MODEL), BF16), jax.ShapeDtypeStruct((s, D_MODEL), BF16),
                   jax.ShapeDtypeStruct((s, D_MODEL), BF16), jax.ShapeDtypeStruct((s, D_MODEL), BF16),
                   jax.ShapeDtypeStruct((s, D_MODEL), BF16)],
        compiler_params=_cparams(("arbitrary",)),
    )(dx1, osb, oswa, gates, wsb, wswa, rest)


def _sb_bwd(sbqkv, osb, dosb, send):
    s = sbqkv.shape[0]
    t = min(SB_TILE, s)
    nq = s // t
    ng = SB_WIDTH // SB_STEP_WIDTH
    heads = range(2 * SB_STEP_WIDTH // LANES)
    ns = len(send)

    def body(*refs):
        q_ref, k_ref, v_ref, o_ref, do_ref = refs[:5]
        srcs = refs[5:5 + ns]
        dq_ref, dk_ref, dv_ref = refs[5 + ns:8 + ns]
        outs = refs[8 + ns:8 + 2 * ns]
        j, i = pl.program_id(0), pl.program_id(1)
        if ns:
            ssem, rsem = refs[8 + 2 * ns:]
            _hosted_exchange(srcs, outs, ssem, rsem, (j == 0) & (i == 0), (j == ng - 1) & (i == nq - 1))

        @pl.when(i == 0)
        def _():
            dk_ref[...] = jnp.zeros_like(dk_ref)
            dv_ref[...] = jnp.zeros_like(dv_ref)

        left = _lane((1, LANES)) < HEAD_DIM
        halves = (left, jnp.logical_not(left))
        pair = lambda a, p: a[:, (p // 2) * LANES:(p // 2 + 1) * LANES]
        only = lambda a, p: jnp.where(halves[p % 2], pair(a, p), jnp.zeros((), a.dtype))
        scale = jnp.asarray(SCALE, BF16)
        q_all = q_ref[...] * scale
        do_all = do_ref[...].astype(BF16)
        prod = do_all.astype(F32) * o_ref[...]
        qh = [only(q_all, p) for p in heads]
        doh = [only(do_all, p) for p in heads]
        delta = [jnp.sum(only(prod, p), axis=-1, keepdims=True) for p in heads]
        r = lax.broadcasted_iota(jnp.int32, (t, t), 0)
        c = lax.broadcasted_iota(jnp.int32, (t, t), 1)
        u_from = (r >= c).astype(BF16)
        u_from2 = _tri_twice(t)

        def tile(n, carry, strict):
            runs, dqs = carry
            off = pl.multiple_of((i - n) * t, t)
            k = k_ref[pl.ds(off, t), :]
            v = v_ref[pl.ds(off, t), :]
            ks = k * scale
            terms = [_sb_tile_terms(qh[p], pair(k, p), strict) for p in heads]
            logw = [_log_weights(*terms[p], runs[p][0], u_from) for p in heads]
            dws = [_dot(doh[p], pair(v, p), NT) for p in heads]
            wbs = [_masked(strict, jnp.exp(logw[p][0])).astype(BF16) for p in heads]
            es = [dws[p] * wbs[p].astype(F32) for p in heads]
            rests = [runs[p][1] + _split_dot(es[p], u_from2) for p in heads]
            betas = [jnp.exp(terms[p][0] + terms[p][1]) for p in heads]
            dzs = [_masked(strict, es[p] - betas[p] * (es[p] - rests[p])).astype(BF16) for p in heads]
            dqs = list(dqs)
            for g in range(len(heads) // 2):
                a, b = 2 * g, 2 * g + 1
                cols = slice(g * LANES, (g + 1) * LANES)
                dv_ref[pl.ds(off, t), cols] += _dot(wbs[a], doh[a], TN) + _dot(wbs[b], doh[b], TN)
                dk_ref[pl.ds(off, t), cols] += _dot(dzs[a], qh[a], TN) + _dot(dzs[b], qh[b], TN)
                dqs[g] = dqs[g] + _dot(dzs[a], only(ks, a), NN) + _dot(dzs[b], only(ks, b), NN)
            new_runs = tuple((logw[p][1], rests[p][:, 0:1]) for p in heads)
            return new_runs, tuple(dqs)

        zero = jnp.zeros((t, 1), F32)
        wide = jnp.zeros((t, LANES), F32)
        carry = tile(0, (tuple((zero, -delta[p]) for p in heads), tuple(wide for _ in heads[::2])), c < r)
        _, dqs = lax.fori_loop(1, i + 1, lambda n, cr: tile(n, cr, None), carry)
        for g, dq in enumerate(dqs):
            dq_ref[:, g * LANES:(g + 1) * LANES] = dq.astype(BF16)

    w = SB_STEP_WIDTH
    blk = pl.BlockSpec((t, w), lambda j, i: (i, j))
    whole = pl.BlockSpec((s, w), lambda j, i: (0, j))
    return pl.pallas_call(
        body, name="sb_bwd_x%d" % ns, grid=(ng, nq),
        in_specs=[blk, pl.BlockSpec((s, w), lambda j, i: (0, ng + j)),
                  pl.BlockSpec((s, w), lambda j, i: (0, 2 * ng + j)), blk, blk] + [HBM] * ns,
        out_specs=[blk, whole, whole] + [HBM] * ns,
        out_shape=[jax.ShapeDtypeStruct((s, SB_WIDTH), BF16), jax.ShapeDtypeStruct((s, SB_WIDTH), F32),
                   jax.ShapeDtypeStruct((s, SB_WIDTH), F32)] + [jax.ShapeDtypeStruct(a.shape, a.dtype) for a in send],
        scratch_shapes=_exchange_scratch(ns) if ns else [],
        compiler_params=_cparams(("arbitrary", "arbitrary")),
    )(sbqkv, sbqkv, sbqkv, osb, dosb, *send)


def _swa_bwd(qn, kn, v, sinks, do):
    s = qn.shape[0]
    t = min(512, s)
    nb = t // SWA_BLOCK
    nheads = SWA_Q_WIDTH // HEAD_DIM

    def body(sink_ref, q_ref, kc_ref, kp_ref, vc_ref, vp_ref, do_ref, dq_ref, dk_ref, dv_ref, dsink_ref):
        i = pl.program_id(0)

        @pl.when(i == 0)
        def _():
            dk_ref[...] = jnp.zeros_like(dk_ref)
            dv_ref[...] = jnp.zeros_like(dv_ref)

        left = _lane((1, LANES)) < HEAD_DIM
        halves = (left, jnp.logical_not(left))
        dsink = [jnp.zeros((1, 1), F32) for _ in range(nheads)]
        for b in range(nb):
            kvar = _swa_variants(_swa_band(kc_ref, kp_ref, b), left)
            vvar = _swa_variants(_swa_band(vc_ref, vp_ref, b), left)
            rows = slice(b * SWA_BLOCK, (b + 1) * SWA_BLOCK)
            valid = _swa_valid(i * nb + b)
            heads = _swa_heads(q_ref, rows, halves)
            doh = []
            for hb, p, _, _ in heads:
                do2 = do_ref[rows, hb * LANES:(hb + 1) * LANES]
                doh.append(jnp.where(halves[p], do2, 0.0).astype(BF16))
            dots = [_dot(qh, kvar[g][p], NT) for _, p, g, qh in heads]
            dps = [_dot(doh[h], vvar[g][p], NT) for h, (_, p, g, _) in enumerate(heads)]
            dss, pbs = [], []
            for h in range(nheads):
                probs, psink = _swa_softmax(dots[h], sink_ref[h], valid)
                delta = jnp.sum(probs * dps[h], axis=-1, keepdims=True)
                dss.append((probs * (dps[h] - delta) * SCALE).astype(BF16))
                pbs.append(probs.astype(BF16))
                dsink[h] = dsink[h] - jnp.sum(psink * delta, axis=0, keepdims=True)
            dk_acc = [jnp.zeros((2 * SWA_BLOCK, LANES), F32) for _ in range(2)]
            dv_acc = [jnp.zeros((2 * SWA_BLOCK, LANES), F32) for _ in range(2)]
            dqs = [_dot(dss[h], kvar[g][p], NN) for h, (_, p, g, _) in enumerate(heads)]
            for h, (_, p, g, qh) in enumerate(heads):
                which = 0 if p == g else 1
                dk_acc[which] = dk_acc[which] + _dot(dss[h], qh, TN)
                dv_acc[which] = dv_acc[which] + _dot(pbs[h], doh[h], TN)
            for hb in range(nheads // 2):
                dq_ref[rows, hb * LANES:(hb + 1) * LANES] = dqs[2 * hb] + dqs[2 * hb + 1]
            dkb = dk_acc[0] + pltpu.roll(dk_acc[1], HEAD_DIM, 1)
            dvb = dv_acc[0] + pltpu.roll(dv_acc[1], HEAD_DIM, 1)
            start = pl.multiple_of((i * nb + b) * SWA_BLOCK, SWA_BLOCK)
            dk_ref[pl.ds(start, SWA_BLOCK), :] += dkb[SWA_BLOCK:]
            dv_ref[pl.ds(start, SWA_BLOCK), :] += dvb[SWA_BLOCK:]

            @pl.when(i * nb + b > 0)
            def _(dkb=dkb, dvb=dvb, start=start):
                before = pl.multiple_of(jnp.maximum(start - SWA_BLOCK, 0), SWA_BLOCK)
                dk_ref[pl.ds(before, SWA_BLOCK), :] += dkb[:SWA_BLOCK]
                dv_ref[pl.ds(before, SWA_BLOCK), :] += dvb[:SWA_BLOCK]

        for h in range(nheads):
            dsink_ref[0, h:h + 1, :] = jnp.broadcast_to(dsink[h], (1, LANES))

    cur, prev = _swa_specs(s, t)
    whole = pl.BlockSpec((s, LANES), lambda i: (0, 0))
    return pl.pallas_call(
        body, name="swa_bwd", grid=(s // t,),
        in_specs=[pl.BlockSpec(memory_space=pltpu.SMEM), _rows(t, SWA_Q_WIDTH), cur, prev, cur, prev,
                  _rows(t, SWA_Q_WIDTH)],
        out_specs=[_rows(t, SWA_Q_WIDTH), whole, whole, pl.BlockSpec((1, 8, LANES), lambda i: (i, 0, 0))],
        out_shape=[jax.ShapeDtypeStruct((s, SWA_Q_WIDTH), F32), jax.ShapeDtypeStruct((s, SWA_KV_WIDTH), F32),
                   jax.ShapeDtypeStruct((s, SWA_KV_WIDTH), F32), jax.ShapeDtypeStruct((s // t, 8, LANES), F32)],
        compiler_params=_cparams(("arbitrary",)),
    )(sinks, qn, kn, kn, v, v, do)


def _swa_post(raw, dqn, dkn, qg, kg, cos, sin):
    s = raw.shape[0]
    tm = min(512, s)
    nq = SWA_Q_WIDTH // LANES

    def body(raw_ref, dq_ref, dk_ref, qg_ref, kg_ref, cos_ref, sin_ref, out_ref, dg_ref):
        @pl.when(pl.program_id(0) == 0)
        def _():
            dg_ref[...] = jnp.zeros_like(dg_ref)

        cs, sn = cos_ref[...], sin_ref[...]
        dgq = jnp.zeros((1, LANES), F32)
        for b in range(nq):
            cols = slice(b * LANES, (b + 1) * LANES)
            dp, dg = _norm_rope_bwd(raw_ref[:, cols], dq_ref[:, cols], qg_ref[...], cs, sn)
            out_ref[:, cols] = dp.astype(BF16)
            dgq = dgq + dg
        cols = slice(SWA_Q_WIDTH, SWA_Q_WIDTH + LANES)
        dp, dgk = _norm_rope_bwd(raw_ref[:, cols], dk_ref[...], kg_ref[...], cs, sn)
        out_ref[:, cols] = dp.astype(BF16)
        dg_ref[0:1, :] += dgq
        dg_ref[1:2, :] += dgk

    return pl.pallas_call(
        body, name="swa_post", grid=(s // tm,),
        in_specs=[_rows(tm, 640), _rows(tm, SWA_Q_WIDTH), _rows(tm, LANES), _resident((1, LANES)),
                  _resident((1, LANES)), _rows(tm, LANES), _rows(tm, LANES)],
        out_specs=[_rows(tm, 640), pl.BlockSpec((8, LANES), lambda i: (0, 0))],
        out_shape=[jax.ShapeDtypeStruct((s, 640), BF16), jax.ShapeDtypeStruct((8, LANES), F32)],
        compiler_params=_cparams(("arbitrary",)),
    )(raw, dqn, dkn, qg, kg, cos, sin)


def _inproj_bwd(dsbq, dsbk, dsbv, dswqk, dswv, dgl, x, dx1, g, w):
    s = x.shape[0]
    tm = min(256, s)

    def body(dsbq_ref, dsbk_ref, dsbv_ref, dswqk_ref, dswv_ref, dgl_ref, x_ref, dx1_ref, g_ref, w_ref,
             dx_ref, dproj_ref, dg_ref):
        @pl.when(pl.program_id(0) == 0)
        def _():
            dg_ref[...] = jnp.zeros_like(dg_ref)

        dproj_ref[:, 0:512] = dsbq_ref[...]
        dproj_ref[:, 512:1024] = dsbk_ref[...].astype(BF16)
        dproj_ref[:, 1024:1536] = dsbv_ref[...].astype(BF16)
        dproj_ref[:, 1536:2176] = dswqk_ref[...]
        dproj_ref[:, 2176:2304] = dswv_ref[...].astype(BF16)
        dproj_ref[:, GATE_OFF:IN_WIDTH] = dgl_ref[...]
        dh = jnp.zeros((tm, D_MODEL), F32)
        for a in range(0, IN_WIDTH, 512):
            b = min(a + 512, IN_WIDTH)
            dh = dh + _dot(dproj_ref[:, a:b], w_ref[:, a:b], NT)
        gam = g_ref[...]
        xn, rstd = _rms(x_ref[...])
        dg_ref[...] += jnp.sum(dh * xn, axis=0, keepdims=True)
        dx_ref[...] = dx1_ref[...] + _rms_bwd(dh, xn, rstd, gam)

    return pl.pallas_call(
        body, name="inproj_bwd", grid=(s // tm,),
        in_specs=[_rows(tm, 512), _rows(tm, 512), _rows(tm, 512), _rows(tm, 640), _rows(tm, LANES),
                  _rows(tm, 2 * D_MODEL), _rows(tm, D_MODEL), _rows(tm, D_MODEL), _resident((1, D_MODEL)),
                  _resident((D_MODEL, IN_WIDTH))],
        out_specs=[_rows(tm, D_MODEL), _rows(tm, IN_WIDTH), pl.BlockSpec((1, D_MODEL), lambda i: (0, 0))],
        out_shape=[jax.ShapeDtypeStruct((s, D_MODEL), F32), jax.ShapeDtypeStruct((s, IN_WIDTH), BF16),
                   jax.ShapeDtypeStruct((1, D_MODEL), F32)],
        compiler_params=_cparams(("arbitrary",)),
    )(dsbq, dsbk, dsbv, dswqk, dswv, dgl, x, dx1, g, w)


def _gather_weights(shard):
    def body(src, out, ssem, rsem, lsem):
        once = pl.program_id(0) == 0
        _hosted_gather(src, out, ssem, rsem, lsem, once, once, once)

    return pl.pallas_call(
        body, name="gather_weights", grid=(1,), in_specs=[HBM], out_specs=HBM,
        out_shape=jax.ShapeDtypeStruct((N_CHIPS,) + shard.shape, shard.dtype),
        scratch_shapes=_gather_scratch(1),
        compiler_params=pltpu.CompilerParams(dimension_semantics=("arbitrary",), has_side_effects=True),
    )(shard)


def _exchange(send):
    ns = len(send)

    def body(*refs):
        once = pl.program_id(0) == 0
        _hosted_exchange(refs[:ns], refs[ns:2 * ns], *refs[2 * ns:], once, once)

    return pl.pallas_call(
        body, name="exchange_x%d" % ns, grid=(1,), in_specs=[HBM] * ns, out_specs=[HBM] * ns,
        out_shape=[jax.ShapeDtypeStruct(a.shape, a.dtype) for a in send],
        scratch_shapes=_exchange_scratch(ns),
        compiler_params=pltpu.CompilerParams(dimension_semantics=("arbitrary",), has_side_effects=True),
    )(*send)


def _pair_swap(arrs):
    n = len(arrs)

    def body(*refs):
        x, y, c, _ = _place()
        cps = [pltpu.make_async_remote_copy(refs[t], refs[n + t], refs[2 * n].at[t], refs[2 * n + 1].at[t],
                                            device_id=(x, y, 1 - c), device_id_type=MESH) for t in range(n)]
        for cp in cps:
            cp.start()
        for cp in cps:
            cp.wait()

    return pl.pallas_call(
        body, name="pair_swap", in_specs=[HBM] * n, out_specs=[HBM] * n,
        out_shape=[jax.ShapeDtypeStruct(a.shape, a.dtype) for a in arrs],
        scratch_shapes=[pltpu.SemaphoreType.DMA((n,)), pltpu.SemaphoreType.DMA((n,))],
        compiler_params=pltpu.CompilerParams(has_side_effects=True),
    )(*arrs)


def _allreduce_small(block):
    def body(src, out, buf, ssem, rsem):
        x, y, c, _ = _place()
        me = 4 * x + 2 * y + c
        buf[me] = src[...]
        cps = []
        for k in range(1, N_DEV):
            peer = (x ^ (k >> 2), y ^ ((k >> 1) & 1), c ^ (k & 1))
            cp = pltpu.make_async_remote_copy(src, buf.at[me], ssem.at[k - 1], rsem.at[k - 1], device_id=peer, device_id_type=MESH)
            cp.start()
            cps.append(cp)
        for k in range(1, N_DEV):
            got = buf.at[me ^ k]
            pltpu.make_async_remote_copy(got, got, ssem.at[k - 1], rsem.at[k - 1], device_id=(x, y, c), device_id_type=MESH).wait_recv()
        for cp in cps:
            cp.wait_send()
        tot = buf[0]
        for d in range(1, N_DEV):
            tot = tot + buf[d]
        out[...] = tot

    vm = pl.BlockSpec(memory_space=pltpu.VMEM)
    return pl.pallas_call(
        body, name="allreduce_small", in_specs=[vm], out_specs=vm,
        out_shape=jax.ShapeDtypeStruct(block.shape, F32),
        scratch_shapes=[pltpu.VMEM((N_DEV,) + block.shape, F32), pltpu.SemaphoreType.DMA((N_DEV - 1,)),
                        pltpu.SemaphoreType.DMA((N_DEV - 1,))],
        compiler_params=pltpu.CompilerParams(has_side_effects=True),
    )(block)


def _sum_chips(landed, own, into, layer, name):
    nq, k, n = landed.shape
    tr = min(256, k)
    me = (2 * lax.axis_index("x") + lax.axis_index("y")).astype(jnp.int32).reshape(1)

    def body(me_ref, p0, p1, p2, p3, own_ref, into_ref, o_ref):
        mine = own_ref[0]
        terms = [jnp.where(me_ref[0] == q, mine, p[0].astype(F32)) for q, p in enumerate((p0, p1, p2, p3))]
        o_ref[0] = ((terms[0] + terms[1]) + terms[2]) + terms[3]

    spec = lambda q: pl.BlockSpec((1, tr, n), lambda i, m, q=q: (q, i, 0))
    return pl.pallas_call(
        body, name=name,
        grid_spec=pltpu.PrefetchScalarGridSpec(
            num_scalar_prefetch=1, grid=(k // tr,),
            in_specs=[spec(q) for q in range(nq)] + [pl.BlockSpec((1, tr, n), lambda i, m: (m[0], i, 0)), HBM],
            out_specs=pl.BlockSpec((1, tr, n), lambda i, m: (layer, i, 0))),
        out_shape=jax.ShapeDtypeStruct(into.shape, F32),
        input_output_aliases={6: 0},
        compiler_params=_cparams(("arbitrary",)),
    )(me, landed, landed, landed, landed, own, into)


def _adamw(w, g, g2, m, v, name):
    shape = w.shape
    if len(shape) == 2:
        shape = (1,) + shape
    nl, rows, cols = shape
    tr = min(256, rows)
    pair = g2 is not None

    def body(*refs):
        if pair:
            w_ref, g_ref, g2_ref, m_ref, v_ref, go_ref, d_ref, nm_ref, nv_ref = refs
            gr = g_ref[...] + g2_ref[...]
        else:
            w_ref, g_ref, m_ref, v_ref, go_ref, d_ref, nm_ref, nv_ref = refs
            gr = g_ref[...]
        go_ref[...] = gr
        nm = ADAM_B1 * m_ref[...] + (1.0 - ADAM_B1) * gr
        nv = ADAM_B2 * v_ref[...] + (1.0 - ADAM_B2) * (gr * gr)
        m_hat = nm / (1.0 - ADAM_B1 ** ADAM_STEP)
        v_hat = nv / (1.0 - ADAM_B2 ** ADAM_STEP)
        d_ref[...] = -ADAM_LR * (m_hat / (jnp.sqrt(v_hat) + ADAM_EPS) + ADAM_WD * w_ref[...])
        nm_ref[...] = nm
        nv_ref[...] = nv

    spec = pl.BlockSpec((1, tr, cols), lambda l, i: (l, i, 0))
    ins = [w, g] + ([g2] if pair else []) + [m, v]
    outs = pl.pallas_call(
        body, name=name, grid=(nl, rows // tr),
        in_specs=[spec] * len(ins), out_specs=[spec] * 4,
        out_shape=[jax.ShapeDtypeStruct(shape, F32)] * 4,
        compiler_params=_cparams(("arbitrary", "arbitrary")),
    )(*[t.reshape(shape) for t in ins])
    return [o.reshape(w.shape) for o in outs]


def _pack_rest(big, l):
    rows = dict(PACK_ROWS)
    return jnp.concatenate([big[name][l].astype(BF16).reshape(rows[name], 1024) for name in REST], axis=0)


def _whole_columns(t):
    return jnp.moveaxis(t, 0, 1).reshape(t.shape[1], N_CHIPS * t.shape[2])


def _branch_weights(rest):
    out, at = [], REST_ROWS - 256
    for _ in range(2):
        out.append(_whole_columns(rest[:, at:at + 128, :].reshape((N_CHIPS,) + SHARD_SHAPES["w_branch_sb"])))
        at += 128
    return out


def _pack_small(d):
    flat = jnp.concatenate([d[n].reshape(-1) for n in SMALL_NAMES])
    return jnp.pad(flat, (0, SMALL_ROWS * LANES - flat.shape[0])).reshape(SMALL_ROWS, LANES)


def _unpack_small(block, like):
    flat, out, at = block.reshape(-1), {}, 0
    for n in SMALL_NAMES:
        size = like[n].size
        out[n] = flat[at:at + size].reshape(like[n].shape)
        at += size
    return out


def _rope_tables(s):
    inv_freq = 1.0 / (ROPE_THETA ** (jnp.arange(0, HEAD_DIM, 2, dtype=F32) / HEAD_DIM))
    ang = jnp.arange(s, dtype=F32)[:, None] * inv_freq[None, :]
    reps = LANES // (HEAD_DIM // 2)
    return jnp.tile(jnp.cos(ang), (1, reps)), jnp.tile(jnp.sin(ang), (1, reps))


def _forward_backward(x, target, big, small):
    s = x.shape[0]
    cos, sin = _rope_tables(s)
    two = lambda gvec: jnp.tile(gvec.reshape(1, HEAD_DIM), (1, 2))
    saved = []
    win_shard = lambda l: big["w_in"][l].astype(BF16)
    win = _whole_columns(_gather_weights(win_shard(0)))
    for l in range(DEPTH):
        gm = small["mix_norm_g"][l].reshape(1, D_MODEL)
        gl = small["mlp_norm_g"][l].reshape(1, D_MODEL)
        qg, kg = two(small["q_norm_g"][l]), two(small["k_norm_g"][l])
        h, sbqkv, raw, qn, kn, v, gates = _inproj_fwd(x, gm, win, qg, kg, cos, sin)
        osb, rest, *nxt = _sb_fwd(sbqkv, [_pack_rest(big, l)] + ([win_shard(l + 1)] if l + 1 < DEPTH else []))
        wsb, wswa = _branch_weights(rest)
        oswa = _swa_fwd(qn, kn, v, small["sinks"][l])
        x1 = _merge_fwd(x, osb, oswa, gates, wsb, wswa, rest)
        x2, u = _mlp_fwd(x1, gl, rest)
        saved.append((x, h, sbqkv, raw, qn, kn, v, gates, osb, oswa, x1, u, gm, gl, qg, kg, win, wsb, wswa, rest))
        x, win = x2, (_whole_columns(nxt[0]) if nxt else None)
    dx, loss = _loss_grad(x, target)

    got = {name: [None] * DEPTH for name in BIG}
    gsmall = {name: [None] * DEPTH for name in SMALL_NAMES}
    late = []
    for l in reversed(range(DEPTH)):
        x0, h, sbqkv, raw, qn, kn, v, gates, osb, oswa, x1, u, gm, gl, qg, kg, win, wsb, wswa, rest = saved[l]
        dx1, du, act, h2, dgl_mlp, dxb = _mlp_bwd(dx, x1, u, gl, rest)
        dosb, doswa, dgate, merged, dysb, dyswa, dx1b = _merge_bwd(dx1, osb, oswa, gates, wsb, wswa, rest)
        ready = [("w_down", l, *_wgrad(act, dxb, "wgrad_down", 512, 1024, shard_axis=0)),
                 ("w_up", l, *_wgrad(h2, du, "wgrad_up", 1024, 512, shard_axis=1)),
                 ("w_out", l, *_wgrad(merged, dx1b, "wgrad_out", 256, 1024, shard_axis=0)),
                 ("w_branch_sb", l, *_wgrad(osb, dysb, "wgrad_bsb", 512, 256, shard_axis=1)),
                 ("w_branch_swa", l, *_wgrad(oswa, dyswa, "wgrad_bswa", 512, 256, shard_axis=1))] + late
        dsbq, dsbk, dsbv, *landed = _sb_bwd(sbqkv, osb, dosb, [narrow for _, _, _, narrow in ready])
        for (name, layer, own, _), arr in zip(ready, landed):
            got[name][layer] = (arr, own)
        dqn, dkn, dswv, dsink = _swa_bwd(qn, kn, v, small["sinks"][l], doswa)
        dswqk, dgqk = _swa_post(raw, dqn, dkn, qg, kg, cos, sin)
        dx, dproj, dg_mix = _inproj_bwd(dsbq, dsbk, dsbv, dswqk, dswv, dgate, x0, dx1, gm, win)
        dwin = _wgrad(h, dproj, "wgrad_in", 1024, 256)
        dwin = dwin.reshape(D_MODEL, N_CHIPS, IN_WIDTH // N_CHIPS).transpose(1, 0, 2)
        late = [("w_in", l, dwin, dwin.astype(BF16))]
        gsmall["mix_norm_g"][l] = dg_mix[0]
        gsmall["mlp_norm_g"][l] = dgl_mlp[0]
        gsmall["q_norm_g"][l] = dgqk[0, :HEAD_DIM] + dgqk[0, HEAD_DIM:]
        gsmall["k_norm_g"][l] = dgqk[1, :HEAD_DIM] + dgqk[1, HEAD_DIM:]
        gsmall["sinks"][l] = jnp.sum(dsink[:, :, 0], axis=0)
    got["w_in"][0] = (_exchange([late[0][3]])[0], late[0][2])
    gsmall = {k: jnp.stack(vs) for k, vs in gsmall.items()}
    return loss, dx, got, gsmall


def kernel(x, mix_norm_g, w_in, q_norm_g, k_norm_g, sinks, w_branch_sb, w_branch_swa, w_out, mlp_norm_g, w_up, w_down, loss_target, m_mix_norm_g, m_w_in, m_q_norm_g, m_k_norm_g, m_sinks, m_w_branch_sb, m_w_branch_swa, m_w_out, m_mlp_norm_g, m_w_up, m_w_down, v_mix_norm_g, v_w_in, v_q_norm_g, v_k_norm_g, v_sinks, v_w_branch_sb, v_w_branch_swa, v_w_out, v_mlp_norm_g, v_w_up, v_w_down):
    big = dict(w_in=w_in, w_branch_sb=w_branch_sb, w_branch_swa=w_branch_swa, w_out=w_out, w_up=w_up, w_down=w_down)
    big_m = dict(w_in=m_w_in, w_branch_sb=m_w_branch_sb, w_branch_swa=m_w_branch_swa, w_out=m_w_out, w_up=m_w_up, w_down=m_w_down)
    big_v = dict(w_in=v_w_in, w_branch_sb=v_w_branch_sb, w_branch_swa=v_w_branch_swa, w_out=v_w_out, w_up=v_w_up, w_down=v_w_down)
    small = dict(mix_norm_g=mix_norm_g, q_norm_g=q_norm_g, k_norm_g=k_norm_g, sinks=sinks, mlp_norm_g=mlp_norm_g)
    small_m = dict(mix_norm_g=m_mix_norm_g, q_norm_g=m_q_norm_g, k_norm_g=m_k_norm_g, sinks=m_sinks, mlp_norm_g=m_mlp_norm_g)
    small_v = dict(mix_norm_g=v_mix_norm_g, q_norm_g=v_q_norm_g, k_norm_g=v_k_norm_g, sinks=v_sinks, mlp_norm_g=v_mlp_norm_g)

    loss_part, grad_x, got, gsmall = _forward_backward(x[0], loss_target[0], big, small)
    loss = lax.psum(loss_part, ("x", "y", "c"))

    mine = []
    for name in BIG:
        tot = lax.empty((DEPTH,) + SHARD_SHAPES[name], F32)
        for l in range(DEPTH):
            tot = _sum_chips(*got[name][l], tot, l, "sum_" + name)
        mine.append(tot)
    theirs = _pair_swap(mine)
    upd = {name: _adamw(big[name], mine[i], theirs[i], big_m[name], big_v[name], "adamw_" + name)
           for i, name in enumerate(BIG)}
    g_small = _allreduce_small(_pack_small(gsmall))
    sm = _adamw(_pack_small(small), g_small, None, _pack_small(small_m), _pack_small(small_v), "adamw_small")
    upd_small = [_unpack_small(t, small) for t in sm]

    names = ("mix_norm_g", "w_in", "q_norm_g", "k_norm_g", "sinks", "w_branch_sb", "w_branch_swa", "w_out",
             "mlp_norm_g", "w_up", "w_down")
    pick = lambda n, i: upd[n][i] if n in upd else upd_small[i][n]
    return (loss, grad_x[None], *[pick(n, 0) for n in names], *[pick(n, 1) for n in names],
            *[pick(n, 2) for n in names], *[pick(n, 3) for n in names])
```

```python
import jax
import jax.numpy as jnp
from jax import lax
from jax.experimental import pallas as pl
from jax.experimental.pallas import tpu as pltpu

F32 = jnp.float32
BF16 = jnp.bfloat16

D_MODEL = 1024
DEPTH = 4
HEAD_DIM = 64
SB_WIDTH = 512
SWA_Q_WIDTH = 512
SWA_KV_WIDTH = 128
D_FF = 4096
IN_WIDTH = 4352
GATE_OFF = 2304
ROPE_THETA = 10000.0
NORM_EPS = 1e-6
SCALE = HEAD_DIM ** -0.5
N_CHIPS = 4
N_DEV = 8

ADAM_LR = 0.001
ADAM_B1 = 0.9
ADAM_B2 = 0.999
ADAM_EPS = 1e-08
ADAM_WD = 0.01
ADAM_STEP = 10

LANES = 128
SB_TILE = 256
SB_STEP_WIDTH = 512
SB_FWD_STEP_WIDTH = 512
SWA_BLOCK = 128
VMEM_LIMIT = 56 << 20

NN = (((1,), (0,)), ((), ()))
NT = (((1,), (1,)), ((), ()))
TN = (((0,), (0,)), ((), ()))
MESH = pl.DeviceIdType.MESH
HBM = pl.BlockSpec(memory_space=pl.ANY)

BIG = ("w_in", "w_branch_sb", "w_branch_swa", "w_out", "w_up", "w_down")
SMALL_NAMES = ("mix_norm_g", "q_norm_g", "k_norm_g", "sinks", "mlp_norm_g")
PACK_ROWS = (("w_in", 1088), ("w_branch_sb", 128), ("w_branch_swa", 128), ("w_out", 256), ("w_up", 1024), ("w_down", 1024))
SHARD_SHAPES = {"w_in": (1024, 1088), "w_branch_sb": (512, 256), "w_branch_swa": (512, 256), "w_out": (256, 1024),
                "w_up": (1024, 1024), "w_down": (1024, 1024)}
ROW_SHARDED = ("w_out", "w_down")
SMALL_ROWS = 72


def _dot(a, b, dims):
    return lax.dot_general(a, b, dims, preferred_element_type=F32)


def _cparams(sem):
    return pltpu.CompilerParams(dimension_semantics=sem, vmem_limit_bytes=VMEM_LIMIT)


def _resident(shape, index=None):
    index = (0,) * len(shape) if index is None else index
    return pl.BlockSpec(shape, lambda *_: index, pipeline_mode=pl.Buffered(1))


REST = ("w_up", "w_down", "w_out", "w_branch_sb", "w_branch_swa")
REST_ROWS = 2560
W_UP_BLOCK = ((N_CHIPS, 1024, 1024), (0, 0, 0))
W_DOWN_BLOCK = ((N_CHIPS, 1024, 1024), (0, 1, 0))
W_OUT_BLOCK = ((N_CHIPS, 256, 1024), (0, 8, 0))


def _rows(tm, width):
    return pl.BlockSpec((tm, width), lambda i: (i, 0))


def _rms(xf):
    rstd = lax.rsqrt(jnp.mean(xf * xf, axis=-1, keepdims=True) + NORM_EPS)
    return xf * rstd, rstd


def _rms_bwd(dh, xn, rstd, g):
    dxn = dh * g
    return rstd * (dxn - xn * jnp.mean(dxn * xn, axis=-1, keepdims=True))


def _lane(shape):
    return lax.broadcasted_iota(jnp.int32, shape, len(shape) - 1)


def _head_mean(v, left):
    sl = jnp.sum(jnp.where(left, v, 0.0), axis=-1, keepdims=True)
    sr = jnp.sum(jnp.where(left, 0.0, v), axis=-1, keepdims=True)
    return jnp.where(left, sl, sr) * (1.0 / HEAD_DIM)


def _rope(y, cs, sn, first):
    up = pltpu.roll(y, 96, 1)
    dn = pltpu.roll(y, 32, 1)
    return y * cs + jnp.where(first, -up, dn) * sn


def _rope_t(d, cs, sn, first):
    t = d * jnp.where(first, -sn, sn)
    return d * cs + jnp.where(first, pltpu.roll(t, 96, 1), pltpu.roll(t, 32, 1))


def _norm_rope(p, g, cs, sn):
    lane = _lane((1, LANES))
    left = lane < HEAD_DIM
    first = (lane % HEAD_DIM) < (HEAD_DIM // 2)
    yn = p * lax.rsqrt(_head_mean(p * p, left) + NORM_EPS)
    return _rope(yn * g, cs, sn, first)


def _norm_rope_bwd(p, dout, g, cs, sn):
    lane = _lane((1, LANES))
    left = lane < HEAD_DIM
    first = (lane % HEAD_DIM) < (HEAD_DIM // 2)
    rstd = lax.rsqrt(_head_mean(p * p, left) + NORM_EPS)
    yn = p * rstd
    dyg = _rope_t(dout, cs, sn, first)
    dg = jnp.sum(dyg * yn, axis=0, keepdims=True)
    dyn = dyg * g
    dp = rstd * (dyn - yn * _head_mean(dyn * yn, left))
    return dp, dg


def _place():
    x, y, c = lax.axis_index("x"), lax.axis_index("y"), lax.axis_index("c")
    return x, y, c, [(1 - x, y), (x, 1 - y), (1 - x, 1 - y)]


def _hosted_gather(src, out, ssem, rsem, lsem, first, mid, last, slot=0):
    x, y, c, chips = _place()
    me = 2 * x + y
    sib = (x, y, 1 - c)
    r2 = src.shape[0] // 2
    base = 6 * slot

    def slab(chip, half):
        return out.at[chip, pl.ds(half * r2, r2)]

    def ici(j):
        cx, cy = chips[j]
        return pltpu.make_async_remote_copy(src.at[pl.ds(c * r2, r2)], slab(me, c), ssem.at[base + j], rsem.at[base + j],
                                            device_id=(cx, cy, c), device_id_type=MESH)

    def landed(j):
        got = slab(2 * chips[j][0] + chips[j][1], c)
        return pltpu.make_async_remote_copy(got, got, ssem.at[base + j], rsem.at[base + j], device_id=sib, device_id_type=MESH)

    def d2d(j, half):
        got = slab(2 * chips[j][0] + chips[j][1], half)
        return pltpu.make_async_remote_copy(got, got, ssem.at[base + 3 + j], rsem.at[base + 3 + j], device_id=sib,
                                            device_id_type=MESH)

    local = pltpu.make_async_copy(src, out.at[me], lsem.at[slot])

    @pl.when(first)
    def _():
        local.start()
        for j in range(3):
            ici(j).start()

    @pl.when(mid)
    def _():
        for j in range(3):
            landed(j).wait_recv()
            d2d(j, c).start()

    @pl.when(last)
    def _():
        for j in range(3):
            d2d(j, 1 - c).wait_recv()
        for j in range(3):
            ici(j).wait_send()
            d2d(j, c).wait_send()
        local.wait()


def _hosted_exchange(srcs, outs, ssem, rsem, first, last):
    x, y, c, chips = _place()
    me = 2 * x + y

    def send(t, j):
        cx, cy = chips[j]
        return pltpu.make_async_remote_copy(srcs[t].at[2 * cx + cy], outs[t].at[me], ssem.at[3 * t + j], rsem.at[3 * t + j],
                                            device_id=(cx, cy, c), device_id_type=MESH)

    def landed(t, j):
        cx, cy = chips[j]
        got = outs[t].at[2 * cx + cy]
        return pltpu.make_async_remote_copy(got, got, ssem.at[3 * t + j], rsem.at[3 * t + j],
                                            device_id=(cx, cy, c), device_id_type=MESH)

    @pl.when(first)
    def _():
        for t in range(len(srcs)):
            for j in range(3):
                send(t, j).start()

    @pl.when(last)
    def _():
        for t in range(len(srcs)):
            for j in range(3):
                landed(t, j).wait_recv()
        for t in range(len(srcs)):
            for j in range(3):
                send(t, j).wait_send()


def _gather_scratch(n):
    return [pltpu.SemaphoreType.DMA((6 * n,)), pltpu.SemaphoreType.DMA((6 * n,)), pltpu.SemaphoreType.DMA((n,))]


def _exchange_scratch(n):
    return [pltpu.SemaphoreType.DMA((3 * n,)), pltpu.SemaphoreType.DMA((3 * n,))]


def _inproj_fwd(x, g, w, qg, kg, cos, sin):
    s = x.shape[0]
    tm = min(512, s)

    def body(x_ref, g_ref, w_ref, qg_ref, kg_ref, cos_ref, sin_ref,
             h_ref, sb_ref, raw_ref, qn_ref, kn_ref, v_ref, gate_ref):
        xn, _ = _rms(x_ref[...])
        h = (xn * g_ref[...]).astype(BF16)
        h_ref[...] = h
        for a in range(0, 3 * SB_WIDTH, 512):
            sb_ref[:, a:a + 512] = _dot(h, w_ref[:, a:a + 512], NN).astype(BF16)
        cs, sn = cos_ref[...], sin_ref[...]
        q0 = 3 * SB_WIDTH
        pq = _dot(h, w_ref[:, q0:q0 + SWA_Q_WIDTH], NN)
        raw_ref[:, 0:SWA_Q_WIDTH] = pq
        for b in range(SWA_Q_WIDTH // LANES):
            blk = pq[:, b * LANES:(b + 1) * LANES]
            qn_ref[:, b * LANES:(b + 1) * LANES] = _norm_rope(blk, qg_ref[...], cs, sn).astype(BF16)
        k0 = q0 + SWA_Q_WIDTH
        pk = _dot(h, w_ref[:, k0:k0 + 2 * SWA_KV_WIDTH], NN)
        raw_ref[:, SWA_Q_WIDTH:SWA_Q_WIDTH + SWA_KV_WIDTH] = pk[:, :SWA_KV_WIDTH]
        kn_ref[...] = _norm_rope(pk[:, :SWA_KV_WIDTH], kg_ref[...], cs, sn).astype(BF16)
        v_ref[...] = pk[:, SWA_KV_WIDTH:].astype(BF16)
        for a in range(GATE_OFF, IN_WIDTH, 512):
            gate_ref[:, a - GATE_OFF:a - GATE_OFF + 512] = jax.nn.sigmoid(_dot(h, w_ref[:, a:a + 512], NN))

    return pl.pallas_call(
        body, name="inproj_fwd", grid=(s // tm,),
        in_specs=[_rows(tm, D_MODEL), _resident((1, D_MODEL)), _resident((D_MODEL, IN_WIDTH)),
                  _resident((1, LANES)), _resident((1, LANES)), _rows(tm, LANES), _rows(tm, LANES)],
        out_specs=[_rows(tm, D_MODEL), _rows(tm, 3 * SB_WIDTH), _rows(tm, 640), _rows(tm, SWA_Q_WIDTH),
                   _rows(tm, SWA_KV_WIDTH), _rows(tm, SWA_KV_WIDTH), _rows(tm, 2 * D_MODEL)],
        out_shape=[jax.ShapeDtypeStruct((s, D_MODEL), BF16), jax.ShapeDtypeStruct((s, 3 * SB_WIDTH), BF16),
                   jax.ShapeDtypeStruct((s, 640), F32), jax.ShapeDtypeStruct((s, SWA_Q_WIDTH), BF16),
                   jax.ShapeDtypeStruct((s, SWA_KV_WIDTH), BF16), jax.ShapeDtypeStruct((s, SWA_KV_WIDTH), BF16),
                   jax.ShapeDtypeStruct((s, 2 * D_MODEL), F32)],
        compiler_params=_cparams(("arbitrary",)),
    )(x, g, w, qg, kg, cos, sin)


def _sb_tile_terms(qh, k, strict):
    z = _dot(qh, k, NT)
    nz = -z
    soft = jnp.log(1.0 + jnp.exp(jnp.minimum(z, nz)))
    log_keep = jnp.minimum(nz, 0.0) - soft
    if strict is not None:
        log_keep = jnp.where(strict, log_keep, 0.0)
    return z, log_keep


def _log_weights(z, log_keep, run, u_from):
    tail = run + _dot(log_keep.astype(BF16), u_from, NN)
    return z + tail, tail[:, 0:1]


def _masked(strict, val):
    return val if strict is None else jnp.where(strict, val, 0.0)


def _tri_twice(t):
    r = lax.broadcasted_iota(jnp.int32, (2 * t, t), 0)
    c = lax.broadcasted_iota(jnp.int32, (2 * t, t), 1)
    return (jnp.where(r >= t, r - t, r) >= c).astype(BF16)


def _split_dot(a, u2):
    hi = a.astype(BF16)
    lo = (a - hi.astype(F32)).astype(BF16)
    return _dot(jnp.concatenate([hi, lo], axis=1), u2, NN)


def _sb_fwd(sbqkv, shards=()):
    s = sbqkv.shape[0]
    t = min(SB_TILE, s)
    nq = s // t
    ng = SB_WIDTH // SB_FWD_STEP_WIDTH
    heads = range(2 * SB_FWD_STEP_WIDTH // LANES)
    nsh = len(shards)

    def body(*refs):
        q_ref, k_ref, v_ref = refs[:3]
        o_ref = refs[3 + nsh]
        j, i = pl.program_id(0), pl.program_id(1)
        for n in range(nsh):
            _hosted_gather(refs[3 + n], refs[4 + nsh + n], *refs[4 + 2 * nsh:], (j == 0) & (i == 0),
                           (j == ng - 1) & (i == (3 * nq) // 4), (j == ng - 1) & (i == nq - 1), slot=n)
        left = _lane((1, LANES)) < HEAD_DIM
        halves = (left, jnp.logical_not(left))
        pair = lambda a, p: a[:, (p // 2) * LANES:(p // 2 + 1) * LANES]
        only = lambda a, p: jnp.where(halves[p % 2], pair(a, p), jnp.zeros((), a.dtype))
        q_all = q_ref[...] * jnp.asarray(SCALE, BF16)
        qh = [only(q_all, p) for p in heads]
        r = lax.broadcasted_iota(jnp.int32, (t, t), 0)
        c = lax.broadcasted_iota(jnp.int32, (t, t), 1)
        u_from = (r >= c).astype(BF16)

        def tile(n, carry, strict):
            runs, accs = carry
            off = pl.multiple_of((i - n) * t, t)
            k = k_ref[pl.ds(off, t), :]
            v = v_ref[pl.ds(off, t), :]
            terms = [_sb_tile_terms(qh[p], pair(k, p), strict) for p in heads]
            logw = [_log_weights(*terms[p], runs[p], u_from) for p in heads]
            ws = [_masked(strict, jnp.exp(logw[p][0])).astype(BF16) for p in heads]
            accs = list(accs)
            for p in heads:
                accs[p // 2] = accs[p // 2] + _dot(ws[p], only(v, p), NN)
            return tuple(logw[p][1] for p in heads), tuple(accs)

        zero = jnp.zeros((t, 1), F32)
        wide = jnp.zeros((t, LANES), F32)
        carry = tile(0, (tuple(zero for _ in heads), tuple(wide for _ in heads[::2])), c < r)
        _, accs = lax.fori_loop(1, i + 1, lambda n, cr: tile(n, cr, None), carry)
        for g, acc in enumerate(accs):
            o_ref[:, g * LANES:(g + 1) * LANES] = acc

    w = SB_FWD_STEP_WIDTH
    blk = pl.BlockSpec((t, w), lambda j, i: (i, j))
    in_specs = [blk, pl.BlockSpec((s, w), lambda j, i: (0, ng + j)), pl.BlockSpec((s, w), lambda j, i: (0, 2 * ng + j))]
    out_specs = [blk]
    out_shape = [jax.ShapeDtypeStruct((s, SB_WIDTH), F32)]
    in_specs += [HBM] * nsh
    out_specs += [HBM] * nsh
    out_shape += [jax.ShapeDtypeStruct((N_CHIPS,) + a.shape, a.dtype) for a in shards]
    return pl.pallas_call(
        body, name="sb_fwd_g%d" % nsh, grid=(ng, nq),
        in_specs=in_specs, out_specs=out_specs, out_shape=out_shape,
        scratch_shapes=_gather_scratch(nsh) if nsh else [],
        compiler_params=_cparams(("arbitrary", "arbitrary")),
    )(sbqkv, sbqkv, sbqkv, *shards)


def _swa_band(cur_ref, prev_ref, b):
    lo = prev_ref[...] if b == 0 else cur_ref[(b - 1) * SWA_BLOCK:b * SWA_BLOCK, :]
    return jnp.concatenate([lo, cur_ref[b * SWA_BLOCK:(b + 1) * SWA_BLOCK, :]], axis=0)


def _swa_variants(band, left):
    f = band.astype(F32)
    sw = pltpu.roll(f, HEAD_DIM, 1)
    halves = (left, jnp.logical_not(left))
    return [[jnp.where(halves[p], f if p == g else sw, 0.0).astype(BF16) for p in range(2)] for g in range(2)]


def _swa_valid(blk):
    ii = lax.broadcasted_iota(jnp.int32, (SWA_BLOCK, 2 * SWA_BLOCK), 0)
    jj = lax.broadcasted_iota(jnp.int32, (SWA_BLOCK, 2 * SWA_BLOCK), 1)
    rel = jj - SWA_BLOCK - ii
    return (rel <= 0) & (rel > -SWA_BLOCK) & (jj + blk * SWA_BLOCK >= SWA_BLOCK)


def _swa_softmax(dots, sink, valid):
    sc = jnp.where(valid, dots * SCALE, -1e30)
    m = jnp.maximum(jnp.max(sc, axis=-1, keepdims=True), sink)
    e = jnp.exp(sc - m)
    es = jnp.exp(sink - m)
    inv = 1.0 / (jnp.sum(e, axis=-1, keepdims=True) + es)
    return e * inv, es * inv


def _swa_heads(q_ref, rows, halves):
    nheads = SWA_Q_WIDTH // HEAD_DIM
    out = []
    for h in range(nheads):
        hb, p = h // 2, h % 2
        q2 = q_ref[rows, hb * LANES:(hb + 1) * LANES]
        out.append((hb, p, h // (nheads // 2), jnp.where(halves[p], q2, jnp.zeros_like(q2))))
    return out


def _swa_specs(s, t):
    nb = t // SWA_BLOCK
    cur = pl.BlockSpec((t, LANES), lambda i: (i, 0))
    prev = pl.BlockSpec((SWA_BLOCK, LANES), lambda i: (jnp.maximum(i * nb - 1, 0), 0))
    return cur, prev


def _swa_fwd(qn, kn, v, sinks):
    s = qn.shape[0]
    t = min(512, s)
    nb = t // SWA_BLOCK
    nheads = SWA_Q_WIDTH // HEAD_DIM

    def body(sink_ref, q_ref, kc_ref, kp_ref, vc_ref, vp_ref, o_ref):
        i = pl.program_id(0)
        left = _lane((1, LANES)) < HEAD_DIM
        halves = (left, jnp.logical_not(left))
        for b in range(nb):
            kvar = _swa_variants(_swa_band(kc_ref, kp_ref, b), left)
            vvar = _swa_variants(_swa_band(vc_ref, vp_ref, b), left)
            rows = slice(b * SWA_BLOCK, (b + 1) * SWA_BLOCK)
            valid = _swa_valid(i * nb + b)
            heads = _swa_heads(q_ref, rows, halves)
            dots = [_dot(qh, kvar[g][p], NT) for _, p, g, qh in heads]
            probs = [_swa_softmax(dots[h], sink_ref[h], valid)[0].astype(BF16) for h in range(nheads)]
            outs = [_dot(probs[h], vvar[g][p], NN) for h, (_, p, g, _) in enumerate(heads)]
            for hb in range(nheads // 2):
                o_ref[rows, hb * LANES:(hb + 1) * LANES] = (outs[2 * hb] + outs[2 * hb + 1]).astype(BF16)

    cur, prev = _swa_specs(s, t)
    return pl.pallas_call(
        body, name="swa_fwd", grid=(s // t,),
        in_specs=[pl.BlockSpec(memory_space=pltpu.SMEM), _rows(t, SWA_Q_WIDTH), cur, prev, cur, prev],
        out_specs=_rows(t, SWA_Q_WIDTH),
        out_shape=jax.ShapeDtypeStruct((s, SWA_Q_WIDTH), BF16),
        compiler_params=_cparams(("arbitrary",)),
    )(sinks, qn, kn, kn, v, v)


def _merge_fwd(x, osb, oswa, gates, wsb, wswa, rest):
    s = x.shape[0]
    tm = min(512, s)
    rc = D_MODEL // N_CHIPS

    def body(x_ref, osb_ref, oswa_ref, gate_ref, wsb_ref, wswa_ref, wout_ref, x1_ref):
        ysb = _dot(osb_ref[...].astype(BF16), wsb_ref[...], NN)
        yswa = _dot(oswa_ref[...], wswa_ref[...], NN)
        merged = (gate_ref[:, :D_MODEL] * ysb + gate_ref[:, D_MODEL:] * yswa).astype(BF16)
        acc = x_ref[...]
        for q in range(N_CHIPS):
            acc = acc + _dot(merged[:, q * rc:(q + 1) * rc], wout_ref[q], NN)
        x1_ref[...] = acc

    return pl.pallas_call(
        body, name="merge_fwd", grid=(s // tm,),
        in_specs=[_rows(tm, D_MODEL), _rows(tm, SB_WIDTH), _rows(tm, SWA_Q_WIDTH), _rows(tm, 2 * D_MODEL),
                  _resident((SB_WIDTH, D_MODEL)), _resident((SWA_Q_WIDTH, D_MODEL)), _resident(*W_OUT_BLOCK)],
        out_specs=_rows(tm, D_MODEL),
        out_shape=jax.ShapeDtypeStruct((s, D_MODEL), F32),
        compiler_params=_cparams(("arbitrary",)),
    )(x, osb, oswa, gates, wsb, wswa, rest)


def _mlp_fwd(x1, g, rest):
    s = x1.shape[0]
    tm = min(512, s)
    fc = D_FF // N_CHIPS

    def body(x_ref, g_ref, wup_ref, wdown_ref, x2_ref, u_ref):
        xf = x_ref[...]
        xn, _ = _rms(xf)
        h2 = (xn * g_ref[...]).astype(BF16)
        acc = xf
        for q in range(N_CHIPS):
            u = _dot(h2, wup_ref[q], NN)
            u_ref[:, q * fc:(q + 1) * fc] = u
            r = jnp.maximum(u, 0.0)
            acc = acc + _dot((r * r).astype(BF16), wdown_ref[q], NN)
        x2_ref[...] = acc

    return pl.pallas_call(
        body, name="mlp_fwd", grid=(s // tm,),
        in_specs=[_rows(tm, D_MODEL), _resident((1, D_MODEL)), _resident(*W_UP_BLOCK), _resident(*W_DOWN_BLOCK)],
        out_specs=[_rows(tm, D_MODEL), _rows(tm, D_FF)],
        out_shape=[jax.ShapeDtypeStruct((s, D_MODEL), F32), jax.ShapeDtypeStruct((s, D_FF), F32)],
        compiler_params=_cparams(("arbitrary",)),
    )(x1, g, rest, rest)


def _loss_grad(y, target):
    s = y.shape[0]
    tm = min(512, s)

    def body(y_ref, t_ref, dy_ref, part_ref):
        err = y_ref[...] - t_ref[...]
        dy_ref[...] = err * (1.0 / D_MODEL)
        tot = jnp.sum(jnp.sum(err * err, axis=-1, keepdims=True), axis=0, keepdims=True)
        part_ref[...] = jnp.broadcast_to(tot.reshape(1, 1, 1), (1, 8, LANES))

    dy, part = pl.pallas_call(
        body, name="loss_grad", grid=(s // tm,),
        in_specs=[_rows(tm, D_MODEL), _rows(tm, D_MODEL)],
        out_specs=[_rows(tm, D_MODEL), pl.BlockSpec((1, 8, LANES), lambda i: (i, 0, 0))],
        out_shape=[jax.ShapeDtypeStruct((s, D_MODEL), F32), jax.ShapeDtypeStruct((s // tm, 8, LANES), F32)],
        compiler_params=_cparams(("arbitrary",)),
    )(y, target)
    return dy, (0.5 / D_MODEL) * jnp.sum(part[:, 0, 0])


def _mlp_bwd(dx2, x1, u, g, rest):
    s = x1.shape[0]
    tm = min(256, s)
    fc = D_FF // N_CHIPS

    def body(dx2_ref, x_ref, u_ref, g_ref, wup_ref, wdown_ref, dx1_ref, du_ref, a_ref, h2_ref, dg_ref, dxb_ref):
        @pl.when(pl.program_id(0) == 0)
        def _():
            dg_ref[...] = jnp.zeros_like(dg_ref)

        gam = g_ref[...]
        xn, rstd = _rms(x_ref[...])
        h2_ref[...] = (xn * gam).astype(BF16)
        dxf = dx2_ref[...]
        dxb = dxf.astype(BF16)
        dxb_ref[...] = dxb
        dh2 = jnp.zeros((tm, D_MODEL), F32)
        for q in range(N_CHIPS):
            cols = slice(q * fc, (q + 1) * fc)
            da = _dot(dxb, wdown_ref[q], NT)
            r = jnp.maximum(u_ref[:, cols], 0.0)
            a_ref[:, cols] = (r * r).astype(BF16)
            du = (da * (2.0 * r)).astype(BF16)
            du_ref[:, cols] = du
            dh2 = dh2 + _dot(du, wup_ref[q], NT)
        dg_ref[...] += jnp.sum(dh2 * xn, axis=0, keepdims=True)
        dx1_ref[...] = dxf + _rms_bwd(dh2, xn, rstd, gam)

    return pl.pallas_call(
        body, name="mlp_bwd", grid=(s // tm,),
        in_specs=[_rows(tm, D_MODEL), _rows(tm, D_MODEL), _rows(tm, D_FF), _resident((1, D_MODEL)),
                  _resident(*W_UP_BLOCK), _resident(*W_DOWN_BLOCK)],
        out_specs=[_rows(tm, D_MODEL), _rows(tm, D_FF), _rows(tm, D_FF), _rows(tm, D_MODEL),
                   pl.BlockSpec((1, D_MODEL), lambda i: (0, 0)), _rows(tm, D_MODEL)],
        out_shape=[jax.ShapeDtypeStruct((s, D_MODEL), F32), jax.ShapeDtypeStruct((s, D_FF), BF16),
                   jax.ShapeDtypeStruct((s, D_FF), BF16), jax.ShapeDtypeStruct((s, D_MODEL), BF16),
                   jax.ShapeDtypeStruct((1, D_MODEL), F32), jax.ShapeDtypeStruct((s, D_MODEL), BF16)],
        compiler_params=_cparams(("arbitrary",)),
    )(dx2, x1, u, g, rest, rest)


def _wgrad(a, b, name, tm, tn, shard_axis=None):
    s, m = a.shape
    n = b.shape[1]

    def body(a_ref, b_ref, o_ref, *narrow):
        res = _dot(a_ref[...].astype(BF16), b_ref[...].astype(BF16), TN)
        o_ref[...] = res.reshape(o_ref.shape)
        for n_ref in narrow:
            n_ref[...] = res.astype(BF16).reshape(n_ref.shape)

    if shard_axis is None:
        out_shape, out_spec = (m, n), pl.BlockSpec((tm, tn), lambda i, j: (i, j))
    elif shard_axis == 0:
        per = m // N_CHIPS // tm
        out_shape, out_spec = (N_CHIPS, m // N_CHIPS, n), pl.BlockSpec((1, tm, tn), lambda i, j: (i // per, i % per, j))
    else:
        per = n // N_CHIPS // tn
        out_shape, out_spec = (N_CHIPS, m, n // N_CHIPS), pl.BlockSpec((1, tm, tn), lambda i, j: (j // per, i, j % per))
    both = shard_axis is not None
    a_spec = _resident((s, tm)) if tm == m else pl.BlockSpec((s, tm), lambda i, j: (0, i))
    b_spec = _resident((s, tn)) if tn == n else pl.BlockSpec((s, tn), lambda i, j: (0, j))
    return pl.pallas_call(
        body, name=name, grid=(m // tm, n // tn),
        in_specs=[a_spec, b_spec],
        out_specs=[out_spec, out_spec] if both else out_spec,
        out_shape=[jax.ShapeDtypeStruct(out_shape, F32), jax.ShapeDtypeStruct(out_shape, BF16)] if both
        else jax.ShapeDtypeStruct(out_shape, F32),
        compiler_params=_cparams(("arbitrary", "arbitrary")),
    )(a, b)


def _merge_bwd(dx1, osb, oswa, gates, wsb, wswa, rest):
    s = dx1.shape[0]
    tm = min(512, s)

    def body(dx_ref, osb_ref, oswa_ref, gate_ref, wsb_ref, wswa_ref, wout_ref,
             dosb_ref, doswa_ref, dgl_ref, merged_ref, dysb_ref, dyswa_ref, dxb_ref):
        dxb = dx_ref[...].astype(BF16)
        dxb_ref[...] = dxb
        dm = jnp.concatenate([_dot(dxb, wout_ref[q], NT) for q in range(N_CHIPS)], axis=1)
        ysb = _dot(osb_ref[...].astype(BF16), wsb_ref[...], NN)
        yswa = _dot(oswa_ref[...], wswa_ref[...], NN)
        g0 = gate_ref[:, :D_MODEL]
        g1 = gate_ref[:, D_MODEL:]
        merged_ref[...] = (g0 * ysb + g1 * yswa).astype(BF16)
        dgl_ref[:, :D_MODEL] = (dm * ysb * (g0 * (1.0 - g0))).astype(BF16)
        dgl_ref[:, D_MODEL:] = (dm * yswa * (g1 * (1.0 - g1))).astype(BF16)
        dysb = (dm * g0).astype(BF16)
        dyswa = (dm * g1).astype(BF16)
        dysb_ref[...] = dysb
        dyswa_ref[...] = dyswa
        dosb_ref[...] = _dot(dysb, wsb_ref[...], NT)
        doswa_ref[...] = _dot(dyswa, wswa_ref[...], NT)

    return pl.pallas_call(
        body, name="merge_bwd", grid=(s // tm,),
        in_specs=[_rows(tm, D_MODEL), _rows(tm, SB_WIDTH), _rows(tm, SWA_Q_WIDTH), _rows(tm, 2 * D_MODEL),
                  _resident((SB_WIDTH, D_MODEL)), _resident((SWA_Q_WIDTH, D_MODEL)), _resident(*W_OUT_BLOCK)],
        out_specs=[_rows(tm, SB_WIDTH), _rows(tm, SWA_Q_WIDTH), _rows(tm, 2 * D_MODEL), _rows(tm, D_MODEL),
                   _rows(tm, D_MODEL), _rows(tm, D_MODEL), _rows(tm, D_MODEL)],
        out_shape=[jax.ShapeDtypeStruct((s, SB_WIDTH), F32), jax.ShapeDtypeStruct((s, SWA_Q_WIDTH), F32),
                   jax.ShapeDtypeStruct((s, 2 * D_MODEL), BF16), jax.ShapeDtypeStruct((s, D_MODEL), BF16),
                   jax.ShapeDtypeStruct((s, D_MODEL), BF16), jax.ShapeDtypeStruct((s, D_MODEL), BF16),
                   jax.ShapeDtypeStruct((s, D_MODEL), BF16)],
        compiler_params=_cparams(("arbitrary",)),
    )(dx1, osb, oswa, gates, wsb, wswa, rest)


def _sb_bwd(sbqkv, osb, dosb, send):
    s = sbqkv.shape[0]
    t = min(SB_TILE, s)
    nq = s // t
    ng = SB_WIDTH // SB_STEP_WIDTH
    heads = range(2 * SB_STEP_WIDTH // LANES)
    ns = len(send)

    def body(*refs):
        q_ref, k_ref, v_ref, o_ref, do_ref = refs[:5]
        srcs = refs[5:5 + ns]
        dq_ref, dk_ref, dv_ref = refs[5 + ns:8 + ns]
        outs = refs[8 + ns:8 + 2 * ns]
        j, i = pl.program_id(0), pl.program_id(1)
        if ns:
            ssem, rsem = refs[8 + 2 * ns:]
            _hosted_exchange(srcs, outs, ssem, rsem, (j == 0) & (i == 0), (j == ng - 1) & (i == nq - 1))

        @pl.when(i == 0)
        def _():
            dk_ref[...] = jnp.zeros_like(dk_ref)
            dv_ref[...] = jnp.zeros_like(dv_ref)

        left = _lane((1, LANES)) < HEAD_DIM
        halves = (left, jnp.logical_not(left))
        pair = lambda a, p: a[:, (p // 2) * LANES:(p // 2 + 1) * LANES]
        only = lambda a, p: jnp.where(halves[p % 2], pair(a, p), jnp.zeros((), a.dtype))
        scale = jnp.asarray(SCALE, BF16)
        q_all = q_ref[...] * scale
        do_all = do_ref[...].astype(BF16)
        prod = do_all.astype(F32) * o_ref[...]
        qh = [only(q_all, p) for p in heads]
        doh = [only(do_all, p) for p in heads]
        delta = [jnp.sum(only(prod, p), axis=-1, keepdims=True) for p in heads]
        r = lax.broadcasted_iota(jnp.int32, (t, t), 0)
        c = lax.broadcasted_iota(jnp.int32, (t, t), 1)
        u_from = (r >= c).astype(BF16)
        u_from2 = _tri_twice(t)

        def tile(n, carry, strict):
            runs, dqs = carry
            off = pl.multiple_of((i - n) * t, t)
            k = k_ref[pl.ds(off, t), :]
            v = v_ref[pl.ds(off, t), :]
            ks = k * scale
            terms = [_sb_tile_terms(qh[p], pair(k, p), strict) for p in heads]
            logw = [_log_weights(*terms[p], runs[p][0], u_from) for p in heads]
            dws = [_dot(doh[p], pair(v, p), NT) for p in heads]
            wbs = [_masked(strict, jnp.exp(logw[p][0])).astype(BF16) for p in heads]
            es = [dws[p] * wbs[p].astype(F32) for p in heads]
            rests = [runs[p][1] + _split_dot(es[p], u_from2) for p in heads]
            betas = [jnp.exp(terms[p][0] + terms[p][1]) for p in heads]
            dzs = [_masked(strict, es[p] - betas[p] * (es[p] - rests[p])).astype(BF16) for p in heads]
            dqs = list(dqs)
            for g in range(len(heads) // 2):
                a, b = 2 * g, 2 * g + 1
                cols = slice(g * LANES, (g + 1) * LANES)
                dv_ref[pl.ds(off, t), cols] += _dot(wbs[a], doh[a], TN) + _dot(wbs[b], doh[b], TN)
                dk_ref[pl.ds(off, t), cols] += _dot(dzs[a], qh[a], TN) + _dot(dzs[b], qh[b], TN)
                dqs[g] = dqs[g] + _dot(dzs[a], only(ks, a), NN) + _dot(dzs[b], only(ks, b), NN)
            new_runs = tuple((logw[p][1], rests[p][:, 0:1]) for p in heads)
            return new_runs, tuple(dqs)

        zero = jnp.zeros((t, 1), F32)
        wide = jnp.zeros((t, LANES), F32)
        carry = tile(0, (tuple((zero, -delta[p]) for p in heads), tuple(wide for _ in heads[::2])), c < r)
        _, dqs = lax.fori_loop(1, i + 1, lambda n, cr: tile(n, cr, None), carry)
        for g, dq in enumerate(dqs):
            dq_ref[:, g * LANES:(g + 1) * LANES] = dq.astype(BF16)

    w = SB_STEP_WIDTH
    blk = pl.BlockSpec((t, w), lambda j, i: (i, j))
    whole = pl.BlockSpec((s, w), lambda j, i: (0, j))
    return pl.pallas_call(
        body, name="sb_bwd_x%d" % ns, grid=(ng, nq),
        in_specs=[blk, pl.BlockSpec((s, w), lambda j, i: (0, ng + j)),
                  pl.BlockSpec((s, w), lambda j, i: (0, 2 * ng + j)), blk, blk] + [HBM] * ns,
        out_specs=[blk, whole, whole] + [HBM] * ns,
        out_shape=[jax.ShapeDtypeStruct((s, SB_WIDTH), BF16), jax.ShapeDtypeStruct((s, SB_WIDTH), F32),
                   jax.ShapeDtypeStruct((s, SB_WIDTH), F32)] + [jax.ShapeDtypeStruct(a.shape, a.dtype) for a in send],
        scratch_shapes=_exchange_scratch(ns) if ns else [],
        compiler_params=_cparams(("arbitrary", "arbitrary")),
    )(sbqkv, sbqkv, sbqkv, osb, dosb, *send)


def _swa_bwd(qn, kn, v, sinks, do):
    s = qn.shape[0]
    t = min(512, s)
    nb = t // SWA_BLOCK
    nheads = SWA_Q_WIDTH // HEAD_DIM

    def body(sink_ref, q_ref, kc_ref, kp_ref, vc_ref, vp_ref, do_ref, dq_ref, dk_ref, dv_ref, dsink_ref):
        i = pl.program_id(0)

        @pl.when(i == 0)
        def _():
            dk_ref[...] = jnp.zeros_like(dk_ref)
            dv_ref[...] = jnp.zeros_like(dv_ref)

        left = _lane((1, LANES)) < HEAD_DIM
        halves = (left, jnp.logical_not(left))
        dsink = [jnp.zeros((1, 1), F32) for _ in range(nheads)]
        for b in range(nb):
            kvar = _swa_variants(_swa_band(kc_ref, kp_ref, b), left)
            vvar = _swa_variants(_swa_band(vc_ref, vp_ref, b), left)
            rows = slice(b * SWA_BLOCK, (b + 1) * SWA_BLOCK)
            valid = _swa_valid(i * nb + b)
            heads = _swa_heads(q_ref, rows, halves)
            doh = []
            for hb, p, _, _ in heads:
                do2 = do_ref[rows, hb * LANES:(hb + 1) * LANES]
                doh.append(jnp.where(halves[p], do2, 0.0).astype(BF16))
            dots = [_dot(qh, kvar[g][p], NT) for _, p, g, qh in heads]
            dps = [_dot(doh[h], vvar[g][p], NT) for h, (_, p, g, _) in enumerate(heads)]
            dss, pbs = [], []
            for h in range(nheads):
                probs, psink = _swa_softmax(dots[h], sink_ref[h], valid)
                delta = jnp.sum(probs * dps[h], axis=-1, keepdims=True)
                dss.append((probs * (dps[h] - delta) * SCALE).astype(BF16))
                pbs.append(probs.astype(BF16))
                dsink[h] = dsink[h] - jnp.sum(psink * delta, axis=0, keepdims=True)
            dk_acc = [jnp.zeros((2 * SWA_BLOCK, LANES), F32) for _ in range(2)]
            dv_acc = [jnp.zeros((2 * SWA_BLOCK, LANES), F32) for _ in range(2)]
            dqs = [_dot(dss[h], kvar[g][p], NN) for h, (_, p, g, _) in enumerate(heads)]
            for h, (_, p, g, qh) in enumerate(heads):
                which = 0 if p == g else 1
                dk_acc[which] = dk_acc[which] + _dot(dss[h], qh, TN)
                dv_acc[which] = dv_acc[which] + _dot(pbs[h], doh[h], TN)
            for hb in range(nheads // 2):
                dq_ref[rows, hb * LANES:(hb + 1) * LANES] = dqs[2 * hb] + dqs[2 * hb + 1]
            dkb = dk_acc[0] + pltpu.roll(dk_acc[1], HEAD_DIM, 1)
            dvb = dv_acc[0] + pltpu.roll(dv_acc[1], HEAD_DIM, 1)
            start = pl.multiple_of((i * nb + b) * SWA_BLOCK, SWA_BLOCK)
            dk_ref[pl.ds(start, SWA_BLOCK), :] += dkb[SWA_BLOCK:]
            dv_ref[pl.ds(start, SWA_BLOCK), :] += dvb[SWA_BLOCK:]

            @pl.when(i * nb + b > 0)
            def _(dkb=dkb, dvb=dvb, start=start):
                before = pl.multiple_of(jnp.maximum(start - SWA_BLOCK, 0), SWA_BLOCK)
                dk_ref[pl.ds(before, SWA_BLOCK), :] += dkb[:SWA_BLOCK]
                dv_ref[pl.ds(before, SWA_BLOCK), :] += dvb[:SWA_BLOCK]

        for h in range(nheads):
            dsink_ref[0, h:h + 1, :] = jnp.broadcast_to(dsink[h], (1, LANES))

    cur, prev = _swa_specs(s, t)
    whole = pl.BlockSpec((s, LANES), lambda i: (0, 0))
    return pl.pallas_call(
        body, name="swa_bwd", grid=(s // t,),
        in_specs=[pl.BlockSpec(memory_space=pltpu.SMEM), _rows(t, SWA_Q_WIDTH), cur, prev, cur, prev,
                  _rows(t, SWA_Q_WIDTH)],
        out_specs=[_rows(t, SWA_Q_WIDTH), whole, whole, pl.BlockSpec((1, 8, LANES), lambda i: (i, 0, 0))],
        out_shape=[jax.ShapeDtypeStruct((s, SWA_Q_WIDTH), F32), jax.ShapeDtypeStruct((s, SWA_KV_WIDTH), F32),
                   jax.ShapeDtypeStruct((s, SWA_KV_WIDTH), F32), jax.ShapeDtypeStruct((s // t, 8, LANES), F32)],
        compiler_params=_cparams(("arbitrary",)),
    )(sinks, qn, kn, kn, v, v, do)


def _swa_post(raw, dqn, dkn, qg, kg, cos, sin):
    s = raw.shape[0]
    tm = min(512, s)
    nq = SWA_Q_WIDTH // LANES

    def body(raw_ref, dq_ref, dk_ref, qg_ref, kg_ref, cos_ref, sin_ref, out_ref, dg_ref):
        @pl.when(pl.program_id(0) == 0)
        def _():
            dg_ref[...] = jnp.zeros_like(dg_ref)

        cs, sn = cos_ref[...], sin_ref[...]
        dgq = jnp.zeros((1, LANES), F32)
        for b in range(nq):
            cols = slice(b * LANES, (b + 1) * LANES)
            dp, dg = _norm_rope_bwd(raw_ref[:, cols], dq_ref[:, cols], qg_ref[...], cs, sn)
            out_ref[:, cols] = dp.astype(BF16)
            dgq = dgq + dg
        cols = slice(SWA_Q_WIDTH, SWA_Q_WIDTH + LANES)
        dp, dgk = _norm_rope_bwd(raw_ref[:, cols], dk_ref[...], kg_ref[...], cs, sn)
        out_ref[:, cols] = dp.astype(BF16)
        dg_ref[0:1, :] += dgq
        dg_ref[1:2, :] += dgk

    return pl.pallas_call(
        body, name="swa_post", grid=(s // tm,),
        in_specs=[_rows(tm, 640), _rows(tm, SWA_Q_WIDTH), _rows(tm, LANES), _resident((1, LANES)),
                  _resident((1, LANES)), _rows(tm, LANES), _rows(tm, LANES)],
        out_specs=[_rows(tm, 640), pl.BlockSpec((8, LANES), lambda i: (0, 0))],
        out_shape=[jax.ShapeDtypeStruct((s, 640), BF16), jax.ShapeDtypeStruct((8, LANES), F32)],
        compiler_params=_cparams(("arbitrary",)),
    )(raw, dqn, dkn, qg, kg, cos, sin)


def _inproj_bwd(dsbq, dsbk, dsbv, dswqk, dswv, dgl, x, dx1, g, w):
    s = x.shape[0]
    tm = min(256, s)

    def body(dsbq_ref, dsbk_ref, dsbv_ref, dswqk_ref, dswv_ref, dgl_ref, x_ref, dx1_ref, g_ref, w_ref,
             dx_ref, dproj_ref, dg_ref):
        @pl.when(pl.program_id(0) == 0)
        def _():
            dg_ref[...] = jnp.zeros_like(dg_ref)

        dproj_ref[:, 0:512] = dsbq_ref[...]
        dproj_ref[:, 512:1024] = dsbk_ref[...].astype(BF16)
        dproj_ref[:, 1024:1536] = dsbv_ref[...].astype(BF16)
        dproj_ref[:, 1536:2176] = dswqk_ref[...]
        dproj_ref[:, 2176:2304] = dswv_ref[...].astype(BF16)
        dproj_ref[:, GATE_OFF:IN_WIDTH] = dgl_ref[...]
        dh = jnp.zeros((tm, D_MODEL), F32)
        for a in range(0, IN_WIDTH, 512):
            b = min(a + 512, IN_WIDTH)
            dh = dh + _dot(dproj_ref[:, a:b], w_ref[:, a:b], NT)
        gam = g_ref[...]
        xn, rstd = _rms(x_ref[...])
        dg_ref[...] += jnp.sum(dh * xn, axis=0, keepdims=True)
        dx_ref[...] = dx1_ref[...] + _rms_bwd(dh, xn, rstd, gam)

    return pl.pallas_call(
        body, name="inproj_bwd", grid=(s // tm,),
        in_specs=[_rows(tm, 512), _rows(tm, 512), _rows(tm, 512), _rows(tm, 640), _rows(tm, LANES),
                  _rows(tm, 2 * D_MODEL), _rows(tm, D_MODEL), _rows(tm, D_MODEL), _resident((1, D_MODEL)),
                  _resident((D_MODEL, IN_WIDTH))],
        out_specs=[_rows(tm, D_MODEL), _rows(tm, IN_WIDTH), pl.BlockSpec((1, D_MODEL), lambda i: (0, 0))],
        out_shape=[jax.ShapeDtypeStruct((s, D_MODEL), F32), jax.ShapeDtypeStruct((s, IN_WIDTH), BF16),
                   jax.ShapeDtypeStruct((1, D_MODEL), F32)],
        compiler_params=_cparams(("arbitrary",)),
    )(dsbq, dsbk, dsbv, dswqk, dswv, dgl, x, dx1, g, w)


def _gather_weights(shard):
    def body(src, out, ssem, rsem, lsem):
        once = pl.program_id(0) == 0
        _hosted_gather(src, out, ssem, rsem, lsem, once, once, once)

    return pl.pallas_call(
        body, name="gather_weights", grid=(1,), in_specs=[HBM], out_specs=HBM,
        out_shape=jax.ShapeDtypeStruct((N_CHIPS,) + shard.shape, shard.dtype),
        scratch_shapes=_gather_scratch(1),
        compiler_params=pltpu.CompilerParams(dimension_semantics=("arbitrary",), has_side_effects=True),
    )(shard)


def _pair_swap(arrs, send=()):
    n, ns = len(arrs), len(send)

    def body(*refs):
        mine, src = refs[:n], refs[n:n + ns]
        theirs, landed = refs[n + ns:2 * n + ns], refs[2 * n + ns:2 * (n + ns)]
        ssem, rsem = refs[2 * (n + ns):2 * (n + ns) + 2]
        once, never = pl.program_id(0) == 0, pl.program_id(0) < 0
        if ns:
            _hosted_exchange(src, landed, *refs[2 * (n + ns) + 2:], once, never)
        x, y, c, _ = _place()
        cps = [pltpu.make_async_remote_copy(mine[t], theirs[t], ssem.at[t], rsem.at[t],
                                            device_id=(x, y, 1 - c), device_id_type=MESH) for t in range(n)]
        for cp in cps:
            cp.start()
        for cp in cps:
            cp.wait()
        if ns:
            _hosted_exchange(src, landed, *refs[2 * (n + ns) + 2:], never, once)

    return pl.pallas_call(
        body, name="pair_swap_x%d" % ns, grid=(1,), in_specs=[HBM] * (n + ns), out_specs=[HBM] * (n + ns),
        out_shape=[jax.ShapeDtypeStruct(a.shape, a.dtype) for a in tuple(arrs) + tuple(send)],
        scratch_shapes=[pltpu.SemaphoreType.DMA((n,)), pltpu.SemaphoreType.DMA((n,))] + (_exchange_scratch(ns) if ns else []),
        compiler_params=pltpu.CompilerParams(dimension_semantics=("arbitrary",), has_side_effects=True),
    )(*arrs, *send)


def _allreduce_small(block):
    def body(src, out, buf, ssem, rsem):
        x, y, c, _ = _place()
        me = 4 * x + 2 * y + c
        buf[me] = src[...]
        cps = []
        for k in range(1, N_DEV):
            peer = (x ^ (k >> 2), y ^ ((k >> 1) & 1), c ^ (k & 1))
            cp = pltpu.make_async_remote_copy(src, buf.at[me], ssem.at[k - 1], rsem.at[k - 1], device_id=peer, device_id_type=MESH)
            cp.start()
            cps.append(cp)
        for k in range(1, N_DEV):
            got = buf.at[me ^ k]
            pltpu.make_async_remote_copy(got, got, ssem.at[k - 1], rsem.at[k - 1], device_id=(x, y, c), device_id_type=MESH).wait_recv()
        for cp in cps:
            cp.wait_send()
        tot = buf[0]
        for d in range(1, N_DEV):
            tot = tot + buf[d]
        out[...] = tot

    vm = pl.BlockSpec(memory_space=pltpu.VMEM)
    return pl.pallas_call(
        body, name="allreduce_small", in_specs=[vm], out_specs=vm,
        out_shape=jax.ShapeDtypeStruct(block.shape, F32),
        scratch_shapes=[pltpu.VMEM((N_DEV,) + block.shape, F32), pltpu.SemaphoreType.DMA((N_DEV - 1,)),
                        pltpu.SemaphoreType.DMA((N_DEV - 1,))],
        compiler_params=pltpu.CompilerParams(has_side_effects=True),
    )(block)


def _sum_chips(landed, own, into, layer, name):
    nq, k, n = landed.shape
    tr = min(256, k)
    me = (2 * lax.axis_index("x") + lax.axis_index("y")).astype(jnp.int32).reshape(1)

    def body(me_ref, p0, p1, p2, p3, own_ref, into_ref, o_ref):
        mine = own_ref[0]
        terms = [jnp.where(me_ref[0] == q, mine, p[0].astype(F32)) for q, p in enumerate((p0, p1, p2, p3))]
        o_ref[0] = ((terms[0] + terms[1]) + terms[2]) + terms[3]

    spec = lambda q: pl.BlockSpec((1, tr, n), lambda i, m, q=q: (q, i, 0))
    return pl.pallas_call(
        body, name=name,
        grid_spec=pltpu.PrefetchScalarGridSpec(
            num_scalar_prefetch=1, grid=(k // tr,),
            in_specs=[spec(q) for q in range(nq)] + [pl.BlockSpec((1, tr, n), lambda i, m: (m[0], i, 0)), HBM],
            out_specs=pl.BlockSpec((1, tr, n), lambda i, m: (layer, i, 0))),
        out_shape=jax.ShapeDtypeStruct(into.shape, F32),
        input_output_aliases={6: 0},
        compiler_params=_cparams(("arbitrary",)),
    )(me, landed, landed, landed, landed, own, into)


def _adamw(w, g, g2, m, v, name):
    shape = w.shape
    cols = shape[-1]
    flat = lambda t: t.reshape(-1, cols)
    rows = flat(w).shape[0]
    tr = min(512, rows)
    pair = g2 is not None

    def body(*refs):
        if pair:
            w_ref, g_ref, g2_ref, m_ref, v_ref, go_ref, d_ref, nm_ref, nv_ref = refs
            gr = g_ref[...] + g2_ref[...]
        else:
            w_ref, g_ref, m_ref, v_ref, go_ref, d_ref, nm_ref, nv_ref = refs
            gr = g_ref[...]
        go_ref[...] = gr
        nm = ADAM_B1 * m_ref[...] + (1.0 - ADAM_B1) * gr
        nv = ADAM_B2 * v_ref[...] + (1.0 - ADAM_B2) * (gr * gr)
        m_hat = nm / (1.0 - ADAM_B1 ** ADAM_STEP)
        v_hat = nv / (1.0 - ADAM_B2 ** ADAM_STEP)
        d_ref[...] = -ADAM_LR * (m_hat / (jnp.sqrt(v_hat) + ADAM_EPS) + ADAM_WD * w_ref[...])
        nm_ref[...] = nm
        nv_ref[...] = nv

    spec = pl.BlockSpec((tr, cols), lambda i: (i, 0))
    ins = [w, g] + ([g2] if pair else []) + [m, v]
    outs = pl.pallas_call(
        body, name=name, grid=(rows // tr,),
        in_specs=[spec] * len(ins), out_specs=[spec] * 4,
        out_shape=[jax.ShapeDtypeStruct((rows, cols), F32)] * 4,
        compiler_params=_cparams(("arbitrary",)),
    )(*[flat(t) for t in ins])
    return [o.reshape(shape) for o in outs]


def _pack_rest(big, l):
    rows = dict(PACK_ROWS)
    return jnp.concatenate([big[name][l].astype(BF16).reshape(rows[name], 1024) for name in REST], axis=0)


def _whole_columns(t):
    return jnp.moveaxis(t, 0, 1).reshape(t.shape[1], N_CHIPS * t.shape[2])


def _branch_weights(rest):
    out, at = [], REST_ROWS - 256
    for _ in range(2):
        out.append(_whole_columns(rest[:, at:at + 128, :].reshape((N_CHIPS,) + SHARD_SHAPES["w_branch_sb"])))
        at += 128
    return out


def _pack_small(d):
    flat = jnp.concatenate([d[n].reshape(-1) for n in SMALL_NAMES])
    return jnp.pad(flat, (0, SMALL_ROWS * LANES - flat.shape[0])).reshape(SMALL_ROWS, LANES)


def _unpack_small(block, like):
    flat, out, at = block.reshape(-1), {}, 0
    for n in SMALL_NAMES:
        size = like[n].size
        out[n] = flat[at:at + size].reshape(like[n].shape)
        at += size
    return out


def _rope_tables(s):
    inv_freq = 1.0 / (ROPE_THETA ** (jnp.arange(0, HEAD_DIM, 2, dtype=F32) / HEAD_DIM))
    ang = jnp.arange(s, dtype=F32)[:, None] * inv_freq[None, :]
    reps = LANES // (HEAD_DIM // 2)
    return jnp.tile(jnp.cos(ang), (1, reps)), jnp.tile(jnp.sin(ang), (1, reps))


def _forward_backward(x, target, big, small):
    s = x.shape[0]
    cos, sin = _rope_tables(s)
    two = lambda gvec: jnp.tile(gvec.reshape(1, HEAD_DIM), (1, 2))
    saved = []
    win_shard = lambda l: big["w_in"][l].astype(BF16)
    win = _whole_columns(_gather_weights(win_shard(0)))
    for l in range(DEPTH):
        gm = small["mix_norm_g"][l].reshape(1, D_MODEL)
        gl = small["mlp_norm_g"][l].reshape(1, D_MODEL)
        qg, kg = two(small["q_norm_g"][l]), two(small["k_norm_g"][l])
        h, sbqkv, raw, qn, kn, v, gates = _inproj_fwd(x, gm, win, qg, kg, cos, sin)
        osb, rest, *nxt = _sb_fwd(sbqkv, [_pack_rest(big, l)] + ([win_shard(l + 1)] if l + 1 < DEPTH else []))
        wsb, wswa = _branch_weights(rest)
        oswa = _swa_fwd(qn, kn, v, small["sinks"][l])
        x1 = _merge_fwd(x, osb, oswa, gates, wsb, wswa, rest)
        x2, u = _mlp_fwd(x1, gl, rest)
        saved.append((x, h, sbqkv, raw, qn, kn, v, gates, osb, oswa, x1, u, gm, gl, qg, kg, win, wsb, wswa, rest))
        x, win = x2, (_whole_columns(nxt[0]) if nxt else None)
    dx, loss = _loss_grad(x, target)

    got = {name: [None] * DEPTH for name in BIG}
    gsmall = {name: [None] * DEPTH for name in SMALL_NAMES}
    late = []
    for l in reversed(range(DEPTH)):
        x0, h, sbqkv, raw, qn, kn, v, gates, osb, oswa, x1, u, gm, gl, qg, kg, win, wsb, wswa, rest = saved[l]
        dx1, du, act, h2, dgl_mlp, dxb = _mlp_bwd(dx, x1, u, gl, rest)
        dosb, doswa, dgate, merged, dysb, dyswa, dx1b = _merge_bwd(dx1, osb, oswa, gates, wsb, wswa, rest)
        ready = [("w_down", l, *_wgrad(act, dxb, "wgrad_down", 512, 1024, shard_axis=0)),
                 ("w_up", l, *_wgrad(h2, du, "wgrad_up", 1024, 512, shard_axis=1)),
                 ("w_out", l, *_wgrad(merged, dx1b, "wgrad_out", 256, 1024, shard_axis=0)),
                 ("w_branch_sb", l, *_wgrad(osb, dysb, "wgrad_bsb", 512, 256, shard_axis=1)),
                 ("w_branch_swa", l, *_wgrad(oswa, dyswa, "wgrad_bswa", 512, 256, shard_axis=1))] + late
        dsbq, dsbk, dsbv, *landed = _sb_bwd(sbqkv, osb, dosb, [narrow for _, _, _, narrow in ready])
        for (name, layer, own, _), arr in zip(ready, landed):
            got[name][layer] = (arr, own)
        dqn, dkn, dswv, dsink = _swa_bwd(qn, kn, v, small["sinks"][l], doswa)
        dswqk, dgqk = _swa_post(raw, dqn, dkn, qg, kg, cos, sin)
        dx, dproj, dg_mix = _inproj_bwd(dsbq, dsbk, dsbv, dswqk, dswv, dgate, x0, dx1, gm, win)
        dwin = _wgrad(h, dproj, "wgrad_in", 1024, 256)
        dwin = dwin.reshape(D_MODEL, N_CHIPS, IN_WIDTH // N_CHIPS).transpose(1, 0, 2)
        late = [("w_in", l, dwin, dwin.astype(BF16))]
        gsmall["mix_norm_g"][l] = dg_mix[0]
        gsmall["mlp_norm_g"][l] = dgl_mlp[0]
        gsmall["q_norm_g"][l] = dgqk[0, :HEAD_DIM] + dgqk[0, HEAD_DIM:]
        gsmall["k_norm_g"][l] = dgqk[1, :HEAD_DIM] + dgqk[1, HEAD_DIM:]
        gsmall["sinks"][l] = jnp.sum(dsink[:, :, 0], axis=0)
    gsmall = {k: jnp.stack(vs) for k, vs in gsmall.items()}
    return loss, dx, got, late[0], gsmall


def kernel(x, mix_norm_g, w_in, q_norm_g, k_norm_g, sinks, w_branch_sb, w_branch_swa, w_out, mlp_norm_g, w_up, w_down, loss_target, m_mix_norm_g, m_w_in, m_q_norm_g, m_k_norm_g, m_sinks, m_w_branch_sb, m_w_branch_swa, m_w_out, m_mlp_norm_g, m_w_up, m_w_down, v_mix_norm_g, v_w_in, v_q_norm_g, v_k_norm_g, v_sinks, v_w_branch_sb, v_w_branch_swa, v_w_out, v_mlp_norm_g, v_w_up, v_w_down):
    big = dict(w_in=w_in, w_branch_sb=w_branch_sb, w_branch_swa=w_branch_swa, w_out=w_out, w_up=w_up, w_down=w_down)
    big_m = dict(w_in=m_w_in, w_branch_sb=m_w_branch_sb, w_branch_swa=m_w_branch_swa, w_out=m_w_out, w_up=m_w_up, w_down=m_w_down)
    big_v = dict(w_in=v_w_in, w_branch_sb=v_w_branch_sb, w_branch_swa=v_w_branch_swa, w_out=v_w_out, w_up=v_w_up, w_down=v_w_down)
    small = dict(mix_norm_g=mix_norm_g, q_norm_g=q_norm_g, k_norm_g=k_norm_g, sinks=sinks, mlp_norm_g=mlp_norm_g)
    small_m = dict(mix_norm_g=m_mix_norm_g, q_norm_g=m_q_norm_g, k_norm_g=m_k_norm_g, sinks=m_sinks, mlp_norm_g=m_mlp_norm_g)
    small_v = dict(mix_norm_g=v_mix_norm_g, q_norm_g=v_q_norm_g, k_norm_g=v_k_norm_g, sinks=v_sinks, mlp_norm_g=v_mlp_norm_g)

    loss_part, grad_x, got, last, gsmall = _forward_backward(x[0], loss_target[0], big, small)
    loss = lax.psum(loss_part, ("x", "y", "c"))

    mine = {}
    for name in BIG:
        tot = lax.empty((DEPTH,) + SHARD_SHAPES[name], F32)
        for l in range(DEPTH):
            if got[name][l] is not None:
                tot = _sum_chips(*got[name][l], tot, l, "sum_" + name)
        mine[name] = tot
    name, layer, own, narrow = last
    *theirs, landed = _pair_swap([mine[n] for n in REST], [narrow])
    theirs = dict(zip(REST, theirs))
    mine[name] = _sum_chips(landed, own, mine[name], layer, "sum_" + name)
    (theirs[name],) = _pair_swap([mine[name]])
    upd = {n: _adamw(big[n], mine[n], theirs[n], big_m[n], big_v[n], "adamw_" + n) for n in BIG}
    g_small = _allreduce_small(_pack_small(gsmall))
    sm = _adamw(_pack_small(small), g_small, None, _pack_small(small_m), _pack_small(small_v), "adamw_small")
    upd_small = [_unpack_small(t, small) for t in sm]

    names = ("mix_norm_g", "w_in", "q_norm_g", "k_norm_g", "sinks", "w_branch_sb", "w_branch_swa", "w_out",
             "mlp_norm_g", "w_up", "w_down")
    pick = lambda n, i: upd[n][i] if n in upd else upd_small[i][n]
    return (loss, grad_x[None], *[pick(n, 0) for n in names], *[pick(n, 1) for n in names],
            *[pick(n, 2) for n in names], *[pick(n, 3) for n in names])
```

```python
import jax
import jax.numpy as jnp
from jax import lax
from jax.experimental import pallas as pl
from jax.experimental.pallas import tpu as pltpu

F32 = jnp.float32
BF16 = jnp.bfloat16

D_MODEL = 1024
DEPTH = 4
HEAD_DIM = 64
SB_WIDTH = 512
SWA_Q_WIDTH = 512
SWA_KV_WIDTH = 128
D_FF = 4096
IN_WIDTH = 4352
GATE_OFF = 2304
ROPE_THETA = 10000.0
NORM_EPS = 1e-6
SCALE = HEAD_DIM ** -0.5
N_CHIPS = 4
N_DEV = 8

ADAM_LR = 0.001
ADAM_B1 = 0.9
ADAM_B2 = 0.999
ADAM_EPS = 1e-08
ADAM_WD = 0.01
ADAM_STEP = 10

LANES = 128
SB_TILE = 256
SB_STEP_WIDTH = 512
SB_FWD_STEP_WIDTH = 512
SWA_BLOCK = 128
VMEM_LIMIT = 56 << 20

NN = (((1,), (0,)), ((), ()))
NT = (((1,), (1,)), ((), ()))
TN = (((0,), (0,)), ((), ()))
MESH = pl.DeviceIdType.MESH
HBM = pl.BlockSpec(memory_space=pl.ANY)

BIG = ("w_in", "w_branch_sb", "w_branch_swa", "w_out", "w_up", "w_down")
SMALL_NAMES = ("mix_norm_g", "q_norm_g", "k_norm_g", "sinks", "mlp_norm_g")
PACK_ROWS = (("w_in", 1088), ("w_branch_sb", 128), ("w_branch_swa", 128), ("w_out", 256), ("w_up", 1024), ("w_down", 1024))
SHARD_SHAPES = {"w_in": (1024, 1088), "w_branch_sb": (512, 256), "w_branch_swa": (512, 256), "w_out": (256, 1024),
                "w_up": (1024, 1024), "w_down": (1024, 1024)}
ROW_SHARDED = ("w_out", "w_down")
SMALL_ROWS = 72


def _dot(a, b, dims):
    return lax.dot_general(a, b, dims, preferred_element_type=F32)


def _cparams(sem):
    return pltpu.CompilerParams(dimension_semantics=sem, vmem_limit_bytes=VMEM_LIMIT)


def _resident(shape, index=None):
    index = (0,) * len(shape) if index is None else index
    return pl.BlockSpec(shape, lambda *_: index, pipeline_mode=pl.Buffered(1))


REST = ("w_up", "w_down", "w_out", "w_branch_sb", "w_branch_swa")
REST_ROWS = 2560
W_UP_BLOCK = ((N_CHIPS, 1024, 1024), (0, 0, 0))
W_DOWN_BLOCK = ((N_CHIPS, 1024, 1024), (0, 1, 0))
W_OUT_BLOCK = ((N_CHIPS, 256, 1024), (0, 8, 0))


def _rows(tm, width):
    return pl.BlockSpec((tm, width), lambda i: (i, 0))


def _rms(xf):
    rstd = lax.rsqrt(jnp.mean(xf * xf, axis=-1, keepdims=True) + NORM_EPS)
    return xf * rstd, rstd


def _rms_bwd(dh, xn, rstd, g):
    dxn = dh * g
    return rstd * (dxn - xn * jnp.mean(dxn * xn, axis=-1, keepdims=True))


def _lane(shape):
    return lax.broadcasted_iota(jnp.int32, shape, len(shape) - 1)


def _head_mean(v, left):
    sl = jnp.sum(jnp.where(left, v, 0.0), axis=-1, keepdims=True)
    sr = jnp.sum(jnp.where(left, 0.0, v), axis=-1, keepdims=True)
    return jnp.where(left, sl, sr) * (1.0 / HEAD_DIM)


def _rope(y, cs, sn, first):
    up = pltpu.roll(y, 96, 1)
    dn = pltpu.roll(y, 32, 1)
    return y * cs + jnp.where(first, -up, dn) * sn


def _rope_t(d, cs, sn, first):
    t = d * jnp.where(first, -sn, sn)
    return d * cs + jnp.where(first, pltpu.roll(t, 96, 1), pltpu.roll(t, 32, 1))


def _norm_rope(p, g, cs, sn):
    lane = _lane((1, LANES))
    left = lane < HEAD_DIM
    first = (lane % HEAD_DIM) < (HEAD_DIM // 2)
    yn = p * lax.rsqrt(_head_mean(p * p, left) + NORM_EPS)
    return _rope(yn * g, cs, sn, first)


def _norm_rope_bwd(p, dout, g, cs, sn):
    lane = _lane((1, LANES))
    left = lane < HEAD_DIM
    first = (lane % HEAD_DIM) < (HEAD_DIM // 2)
    rstd = lax.rsqrt(_head_mean(p * p, left) + NORM_EPS)
    yn = p * rstd
    dyg = _rope_t(dout, cs, sn, first)
    dg = jnp.sum(dyg * yn, axis=0, keepdims=True)
    dyn = dyg * g
    dp = rstd * (dyn - yn * _head_mean(dyn * yn, left))
    return dp, dg


def _place():
    x, y, c = lax.axis_index("x"), lax.axis_index("y"), lax.axis_index("c")
    return x, y, c, [(1 - x, y), (x, 1 - y), (1 - x, 1 - y)]


def _hosted_gather(src, out, ssem, rsem, lsem, first, mid, last, slot=0):
    x, y, c, chips = _place()
    me = 2 * x + y
    sib = (x, y, 1 - c)
    r2 = src.shape[0] // 2
    base = 6 * slot

    def slab(chip, half):
        return out.at[chip, pl.ds(half * r2, r2)]

    def ici(j):
        cx, cy = chips[j]
        return pltpu.make_async_remote_copy(src.at[pl.ds(c * r2, r2)], slab(me, c), ssem.at[base + j], rsem.at[base + j],
                                            device_id=(cx, cy, c), device_id_type=MESH)

    def landed(j):
        got = slab(2 * chips[j][0] + chips[j][1], c)
        return pltpu.make_async_remote_copy(got, got, ssem.at[base + j], rsem.at[base + j], device_id=sib, device_id_type=MESH)

    def d2d(j, half):
        got = slab(2 * chips[j][0] + chips[j][1], half)
        return pltpu.make_async_remote_copy(got, got, ssem.at[base + 3 + j], rsem.at[base + 3 + j], device_id=sib,
                                            device_id_type=MESH)

    local = pltpu.make_async_copy(src, out.at[me], lsem.at[slot])

    @pl.when(first)
    def _():
        local.start()
        for j in range(3):
            ici(j).start()

    @pl.when(mid)
    def _():
        for j in range(3):
            landed(j).wait_recv()
            d2d(j, c).start()

    @pl.when(last)
    def _():
        for j in range(3):
            d2d(j, 1 - c).wait_recv()
        for j in range(3):
            ici(j).wait_send()
            d2d(j, c).wait_send()
        local.wait()


def _hosted_exchange(srcs, outs, ssem, rsem, first, last):
    x, y, c, chips = _place()
    me = 2 * x + y

    def send(t, j):
        cx, cy = chips[j]
        return pltpu.make_async_remote_copy(srcs[t].at[2 * cx + cy], outs[t].at[me], ssem.at[3 * t + j], rsem.at[3 * t + j],
                                            device_id=(cx, cy, c), device_id_type=MESH)

    def landed(t, j):
        cx, cy = chips[j]
        got = outs[t].at[2 * cx + cy]
        return pltpu.make_async_remote_copy(got, got, ssem.at[3 * t + j], rsem.at[3 * t + j],
                                            device_id=(cx, cy, c), device_id_type=MESH)

    @pl.when(first)
    def _():
        for t in range(len(srcs)):
            for j in range(3):
                send(t, j).start()

    @pl.when(last)
    def _():
        for t in range(len(srcs)):
            for j in range(3):
                landed(t, j).wait_recv()
        for t in range(len(srcs)):
            for j in range(3):
                send(t, j).wait_send()


def _gather_scratch(n):
    return [pltpu.SemaphoreType.DMA((6 * n,)), pltpu.SemaphoreType.DMA((6 * n,)), pltpu.SemaphoreType.DMA((n,))]


def _exchange_scratch(n):
    return [pltpu.SemaphoreType.DMA((3 * n,)), pltpu.SemaphoreType.DMA((3 * n,))]


def _inproj_fwd(x, g, w, qg, kg, cos, sin):
    s = x.shape[0]
    tm = min(512, s)

    def body(x_ref, g_ref, w_ref, qg_ref, kg_ref, cos_ref, sin_ref,
             h_ref, sb_ref, raw_ref, qn_ref, kn_ref, v_ref, gate_ref):
        xn, _ = _rms(x_ref[...])
        h = (xn * g_ref[...]).astype(BF16)
        h_ref[...] = h
        for a in range(0, 3 * SB_WIDTH, 512):
            sb_ref[:, a:a + 512] = _dot(h, w_ref[:, a:a + 512], NN).astype(BF16)
        cs, sn = cos_ref[...], sin_ref[...]
        q0 = 3 * SB_WIDTH
        pq = _dot(h, w_ref[:, q0:q0 + SWA_Q_WIDTH], NN)
        raw_ref[:, 0:SWA_Q_WIDTH] = pq
        for b in range(SWA_Q_WIDTH // LANES):
            blk = pq[:, b * LANES:(b + 1) * LANES]
            qn_ref[:, b * LANES:(b + 1) * LANES] = _norm_rope(blk, qg_ref[...], cs, sn).astype(BF16)
        k0 = q0 + SWA_Q_WIDTH
        pk = _dot(h, w_ref[:, k0:k0 + 2 * SWA_KV_WIDTH], NN)
        raw_ref[:, SWA_Q_WIDTH:SWA_Q_WIDTH + SWA_KV_WIDTH] = pk[:, :SWA_KV_WIDTH]
        kn_ref[...] = _norm_rope(pk[:, :SWA_KV_WIDTH], kg_ref[...], cs, sn).astype(BF16)
        v_ref[...] = pk[:, SWA_KV_WIDTH:].astype(BF16)
        for a in range(GATE_OFF, IN_WIDTH, 512):
            gate_ref[:, a - GATE_OFF:a - GATE_OFF + 512] = jax.nn.sigmoid(_dot(h, w_ref[:, a:a + 512], NN))

    return pl.pallas_call(
        body, name="inproj_fwd", grid=(s // tm,),
        in_specs=[_rows(tm, D_MODEL), _resident((1, D_MODEL)), _resident((D_MODEL, IN_WIDTH)),
                  _resident((1, LANES)), _resident((1, LANES)), _rows(tm, LANES), _rows(tm, LANES)],
        out_specs=[_rows(tm, D_MODEL), _rows(tm, 3 * SB_WIDTH), _rows(tm, 640), _rows(tm, SWA_Q_WIDTH),
                   _rows(tm, SWA_KV_WIDTH), _rows(tm, SWA_KV_WIDTH), _rows(tm, 2 * D_MODEL)],
        out_shape=[jax.ShapeDtypeStruct((s, D_MODEL), BF16), jax.ShapeDtypeStruct((s, 3 * SB_WIDTH), BF16),
                   jax.ShapeDtypeStruct((s, 640), F32), jax.ShapeDtypeStruct((s, SWA_Q_WIDTH), BF16),
                   jax.ShapeDtypeStruct((s, SWA_KV_WIDTH), BF16), jax.ShapeDtypeStruct((s, SWA_KV_WIDTH), BF16),
                   jax.ShapeDtypeStruct((s, 2 * D_MODEL), F32)],
        compiler_params=_cparams(("arbitrary",)),
    )(x, g, w, qg, kg, cos, sin)


def _sb_tile_terms(qh, k, strict):
    z = _dot(qh, k, NT)
    nz = -z
    soft = jnp.log(1.0 + jnp.exp(jnp.minimum(z, nz)))
    log_keep = jnp.minimum(nz, 0.0) - soft
    if strict is not None:
        log_keep = jnp.where(strict, log_keep, 0.0)
    return z, log_keep


def _log_weights(z, log_keep, run, u_from):
    tail = run + _dot(log_keep.astype(BF16), u_from, NN)
    return z + tail, tail[:, 0:1]


def _masked(strict, val):
    return val if strict is None else jnp.where(strict, val, 0.0)


def _tri_twice(t):
    r = lax.broadcasted_iota(jnp.int32, (2 * t, t), 0)
    c = lax.broadcasted_iota(jnp.int32, (2 * t, t), 1)
    return (jnp.where(r >= t, r - t, r) >= c).astype(BF16)


def _split_dot(a, u2):
    hi = a.astype(BF16)
    lo = (a - hi.astype(F32)).astype(BF16)
    return _dot(jnp.concatenate([hi, lo], axis=1), u2, NN)


def _sb_fwd(sbqkv, shards=()):
    s = sbqkv.shape[0]
    t = min(SB_TILE, s)
    nq = s // t
    ng = SB_WIDTH // SB_FWD_STEP_WIDTH
    heads = range(2 * SB_FWD_STEP_WIDTH // LANES)
    nsh = len(shards)

    def body(*refs):
        q_ref, k_ref, v_ref = refs[:3]
        o_ref = refs[3 + nsh]
        j, i = pl.program_id(0), pl.program_id(1)
        for n in range(nsh):
            _hosted_gather(refs[3 + n], refs[4 + nsh + n], *refs[4 + 2 * nsh:], (j == 0) & (i == 0),
                           (j == ng - 1) & (i == (3 * nq) // 4), (j == ng - 1) & (i == nq - 1), slot=n)
        left = _lane((1, LANES)) < HEAD_DIM
        halves = (left, jnp.logical_not(left))
        pair = lambda a, p: a[:, (p // 2) * LANES:(p // 2 + 1) * LANES]
        only = lambda a, p: jnp.where(halves[p % 2], pair(a, p), jnp.zeros((), a.dtype))
        q_all = q_ref[...] * jnp.asarray(SCALE, BF16)
        qh = [only(q_all, p) for p in heads]
        r = lax.broadcasted_iota(jnp.int32, (t, t), 0)
        c = lax.broadcasted_iota(jnp.int32, (t, t), 1)
        u_from = (r >= c).astype(BF16)

        def tile(n, carry, strict):
            runs, accs = carry
            off = pl.multiple_of((i - n) * t, t)
            k = k_ref[pl.ds(off, t), :]
            v = v_ref[pl.ds(off, t), :]
            terms = [_sb_tile_terms(qh[p], pair(k, p), strict) for p in heads]
            logw = [_log_weights(*terms[p], runs[p], u_from) for p in heads]
            ws = [_masked(strict, jnp.exp(logw[p][0])).astype(BF16) for p in heads]
            accs = list(accs)
            for p in heads:
                accs[p // 2] = accs[p // 2] + _dot(ws[p], only(v, p), NN)
            return tuple(logw[p][1] for p in heads), tuple(accs)

        zero = jnp.zeros((t, 1), F32)
        wide = jnp.zeros((t, LANES), F32)
        carry = tile(0, (tuple(zero for _ in heads), tuple(wide for _ in heads[::2])), c < r)
        _, accs = lax.fori_loop(1, i + 1, lambda n, cr: tile(n, cr, None), carry)
        for g, acc in enumerate(accs):
            o_ref[:, g * LANES:(g + 1) * LANES] = acc

    w = SB_FWD_STEP_WIDTH
    blk = pl.BlockSpec((t, w), lambda j, i: (i, j))
    in_specs = [blk, pl.BlockSpec((s, w), lambda j, i: (0, ng + j)), pl.BlockSpec((s, w), lambda j, i: (0, 2 * ng + j))]
    out_specs = [blk]
    out_shape = [jax.ShapeDtypeStruct((s, SB_WIDTH), F32)]
    in_specs += [HBM] * nsh
    out_specs += [HBM] * nsh
    out_shape += [jax.ShapeDtypeStruct((N_CHIPS,) + a.shape, a.dtype) for a in shards]
    return pl.pallas_call(
        body, name="sb_fwd_g%d" % nsh, grid=(ng, nq),
        in_specs=in_specs, out_specs=out_specs, out_shape=out_shape,
        scratch_shapes=_gather_scratch(nsh) if nsh else [],
        compiler_params=_cparams(("arbitrary", "arbitrary")),
    )(sbqkv, sbqkv, sbqkv, *shards)


def _swa_band(cur_ref, prev_ref, b):
    lo = prev_ref[...] if b == 0 else cur_ref[(b - 1) * SWA_BLOCK:b * SWA_BLOCK, :]
    return jnp.concatenate([lo, cur_ref[b * SWA_BLOCK:(b + 1) * SWA_BLOCK, :]], axis=0)


def _swa_variants(band, left):
    f = band.astype(F32)
    sw = pltpu.roll(f, HEAD_DIM, 1)
    halves = (left, jnp.logical_not(left))
    return [[jnp.where(halves[p], f if p == g else sw, 0.0).astype(BF16) for p in range(2)] for g in range(2)]


def _swa_valid(blk):
    ii = lax.broadcasted_iota(jnp.int32, (SWA_BLOCK, 2 * SWA_BLOCK), 0)
    jj = lax.broadcasted_iota(jnp.int32, (SWA_BLOCK, 2 * SWA_BLOCK), 1)
    rel = jj - SWA_BLOCK - ii
    return (rel <= 0) & (rel > -SWA_BLOCK) & (jj + blk * SWA_BLOCK >= SWA_BLOCK)


def _swa_softmax(dots, sink, valid):
    sc = jnp.where(valid, dots * SCALE, -1e30)
    m = jnp.maximum(jnp.max(sc, axis=-1, keepdims=True), sink)
    e = jnp.exp(sc - m)
    es = jnp.exp(sink - m)
    inv = 1.0 / (jnp.sum(e, axis=-1, keepdims=True) + es)
    return e * inv, es * inv


def _swa_heads(q_ref, rows, halves):
    nheads = SWA_Q_WIDTH // HEAD_DIM
    out = []
    for h in range(nheads):
        hb, p = h // 2, h % 2
        q2 = q_ref[rows, hb * LANES:(hb + 1) * LANES]
        out.append((hb, p, h // (nheads // 2), jnp.where(halves[p], q2, jnp.zeros_like(q2))))
    return out


def _swa_specs(s, t):
    nb = t // SWA_BLOCK
    cur = pl.BlockSpec((t, LANES), lambda i: (i, 0))
    prev = pl.BlockSpec((SWA_BLOCK, LANES), lambda i: (jnp.maximum(i * nb - 1, 0), 0))
    return cur, prev


def _swa_fwd(qn, kn, v, sinks):
    s = qn.shape[0]
    t = min(512, s)
    nb = t // SWA_BLOCK
    nheads = SWA_Q_WIDTH // HEAD_DIM

    def body(sink_ref, q_ref, kc_ref, kp_ref, vc_ref, vp_ref, o_ref):
        i = pl.program_id(0)
        left = _lane((1, LANES)) < HEAD_DIM
        halves = (left, jnp.logical_not(left))
        for b in range(nb):
            kvar = _swa_variants(_swa_band(kc_ref, kp_ref, b), left)
            vvar = _swa_variants(_swa_band(vc_ref, vp_ref, b), left)
            rows = slice(b * SWA_BLOCK, (b + 1) * SWA_BLOCK)
            valid = _swa_valid(i * nb + b)
            heads = _swa_heads(q_ref, rows, halves)
            dots = [_dot(qh, kvar[g][p], NT) for _, p, g, qh in heads]
            probs = [_swa_softmax(dots[h], sink_ref[h], valid)[0].astype(BF16) for h in range(nheads)]
            outs = [_dot(probs[h], vvar[g][p], NN) for h, (_, p, g, _) in enumerate(heads)]
            for hb in range(nheads // 2):
                o_ref[rows, hb * LANES:(hb + 1) * LANES] = (outs[2 * hb] + outs[2 * hb + 1]).astype(BF16)

    cur, prev = _swa_specs(s, t)
    return pl.pallas_call(
        body, name="swa_fwd", grid=(s // t,),
        in_specs=[pl.BlockSpec(memory_space=pltpu.SMEM), _rows(t, SWA_Q_WIDTH), cur, prev, cur, prev],
        out_specs=_rows(t, SWA_Q_WIDTH),
        out_shape=jax.ShapeDtypeStruct((s, SWA_Q_WIDTH), BF16),
        compiler_params=_cparams(("arbitrary",)),
    )(sinks, qn, kn, kn, v, v)


def _merge_fwd(x, osb, oswa, gates, wsb, wswa, rest):
    s = x.shape[0]
    tm = min(512, s)
    rc = D_MODEL // N_CHIPS

    def body(x_ref, osb_ref, oswa_ref, gate_ref, wsb_ref, wswa_ref, wout_ref, x1_ref):
        ysb = _dot(osb_ref[...].astype(BF16), wsb_ref[...], NN)
        yswa = _dot(oswa_ref[...], wswa_ref[...], NN)
        merged = (gate_ref[:, :D_MODEL] * ysb + gate_ref[:, D_MODEL:] * yswa).astype(BF16)
        acc = x_ref[...]
        for q in range(N_CHIPS):
            acc = acc + _dot(merged[:, q * rc:(q + 1) * rc], wout_ref[q], NN)
        x1_ref[...] = acc

    return pl.pallas_call(
        body, name="merge_fwd", grid=(s // tm,),
        in_specs=[_rows(tm, D_MODEL), _rows(tm, SB_WIDTH), _rows(tm, SWA_Q_WIDTH), _rows(tm, 2 * D_MODEL),
                  _resident((SB_WIDTH, D_MODEL)), _resident((SWA_Q_WIDTH, D_MODEL)), _resident(*W_OUT_BLOCK)],
        out_specs=_rows(tm, D_MODEL),
        out_shape=jax.ShapeDtypeStruct((s, D_MODEL), F32),
        compiler_params=_cparams(("arbitrary",)),
    )(x, osb, oswa, gates, wsb, wswa, rest)


def _mlp_fwd(x1, g, rest):
    s = x1.shape[0]
    tm = min(512, s)
    fc = D_FF // N_CHIPS

    def body(x_ref, g_ref, wup_ref, wdown_ref, x2_ref, u_ref):
        xf = x_ref[...]
        xn, _ = _rms(xf)
        h2 = (xn * g_ref[...]).astype(BF16)
        acc = xf

        def act(q):
            u = _dot(h2, wup_ref[q], NN)
            u_ref[:, q * fc:(q + 1) * fc] = u
            r = jnp.maximum(u, 0.0)
            return (r * r).astype(BF16)

        ahead = act(0)
        for q in range(N_CHIPS):
            nxt = act(q + 1) if q + 1 < N_CHIPS else None
            acc = acc + _dot(ahead, wdown_ref[q], NN)
            ahead = nxt
        x2_ref[...] = acc

    return pl.pallas_call(
        body, name="mlp_fwd", grid=(s // tm,),
        in_specs=[_rows(tm, D_MODEL), _resident((1, D_MODEL)), _resident(*W_UP_BLOCK), _resident(*W_DOWN_BLOCK)],
        out_specs=[_rows(tm, D_MODEL), _rows(tm, D_FF)],
        out_shape=[jax.ShapeDtypeStruct((s, D_MODEL), F32), jax.ShapeDtypeStruct((s, D_FF), F32)],
        compiler_params=_cparams(("arbitrary",)),
    )(x1, g, rest, rest)


def _loss_grad(y, target):
    s = y.shape[0]
    tm = min(512, s)

    def body(y_ref, t_ref, dy_ref, part_ref):
        err = y_ref[...] - t_ref[...]
        dy_ref[...] = err * (1.0 / D_MODEL)
        tot = jnp.sum(jnp.sum(err * err, axis=-1, keepdims=True), axis=0, keepdims=True)
        part_ref[...] = jnp.broadcast_to(tot.reshape(1, 1, 1), (1, 8, LANES))

    dy, part = pl.pallas_call(
        body, name="loss_grad", grid=(s // tm,),
        in_specs=[_rows(tm, D_MODEL), _rows(tm, D_MODEL)],
        out_specs=[_rows(tm, D_MODEL), pl.BlockSpec((1, 8, LANES), lambda i: (i, 0, 0))],
        out_shape=[jax.ShapeDtypeStruct((s, D_MODEL), F32), jax.ShapeDtypeStruct((s // tm, 8, LANES), F32)],
        compiler_params=_cparams(("arbitrary",)),
    )(y, target)
    return dy, (0.5 / D_MODEL) * jnp.sum(part[:, 0, 0])


def _mlp_bwd(dx2, x1, u, g, rest):
    s = x1.shape[0]
    tm = min(256, s)
    fc = D_FF // N_CHIPS

    def body(dx2_ref, x_ref, u_ref, g_ref, wup_ref, wdown_ref, dx1_ref, du_ref, a_ref, h2_ref, dg_ref, dxb_ref):
        @pl.when(pl.program_id(0) == 0)
        def _():
            dg_ref[...] = jnp.zeros_like(dg_ref)

        gam = g_ref[...]
        xn, rstd = _rms(x_ref[...])
        h2_ref[...] = (xn * gam).astype(BF16)
        dxf = dx2_ref[...]
        dxb = dxf.astype(BF16)
        dxb_ref[...] = dxb
        dh2 = jnp.zeros((tm, D_MODEL), F32)

        def grad_u(q):
            cols = slice(q * fc, (q + 1) * fc)
            da = _dot(dxb, wdown_ref[q], NT)
            r = jnp.maximum(u_ref[:, cols], 0.0)
            a_ref[:, cols] = (r * r).astype(BF16)
            du = (da * (2.0 * r)).astype(BF16)
            du_ref[:, cols] = du
            return du

        ahead = grad_u(0)
        for q in range(N_CHIPS):
            nxt = grad_u(q + 1) if q + 1 < N_CHIPS else None
            dh2 = dh2 + _dot(ahead, wup_ref[q], NT)
            ahead = nxt
        dg_ref[...] += jnp.sum(dh2 * xn, axis=0, keepdims=True)
        dx1_ref[...] = dxf + _rms_bwd(dh2, xn, rstd, gam)

    return pl.pallas_call(
        body, name="mlp_bwd", grid=(s // tm,),
        in_specs=[_rows(tm, D_MODEL), _rows(tm, D_MODEL), _rows(tm, D_FF), _resident((1, D_MODEL)),
                  _resident(*W_UP_BLOCK), _resident(*W_DOWN_BLOCK)],
        out_specs=[_rows(tm, D_MODEL), _rows(tm, D_FF), _rows(tm, D_FF), _rows(tm, D_MODEL),
                   pl.BlockSpec((1, D_MODEL), lambda i: (0, 0)), _rows(tm, D_MODEL)],
        out_shape=[jax.ShapeDtypeStruct((s, D_MODEL), F32), jax.ShapeDtypeStruct((s, D_FF), BF16),
                   jax.ShapeDtypeStruct((s, D_FF), BF16), jax.ShapeDtypeStruct((s, D_MODEL), BF16),
                   jax.ShapeDtypeStruct((1, D_MODEL), F32), jax.ShapeDtypeStruct((s, D_MODEL), BF16)],
        compiler_params=_cparams(("arbitrary",)),
    )(dx2, x1, u, g, rest, rest)


def _wgrad(a, b, name, tm, tn, shard_axis=None):
    s, m = a.shape
    n = b.shape[1]

    def body(a_ref, b_ref, o_ref, *narrow):
        res = _dot(a_ref[...].astype(BF16), b_ref[...].astype(BF16), TN)
        o_ref[...] = res.reshape(o_ref.shape)
        for n_ref in narrow:
            n_ref[...] = res.astype(BF16).reshape(n_ref.shape)

    if shard_axis is None:
        out_shape, out_spec = (m, n), pl.BlockSpec((tm, tn), lambda i, j: (i, j))
    elif shard_axis == 0:
        per = m // N_CHIPS // tm
        out_shape, out_spec = (N_CHIPS, m // N_CHIPS, n), pl.BlockSpec((1, tm, tn), lambda i, j: (i // per, i % per, j))
    else:
        per = n // N_CHIPS // tn
        out_shape, out_spec = (N_CHIPS, m, n // N_CHIPS), pl.BlockSpec((1, tm, tn), lambda i, j: (j // per, i, j % per))
    both = shard_axis is not None
    a_spec = _resident((s, tm)) if tm == m else pl.BlockSpec((s, tm), lambda i, j: (0, i))
    b_spec = _resident((s, tn)) if tn == n else pl.BlockSpec((s, tn), lambda i, j: (0, j))
    return pl.pallas_call(
        body, name=name, grid=(m // tm, n // tn),
        in_specs=[a_spec, b_spec],
        out_specs=[out_spec, out_spec] if both else out_spec,
        out_shape=[jax.ShapeDtypeStruct(out_shape, F32), jax.ShapeDtypeStruct(out_shape, BF16)] if both
        else jax.ShapeDtypeStruct(out_shape, F32),
        compiler_params=_cparams(("arbitrary", "arbitrary")),
    )(a, b)


def _merge_bwd(dx1, osb, oswa, gates, wsb, wswa, rest):
    s = dx1.shape[0]
    tm = min(512, s)

    def body(dx_ref, osb_ref, oswa_ref, gate_ref, wsb_ref, wswa_ref, wout_ref,
             dosb_ref, doswa_ref, dgl_ref, merged_ref, dysb_ref, dyswa_ref, dxb_ref):
        dxb = dx_ref[...].astype(BF16)
        dxb_ref[...] = dxb
        dm = jnp.concatenate([_dot(dxb, wout_ref[q], NT) for q in range(N_CHIPS)], axis=1)
        ysb = _dot(osb_ref[...].astype(BF16), wsb_ref[...], NN)
        yswa = _dot(oswa_ref[...], wswa_ref[...], NN)
        g0 = gate_ref[:, :D_MODEL]
        g1 = gate_ref[:, D_MODEL:]
        merged_ref[...] = (g0 * ysb + g1 * yswa).astype(BF16)
        dgl_ref[:, :D_MODEL] = (dm * ysb * (g0 * (1.0 - g0))).astype(BF16)
        dgl_ref[:, D_MODEL:] = (dm * yswa * (g1 * (1.0 - g1))).astype(BF16)
        dysb = (dm * g0).astype(BF16)
        dyswa = (dm * g1).astype(BF16)
        dysb_ref[...] = dysb
        dyswa_ref[...] = dyswa
        dosb_ref[...] = _dot(dysb, wsb_ref[...], NT)
        doswa_ref[...] = _dot(dyswa, wswa_ref[...], NT)

    return pl.pallas_call(
        body, name="merge_bwd", grid=(s // tm,),
        in_specs=[_rows(tm, D_MODEL), _rows(tm, SB_WIDTH), _rows(tm, SWA_Q_WIDTH), _rows(tm, 2 * D_MODEL),
                  _resident((SB_WIDTH, D_MODEL)), _resident((SWA_Q_WIDTH, D_MODEL)), _resident(*W_OUT_BLOCK)],
        out_specs=[_rows(tm, SB_WIDTH), _rows(tm, SWA_Q_WIDTH), _rows(tm, 2 * D_MODEL), _rows(tm, D_MODEL),
                   _rows(tm, D_MODEL), _rows(tm, D_MODEL), _rows(tm, D_MODEL)],
        out_shape=[jax.ShapeDtypeStruct((s, SB_WIDTH), F32), jax.ShapeDtypeStruct((s, SWA_Q_WIDTH), F32),
                   jax.ShapeDtypeStruct((s, 2 * D_MODEL), BF16), jax.ShapeDtypeStruct((s, D_MODEL), BF16),
                   jax.ShapeDtypeStruct((s, D_MODEL), BF16), jax.ShapeDtypeStruct((s, D_MODEL), BF16),
                   jax.ShapeDtypeStruct((s, D_MODEL), BF16)],
        compiler_params=_cparams(("arbitrary",)),
    )(dx1, osb, oswa, gates, wsb, wswa, rest)


def _sb_bwd(sbqkv, osb, dosb, send):
    s = sbqkv.shape[0]
    t = min(SB_TILE, s)
    nq = s // t
    ng = SB_WIDTH // SB_STEP_WIDTH
    heads = range(2 * SB_STEP_WIDTH // LANES)
    ns = len(send)

    def body(*refs):
        q_ref, k_ref, v_ref, o_ref, do_ref = refs[:5]
        srcs = refs[5:5 + ns]
        dq_ref, dk_ref, dv_ref = refs[5 + ns:8 + ns]
        outs = refs[8 + ns:8 + 2 * ns]
        j, i = pl.program_id(0), pl.program_id(1)
        if ns:
            ssem, rsem = refs[8 + 2 * ns:]
            _hosted_exchange(srcs, outs, ssem, rsem, (j == 0) & (i == 0), (j == ng - 1) & (i == nq - 1))

        @pl.when(i == 0)
        def _():
            dk_ref[...] = jnp.zeros_like(dk_ref)
            dv_ref[...] = jnp.zeros_like(dv_ref)

        left = _lane((1, LANES)) < HEAD_DIM
        halves = (left, jnp.logical_not(left))
        pair = lambda a, p: a[:, (p // 2) * LANES:(p // 2 + 1) * LANES]
        only = lambda a, p: jnp.where(halves[p % 2], pair(a, p), jnp.zeros((), a.dtype))
        scale = jnp.asarray(SCALE, BF16)
        q_all = q_ref[...] * scale
        do_all = do_ref[...].astype(BF16)
        prod = do_all.astype(F32) * o_ref[...]
        qh = [only(q_all, p) for p in heads]
        doh = [only(do_all, p) for p in heads]
        delta = [jnp.sum(only(prod, p), axis=-1, keepdims=True) for p in heads]
        r = lax.broadcasted_iota(jnp.int32, (t, t), 0)
        c = lax.broadcasted_iota(jnp.int32, (t, t), 1)
        u_from = (r >= c).astype(BF16)
        u_from2 = _tri_twice(t)

        def tile(n, carry, strict):
            runs, dqs = carry
            off = pl.multiple_of((i - n) * t, t)
            k = k_ref[pl.ds(off, t), :]
            v = v_ref[pl.ds(off, t), :]
            ks = k * scale
            terms = [_sb_tile_terms(qh[p], pair(k, p), strict) for p in heads]
            logw = [_log_weights(*terms[p], runs[p][0], u_from) for p in heads]
            dws = [_dot(doh[p], pair(v, p), NT) for p in heads]
            wbs = [_masked(strict, jnp.exp(logw[p][0])).astype(BF16) for p in heads]
            es = [dws[p] * wbs[p].astype(F32) for p in heads]
            rests = [runs[p][1] + _split_dot(es[p], u_from2) for p in heads]
            betas = [jnp.exp(terms[p][0] + terms[p][1]) for p in heads]
            dzs = [_masked(strict, es[p] - betas[p] * (es[p] - rests[p])).astype(BF16) for p in heads]
            dqs = list(dqs)
            for g in range(len(heads) // 2):
                a, b = 2 * g, 2 * g + 1
                cols = slice(g * LANES, (g + 1) * LANES)
                dv_ref[pl.ds(off, t), cols] += _dot(wbs[a], doh[a], TN) + _dot(wbs[b], doh[b], TN)
                dk_ref[pl.ds(off, t), cols] += _dot(dzs[a], qh[a], TN) + _dot(dzs[b], qh[b], TN)
                dqs[g] = dqs[g] + _dot(dzs[a], only(ks, a), NN) + _dot(dzs[b], only(ks, b), NN)
            new_runs = tuple((logw[p][1], rests[p][:, 0:1]) for p in heads)
            return new_runs, tuple(dqs)

        zero = jnp.zeros((t, 1), F32)
        wide = jnp.zeros((t, LANES), F32)
        carry = tile(0, (tuple((zero, -delta[p]) for p in heads), tuple(wide for _ in heads[::2])), c < r)
        _, dqs = lax.fori_loop(1, i + 1, lambda n, cr: tile(n, cr, None), carry)
        for g, dq in enumerate(dqs):
            dq_ref[:, g * LANES:(g + 1) * LANES] = dq.astype(BF16)

    w = SB_STEP_WIDTH
    blk = pl.BlockSpec((t, w), lambda j, i: (i, j))
    whole = pl.BlockSpec((s, w), lambda j, i: (0, j))
    return pl.pallas_call(
        body, name="sb_bwd_x%d" % ns, grid=(ng, nq),
        in_specs=[blk, pl.BlockSpec((s, w), lambda j, i: (0, ng + j)),
                  pl.BlockSpec((s, w), lambda j, i: (0, 2 * ng + j)), blk, blk] + [HBM] * ns,
        out_specs=[blk, whole, whole] + [HBM] * ns,
        out_shape=[jax.ShapeDtypeStruct((s, SB_WIDTH), BF16), jax.ShapeDtypeStruct((s, SB_WIDTH), F32),
                   jax.ShapeDtypeStruct((s, SB_WIDTH), F32)] + [jax.ShapeDtypeStruct(a.shape, a.dtype) for a in send],
        scratch_shapes=_exchange_scratch(ns) if ns else [],
        compiler_params=_cparams(("arbitrary", "arbitrary")),
    )(sbqkv, sbqkv, sbqkv, osb, dosb, *send)


def _swa_bwd(qn, kn, v, sinks, do):
    s = qn.shape[0]
    t = min(512, s)
    nb = t // SWA_BLOCK
    nheads = SWA_Q_WIDTH // HEAD_DIM

    def body(sink_ref, q_ref, kc_ref, kp_ref, vc_ref, vp_ref, do_ref, dq_ref, dk_ref, dv_ref, dsink_ref):
        i = pl.program_id(0)

        @pl.when(i == 0)
        def _():
            dk_ref[...] = jnp.zeros_like(dk_ref)
            dv_ref[...] = jnp.zeros_like(dv_ref)

        left = _lane((1, LANES)) < HEAD_DIM
        halves = (left, jnp.logical_not(left))
        dsink = [jnp.zeros((1, 1), F32) for _ in range(nheads)]
        for b in range(nb):
            kvar = _swa_variants(_swa_band(kc_ref, kp_ref, b), left)
            vvar = _swa_variants(_swa_band(vc_ref, vp_ref, b), left)
            rows = slice(b * SWA_BLOCK, (b + 1) * SWA_BLOCK)
            valid = _swa_valid(i * nb + b)
            heads = _swa_heads(q_ref, rows, halves)
            doh = []
            for hb, p, _, _ in heads:
                do2 = do_ref[rows, hb * LANES:(hb + 1) * LANES]
                doh.append(jnp.where(halves[p], do2, 0.0).astype(BF16))
            dots = [_dot(qh, kvar[g][p], NT) for _, p, g, qh in heads]
            dps = [_dot(doh[h], vvar[g][p], NT) for h, (_, p, g, _) in enumerate(heads)]
            dss, pbs = [], []
            for h in range(nheads):
                probs, psink = _swa_softmax(dots[h], sink_ref[h], valid)
                delta = jnp.sum(probs * dps[h], axis=-1, keepdims=True)
                dss.append((probs * (dps[h] - delta) * SCALE).astype(BF16))
                pbs.append(probs.astype(BF16))
                dsink[h] = dsink[h] - jnp.sum(psink * delta, axis=0, keepdims=True)
            dk_acc = [jnp.zeros((2 * SWA_BLOCK, LANES), F32) for _ in range(2)]
            dv_acc = [jnp.zeros((2 * SWA_BLOCK, LANES), F32) for _ in range(2)]
            dqs = [_dot(dss[h], kvar[g][p], NN) for h, (_, p, g, _) in enumerate(heads)]
            for h, (_, p, g, qh) in enumerate(heads):
                which = 0 if p == g else 1
                dk_acc[which] = dk_acc[which] + _dot(dss[h], qh, TN)
                dv_acc[which] = dv_acc[which] + _dot(pbs[h], doh[h], TN)
            for hb in range(nheads // 2):
                dq_ref[rows, hb * LANES:(hb + 1) * LANES] = dqs[2 * hb] + dqs[2 * hb + 1]
            dkb = dk_acc[0] + pltpu.roll(dk_acc[1], HEAD_DIM, 1)
            dvb = dv_acc[0] + pltpu.roll(dv_acc[1], HEAD_DIM, 1)
            start = pl.multiple_of((i * nb + b) * SWA_BLOCK, SWA_BLOCK)
            dk_ref[pl.ds(start, SWA_BLOCK), :] += dkb[SWA_BLOCK:]
            dv_ref[pl.ds(start, SWA_BLOCK), :] += dvb[SWA_BLOCK:]

            @pl.when(i * nb + b > 0)
            def _(dkb=dkb, dvb=dvb, start=start):
                before = pl.multiple_of(jnp.maximum(start - SWA_BLOCK, 0), SWA_BLOCK)
                dk_ref[pl.ds(before, SWA_BLOCK), :] += dkb[:SWA_BLOCK]
                dv_ref[pl.ds(before, SWA_BLOCK), :] += dvb[:SWA_BLOCK]

        for h in range(nheads):
            dsink_ref[0, h:h + 1, :] = jnp.broadcast_to(dsink[h], (1, LANES))

    cur, prev = _swa_specs(s, t)
    whole = pl.BlockSpec((s, LANES), lambda i: (0, 0))
    return pl.pallas_call(
        body, name="swa_bwd", grid=(s // t,),
        in_specs=[pl.BlockSpec(memory_space=pltpu.SMEM), _rows(t, SWA_Q_WIDTH), cur, prev, cur, prev,
                  _rows(t, SWA_Q_WIDTH)],
        out_specs=[_rows(t, SWA_Q_WIDTH), whole, whole, pl.BlockSpec((1, 8, LANES), lambda i: (i, 0, 0))],
        out_shape=[jax.ShapeDtypeStruct((s, SWA_Q_WIDTH), F32), jax.ShapeDtypeStruct((s, SWA_KV_WIDTH), F32),
                   jax.ShapeDtypeStruct((s, SWA_KV_WIDTH), F32), jax.ShapeDtypeStruct((s // t, 8, LANES), F32)],
        compiler_params=_cparams(("arbitrary",)),
    )(sinks, qn, kn, kn, v, v, do)


def _swa_post(raw, dqn, dkn, qg, kg, cos, sin):
    s = raw.shape[0]
    tm = min(512, s)
    nq = SWA_Q_WIDTH // LANES

    def body(raw_ref, dq_ref, dk_ref, qg_ref, kg_ref, cos_ref, sin_ref, out_ref, dg_ref):
        @pl.when(pl.program_id(0) == 0)
        def _():
            dg_ref[...] = jnp.zeros_like(dg_ref)

        cs, sn = cos_ref[...], sin_ref[...]
        dgq = jnp.zeros((1, LANES), F32)
        for b in range(nq):
            cols = slice(b * LANES, (b + 1) * LANES)
            dp, dg = _norm_rope_bwd(raw_ref[:, cols], dq_ref[:, cols], qg_ref[...], cs, sn)
            out_ref[:, cols] = dp.astype(BF16)
            dgq = dgq + dg
        cols = slice(SWA_Q_WIDTH, SWA_Q_WIDTH + LANES)
        dp, dgk = _norm_rope_bwd(raw_ref[:, cols], dk_ref[...], kg_ref[...], cs, sn)
        out_ref[:, cols] = dp.astype(BF16)
        dg_ref[0:1, :] += dgq
        dg_ref[1:2, :] += dgk

    return pl.pallas_call(
        body, name="swa_post", grid=(s // tm,),
        in_specs=[_rows(tm, 640), _rows(tm, SWA_Q_WIDTH), _rows(tm, LANES), _resident((1, LANES)),
                  _resident((1, LANES)), _rows(tm, LANES), _rows(tm, LANES)],
        out_specs=[_rows(tm, 640), pl.BlockSpec((8, LANES), lambda i: (0, 0))],
        out_shape=[jax.ShapeDtypeStruct((s, 640), BF16), jax.ShapeDtypeStruct((8, LANES), F32)],
        compiler_params=_cparams(("arbitrary",)),
    )(raw, dqn, dkn, qg, kg, cos, sin)


def _inproj_bwd(dsbq, dsbk, dsbv, dswqk, dswv, dgl, x, dx1, g, w):
    s = x.shape[0]
    tm = min(512, s)

    def body(dsbq_ref, dsbk_ref, dsbv_ref, dswqk_ref, dswv_ref, dgl_ref, x_ref, dx1_ref, g_ref, w_ref,
             dx_ref, dproj_ref, dg_ref):
        @pl.when(pl.program_id(0) == 0)
        def _():
            dg_ref[...] = jnp.zeros_like(dg_ref)

        dproj_ref[:, 0:512] = dsbq_ref[...]
        dproj_ref[:, 512:1024] = dsbk_ref[...].astype(BF16)
        dproj_ref[:, 1024:1536] = dsbv_ref[...].astype(BF16)
        dproj_ref[:, 1536:2176] = dswqk_ref[...]
        dproj_ref[:, 2176:2304] = dswv_ref[...].astype(BF16)
        dproj_ref[:, GATE_OFF:IN_WIDTH] = dgl_ref[...]
        dh = jnp.zeros((tm, D_MODEL), F32)
        for a in range(0, IN_WIDTH, 512):
            b = min(a + 512, IN_WIDTH)
            dh = dh + _dot(dproj_ref[:, a:b], w_ref[:, a:b], NT)
        gam = g_ref[...]
        xn, rstd = _rms(x_ref[...])
        dg_ref[...] += jnp.sum(dh * xn, axis=0, keepdims=True)
        dx_ref[...] = dx1_ref[...] + _rms_bwd(dh, xn, rstd, gam)

    return pl.pallas_call(
        body, name="inproj_bwd", grid=(s // tm,),
        in_specs=[_rows(tm, 512), _rows(tm, 512), _rows(tm, 512), _rows(tm, 640), _rows(tm, LANES),
                  _rows(tm, 2 * D_MODEL), _rows(tm, D_MODEL), _rows(tm, D_MODEL), _resident((1, D_MODEL)),
                  _resident((D_MODEL, IN_WIDTH))],
        out_specs=[_rows(tm, D_MODEL), _rows(tm, IN_WIDTH), pl.BlockSpec((1, D_MODEL), lambda i: (0, 0))],
        out_shape=[jax.ShapeDtypeStruct((s, D_MODEL), F32), jax.ShapeDtypeStruct((s, IN_WIDTH), BF16),
                   jax.ShapeDtypeStruct((1, D_MODEL), F32)],
        compiler_params=_cparams(("arbitrary",)),
    )(dsbq, dsbk, dsbv, dswqk, dswv, dgl, x, dx1, g, w)


def _gather_weights(shard):
    def body(src, out, ssem, rsem, lsem):
        once = pl.program_id(0) == 0
        _hosted_gather(src, out, ssem, rsem, lsem, once, once, once)

    return pl.pallas_call(
        body, name="gather_weights", grid=(1,), in_specs=[HBM], out_specs=HBM,
        out_shape=jax.ShapeDtypeStruct((N_CHIPS,) + shard.shape, shard.dtype),
        scratch_shapes=_gather_scratch(1),
        compiler_params=pltpu.CompilerParams(dimension_semantics=("arbitrary",), has_side_effects=True),
    )(shard)


def _pair_swap(arrs, send=()):
    n, ns = len(arrs), len(send)

    def body(*refs):
        mine, src = refs[:n], refs[n:n + ns]
        theirs, landed = refs[n + ns:2 * n + ns], refs[2 * n + ns:2 * (n + ns)]
        ssem, rsem = refs[2 * (n + ns):2 * (n + ns) + 2]
        once, never = pl.program_id(0) == 0, pl.program_id(0) < 0
        if ns:
            _hosted_exchange(src, landed, *refs[2 * (n + ns) + 2:], once, never)
        x, y, c, _ = _place()
        cps = [pltpu.make_async_remote_copy(mine[t], theirs[t], ssem.at[t], rsem.at[t],
                                            device_id=(x, y, 1 - c), device_id_type=MESH) for t in range(n)]
        for cp in cps:
            cp.start()
        for cp in cps:
            cp.wait()
        if ns:
            _hosted_exchange(src, landed, *refs[2 * (n + ns) + 2:], never, once)

    return pl.pallas_call(
        body, name="pair_swap_x%d" % ns, grid=(1,), in_specs=[HBM] * (n + ns), out_specs=[HBM] * (n + ns),
        out_shape=[jax.ShapeDtypeStruct(a.shape, a.dtype) for a in tuple(arrs) + tuple(send)],
        scratch_shapes=[pltpu.SemaphoreType.DMA((n,)), pltpu.SemaphoreType.DMA((n,))] + (_exchange_scratch(ns) if ns else []),
        compiler_params=pltpu.CompilerParams(dimension_semantics=("arbitrary",), has_side_effects=True),
    )(*arrs, *send)


def _allreduce_small(block):
    def body(src, out, buf, ssem, rsem):
        x, y, c, _ = _place()
        me = 4 * x + 2 * y + c
        buf[me] = src[...]
        cps = []
        for k in range(1, N_DEV):
            peer = (x ^ (k >> 2), y ^ ((k >> 1) & 1), c ^ (k & 1))
            cp = pltpu.make_async_remote_copy(src, buf.at[me], ssem.at[k - 1], rsem.at[k - 1], device_id=peer, device_id_type=MESH)
            cp.start()
            cps.append(cp)
        for k in range(1, N_DEV):
            got = buf.at[me ^ k]
            pltpu.make_async_remote_copy(got, got, ssem.at[k - 1], rsem.at[k - 1], device_id=(x, y, c), device_id_type=MESH).wait_recv()
        for cp in cps:
            cp.wait_send()
        tot = buf[0]
        for d in range(1, N_DEV):
            tot = tot + buf[d]
        out[...] = tot

    vm = pl.BlockSpec(memory_space=pltpu.VMEM)
    return pl.pallas_call(
        body, name="allreduce_small", in_specs=[vm], out_specs=vm,
        out_shape=jax.ShapeDtypeStruct(block.shape, F32),
        scratch_shapes=[pltpu.VMEM((N_DEV,) + block.shape, F32), pltpu.SemaphoreType.DMA((N_DEV - 1,)),
                        pltpu.SemaphoreType.DMA((N_DEV - 1,))],
        compiler_params=pltpu.CompilerParams(has_side_effects=True),
    )(block)


def _sum_chips(landed, own, into, layer, name):
    nq, k, n = landed.shape
    tr = min(256, k)
    me = (2 * lax.axis_index("x") + lax.axis_index("y")).astype(jnp.int32).reshape(1)

    def body(me_ref, p0, p1, p2, p3, own_ref, into_ref, o_ref):
        mine = own_ref[0]
        terms = [jnp.where(me_ref[0] == q, mine, p[0].astype(F32)) for q, p in enumerate((p0, p1, p2, p3))]
        o_ref[0] = ((terms[0] + terms[1]) + terms[2]) + terms[3]

    spec = lambda q: pl.BlockSpec((1, tr, n), lambda i, m, q=q: (q, i, 0))
    return pl.pallas_call(
        body, name=name,
        grid_spec=pltpu.PrefetchScalarGridSpec(
            num_scalar_prefetch=1, grid=(k // tr,),
            in_specs=[spec(q) for q in range(nq)] + [pl.BlockSpec((1, tr, n), lambda i, m: (m[0], i, 0)), HBM],
            out_specs=pl.BlockSpec((1, tr, n), lambda i, m: (layer, i, 0))),
        out_shape=jax.ShapeDtypeStruct(into.shape, F32),
        input_output_aliases={6: 0},
        compiler_params=_cparams(("arbitrary",)),
    )(me, landed, landed, landed, landed, own, into)


def _adamw(w, g, g2, m, v, name):
    shape = w.shape
    cols = shape[-1]
    flat = lambda t: t.reshape(-1, cols)
    rows = flat(w).shape[0]
    tr = min(512, rows)
    pair = g2 is not None

    def body(*refs):
        if pair:
            w_ref, g_ref, g2_ref, m_ref, v_ref, go_ref, d_ref, nm_ref, nv_ref = refs
            gr = g_ref[...] + g2_ref[...]
        else:
            w_ref, g_ref, m_ref, v_ref, go_ref, d_ref, nm_ref, nv_ref = refs
            gr = g_ref[...]
        go_ref[...] = gr
        nm = ADAM_B1 * m_ref[...] + (1.0 - ADAM_B1) * gr
        nv = ADAM_B2 * v_ref[...] + (1.0 - ADAM_B2) * (gr * gr)
        m_hat = nm / (1.0 - ADAM_B1 ** ADAM_STEP)
        v_hat = nv / (1.0 - ADAM_B2 ** ADAM_STEP)
        d_ref[...] = -ADAM_LR * (m_hat / (jnp.sqrt(v_hat) + ADAM_EPS) + ADAM_WD * w_ref[...])
        nm_ref[...] = nm
        nv_ref[...] = nv

    spec = pl.BlockSpec((tr, cols), lambda i: (i, 0))
    ins = [w, g] + ([g2] if pair else []) + [m, v]
    outs = pl.pallas_call(
        body, name=name, grid=(rows // tr,),
        in_specs=[spec] * len(ins), out_specs=[spec] * 4,
        out_shape=[jax.ShapeDtypeStruct((rows, cols), F32)] * 4,
        compiler_params=_cparams(("arbitrary",)),
    )(*[flat(t) for t in ins])
    return [o.reshape(shape) for o in outs]


def _pack_rest(big, l):
    rows = dict(PACK_ROWS)
    return jnp.concatenate([big[name][l].astype(BF16).reshape(rows[name], 1024) for name in REST], axis=0)


def _whole_columns(t):
    return jnp.moveaxis(t, 0, 1).reshape(t.shape[1], N_CHIPS * t.shape[2])


def _branch_weights(rest):
    out, at = [], REST_ROWS - 256
    for _ in range(2):
        out.append(_whole_columns(rest[:, at:at + 128, :].reshape((N_CHIPS,) + SHARD_SHAPES["w_branch_sb"])))
        at += 128
    return out


def _pack_small(d):
    flat = jnp.concatenate([d[n].reshape(-1) for n in SMALL_NAMES])
    return jnp.pad(flat, (0, SMALL_ROWS * LANES - flat.shape[0])).reshape(SMALL_ROWS, LANES)


def _unpack_small(block, like):
    flat, out, at = block.reshape(-1), {}, 0
    for n in SMALL_NAMES:
        size = like[n].size
        out[n] = flat[at:at + size].reshape(like[n].shape)
        at += size
    return out


def _rope_tables(s):
    inv_freq = 1.0 / (ROPE_THETA ** (jnp.arange(0, HEAD_DIM, 2, dtype=F32) / HEAD_DIM))
    ang = jnp.arange(s, dtype=F32)[:, None] * inv_freq[None, :]
    reps = LANES // (HEAD_DIM // 2)
    return jnp.tile(jnp.cos(ang), (1, reps)), jnp.tile(jnp.sin(ang), (1, reps))


def _forward_backward(x, target, big, small):
    s = x.shape[0]
    cos, sin = _rope_tables(s)
    two = lambda gvec: jnp.tile(gvec.reshape(1, HEAD_DIM), (1, 2))
    saved = []
    win_shard = lambda l: big["w_in"][l].astype(BF16)
    win = _whole_columns(_gather_weights(win_shard(0)))
    for l in range(DEPTH):
        gm = small["mix_norm_g"][l].reshape(1, D_MODEL)
        gl = small["mlp_norm_g"][l].reshape(1, D_MODEL)
        qg, kg = two(small["q_norm_g"][l]), two(small["k_norm_g"][l])
        h, sbqkv, raw, qn, kn, v, gates = _inproj_fwd(x, gm, win, qg, kg, cos, sin)
        osb, rest, *nxt = _sb_fwd(sbqkv, [_pack_rest(big, l)] + ([win_shard(l + 1)] if l + 1 < DEPTH else []))
        wsb, wswa = _branch_weights(rest)
        oswa = _swa_fwd(qn, kn, v, small["sinks"][l])
        x1 = _merge_fwd(x, osb, oswa, gates, wsb, wswa, rest)
        x2, u = _mlp_fwd(x1, gl, rest)
        saved.append((x, h, sbqkv, raw, qn, kn, v, gates, osb, oswa, x1, u, gm, gl, qg, kg, win, wsb, wswa, rest))
        x, win = x2, (_whole_columns(nxt[0]) if nxt else None)
    dx, loss = _loss_grad(x, target)

    got = {name: [None] * DEPTH for name in BIG}
    gsmall = {name: [None] * DEPTH for name in SMALL_NAMES}
    late = []
    for l in reversed(range(DEPTH)):
        x0, h, sbqkv, raw, qn, kn, v, gates, osb, oswa, x1, u, gm, gl, qg, kg, win, wsb, wswa, rest = saved[l]
        dx1, du, act, h2, dgl_mlp, dxb = _mlp_bwd(dx, x1, u, gl, rest)
        dosb, doswa, dgate, merged, dysb, dyswa, dx1b = _merge_bwd(dx1, osb, oswa, gates, wsb, wswa, rest)
        ready = [("w_down", l, *_wgrad(act, dxb, "wgrad_down", 512, 1024, shard_axis=0)),
                 ("w_up", l, *_wgrad(h2, du, "wgrad_up", 1024, 512, shard_axis=1)),
                 ("w_out", l, *_wgrad(merged, dx1b, "wgrad_out", 256, 1024, shard_axis=0)),
                 ("w_branch_sb", l, *_wgrad(osb, dysb, "wgrad_bsb", 512, 256, shard_axis=1)),
                 ("w_branch_swa", l, *_wgrad(oswa, dyswa, "wgrad_bswa", 512, 256, shard_axis=1))] + late
        dsbq, dsbk, dsbv, *landed = _sb_bwd(sbqkv, osb, dosb, [narrow for _, _, _, narrow in ready])
        for (name, layer, own, _), arr in zip(ready, landed):
            got[name][layer] = (arr, own)
        dqn, dkn, dswv, dsink = _swa_bwd(qn, kn, v, small["sinks"][l], doswa)
        dswqk, dgqk = _swa_post(raw, dqn, dkn, qg, kg, cos, sin)
        dx, dproj, dg_mix = _inproj_bwd(dsbq, dsbk, dsbv, dswqk, dswv, dgate, x0, dx1, gm, win)
        dwin = _wgrad(h, dproj, "wgrad_in", 1024, 256)
        dwin = dwin.reshape(D_MODEL, N_CHIPS, IN_WIDTH // N_CHIPS).transpose(1, 0, 2)
        late = [("w_in", l, dwin, dwin.astype(BF16))]
        gsmall["mix_norm_g"][l] = dg_mix[0]
        gsmall["mlp_norm_g"][l] = dgl_mlp[0]
        gsmall["q_norm_g"][l] = dgqk[0, :HEAD_DIM] + dgqk[0, HEAD_DIM:]
        gsmall["k_norm_g"][l] = dgqk[1, :HEAD_DIM] + dgqk[1, HEAD_DIM:]
        gsmall["sinks"][l] = jnp.sum(dsink[:, :, 0], axis=0)
    gsmall = {k: jnp.stack(vs) for k, vs in gsmall.items()}
    return loss, dx, got, late[0], gsmall


def kernel(x, mix_norm_g, w_in, q_norm_g, k_norm_g, sinks, w_branch_sb, w_branch_swa, w_out, mlp_norm_g, w_up, w_down, loss_target, m_mix_norm_g, m_w_in, m_q_norm_g, m_k_norm_g, m_sinks, m_w_branch_sb, m_w_branch_swa, m_w_out, m_mlp_norm_g, m_w_up, m_w_down, v_mix_norm_g, v_w_in, v_q_norm_g, v_k_norm_g, v_sinks, v_w_branch_sb, v_w_branch_swa, v_w_out, v_mlp_norm_g, v_w_up, v_w_down):
    big = dict(w_in=w_in, w_branch_sb=w_branch_sb, w_branch_swa=w_branch_swa, w_out=w_out, w_up=w_up, w_down=w_down)
    big_m = dict(w_in=m_w_in, w_branch_sb=m_w_branch_sb, w_branch_swa=m_w_branch_swa, w_out=m_w_out, w_up=m_w_up, w_down=m_w_down)
    big_v = dict(w_in=v_w_in, w_branch_sb=v_w_branch_sb, w_branch_swa=v_w_branch_swa, w_out=v_w_out, w_up=v_w_up, w_down=v_w_down)
    small = dict(mix_norm_g=mix_norm_g, q_norm_g=q_norm_g, k_norm_g=k_norm_g, sinks=sinks, mlp_norm_g=mlp_norm_g)
    small_m = dict(mix_norm_g=m_mix_norm_g, q_norm_g=m_q_norm_g, k_norm_g=m_k_norm_g, sinks=m_sinks, mlp_norm_g=m_mlp_norm_g)
    small_v = dict(mix_norm_g=v_mix_norm_g, q_norm_g=v_q_norm_g, k_norm_g=v_k_norm_g, sinks=v_sinks, mlp_norm_g=v_mlp_norm_g)

    loss_part, grad_x, got, last, gsmall = _forward_backward(x[0], loss_target[0], big, small)
    loss = lax.psum(loss_part, ("x", "y", "c"))

    mine = {}
    for name in BIG:
        tot = lax.empty((DEPTH,) + SHARD_SHAPES[name], F32)
        for l in range(DEPTH):
            if got[name][l] is not None:
                tot = _sum_chips(*got[name][l], tot, l, "sum_" + name)
        mine[name] = tot
    name, layer, own, narrow = last
    *theirs, landed = _pair_swap([mine[n] for n in REST], [narrow])
    theirs = dict(zip(REST, theirs))
    mine[name] = _sum_chips(landed, own, mine[name], layer, "sum_" + name)
    (theirs[name],) = _pair_swap([mine[name]])
    upd = {n: _adamw(big[n], mine[n], theirs[n], big_m[n], big_v[n], "adamw_" + n) for n in BIG}
    g_small = _allreduce_small(_pack_small(gsmall))
    sm = _adamw(_pack_small(small), g_small, None, _pack_small(small_m), _pack_small(small_v), "adamw_small")
    upd_small = [_unpack_small(t, small) for t in sm]

    names = ("mix_norm_g", "w_in", "q_norm_g", "k_norm_g", "sinks", "w_branch_sb", "w_branch_swa", "w_out",
             "mlp_norm_g", "w_up", "w_down")
    pick = lambda n, i: upd[n][i] if n in upd else upd_small[i][n]
    return (loss, grad_x[None], *[pick(n, 0) for n in names], *[pick(n, 1) for n in names],
            *[pick(n, 2) for n in names], *[pick(n, 3) for n in names])
```

```python
import jax
import jax.numpy as jnp
from jax import lax
from jax.experimental import pallas as pl
from jax.experimental.pallas import tpu as pltpu

F32 = jnp.float32
BF16 = jnp.bfloat16

D_MODEL = 1024
DEPTH = 4
HEAD_DIM = 64
SB_WIDTH = 512
SWA_Q_WIDTH = 512
SWA_KV_WIDTH = 128
D_FF = 4096
IN_WIDTH = 4352
GATE_OFF = 2304
ROPE_THETA = 10000.0
NORM_EPS = 1e-6
SCALE = HEAD_DIM ** -0.5
N_CHIPS = 4
N_DEV = 8

ADAM_LR = 0.001
ADAM_B1 = 0.9
ADAM_B2 = 0.999
ADAM_EPS = 1e-08
ADAM_WD = 0.01
ADAM_STEP = 10

LANES = 128
SB_TILE = 256
SB_STEP_WIDTH = 512
SB_FWD_STEP_WIDTH = 512
SWA_BLOCK = 128
VMEM_LIMIT = 56 << 20

NN = (((1,), (0,)), ((), ()))
NT = (((1,), (1,)), ((), ()))
TN = (((0,), (0,)), ((), ()))
MESH = pl.DeviceIdType.MESH
HBM = pl.BlockSpec(memory_space=pl.ANY)

BIG = ("w_in", "w_branch_sb", "w_branch_swa", "w_out", "w_up", "w_down")
SMALL_NAMES = ("mix_norm_g", "q_norm_g", "k_norm_g", "sinks", "mlp_norm_g")
PACK_ROWS = (("w_in", 1088), ("w_branch_sb", 128), ("w_branch_swa", 128), ("w_out", 256), ("w_up", 1024), ("w_down", 1024))
SHARD_SHAPES = {"w_in": (1024, 1088), "w_branch_sb": (512, 256), "w_branch_swa": (512, 256), "w_out": (256, 1024),
                "w_up": (1024, 1024), "w_down": (1024, 1024)}
ROW_SHARDED = ("w_out", "w_down")
SMALL_ROWS = 72


def _dot(a, b, dims):
    return lax.dot_general(a, b, dims, preferred_element_type=F32)


def _cparams(sem):
    return pltpu.CompilerParams(dimension_semantics=sem, vmem_limit_bytes=VMEM_LIMIT)


def _resident(shape, index=None):
    index = (0,) * len(shape) if index is None else index
    return pl.BlockSpec(shape, lambda *_: index, pipeline_mode=pl.Buffered(1))


REST = ("w_up", "w_down", "w_out", "w_branch_sb", "w_branch_swa")
REST_ROWS = 2560
W_UP_BLOCK = ((N_CHIPS, 1024, 1024), (0, 0, 0))
W_DOWN_BLOCK = ((N_CHIPS, 1024, 1024), (0, 1, 0))
W_OUT_BLOCK = ((N_CHIPS, 256, 1024), (0, 8, 0))


def _rows(tm, width):
    return pl.BlockSpec((tm, width), lambda i: (i, 0))


def _rms(xf):
    rstd = lax.rsqrt(jnp.mean(xf * xf, axis=-1, keepdims=True) + NORM_EPS)
    return xf * rstd, rstd


def _rms_bwd(dh, xn, rstd, g):
    dxn = dh * g
    return rstd * (dxn - xn * jnp.mean(dxn * xn, axis=-1, keepdims=True))


def _lane(shape):
    return lax.broadcasted_iota(jnp.int32, shape, len(shape) - 1)


def _head_mean(v, left):
    sl = jnp.sum(jnp.where(left, v, 0.0), axis=-1, keepdims=True)
    sr = jnp.sum(jnp.where(left, 0.0, v), axis=-1, keepdims=True)
    return jnp.where(left, sl, sr) * (1.0 / HEAD_DIM)


def _rope(y, cs, sn, first):
    up = pltpu.roll(y, 96, 1)
    dn = pltpu.roll(y, 32, 1)
    return y * cs + jnp.where(first, -up, dn) * sn


def _rope_t(d, cs, sn, first):
    t = d * jnp.where(first, -sn, sn)
    return d * cs + jnp.where(first, pltpu.roll(t, 96, 1), pltpu.roll(t, 32, 1))


def _norm_rope(p, g, cs, sn):
    lane = _lane((1, LANES))
    left = lane < HEAD_DIM
    first = (lane % HEAD_DIM) < (HEAD_DIM // 2)
    yn = p * lax.rsqrt(_head_mean(p * p, left) + NORM_EPS)
    return _rope(yn * g, cs, sn, first)


def _norm_rope_bwd(p, dout, g, cs, sn):
    lane = _lane((1, LANES))
    left = lane < HEAD_DIM
    first = (lane % HEAD_DIM) < (HEAD_DIM // 2)
    rstd = lax.rsqrt(_head_mean(p * p, left) + NORM_EPS)
    yn = p * rstd
    dyg = _rope_t(dout, cs, sn, first)
    dg = jnp.sum(dyg * yn, axis=0, keepdims=True)
    dyn = dyg * g
    dp = rstd * (dyn - yn * _head_mean(dyn * yn, left))
    return dp, dg


def _place():
    x, y, c = lax.axis_index("x"), lax.axis_index("y"), lax.axis_index("c")
    return x, y, c, [(1 - x, y), (x, 1 - y), (1 - x, 1 - y)]


def _hosted_gather(src, out, ssem, rsem, lsem, first, mid, last, slot=0):
    x, y, c, chips = _place()
    me = 2 * x + y
    sib = (x, y, 1 - c)
    r2 = src.shape[0] // 2
    base = 6 * slot

    def slab(chip, half):
        return out.at[chip, pl.ds(half * r2, r2)]

    def ici(j):
        cx, cy = chips[j]
        return pltpu.make_async_remote_copy(src.at[pl.ds(c * r2, r2)], slab(me, c), ssem.at[base + j], rsem.at[base + j],
                                            device_id=(cx, cy, c), device_id_type=MESH)

    def landed(j):
        got = slab(2 * chips[j][0] + chips[j][1], c)
        return pltpu.make_async_remote_copy(got, got, ssem.at[base + j], rsem.at[base + j], device_id=sib, device_id_type=MESH)

    def d2d(j, half):
        got = slab(2 * chips[j][0] + chips[j][1], half)
        return pltpu.make_async_remote_copy(got, got, ssem.at[base + 3 + j], rsem.at[base + 3 + j], device_id=sib,
                                            device_id_type=MESH)

    local = pltpu.make_async_copy(src, out.at[me], lsem.at[slot])

    @pl.when(first)
    def _():
        local.start()
        for j in range(3):
            ici(j).start()

    @pl.when(mid)
    def _():
        for j in range(3):
            landed(j).wait_recv()
            d2d(j, c).start()

    @pl.when(last)
    def _():
        for j in range(3):
            d2d(j, 1 - c).wait_recv()
        for j in range(3):
            ici(j).wait_send()
            d2d(j, c).wait_send()
        local.wait()


def _hosted_exchange(srcs, outs, ssem, rsem, first, last):
    x, y, c, chips = _place()
    me = 2 * x + y

    def send(t, j):
        cx, cy = chips[j]
        return pltpu.make_async_remote_copy(srcs[t].at[2 * cx + cy], outs[t].at[me], ssem.at[3 * t + j], rsem.at[3 * t + j],
                                            device_id=(cx, cy, c), device_id_type=MESH)

    def landed(t, j):
        cx, cy = chips[j]
        got = outs[t].at[2 * cx + cy]
        return pltpu.make_async_remote_copy(got, got, ssem.at[3 * t + j], rsem.at[3 * t + j],
                                            device_id=(cx, cy, c), device_id_type=MESH)

    @pl.when(first)
    def _():
        for t in range(len(srcs)):
            for j in range(3):
                send(t, j).start()

    @pl.when(last)
    def _():
        for t in range(len(srcs)):
            for j in range(3):
                landed(t, j).wait_recv()
        for t in range(len(srcs)):
            for j in range(3):
                send(t, j).wait_send()


def _gather_scratch(n):
    return [pltpu.SemaphoreType.DMA((6 * n,)), pltpu.SemaphoreType.DMA((6 * n,)), pltpu.SemaphoreType.DMA((n,))]


def _exchange_scratch(n):
    return [pltpu.SemaphoreType.DMA((3 * n,)), pltpu.SemaphoreType.DMA((3 * n,))]


def _inproj_fwd(x, g, w, qg, kg, cos, sin):
    s = x.shape[0]
    tm = min(512, s)

    def body(x_ref, g_ref, w_ref, qg_ref, kg_ref, cos_ref, sin_ref,
             h_ref, sb_ref, raw_ref, qn_ref, kn_ref, v_ref, gate_ref):
        xn, _ = _rms(x_ref[...])
        h = (xn * g_ref[...]).astype(BF16)
        h_ref[...] = h
        for a in range(0, 3 * SB_WIDTH, 512):
            sb_ref[:, a:a + 512] = _dot(h, w_ref[:, a:a + 512], NN).astype(BF16)
        cs, sn = cos_ref[...], sin_ref[...]
        q0 = 3 * SB_WIDTH
        pq = _dot(h, w_ref[:, q0:q0 + SWA_Q_WIDTH], NN)
        raw_ref[:, 0:SWA_Q_WIDTH] = pq
        for b in range(SWA_Q_WIDTH // LANES):
            blk = pq[:, b * LANES:(b + 1) * LANES]
            qn_ref[:, b * LANES:(b + 1) * LANES] = _norm_rope(blk, qg_ref[...], cs, sn).astype(BF16)
        k0 = q0 + SWA_Q_WIDTH
        pk = _dot(h, w_ref[:, k0:k0 + 2 * SWA_KV_WIDTH], NN)
        raw_ref[:, SWA_Q_WIDTH:SWA_Q_WIDTH + SWA_KV_WIDTH] = pk[:, :SWA_KV_WIDTH]
        kn_ref[...] = _norm_rope(pk[:, :SWA_KV_WIDTH], kg_ref[...], cs, sn).astype(BF16)
        v_ref[...] = pk[:, SWA_KV_WIDTH:].astype(BF16)
        for a in range(GATE_OFF, IN_WIDTH, 512):
            gate_ref[:, a - GATE_OFF:a - GATE_OFF + 512] = jax.nn.sigmoid(_dot(h, w_ref[:, a:a + 512], NN))

    return pl.pallas_call(
        body, name="inproj_fwd", grid=(s // tm,),
        in_specs=[_rows(tm, D_MODEL), _resident((1, D_MODEL)), _resident((D_MODEL, IN_WIDTH)),
                  _resident((1, LANES)), _resident((1, LANES)), _rows(tm, LANES), _rows(tm, LANES)],
        out_specs=[_rows(tm, D_MODEL), _rows(tm, 3 * SB_WIDTH), _rows(tm, 640), _rows(tm, SWA_Q_WIDTH),
                   _rows(tm, SWA_KV_WIDTH), _rows(tm, SWA_KV_WIDTH), _rows(tm, 2 * D_MODEL)],
        out_shape=[jax.ShapeDtypeStruct((s, D_MODEL), BF16), jax.ShapeDtypeStruct((s, 3 * SB_WIDTH), BF16),
                   jax.ShapeDtypeStruct((s, 640), F32), jax.ShapeDtypeStruct((s, SWA_Q_WIDTH), BF16),
                   jax.ShapeDtypeStruct((s, SWA_KV_WIDTH), BF16), jax.ShapeDtypeStruct((s, SWA_KV_WIDTH), BF16),
                   jax.ShapeDtypeStruct((s, 2 * D_MODEL), F32)],
        compiler_params=_cparams(("arbitrary",)),
    )(x, g, w, qg, kg, cos, sin)


def _sb_tile_terms(qh, k, strict):
    z = _dot(qh, k, NT)
    nz = -z
    soft = jnp.log(1.0 + jnp.exp(jnp.minimum(z, nz)))
    log_keep = jnp.minimum(nz, 0.0) - soft
    if strict is not None:
        log_keep = jnp.where(strict, log_keep, 0.0)
    return z, log_keep


def _log_weights(z, log_keep, run, u_from):
    tail = run + _dot(log_keep.astype(BF16), u_from, NN)
    return z + tail, tail[:, 0:1]


def _masked(strict, val):
    return val if strict is None else jnp.where(strict, val, 0.0)


def _tri_twice(t):
    r = lax.broadcasted_iota(jnp.int32, (2 * t, t), 0)
    c = lax.broadcasted_iota(jnp.int32, (2 * t, t), 1)
    return (jnp.where(r >= t, r - t, r) >= c).astype(BF16)


def _split_dot(a, u2):
    hi = a.astype(BF16)
    lo = (a - hi.astype(F32)).astype(BF16)
    return _dot(jnp.concatenate([hi, lo], axis=1), u2, NN)


def _sb_fwd(sbqkv, shards=()):
    s = sbqkv.shape[0]
    t = min(SB_TILE, s)
    nq = s // t
    ng = SB_WIDTH // SB_FWD_STEP_WIDTH
    heads = range(2 * SB_FWD_STEP_WIDTH // LANES)
    nsh = len(shards)

    def body(*refs):
        q_ref, k_ref, v_ref = refs[:3]
        o_ref = refs[3 + nsh]
        j, i = pl.program_id(0), pl.program_id(1)
        for n in range(nsh):
            _hosted_gather(refs[3 + n], refs[4 + nsh + n], *refs[4 + 2 * nsh:], (j == 0) & (i == 0),
                           (j == ng - 1) & (i == (3 * nq) // 4), (j == ng - 1) & (i == nq - 1), slot=n)
        left = _lane((1, LANES)) < HEAD_DIM
        halves = (left, jnp.logical_not(left))
        pair = lambda a, p: a[:, (p // 2) * LANES:(p // 2 + 1) * LANES]
        only = lambda a, p: jnp.where(halves[p % 2], pair(a, p), jnp.zeros((), a.dtype))
        q_all = q_ref[...] * jnp.asarray(SCALE, BF16)
        qh = [only(q_all, p) for p in heads]
        r = lax.broadcasted_iota(jnp.int32, (t, t), 0)
        c = lax.broadcasted_iota(jnp.int32, (t, t), 1)
        u_from = (r >= c).astype(BF16)

        def tile(n, carry, strict):
            runs, accs = carry
            off = pl.multiple_of((i - n) * t, t)
            k = k_ref[pl.ds(off, t), :]
            v = v_ref[pl.ds(off, t), :]
            terms = [_sb_tile_terms(qh[p], pair(k, p), strict) for p in heads]
            logw = [_log_weights(*terms[p], runs[p], u_from) for p in heads]
            ws = [_masked(strict, jnp.exp(logw[p][0])).astype(BF16) for p in heads]
            accs = list(accs)
            for p in heads:
                accs[p // 2] = accs[p // 2] + _dot(ws[p], only(v, p), NN)
            return tuple(logw[p][1] for p in heads), tuple(accs)

        zero = jnp.zeros((t, 1), F32)
        wide = jnp.zeros((t, LANES), F32)
        carry = tile(0, (tuple(zero for _ in heads), tuple(wide for _ in heads[::2])), c < r)
        _, accs = lax.fori_loop(1, i + 1, lambda n, cr: tile(n, cr, None), carry)
        for g, acc in enumerate(accs):
            o_ref[:, g * LANES:(g + 1) * LANES] = acc

    w = SB_FWD_STEP_WIDTH
    blk = pl.BlockSpec((t, w), lambda j, i: (i, j))
    in_specs = [blk, pl.BlockSpec((s, w), lambda j, i: (0, ng + j)), pl.BlockSpec((s, w), lambda j, i: (0, 2 * ng + j))]
    out_specs = [blk]
    out_shape = [jax.ShapeDtypeStruct((s, SB_WIDTH), F32)]
    in_specs += [HBM] * nsh
    out_specs += [HBM] * nsh
    out_shape += [jax.ShapeDtypeStruct((N_CHIPS,) + a.shape, a.dtype) for a in shards]
    return pl.pallas_call(
        body, name="sb_fwd_g%d" % nsh, grid=(ng, nq),
        in_specs=in_specs, out_specs=out_specs, out_shape=out_shape,
        scratch_shapes=_gather_scratch(nsh) if nsh else [],
        compiler_params=_cparams(("arbitrary", "arbitrary")),
    )(sbqkv, sbqkv, sbqkv, *shards)


def _swa_band(cur_ref, prev_ref, b):
    lo = prev_ref[...] if b == 0 else cur_ref[(b - 1) * SWA_BLOCK:b * SWA_BLOCK, :]
    return jnp.concatenate([lo, cur_ref[b * SWA_BLOCK:(b + 1) * SWA_BLOCK, :]], axis=0)


def _swa_variants(band, left):
    f = band.astype(F32)
    sw = pltpu.roll(f, HEAD_DIM, 1)
    halves = (left, jnp.logical_not(left))
    return [[jnp.where(halves[p], f if p == g else sw, 0.0).astype(BF16) for p in range(2)] for g in range(2)]


def _swa_valid(blk):
    ii = lax.broadcasted_iota(jnp.int32, (SWA_BLOCK, 2 * SWA_BLOCK), 0)
    jj = lax.broadcasted_iota(jnp.int32, (SWA_BLOCK, 2 * SWA_BLOCK), 1)
    rel = jj - SWA_BLOCK - ii
    return (rel <= 0) & (rel > -SWA_BLOCK) & (jj + blk * SWA_BLOCK >= SWA_BLOCK)


def _swa_softmax(dots, sink, valid):
    sc = jnp.where(valid, dots * SCALE, -1e30)
    m = jnp.maximum(jnp.max(sc, axis=-1, keepdims=True), sink)
    e = jnp.exp(sc - m)
    es = jnp.exp(sink - m)
    inv = 1.0 / (jnp.sum(e, axis=-1, keepdims=True) + es)
    return e * inv, es * inv


def _swa_heads(q_ref, rows, halves):
    nheads = SWA_Q_WIDTH // HEAD_DIM
    out = []
    for h in range(nheads):
        hb, p = h // 2, h % 2
        q2 = q_ref[rows, hb * LANES:(hb + 1) * LANES]
        out.append((hb, p, h // (nheads // 2), jnp.where(halves[p], q2, jnp.zeros_like(q2))))
    return out


def _swa_specs(s, t):
    nb = t // SWA_BLOCK
    cur = pl.BlockSpec((t, LANES), lambda i: (i, 0))
    prev = pl.BlockSpec((SWA_BLOCK, LANES), lambda i: (jnp.maximum(i * nb - 1, 0), 0))
    return cur, prev


def _swa_fwd(qn, kn, v, sinks):
    s = qn.shape[0]
    t = min(512, s)
    nb = t // SWA_BLOCK
    nheads = SWA_Q_WIDTH // HEAD_DIM

    def body(sink_ref, q_ref, kc_ref, kp_ref, vc_ref, vp_ref, o_ref):
        i = pl.program_id(0)
        left = _lane((1, LANES)) < HEAD_DIM
        halves = (left, jnp.logical_not(left))
        for b in range(nb):
            kvar = _swa_variants(_swa_band(kc_ref, kp_ref, b), left)
            vvar = _swa_variants(_swa_band(vc_ref, vp_ref, b), left)
            rows = slice(b * SWA_BLOCK, (b + 1) * SWA_BLOCK)
            valid = _swa_valid(i * nb + b)
            heads = _swa_heads(q_ref, rows, halves)
            dots = [_dot(qh, kvar[g][p], NT) for _, p, g, qh in heads]
            probs = [_swa_softmax(dots[h], sink_ref[h], valid)[0].astype(BF16) for h in range(nheads)]
            outs = [_dot(probs[h], vvar[g][p], NN) for h, (_, p, g, _) in enumerate(heads)]
            for hb in range(nheads // 2):
                o_ref[rows, hb * LANES:(hb + 1) * LANES] = (outs[2 * hb] + outs[2 * hb + 1]).astype(BF16)

    cur, prev = _swa_specs(s, t)
    return pl.pallas_call(
        body, name="swa_fwd", grid=(s // t,),
        in_specs=[pl.BlockSpec(memory_space=pltpu.SMEM), _rows(t, SWA_Q_WIDTH), cur, prev, cur, prev],
        out_specs=_rows(t, SWA_Q_WIDTH),
        out_shape=jax.ShapeDtypeStruct((s, SWA_Q_WIDTH), BF16),
        compiler_params=_cparams(("arbitrary",)),
    )(sinks, qn, kn, kn, v, v)


def _merge_fwd(x, osb, oswa, gates, wsb, wswa, rest):
    s = x.shape[0]
    tm = min(512, s)
    rc = D_MODEL // N_CHIPS

    def body(x_ref, osb_ref, oswa_ref, gate_ref, wsb_ref, wswa_ref, wout_ref, x1_ref):
        ysb = _dot(osb_ref[...].astype(BF16), wsb_ref[...], NN)
        yswa = _dot(oswa_ref[...], wswa_ref[...], NN)
        merged = (gate_ref[:, :D_MODEL] * ysb + gate_ref[:, D_MODEL:] * yswa).astype(BF16)
        acc = x_ref[...]
        for q in range(N_CHIPS):
            acc = acc + _dot(merged[:, q * rc:(q + 1) * rc], wout_ref[q], NN)
        x1_ref[...] = acc

    return pl.pallas_call(
        body, name="merge_fwd", grid=(s // tm,),
        in_specs=[_rows(tm, D_MODEL), _rows(tm, SB_WIDTH), _rows(tm, SWA_Q_WIDTH), _rows(tm, 2 * D_MODEL),
                  _resident((SB_WIDTH, D_MODEL)), _resident((SWA_Q_WIDTH, D_MODEL)), _resident(*W_OUT_BLOCK)],
        out_specs=_rows(tm, D_MODEL),
        out_shape=jax.ShapeDtypeStruct((s, D_MODEL), F32),
        compiler_params=_cparams(("arbitrary",)),
    )(x, osb, oswa, gates, wsb, wswa, rest)


def _mlp_fwd(x1, g, rest):
    s = x1.shape[0]
    tm = min(512, s)
    fc = D_FF // N_CHIPS

    def body(x_ref, g_ref, wup_ref, wdown_ref, x2_ref, u_ref):
        xf = x_ref[...]
        xn, _ = _rms(xf)
        h2 = (xn * g_ref[...]).astype(BF16)
        acc = xf

        def act(q):
            u = _dot(h2, wup_ref[q], NN)
            u_ref[:, q * fc:(q + 1) * fc] = u
            r = jnp.maximum(u, 0.0)
            return (r * r).astype(BF16)

        ahead = act(0)
        for q in range(N_CHIPS):
            nxt = act(q + 1) if q + 1 < N_CHIPS else None
            acc = acc + _dot(ahead, wdown_ref[q], NN)
            ahead = nxt
        x2_ref[...] = acc

    return pl.pallas_call(
        body, name="mlp_fwd", grid=(s // tm,),
        in_specs=[_rows(tm, D_MODEL), _resident((1, D_MODEL)), _resident(*W_UP_BLOCK), _resident(*W_DOWN_BLOCK)],
        out_specs=[_rows(tm, D_MODEL), _rows(tm, D_FF)],
        out_shape=[jax.ShapeDtypeStruct((s, D_MODEL), F32), jax.ShapeDtypeStruct((s, D_FF), F32)],
        compiler_params=_cparams(("arbitrary",)),
    )(x1, g, rest, rest)


def _loss_grad(y, target):
    s = y.shape[0]
    tm = min(512, s)

    def body(y_ref, t_ref, dy_ref, part_ref):
        err = y_ref[...] - t_ref[...]
        dy_ref[...] = err * (1.0 / D_MODEL)
        tot = jnp.sum(jnp.sum(err * err, axis=-1, keepdims=True), axis=0, keepdims=True)
        part_ref[...] = jnp.broadcast_to(tot.reshape(1, 1, 1), (1, 8, LANES))

    dy, part = pl.pallas_call(
        body, name="loss_grad", grid=(s // tm,),
        in_specs=[_rows(tm, D_MODEL), _rows(tm, D_MODEL)],
        out_specs=[_rows(tm, D_MODEL), pl.BlockSpec((1, 8, LANES), lambda i: (i, 0, 0))],
        out_shape=[jax.ShapeDtypeStruct((s, D_MODEL), F32), jax.ShapeDtypeStruct((s // tm, 8, LANES), F32)],
        compiler_params=_cparams(("arbitrary",)),
    )(y, target)
    return dy, (0.5 / D_MODEL) * jnp.sum(part[:, 0, 0])


def _mlp_bwd(dx2, x1, u, g, rest):
    s = x1.shape[0]
    tm = min(256, s)
    fc = D_FF // N_CHIPS

    def body(dx2_ref, x_ref, u_ref, g_ref, wup_ref, wdown_ref, dx1_ref, du_ref, a_ref, h2_ref, dg_ref, dxb_ref):
        @pl.when(pl.program_id(0) == 0)
        def _():
            dg_ref[...] = jnp.zeros_like(dg_ref)

        gam = g_ref[...]
        xn, rstd = _rms(x_ref[...])
        h2_ref[...] = (xn * gam).astype(BF16)
        dxf = dx2_ref[...]
        dxb = dxf.astype(BF16)
        dxb_ref[...] = dxb
        dh2 = jnp.zeros((tm, D_MODEL), F32)

        def grad_u(q):
            cols = slice(q * fc, (q + 1) * fc)
            da = _dot(dxb, wdown_ref[q], NT)
            r = jnp.maximum(u_ref[:, cols], 0.0)
            a_ref[:, cols] = (r * r).astype(BF16)
            du = (da * (2.0 * r)).astype(BF16)
            du_ref[:, cols] = du
            return du

        ahead = grad_u(0)
        for q in range(N_CHIPS):
            nxt = grad_u(q + 1) if q + 1 < N_CHIPS else None
            dh2 = dh2 + _dot(ahead, wup_ref[q], NT)
            ahead = nxt
        dg_ref[...] += jnp.sum(dh2 * xn, axis=0, keepdims=True)
        dx1_ref[...] = dxf + _rms_bwd(dh2, xn, rstd, gam)

    return pl.pallas_call(
        body, name="mlp_bwd", grid=(s // tm,),
        in_specs=[_rows(tm, D_MODEL), _rows(tm, D_MODEL), _rows(tm, D_FF), _resident((1, D_MODEL)),
                  _resident(*W_UP_BLOCK), _resident(*W_DOWN_BLOCK)],
        out_specs=[_rows(tm, D_MODEL), _rows(tm, D_FF), _rows(tm, D_FF), _rows(tm, D_MODEL),
                   pl.BlockSpec((1, D_MODEL), lambda i: (0, 0)), _rows(tm, D_MODEL)],
        out_shape=[jax.ShapeDtypeStruct((s, D_MODEL), F32), jax.ShapeDtypeStruct((s, D_FF), BF16),
                   jax.ShapeDtypeStruct((s, D_FF), BF16), jax.ShapeDtypeStruct((s, D_MODEL), BF16),
                   jax.ShapeDtypeStruct((1, D_MODEL), F32), jax.ShapeDtypeStruct((s, D_MODEL), BF16)],
        compiler_params=_cparams(("arbitrary",)),
    )(dx2, x1, u, g, rest, rest)


def _wgrad(a, b, name, tm, tn, shard_axis=None):
    s, m = a.shape
    n = b.shape[1]

    def body(a_ref, b_ref, o_ref, *narrow):
        res = _dot(a_ref[...].astype(BF16), b_ref[...].astype(BF16), TN)
        o_ref[...] = res.reshape(o_ref.shape)
        for n_ref in narrow:
            n_ref[...] = res.astype(BF16).reshape(n_ref.shape)

    if shard_axis is None:
        out_shape, out_spec = (m, n), pl.BlockSpec((tm, tn), lambda i, j: (i, j))
    elif shard_axis == 0:
        per = m // N_CHIPS // tm
        out_shape, out_spec = (N_CHIPS, m // N_CHIPS, n), pl.BlockSpec((1, tm, tn), lambda i, j: (i // per, i % per, j))
    else:
        per = n // N_CHIPS // tn
        out_shape, out_spec = (N_CHIPS, m, n // N_CHIPS), pl.BlockSpec((1, tm, tn), lambda i, j: (j // per, i, j % per))
    both = shard_axis is not None
    a_spec = _resident((s, tm)) if tm == m else pl.BlockSpec((s, tm), lambda i, j: (0, i))
    b_spec = _resident((s, tn)) if tn == n else pl.BlockSpec((s, tn), lambda i, j: (0, j))
    return pl.pallas_call(
        body, name=name, grid=(m // tm, n // tn),
        in_specs=[a_spec, b_spec],
        out_specs=[out_spec, out_spec] if both else out_spec,
        out_shape=[jax.ShapeDtypeStruct(out_shape, F32), jax.ShapeDtypeStruct(out_shape, BF16)] if both
        else jax.ShapeDtypeStruct(out_shape, F32),
        compiler_params=_cparams(("arbitrary", "arbitrary")),
    )(a, b)


def _merge_bwd(dx1, osb, oswa, gates, wsb, wswa, rest):
    s = dx1.shape[0]
    tm = min(512, s)

    def body(dx_ref, osb_ref, oswa_ref, gate_ref, wsb_ref, wswa_ref, wout_ref,
             dosb_ref, doswa_ref, dgl_ref, merged_ref, dysb_ref, dyswa_ref, dxb_ref):
        dxb = dx_ref[...].astype(BF16)
        dxb_ref[...] = dxb
        dm = jnp.concatenate([_dot(dxb, wout_ref[q], NT) for q in range(N_CHIPS)], axis=1)
        ysb = _dot(osb_ref[...].astype(BF16), wsb_ref[...], NN)
        yswa = _dot(oswa_ref[...], wswa_ref[...], NN)
        g0 = gate_ref[:, :D_MODEL]
        g1 = gate_ref[:, D_MODEL:]
        merged_ref[...] = (g0 * ysb + g1 * yswa).astype(BF16)
        dgl_ref[:, :D_MODEL] = (dm * ysb * (g0 * (1.0 - g0))).astype(BF16)
        dgl_ref[:, D_MODEL:] = (dm * yswa * (g1 * (1.0 - g1))).astype(BF16)
        dysb = (dm * g0).astype(BF16)
        dyswa = (dm * g1).astype(BF16)
        dysb_ref[...] = dysb
        dyswa_ref[...] = dyswa
        dosb_ref[...] = _dot(dysb, wsb_ref[...], NT)
        doswa_ref[...] = _dot(dyswa, wswa_ref[...], NT)

    return pl.pallas_call(
        body, name="merge_bwd", grid=(s // tm,),
        in_specs=[_rows(tm, D_MODEL), _rows(tm, SB_WIDTH), _rows(tm, SWA_Q_WIDTH), _rows(tm, 2 * D_MODEL),
                  _resident((SB_WIDTH, D_MODEL)), _resident((SWA_Q_WIDTH, D_MODEL)), _resident(*W_OUT_BLOCK)],
        out_specs=[_rows(tm, SB_WIDTH), _rows(tm, SWA_Q_WIDTH), _rows(tm, 2 * D_MODEL), _rows(tm, D_MODEL),
                   _rows(tm, D_MODEL), _rows(tm, D_MODEL), _rows(tm, D_MODEL)],
        out_shape=[jax.ShapeDtypeStruct((s, SB_WIDTH), F32), jax.ShapeDtypeStruct((s, SWA_Q_WIDTH), F32),
                   jax.ShapeDtypeStruct((s, 2 * D_MODEL), BF16), jax.ShapeDtypeStruct((s, D_MODEL), BF16),
                   jax.ShapeDtypeStruct((s, D_MODEL), BF16), jax.ShapeDtypeStruct((s, D_MODEL), BF16),
                   jax.ShapeDtypeStruct((s, D_MODEL), BF16)],
        compiler_params=_cparams(("arbitrary",)),
    )(dx1, osb, oswa, gates, wsb, wswa, rest)


def _sb_bwd(sbqkv, osb, dosb, send):
    s = sbqkv.shape[0]
    t = min(SB_TILE, s)
    nq = s // t
    ng = SB_WIDTH // SB_STEP_WIDTH
    heads = range(2 * SB_STEP_WIDTH // LANES)
    ns = len(send)

    def body(*refs):
        q_ref, k_ref, v_ref, o_ref, do_ref = refs[:5]
        srcs = refs[5:5 + ns]
        dq_ref, dk_ref, dv_ref = refs[5 + ns:8 + ns]
        outs = refs[8 + ns:8 + 2 * ns]
        j, i = pl.program_id(0), pl.program_id(1)
        if ns:
            ssem, rsem = refs[8 + 2 * ns:]
            _hosted_exchange(srcs, outs, ssem, rsem, (j == 0) & (i == 0), (j == ng - 1) & (i == nq - 1))

        @pl.when(i == 0)
        def _():
            dk_ref[...] = jnp.zeros_like(dk_ref)
            dv_ref[...] = jnp.zeros_like(dv_ref)

        left = _lane((1, LANES)) < HEAD_DIM
        halves = (left, jnp.logical_not(left))
        pair = lambda a, p: a[:, (p // 2) * LANES:(p // 2 + 1) * LANES]
        only = lambda a, p: jnp.where(halves[p % 2], pair(a, p), jnp.zeros((), a.dtype))
        scale = jnp.asarray(SCALE, BF16)
        q_all = q_ref[...] * scale
        do_all = do_ref[...].astype(BF16)
        prod = do_all.astype(F32) * o_ref[...]
        qh = [only(q_all, p) for p in heads]
        doh = [only(do_all, p) for p in heads]
        delta = [jnp.sum(only(prod, p), axis=-1, keepdims=True) for p in heads]
        r = lax.broadcasted_iota(jnp.int32, (t, t), 0)
        c = lax.broadcasted_iota(jnp.int32, (t, t), 1)
        u_from = (r >= c).astype(BF16)
        u_from2 = _tri_twice(t)

        def tile(n, carry, strict):
            runs, dqs = carry
            off = pl.multiple_of((i - n) * t, t)
            k = k_ref[pl.ds(off, t), :]
            v = v_ref[pl.ds(off, t), :]
            ks = k * scale
            terms = [_sb_tile_terms(qh[p], pair(k, p), strict) for p in heads]
            logw = [_log_weights(*terms[p], runs[p][0], u_from) for p in heads]
            dws = [_dot(doh[p], pair(v, p), NT) for p in heads]
            wbs = [_masked(strict, jnp.exp(logw[p][0])).astype(BF16) for p in heads]
            es = [dws[p] * wbs[p].astype(F32) for p in heads]
            rests = [runs[p][1] + _split_dot(es[p], u_from2) for p in heads]
            betas = [jnp.exp(terms[p][0] + terms[p][1]) for p in heads]
            dzs = [_masked(strict, es[p] - betas[p] * (es[p] - rests[p])).astype(BF16) for p in heads]
            dqs = list(dqs)
            for g in range(len(heads) // 2):
                a, b = 2 * g, 2 * g + 1
                cols = slice(g * LANES, (g + 1) * LANES)
                dv_ref[pl.ds(off, t), cols] += _dot(wbs[a], doh[a], TN) + _dot(wbs[b], doh[b], TN)
                dk_ref[pl.ds(off, t), cols] += _dot(dzs[a], qh[a], TN) + _dot(dzs[b], qh[b], TN)
                dqs[g] = dqs[g] + _dot(dzs[a], only(ks, a), NN) + _dot(dzs[b], only(ks, b), NN)
            new_runs = tuple((logw[p][1], rests[p][:, 0:1]) for p in heads)
            return new_runs, tuple(dqs)

        zero = jnp.zeros((t, 1), F32)
        wide = jnp.zeros((t, LANES), F32)
        carry = tile(0, (tuple((zero, -delta[p]) for p in heads), tuple(wide for _ in heads[::2])), c < r)
        _, dqs = lax.fori_loop(1, i + 1, lambda n, cr: tile(n, cr, None), carry)
        for g, dq in enumerate(dqs):
            dq_ref[:, g * LANES:(g + 1) * LANES] = dq.astype(BF16)

    w = SB_STEP_WIDTH
    blk = pl.BlockSpec((t, w), lambda j, i: (i, j))
    whole = pl.BlockSpec((s, w), lambda j, i: (0, j))
    return pl.pallas_call(
        body, name="sb_bwd_x%d" % ns, grid=(ng, nq),
        in_specs=[blk, pl.BlockSpec((s, w), lambda j, i: (0, ng + j)),
                  pl.BlockSpec((s, w), lambda j, i: (0, 2 * ng + j)), blk, blk] + [HBM] * ns,
        out_specs=[blk, whole, whole] + [HBM] * ns,
        out_shape=[jax.ShapeDtypeStruct((s, SB_WIDTH), BF16), jax.ShapeDtypeStruct((s, SB_WIDTH), F32),
                   jax.ShapeDtypeStruct((s, SB_WIDTH), F32)] + [jax.ShapeDtypeStruct(a.shape, a.dtype) for a in send],
        scratch_shapes=_exchange_scratch(ns) if ns else [],
        compiler_params=_cparams(("arbitrary", "arbitrary")),
    )(sbqkv, sbqkv, sbqkv, osb, dosb, *send)


def _swa_bwd(qn, kn, v, sinks, do):
    s = qn.shape[0]
    t = min(512, s)
    nb = t // SWA_BLOCK
    nheads = SWA_Q_WIDTH // HEAD_DIM

    def body(sink_ref, q_ref, kc_ref, kp_ref, vc_ref, vp_ref, do_ref, dq_ref, dk_ref, dv_ref, dsink_ref):
        i = pl.program_id(0)

        @pl.when(i == 0)
        def _():
            dk_ref[...] = jnp.zeros_like(dk_ref)
            dv_ref[...] = jnp.zeros_like(dv_ref)

        left = _lane((1, LANES)) < HEAD_DIM
        halves = (left, jnp.logical_not(left))
        dsink = [jnp.zeros((1, 1), F32) for _ in range(nheads)]
        for b in range(nb):
            kvar = _swa_variants(_swa_band(kc_ref, kp_ref, b), left)
            vvar = _swa_variants(_swa_band(vc_ref, vp_ref, b), left)
            rows = slice(b * SWA_BLOCK, (b + 1) * SWA_BLOCK)
            valid = _swa_valid(i * nb + b)
            heads = _swa_heads(q_ref, rows, halves)
            doh = []
            for hb, p, _, _ in heads:
                do2 = do_ref[rows, hb * LANES:(hb + 1) * LANES]
                doh.append(jnp.where(halves[p], do2, 0.0).astype(BF16))
            dots = [_dot(qh, kvar[g][p], NT) for _, p, g, qh in heads]
            dps = [_dot(doh[h], vvar[g][p], NT) for h, (_, p, g, _) in enumerate(heads)]
            dss, pbs = [], []
            for h in range(nheads):
                probs, psink = _swa_softmax(dots[h], sink_ref[h], valid)
                delta = jnp.sum(probs * dps[h], axis=-1, keepdims=True)
                dss.append((probs * (dps[h] - delta) * SCALE).astype(BF16))
                pbs.append(probs.astype(BF16))
                dsink[h] = dsink[h] - jnp.sum(psink * delta, axis=0, keepdims=True)
            dk_acc = [jnp.zeros((2 * SWA_BLOCK, LANES), F32) for _ in range(2)]
            dv_acc = [jnp.zeros((2 * SWA_BLOCK, LANES), F32) for _ in range(2)]
            dqs = [_dot(dss[h], kvar[g][p], NN) for h, (_, p, g, _) in enumerate(heads)]
            for h, (_, p, g, qh) in enumerate(heads):
                which = 0 if p == g else 1
                dk_acc[which] = dk_acc[which] + _dot(dss[h], qh, TN)
                dv_acc[which] = dv_acc[which] + _dot(pbs[h], doh[h], TN)
            for hb in range(nheads // 2):
                dq_ref[rows, hb * LANES:(hb + 1) * LANES] = dqs[2 * hb] + dqs[2 * hb + 1]
            dkb = dk_acc[0] + pltpu.roll(dk_acc[1], HEAD_DIM, 1)
            dvb = dv_acc[0] + pltpu.roll(dv_acc[1], HEAD_DIM, 1)
            start = pl.multiple_of((i * nb + b) * SWA_BLOCK, SWA_BLOCK)
            dk_ref[pl.ds(start, SWA_BLOCK), :] += dkb[SWA_BLOCK:]
            dv_ref[pl.ds(start, SWA_BLOCK), :] += dvb[SWA_BLOCK:]

            @pl.when(i * nb + b > 0)
            def _(dkb=dkb, dvb=dvb, start=start):
                before = pl.multiple_of(jnp.maximum(start - SWA_BLOCK, 0), SWA_BLOCK)
                dk_ref[pl.ds(before, SWA_BLOCK), :] += dkb[:SWA_BLOCK]
                dv_ref[pl.ds(before, SWA_BLOCK), :] += dvb[:SWA_BLOCK]

        for h in range(nheads):
            dsink_ref[0, h:h + 1, :] = jnp.broadcast_to(dsink[h], (1, LANES))

    cur, prev = _swa_specs(s, t)
    whole = pl.BlockSpec((s, LANES), lambda i: (0, 0))
    return pl.pallas_call(
        body, name="swa_bwd", grid=(s // t,),
        in_specs=[pl.BlockSpec(memory_space=pltpu.SMEM), _rows(t, SWA_Q_WIDTH), cur, prev, cur, prev,
                  _rows(t, SWA_Q_WIDTH)],
        out_specs=[_rows(t, SWA_Q_WIDTH), whole, whole, pl.BlockSpec((1, 8, LANES), lambda i: (i, 0, 0))],
        out_shape=[jax.ShapeDtypeStruct((s, SWA_Q_WIDTH), F32), jax.ShapeDtypeStruct((s, SWA_KV_WIDTH), F32),
                   jax.ShapeDtypeStruct((s, SWA_KV_WIDTH), F32), jax.ShapeDtypeStruct((s // t, 8, LANES), F32)],
        compiler_params=_cparams(("arbitrary",)),
    )(sinks, qn, kn, kn, v, v, do)


def _swa_post(raw, dqn, dkn, qg, kg, cos, sin):
    s = raw.shape[0]
    tm = min(512, s)
    nq = SWA_Q_WIDTH // LANES

    def body(raw_ref, dq_ref, dk_ref, qg_ref, kg_ref, cos_ref, sin_ref, out_ref, dg_ref):
        @pl.when(pl.program_id(0) == 0)
        def _():
            dg_ref[...] = jnp.zeros_like(dg_ref)

        cs, sn = cos_ref[...], sin_ref[...]
        dgq = jnp.zeros((1, LANES), F32)
        for b in range(nq):
            cols = slice(b * LANES, (b + 1) * LANES)
            dp, dg = _norm_rope_bwd(raw_ref[:, cols], dq_ref[:, cols], qg_ref[...], cs, sn)
            out_ref[:, cols] = dp.astype(BF16)
            dgq = dgq + dg
        cols = slice(SWA_Q_WIDTH, SWA_Q_WIDTH + LANES)
        dp, dgk = _norm_rope_bwd(raw_ref[:, cols], dk_ref[...], kg_ref[...], cs, sn)
        out_ref[:, cols] = dp.astype(BF16)
        dg_ref[0:1, :] += dgq
        dg_ref[1:2, :] += dgk

    return pl.pallas_call(
        body, name="swa_post", grid=(s // tm,),
        in_specs=[_rows(tm, 640), _rows(tm, SWA_Q_WIDTH), _rows(tm, LANES), _resident((1, LANES)),
                  _resident((1, LANES)), _rows(tm, LANES), _rows(tm, LANES)],
        out_specs=[_rows(tm, 640), pl.BlockSpec((8, LANES), lambda i: (0, 0))],
        out_shape=[jax.ShapeDtypeStruct((s, 640), BF16), jax.ShapeDtypeStruct((8, LANES), F32)],
        compiler_params=_cparams(("arbitrary",)),
    )(raw, dqn, dkn, qg, kg, cos, sin)


def _inproj_bwd(dsbq, dsbk, dsbv, dswqk, dswv, dgl, x, dx1, g, w):
    s = x.shape[0]
    tm = min(512, s)

    def body(dsbq_ref, dsbk_ref, dsbv_ref, dswqk_ref, dswv_ref, dgl_ref, x_ref, dx1_ref, g_ref, w_ref,
             dx_ref, dproj_ref, dg_ref):
        @pl.when(pl.program_id(0) == 0)
        def _():
            dg_ref[...] = jnp.zeros_like(dg_ref)

        dproj_ref[:, 0:512] = dsbq_ref[...]
        dproj_ref[:, 512:1024] = dsbk_ref[...].astype(BF16)
        dproj_ref[:, 1024:1536] = dsbv_ref[...].astype(BF16)
        dproj_ref[:, 1536:2176] = dswqk_ref[...]
        dproj_ref[:, 2176:2304] = dswv_ref[...].astype(BF16)
        dproj_ref[:, GATE_OFF:IN_WIDTH] = dgl_ref[...]
        dh = jnp.zeros((tm, D_MODEL), F32)
        for a in range(0, IN_WIDTH, 512):
            b = min(a + 512, IN_WIDTH)
            dh = dh + _dot(dproj_ref[:, a:b], w_ref[:, a:b], NT)
        gam = g_ref[...]
        xn, rstd = _rms(x_ref[...])
        dg_ref[...] += jnp.sum(dh * xn, axis=0, keepdims=True)
        dx_ref[...] = dx1_ref[...] + _rms_bwd(dh, xn, rstd, gam)

    return pl.pallas_call(
        body, name="inproj_bwd", grid=(s // tm,),
        in_specs=[_rows(tm, 512), _rows(tm, 512), _rows(tm, 512), _rows(tm, 640), _rows(tm, LANES),
                  _rows(tm, 2 * D_MODEL), _rows(tm, D_MODEL), _rows(tm, D_MODEL), _resident((1, D_MODEL)),
                  _resident((D_MODEL, IN_WIDTH))],
        out_specs=[_rows(tm, D_MODEL), _rows(tm, IN_WIDTH), pl.BlockSpec((1, D_MODEL), lambda i: (0, 0))],
        out_shape=[jax.ShapeDtypeStruct((s, D_MODEL), F32), jax.ShapeDtypeStruct((s, IN_WIDTH), BF16),
                   jax.ShapeDtypeStruct((1, D_MODEL), F32)],
        compiler_params=_cparams(("arbitrary",)),
    )(dsbq, dsbk, dsbv, dswqk, dswv, dgl, x, dx1, g, w)


def _gather_weights(shard):
    def body(src, out, ssem, rsem, lsem):
        once = pl.program_id(0) == 0
        _hosted_gather(src, out, ssem, rsem, lsem, once, once, once)

    return pl.pallas_call(
        body, name="gather_weights", grid=(1,), in_specs=[HBM], out_specs=HBM,
        out_shape=jax.ShapeDtypeStruct((N_CHIPS,) + shard.shape, shard.dtype),
        scratch_shapes=_gather_scratch(1),
        compiler_params=pltpu.CompilerParams(dimension_semantics=("arbitrary",), has_side_effects=True),
    )(shard)


def _pair_swap(arrs, send=()):
    n, ns = len(arrs), len(send)

    def body(*refs):
        mine, src = refs[:n], refs[n:n + ns]
        theirs, landed = refs[n + ns:2 * n + ns], refs[2 * n + ns:2 * (n + ns)]
        ssem, rsem = refs[2 * (n + ns):2 * (n + ns) + 2]
        once, never = pl.program_id(0) == 0, pl.program_id(0) < 0
        if ns:
            _hosted_exchange(src, landed, *refs[2 * (n + ns) + 2:], once, never)
        x, y, c, _ = _place()
        cps = [pltpu.make_async_remote_copy(mine[t], theirs[t], ssem.at[t], rsem.at[t],
                                            device_id=(x, y, 1 - c), device_id_type=MESH) for t in range(n)]
        for cp in cps:
            cp.start()
        for cp in cps:
            cp.wait()
        if ns:
            _hosted_exchange(src, landed, *refs[2 * (n + ns) + 2:], never, once)

    return pl.pallas_call(
        body, name="pair_swap_x%d" % ns, grid=(1,), in_specs=[HBM] * (n + ns), out_specs=[HBM] * (n + ns),
        out_shape=[jax.ShapeDtypeStruct(a.shape, a.dtype) for a in tuple(arrs) + tuple(send)],
        scratch_shapes=[pltpu.SemaphoreType.DMA((n,)), pltpu.SemaphoreType.DMA((n,))] + (_exchange_scratch(ns) if ns else []),
        compiler_params=pltpu.CompilerParams(dimension_semantics=("arbitrary",), has_side_effects=True),
    )(*arrs, *send)


def _allreduce_small(block):
    def body(src, out, buf, ssem, rsem):
        x, y, c, _ = _place()
        me = 4 * x + 2 * y + c
        buf[me] = src[...]
        cps = []
        for k in range(1, N_DEV):
            peer = (x ^ (k >> 2), y ^ ((k >> 1) & 1), c ^ (k & 1))
            cp = pltpu.make_async_remote_copy(src, buf.at[me], ssem.at[k - 1], rsem.at[k - 1], device_id=peer, device_id_type=MESH)
            cp.start()
            cps.append(cp)
        for k in range(1, N_DEV):
            got = buf.at[me ^ k]
            pltpu.make_async_remote_copy(got, got, ssem.at[k - 1], rsem.at[k - 1], device_id=(x, y, c), device_id_type=MESH).wait_recv()
        for cp in cps:
            cp.wait_send()
        tot = buf[0]
        for d in range(1, N_DEV):
            tot = tot + buf[d]
        out[...] = tot

    vm = pl.BlockSpec(memory_space=pltpu.VMEM)
    return pl.pallas_call(
        body, name="allreduce_small", in_specs=[vm], out_specs=vm,
        out_shape=jax.ShapeDtypeStruct(block.shape, F32),
        scratch_shapes=[pltpu.VMEM((N_DEV,) + block.shape, F32), pltpu.SemaphoreType.DMA((N_DEV - 1,)),
                        pltpu.SemaphoreType.DMA((N_DEV - 1,))],
        compiler_params=pltpu.CompilerParams(has_side_effects=True),
    )(block)


def _sum_chips(landed, own, into, layer, name):
    nq, k, n = landed.shape
    tr = min(256, k) if k % 256 == 0 else k // 4
    me = (2 * lax.axis_index("x") + lax.axis_index("y")).astype(jnp.int32).reshape(1)

    def body(me_ref, p0, p1, p2, p3, own_ref, into_ref, o_ref):
        mine = own_ref[0]
        terms = [jnp.where(me_ref[0] == q, mine, p[0].astype(F32)) for q, p in enumerate((p0, p1, p2, p3))]
        o_ref[0] = ((terms[0] + terms[1]) + terms[2]) + terms[3]

    spec = lambda q: pl.BlockSpec((1, tr, n), lambda i, m, q=q: (q, i, 0))
    return pl.pallas_call(
        body, name=name,
        grid_spec=pltpu.PrefetchScalarGridSpec(
            num_scalar_prefetch=1, grid=(k // tr,),
            in_specs=[spec(q) for q in range(nq)] + [pl.BlockSpec((1, tr, n), lambda i, m: (m[0], i, 0)), HBM],
            out_specs=pl.BlockSpec((1, tr, n), lambda i, m: (layer, i, 0))),
        out_shape=jax.ShapeDtypeStruct(into.shape, F32),
        input_output_aliases={6: 0},
        compiler_params=_cparams(("arbitrary",)),
    )(me, landed, landed, landed, landed, own, into)


def _adamw(w, g, g2, m, v, name):
    shape = w.shape
    cols = shape[-1]
    flat = lambda t: t.reshape(-1, cols)
    rows = flat(w).shape[0]
    tr = min(512, rows) if rows % 512 == 0 or rows < 512 else 256
    pair = g2 is not None

    def body(*refs):
        if pair:
            w_ref, g_ref, g2_ref, m_ref, v_ref, go_ref, d_ref, nm_ref, nv_ref = refs
            gr = g_ref[...] + g2_ref[...]
        else:
            w_ref, g_ref, m_ref, v_ref, go_ref, d_ref, nm_ref, nv_ref = refs
            gr = g_ref[...]
        go_ref[...] = gr
        nm = ADAM_B1 * m_ref[...] + (1.0 - ADAM_B1) * gr
        nv = ADAM_B2 * v_ref[...] + (1.0 - ADAM_B2) * (gr * gr)
        m_hat = nm / (1.0 - ADAM_B1 ** ADAM_STEP)
        v_hat = nv / (1.0 - ADAM_B2 ** ADAM_STEP)
        d_ref[...] = -ADAM_LR * (m_hat / (jnp.sqrt(v_hat) + ADAM_EPS) + ADAM_WD * w_ref[...])
        nm_ref[...] = nm
        nv_ref[...] = nv

    spec = pl.BlockSpec((tr, cols), lambda i: (i, 0))
    ins = [w, g] + ([g2] if pair else []) + [m, v]
    outs = pl.pallas_call(
        body, name=name, grid=(rows // tr,),
        in_specs=[spec] * len(ins), out_specs=[spec] * 4,
        out_shape=[jax.ShapeDtypeStruct((rows, cols), F32)] * 4,
        compiler_params=_cparams(("arbitrary",)),
    )(*[flat(t) for t in ins])
    return [o.reshape(shape) for o in outs]


def _pack_rest(big, l):
    rows = dict(PACK_ROWS)
    return jnp.concatenate([big[name][l].astype(BF16).reshape(rows[name], 1024) for name in REST], axis=0)


def _whole_columns(t):
    return jnp.moveaxis(t, 0, 1).reshape(t.shape[1], N_CHIPS * t.shape[2])


def _branch_weights(rest):
    out, at = [], REST_ROWS - 256
    for _ in range(2):
        out.append(_whole_columns(rest[:, at:at + 128, :].reshape((N_CHIPS,) + SHARD_SHAPES["w_branch_sb"])))
        at += 128
    return out


def _pack_small(d):
    flat = jnp.concatenate([d[n].reshape(-1) for n in SMALL_NAMES])
    return jnp.pad(flat, (0, SMALL_ROWS * LANES - flat.shape[0])).reshape(SMALL_ROWS, LANES)


def _unpack_small(block, like):
    flat, out, at = block.reshape(-1), {}, 0
    for n in SMALL_NAMES:
        size = like[n].size
        out[n] = flat[at:at + size].reshape(like[n].shape)
        at += size
    return out


def _rope_tables(s):
    inv_freq = 1.0 / (ROPE_THETA ** (jnp.arange(0, HEAD_DIM, 2, dtype=F32) / HEAD_DIM))
    ang = jnp.arange(s, dtype=F32)[:, None] * inv_freq[None, :]
    reps = LANES // (HEAD_DIM // 2)
    return jnp.tile(jnp.cos(ang), (1, reps)), jnp.tile(jnp.sin(ang), (1, reps))


def _forward_backward(x, target, big, small):
    s = x.shape[0]
    cos, sin = _rope_tables(s)
    two = lambda gvec: jnp.tile(gvec.reshape(1, HEAD_DIM), (1, 2))
    saved = []
    win_shard = lambda l: big["w_in"][l].astype(BF16)
    win = _whole_columns(_gather_weights(win_shard(0)))
    for l in range(DEPTH):
        gm = small["mix_norm_g"][l].reshape(1, D_MODEL)
        gl = small["mlp_norm_g"][l].reshape(1, D_MODEL)
        qg, kg = two(small["q_norm_g"][l]), two(small["k_norm_g"][l])
        h, sbqkv, raw, qn, kn, v, gates = _inproj_fwd(x, gm, win, qg, kg, cos, sin)
        osb, rest, *nxt = _sb_fwd(sbqkv, [_pack_rest(big, l)] + ([win_shard(l + 1)] if l + 1 < DEPTH else []))
        wsb, wswa = _branch_weights(rest)
        oswa = _swa_fwd(qn, kn, v, small["sinks"][l])
        x1 = _merge_fwd(x, osb, oswa, gates, wsb, wswa, rest)
        x2, u = _mlp_fwd(x1, gl, rest)
        saved.append((x, h, sbqkv, raw, qn, kn, v, gates, osb, oswa, x1, u, gm, gl, qg, kg, win, wsb, wswa, rest))
        x, win = x2, (_whole_columns(nxt[0]) if nxt else None)
    dx, loss = _loss_grad(x, target)

    got = {name: [None] * DEPTH for name in BIG}
    gsmall = {name: [None] * DEPTH for name in SMALL_NAMES}
    late = []
    for l in reversed(range(DEPTH)):
        x0, h, sbqkv, raw, qn, kn, v, gates, osb, oswa, x1, u, gm, gl, qg, kg, win, wsb, wswa, rest = saved[l]
        dx1, du, act, h2, dgl_mlp, dxb = _mlp_bwd(dx, x1, u, gl, rest)
        dosb, doswa, dgate, merged, dysb, dyswa, dx1b = _merge_bwd(dx1, osb, oswa, gates, wsb, wswa, rest)
        ready = [("w_down", l, *_wgrad(act, dxb, "wgrad_down", 512, 1024, shard_axis=0)),
                 ("w_up", l, *_wgrad(h2, du, "wgrad_up", 1024, 512, shard_axis=1)),
                 ("w_out", l, *_wgrad(merged, dx1b, "wgrad_out", 256, 1024, shard_axis=0)),
                 ("w_branch_sb", l, *_wgrad(osb, dysb, "wgrad_bsb", 512, 256, shard_axis=1)),
                 ("w_branch_swa", l, *_wgrad(oswa, dyswa, "wgrad_bswa", 512, 256, shard_axis=1))] + late
        dsbq, dsbk, dsbv, *landed = _sb_bwd(sbqkv, osb, dosb, [narrow for _, _, _, narrow in ready])
        for (name, layer, own, _), arr in zip(ready, landed):
            got[name][layer] = (arr, own)
        dqn, dkn, dswv, dsink = _swa_bwd(qn, kn, v, small["sinks"][l], doswa)
        dswqk, dgqk = _swa_post(raw, dqn, dkn, qg, kg, cos, sin)
        dx, dproj, dg_mix = _inproj_bwd(dsbq, dsbk, dsbv, dswqk, dswv, dgate, x0, dx1, gm, win)
        dwin = _wgrad(h, dproj, "wgrad_in", 1024, 256)
        dwin = dwin.T.reshape(N_CHIPS, IN_WIDTH // N_CHIPS, D_MODEL)
        late = [("w_in", l, dwin, dwin.astype(BF16))]
        gsmall["mix_norm_g"][l] = dg_mix[0]
        gsmall["mlp_norm_g"][l] = dgl_mlp[0]
        gsmall["q_norm_g"][l] = dgqk[0, :HEAD_DIM] + dgqk[0, HEAD_DIM:]
        gsmall["k_norm_g"][l] = dgqk[1, :HEAD_DIM] + dgqk[1, HEAD_DIM:]
        gsmall["sinks"][l] = jnp.sum(dsink[:, :, 0], axis=0)
    gsmall = {k: jnp.stack(vs) for k, vs in gsmall.items()}
    return loss, dx, got, late[0], gsmall


def kernel(x, mix_norm_g, w_in, q_norm_g, k_norm_g, sinks, w_branch_sb, w_branch_swa, w_out, mlp_norm_g, w_up, w_down, loss_target, m_mix_norm_g, m_w_in, m_q_norm_g, m_k_norm_g, m_sinks, m_w_branch_sb, m_w_branch_swa, m_w_out, m_mlp_norm_g, m_w_up, m_w_down, v_mix_norm_g, v_w_in, v_q_norm_g, v_k_norm_g, v_sinks, v_w_branch_sb, v_w_branch_swa, v_w_out, v_mlp_norm_g, v_w_up, v_w_down):
    big = dict(w_in=w_in, w_branch_sb=w_branch_sb, w_branch_swa=w_branch_swa, w_out=w_out, w_up=w_up, w_down=w_down)
    big_m = dict(w_in=m_w_in, w_branch_sb=m_w_branch_sb, w_branch_swa=m_w_branch_swa, w_out=m_w_out, w_up=m_w_up, w_down=m_w_down)
    big_v = dict(w_in=v_w_in, w_branch_sb=v_w_branch_sb, w_branch_swa=v_w_branch_swa, w_out=v_w_out, w_up=v_w_up, w_down=v_w_down)
    small = dict(mix_norm_g=mix_norm_g, q_norm_g=q_norm_g, k_norm_g=k_norm_g, sinks=sinks, mlp_norm_g=mlp_norm_g)
    small_m = dict(mix_norm_g=m_mix_norm_g, q_norm_g=m_q_norm_g, k_norm_g=m_k_norm_g, sinks=m_sinks, mlp_norm_g=m_mlp_norm_g)
    small_v = dict(mix_norm_g=v_mix_norm_g, q_norm_g=v_q_norm_g, k_norm_g=v_k_norm_g, sinks=v_sinks, mlp_norm_g=v_mlp_norm_g)

    loss_part, grad_x, got, last, gsmall = _forward_backward(x[0], loss_target[0], big, small)
    loss = lax.psum(loss_part, ("x", "y", "c"))

    mine = {}
    for name in BIG:
        shard = SHARD_SHAPES[name][::-1] if name == "w_in" else SHARD_SHAPES[name]
        tot = lax.empty((DEPTH,) + shard, F32)
        for l in range(DEPTH):
            if got[name][l] is not None:
                tot = _sum_chips(*got[name][l], tot, l, "sum_" + name)
        mine[name] = tot
    name, layer, own, narrow = last
    *theirs, landed = _pair_swap([mine[n] for n in REST], [narrow])
    theirs = dict(zip(REST, theirs))
    mine[name] = _sum_chips(landed, own, mine[name], layer, "sum_" + name)
    (theirs[name],) = _pair_swap([mine[name]])
    flip = lambda n, t: jnp.swapaxes(t, 1, 2) if n == "w_in" else t
    upd = {n: [flip(n, t) for t in _adamw(flip(n, big[n]), mine[n], theirs[n], flip(n, big_m[n]), flip(n, big_v[n]),
                                            "adamw_" + n)] for n in BIG}
    g_small = _allreduce_small(_pack_small(gsmall))
    sm = _adamw(_pack_small(small), g_small, None, _pack_small(small_m), _pack_small(small_v), "adamw_small")
    upd_small = [_unpack_small(t, small) for t in sm]

    names = ("mix_norm_g", "w_in", "q_norm_g", "k_norm_g", "sinks", "w_branch_sb", "w_branch_swa", "w_out",
             "mlp_norm_g", "w_up", "w_down")
    pick = lambda n, i: upd[n][i] if n in upd else upd_small[i][n]
    return (loss, grad_x[None], *[pick(n, 0) for n in names], *[pick(n, 1) for n in names],
            *[pick(n, 2) for n in names], *[pick(n, 3) for n in names])
```

```python
import jax
import jax.numpy as jnp
from jax import lax
from jax.experimental import pallas as pl
from jax.experimental.pallas import tpu as pltpu

F32 = jnp.float32
BF16 = jnp.bfloat16

D_MODEL = 1024
DEPTH = 4
HEAD_DIM = 64
SB_WIDTH = 512
SWA_Q_WIDTH = 512
SWA_KV_WIDTH = 128
D_FF = 4096
IN_WIDTH = 4352
GATE_OFF = 2304
ROPE_THETA = 10000.0
NORM_EPS = 1e-6
SCALE = HEAD_DIM ** -0.5
N_CHIPS = 4
N_DEV = 8

ADAM_LR = 0.001
ADAM_B1 = 0.9
ADAM_B2 = 0.999
ADAM_EPS = 1e-08
ADAM_WD = 0.01
ADAM_STEP = 10

LANES = 128
SB_TILE = 256
SB_STEP_WIDTH = 512
SB_FWD_STEP_WIDTH = 512
SWA_BLOCK = 128
VMEM_LIMIT = 56 << 20

NN = (((1,), (0,)), ((), ()))
NT = (((1,), (1,)), ((), ()))
TN = (((0,), (0,)), ((), ()))
MESH = pl.DeviceIdType.MESH
HBM = pl.BlockSpec(memory_space=pl.ANY)

BIG = ("w_in", "w_branch_sb", "w_branch_swa", "w_out", "w_up", "w_down")
SMALL_NAMES = ("mix_norm_g", "q_norm_g", "k_norm_g", "sinks", "mlp_norm_g")
PACK_ROWS = (("w_in", 1088), ("w_branch_sb", 128), ("w_branch_swa", 128), ("w_out", 256), ("w_up", 1024), ("w_down", 1024))
SHARD_SHAPES = {"w_in": (1024, 1088), "w_branch_sb": (512, 256), "w_branch_swa": (512, 256), "w_out": (256, 1024),
                "w_up": (1024, 1024), "w_down": (1024, 1024)}
ROW_SHARDED = ("w_out", "w_down")
SMALL_ROWS = 72


def _dot(a, b, dims):
    return lax.dot_general(a, b, dims, preferred_element_type=F32)


def _cparams(sem):
    return pltpu.CompilerParams(dimension_semantics=sem, vmem_limit_bytes=VMEM_LIMIT)


def _resident(shape, index=None):
    index = (0,) * len(shape) if index is None else index
    return pl.BlockSpec(shape, lambda *_: index, pipeline_mode=pl.Buffered(1))


REST = ("w_up", "w_down", "w_out", "w_branch_sb", "w_branch_swa")
REST_ROWS = 2560
W_UP_BLOCK = ((N_CHIPS, 1024, 1024), (0, 0, 0))
W_DOWN_BLOCK = ((N_CHIPS, 1024, 1024), (0, 1, 0))
W_OUT_BLOCK = ((N_CHIPS, 256, 1024), (0, 8, 0))


def _rows(tm, width):
    return pl.BlockSpec((tm, width), lambda i: (i, 0))


def _rms(xf):
    rstd = lax.rsqrt(jnp.mean(xf * xf, axis=-1, keepdims=True) + NORM_EPS)
    return xf * rstd, rstd


def _rms_bwd(dh, xn, rstd, g):
    dxn = dh * g
    return rstd * (dxn - xn * jnp.mean(dxn * xn, axis=-1, keepdims=True))


def _lane(shape):
    return lax.broadcasted_iota(jnp.int32, shape, len(shape) - 1)


def _head_mean(v, left):
    sl = jnp.sum(jnp.where(left, v, 0.0), axis=-1, keepdims=True)
    sr = jnp.sum(jnp.where(left, 0.0, v), axis=-1, keepdims=True)
    return jnp.where(left, sl, sr) * (1.0 / HEAD_DIM)


def _rope(y, cs, sn, first):
    up = pltpu.roll(y, 96, 1)
    dn = pltpu.roll(y, 32, 1)
    return y * cs + jnp.where(first, -up, dn) * sn


def _rope_t(d, cs, sn, first):
    t = d * jnp.where(first, -sn, sn)
    return d * cs + jnp.where(first, pltpu.roll(t, 96, 1), pltpu.roll(t, 32, 1))


def _norm_rope(p, g, cs, sn):
    lane = _lane((1, LANES))
    left = lane < HEAD_DIM
    first = (lane % HEAD_DIM) < (HEAD_DIM // 2)
    yn = p * lax.rsqrt(_head_mean(p * p, left) + NORM_EPS)
    return _rope(yn * g, cs, sn, first)


def _norm_rope_bwd(p, dout, g, cs, sn):
    lane = _lane((1, LANES))
    left = lane < HEAD_DIM
    first = (lane % HEAD_DIM) < (HEAD_DIM // 2)
    rstd = lax.rsqrt(_head_mean(p * p, left) + NORM_EPS)
    yn = p * rstd
    dyg = _rope_t(dout, cs, sn, first)
    dg = jnp.sum(dyg * yn, axis=0, keepdims=True)
    dyn = dyg * g
    dp = rstd * (dyn - yn * _head_mean(dyn * yn, left))
    return dp, dg


def _place():
    x, y, c = lax.axis_index("x"), lax.axis_index("y"), lax.axis_index("c")
    return x, y, c, [(1 - x, y), (x, 1 - y), (1 - x, 1 - y)]


def _hosted_gather(src, out, ssem, rsem, lsem, first, mid, last, slot=0):
    x, y, c, chips = _place()
    me = 2 * x + y
    sib = (x, y, 1 - c)
    r2 = src.shape[0] // 2
    base = 6 * slot

    def slab(chip, half):
        return out.at[chip, pl.ds(half * r2, r2)]

    def ici(j):
        cx, cy = chips[j]
        return pltpu.make_async_remote_copy(src.at[pl.ds(c * r2, r2)], slab(me, c), ssem.at[base + j], rsem.at[base + j],
                                            device_id=(cx, cy, c), device_id_type=MESH)

    def landed(j):
        got = slab(2 * chips[j][0] + chips[j][1], c)
        return pltpu.make_async_remote_copy(got, got, ssem.at[base + j], rsem.at[base + j], device_id=sib, device_id_type=MESH)

    def d2d(j, half):
        got = slab(2 * chips[j][0] + chips[j][1], half)
        return pltpu.make_async_remote_copy(got, got, ssem.at[base + 3 + j], rsem.at[base + 3 + j], device_id=sib,
                                            device_id_type=MESH)

    local = pltpu.make_async_copy(src, out.at[me], lsem.at[slot])

    @pl.when(first)
    def _():
        local.start()
        for j in range(3):
            ici(j).start()

    @pl.when(mid)
    def _():
        for j in range(3):
            landed(j).wait_recv()
            d2d(j, c).start()

    @pl.when(last)
    def _():
        for j in range(3):
            d2d(j, 1 - c).wait_recv()
        for j in range(3):
            ici(j).wait_send()
            d2d(j, c).wait_send()
        local.wait()


def _hosted_exchange(srcs, outs, ssem, rsem, first, last):
    x, y, c, chips = _place()
    me = 2 * x + y

    def send(t, j):
        cx, cy = chips[j]
        return pltpu.make_async_remote_copy(srcs[t].at[2 * cx + cy], outs[t].at[me], ssem.at[3 * t + j], rsem.at[3 * t + j],
                                            device_id=(cx, cy, c), device_id_type=MESH)

    def landed(t, j):
        cx, cy = chips[j]
        got = outs[t].at[2 * cx + cy]
        return pltpu.make_async_remote_copy(got, got, ssem.at[3 * t + j], rsem.at[3 * t + j],
                                            device_id=(cx, cy, c), device_id_type=MESH)

    @pl.when(first)
    def _():
        for t in range(len(srcs)):
            for j in range(3):
                send(t, j).start()

    @pl.when(last)
    def _():
        for t in range(len(srcs)):
            for j in range(3):
                landed(t, j).wait_recv()
        for t in range(len(srcs)):
            for j in range(3):
                send(t, j).wait_send()


def _gather_scratch(n):
    return [pltpu.SemaphoreType.DMA((6 * n,)), pltpu.SemaphoreType.DMA((6 * n,)), pltpu.SemaphoreType.DMA((n,))]


def _exchange_scratch(n):
    return [pltpu.SemaphoreType.DMA((3 * n,)), pltpu.SemaphoreType.DMA((3 * n,))]


def _inproj_fwd(x, g, w, qg, kg, cos, sin):
    s = x.shape[0]
    tm = min(512, s)

    def body(x_ref, g_ref, w_ref, qg_ref, kg_ref, cos_ref, sin_ref,
             h_ref, sb_ref, raw_ref, qn_ref, kn_ref, v_ref, gate_ref):
        xn, _ = _rms(x_ref[...])
        h = (xn * g_ref[...]).astype(BF16)
        h_ref[...] = h
        for a in range(0, 3 * SB_WIDTH, 512):
            sb_ref[:, a:a + 512] = _dot(h, w_ref[:, a:a + 512], NN).astype(BF16)
        cs, sn = cos_ref[...], sin_ref[...]
        q0 = 3 * SB_WIDTH
        pq = _dot(h, w_ref[:, q0:q0 + SWA_Q_WIDTH], NN)
        raw_ref[:, 0:SWA_Q_WIDTH] = pq
        for b in range(SWA_Q_WIDTH // LANES):
            blk = pq[:, b * LANES:(b + 1) * LANES]
            qn_ref[:, b * LANES:(b + 1) * LANES] = _norm_rope(blk, qg_ref[...], cs, sn).astype(BF16)
        k0 = q0 + SWA_Q_WIDTH
        pk = _dot(h, w_ref[:, k0:k0 + 2 * SWA_KV_WIDTH], NN)
        raw_ref[:, SWA_Q_WIDTH:SWA_Q_WIDTH + SWA_KV_WIDTH] = pk[:, :SWA_KV_WIDTH]
        kn_ref[...] = _norm_rope(pk[:, :SWA_KV_WIDTH], kg_ref[...], cs, sn).astype(BF16)
        v_ref[...] = pk[:, SWA_KV_WIDTH:].astype(BF16)
        for a in range(GATE_OFF, IN_WIDTH, 512):
            gate_ref[:, a - GATE_OFF:a - GATE_OFF + 512] = jax.nn.sigmoid(_dot(h, w_ref[:, a:a + 512], NN))

    return pl.pallas_call(
        body, name="inproj_fwd", grid=(s // tm,),
        in_specs=[_rows(tm, D_MODEL), _resident((1, D_MODEL)), _resident((D_MODEL, IN_WIDTH)),
                  _resident((1, LANES)), _resident((1, LANES)), _rows(tm, LANES), _rows(tm, LANES)],
        out_specs=[_rows(tm, D_MODEL), _rows(tm, 3 * SB_WIDTH), _rows(tm, 640), _rows(tm, SWA_Q_WIDTH),
                   _rows(tm, SWA_KV_WIDTH), _rows(tm, SWA_KV_WIDTH), _rows(tm, 2 * D_MODEL)],
        out_shape=[jax.ShapeDtypeStruct((s, D_MODEL), BF16), jax.ShapeDtypeStruct((s, 3 * SB_WIDTH), BF16),
                   jax.ShapeDtypeStruct((s, 640), F32), jax.ShapeDtypeStruct((s, SWA_Q_WIDTH), BF16),
                   jax.ShapeDtypeStruct((s, SWA_KV_WIDTH), BF16), jax.ShapeDtypeStruct((s, SWA_KV_WIDTH), BF16),
                   jax.ShapeDtypeStruct((s, 2 * D_MODEL), F32)],
        compiler_params=_cparams(("arbitrary",)),
    )(x, g, w, qg, kg, cos, sin)


def _sb_tile_terms(qh, k, strict):
    z = _dot(qh, k, NT)
    nz = -z
    soft = jnp.log(1.0 + jnp.exp(jnp.minimum(z, nz)))
    log_keep = jnp.minimum(nz, 0.0) - soft
    if strict is not None:
        log_keep = jnp.where(strict, log_keep, 0.0)
    return z, log_keep


def _log_weights(z, log_keep, run, u_from):
    tail = run + _dot(log_keep.astype(BF16), u_from, NN)
    return z + tail, tail[:, 0:1]


def _masked(strict, val):
    return val if strict is None else jnp.where(strict, val, 0.0)


def _tri_twice(t):
    r = lax.broadcasted_iota(jnp.int32, (2 * t, t), 0)
    c = lax.broadcasted_iota(jnp.int32, (2 * t, t), 1)
    return (jnp.where(r >= t, r - t, r) >= c).astype(BF16)


def _split_dot(a, u2):
    hi = a.astype(BF16)
    lo = (a - hi.astype(F32)).astype(BF16)
    return _dot(jnp.concatenate([hi, lo], axis=1), u2, NN)


def _sb_fwd(sbqkv, shards=()):
    s = sbqkv.shape[0]
    t = min(SB_TILE, s)
    nq = s // t
    ng = SB_WIDTH // SB_FWD_STEP_WIDTH
    heads = range(2 * SB_FWD_STEP_WIDTH // LANES)
    nsh = len(shards)

    def body(*refs):
        q_ref, k_ref, v_ref = refs[:3]
        o_ref = refs[3 + nsh]
        j, i = pl.program_id(0), pl.program_id(1)
        for n in range(nsh):
            _hosted_gather(refs[3 + n], refs[4 + nsh + n], *refs[4 + 2 * nsh:], (j == 0) & (i == 0),
                           (j == ng - 1) & (i == (3 * nq) // 4), (j == ng - 1) & (i == nq - 1), slot=n)
        left = _lane((1, LANES)) < HEAD_DIM
        halves = (left, jnp.logical_not(left))
        pair = lambda a, p: a[:, (p // 2) * LANES:(p // 2 + 1) * LANES]
        only = lambda a, p: jnp.where(halves[p % 2], pair(a, p), jnp.zeros((), a.dtype))
        q_all = q_ref[...] * jnp.asarray(SCALE, BF16)
        qh = [only(q_all, p) for p in heads]
        r = lax.broadcasted_iota(jnp.int32, (t, t), 0)
        c = lax.broadcasted_iota(jnp.int32, (t, t), 1)
        u_from = (r >= c).astype(BF16)

        def tile(n, carry, strict):
            runs, accs = carry
            off = pl.multiple_of((i - n) * t, t)
            k = k_ref[pl.ds(off, t), :]
            v = v_ref[pl.ds(off, t), :]
            terms = [_sb_tile_terms(qh[p], pair(k, p), strict) for p in heads]
            logw = [_log_weights(*terms[p], runs[p], u_from) for p in heads]
            ws = [_masked(strict, jnp.exp(logw[p][0])).astype(BF16) for p in heads]
            accs = list(accs)
            for p in heads:
                accs[p // 2] = accs[p // 2] + _dot(ws[p], only(v, p), NN)
            return tuple(logw[p][1] for p in heads), tuple(accs)

        zero = jnp.zeros((t, 1), F32)
        wide = jnp.zeros((t, LANES), F32)
        carry = tile(0, (tuple(zero for _ in heads), tuple(wide for _ in heads[::2])), c < r)
        _, accs = lax.fori_loop(1, i + 1, lambda n, cr: tile(n, cr, None), carry)
        for g, acc in enumerate(accs):
            o_ref[:, g * LANES:(g + 1) * LANES] = acc

    w = SB_FWD_STEP_WIDTH
    blk = pl.BlockSpec((t, w), lambda j, i: (i, j))
    in_specs = [blk, pl.BlockSpec((s, w), lambda j, i: (0, ng + j)), pl.BlockSpec((s, w), lambda j, i: (0, 2 * ng + j))]
    out_specs = [blk]
    out_shape = [jax.ShapeDtypeStruct((s, SB_WIDTH), F32)]
    in_specs += [HBM] * nsh
    out_specs += [HBM] * nsh
    out_shape += [jax.ShapeDtypeStruct((N_CHIPS,) + a.shape, a.dtype) for a in shards]
    return pl.pallas_call(
        body, name="sb_fwd_g%d" % nsh, grid=(ng, nq),
        in_specs=in_specs, out_specs=out_specs, out_shape=out_shape,
        scratch_shapes=_gather_scratch(nsh) if nsh else [],
        compiler_params=_cparams(("arbitrary", "arbitrary")),
    )(sbqkv, sbqkv, sbqkv, *shards)


def _swa_band(cur_ref, prev_ref, b):
    lo = prev_ref[...] if b == 0 else cur_ref[(b - 1) * SWA_BLOCK:b * SWA_BLOCK, :]
    return jnp.concatenate([lo, cur_ref[b * SWA_BLOCK:(b + 1) * SWA_BLOCK, :]], axis=0)


def _swa_variants(band, left):
    f = band.astype(F32)
    sw = pltpu.roll(f, HEAD_DIM, 1)
    halves = (left, jnp.logical_not(left))
    return [[jnp.where(halves[p], f if p == g else sw, 0.0).astype(BF16) for p in range(2)] for g in range(2)]


def _swa_valid(blk):
    ii = lax.broadcasted_iota(jnp.int32, (SWA_BLOCK, 2 * SWA_BLOCK), 0)
    jj = lax.broadcasted_iota(jnp.int32, (SWA_BLOCK, 2 * SWA_BLOCK), 1)
    rel = jj - SWA_BLOCK - ii
    return (rel <= 0) & (rel > -SWA_BLOCK) & (jj + blk * SWA_BLOCK >= SWA_BLOCK)


def _swa_softmax(dots, sink, valid):
    sc = jnp.where(valid, dots * SCALE, -1e30)
    m = jnp.maximum(jnp.max(sc, axis=-1, keepdims=True), sink)
    e = jnp.exp(sc - m)
    es = jnp.exp(sink - m)
    inv = 1.0 / (jnp.sum(e, axis=-1, keepdims=True) + es)
    return e * inv, es * inv


def _swa_heads(q_ref, rows, halves):
    nheads = SWA_Q_WIDTH // HEAD_DIM
    out = []
    for h in range(nheads):
        hb, p = h // 2, h % 2
        q2 = q_ref[rows, hb * LANES:(hb + 1) * LANES]
        out.append((hb, p, h // (nheads // 2), jnp.where(halves[p], q2, jnp.zeros_like(q2))))
    return out


def _swa_specs(s, t):
    nb = t // SWA_BLOCK
    cur = pl.BlockSpec((t, LANES), lambda i: (i, 0))
    prev = pl.BlockSpec((SWA_BLOCK, LANES), lambda i: (jnp.maximum(i * nb - 1, 0), 0))
    return cur, prev


def _swa_fwd(qn, kn, v, sinks):
    s = qn.shape[0]
    t = min(512, s)
    nb = t // SWA_BLOCK
    nheads = SWA_Q_WIDTH // HEAD_DIM

    def body(sink_ref, q_ref, kc_ref, kp_ref, vc_ref, vp_ref, o_ref):
        i = pl.program_id(0)
        left = _lane((1, LANES)) < HEAD_DIM
        halves = (left, jnp.logical_not(left))
        for b in range(nb):
            kvar = _swa_variants(_swa_band(kc_ref, kp_ref, b), left)
            vvar = _swa_variants(_swa_band(vc_ref, vp_ref, b), left)
            rows = slice(b * SWA_BLOCK, (b + 1) * SWA_BLOCK)
            valid = _swa_valid(i * nb + b)
            heads = _swa_heads(q_ref, rows, halves)
            dots = [_dot(qh, kvar[g][p], NT) for _, p, g, qh in heads]
            probs = [_swa_softmax(dots[h], sink_ref[h], valid)[0].astype(BF16) for h in range(nheads)]
            outs = [_dot(probs[h], vvar[g][p], NN) for h, (_, p, g, _) in enumerate(heads)]
            for hb in range(nheads // 2):
                o_ref[rows, hb * LANES:(hb + 1) * LANES] = (outs[2 * hb] + outs[2 * hb + 1]).astype(BF16)

    cur, prev = _swa_specs(s, t)
    return pl.pallas_call(
        body, name="swa_fwd", grid=(s // t,),
        in_specs=[pl.BlockSpec(memory_space=pltpu.SMEM), _rows(t, SWA_Q_WIDTH), cur, prev, cur, prev],
        out_specs=_rows(t, SWA_Q_WIDTH),
        out_shape=jax.ShapeDtypeStruct((s, SWA_Q_WIDTH), BF16),
        compiler_params=_cparams(("arbitrary",)),
    )(sinks, qn, kn, kn, v, v)


def _merge_fwd(x, osb, oswa, gates, wsb, wswa, rest):
    s = x.shape[0]
    tm = min(512, s)
    rc = D_MODEL // N_CHIPS

    def body(x_ref, osb_ref, oswa_ref, gate_ref, wsb_ref, wswa_ref, wout_ref, x1_ref):
        ysb = _dot(osb_ref[...].astype(BF16), wsb_ref[...], NN)
        yswa = _dot(oswa_ref[...], wswa_ref[...], NN)
        merged = (gate_ref[:, :D_MODEL] * ysb + gate_ref[:, D_MODEL:] * yswa).astype(BF16)
        acc = x_ref[...]
        for q in range(N_CHIPS):
            acc = acc + _dot(merged[:, q * rc:(q + 1) * rc], wout_ref[q], NN)
        x1_ref[...] = acc

    return pl.pallas_call(
        body, name="merge_fwd", grid=(s // tm,),
        in_specs=[_rows(tm, D_MODEL), _rows(tm, SB_WIDTH), _rows(tm, SWA_Q_WIDTH), _rows(tm, 2 * D_MODEL),
                  _resident((SB_WIDTH, D_MODEL)), _resident((SWA_Q_WIDTH, D_MODEL)), _resident(*W_OUT_BLOCK)],
        out_specs=_rows(tm, D_MODEL),
        out_shape=jax.ShapeDtypeStruct((s, D_MODEL), F32),
        compiler_params=_cparams(("arbitrary",)),
    )(x, osb, oswa, gates, wsb, wswa, rest)


def _mlp_fwd(x1, g, rest):
    s = x1.shape[0]
    tm = min(512, s)
    fc = D_FF // N_CHIPS

    def body(x_ref, g_ref, wup_ref, wdown_ref, x2_ref, u_ref):
        xf = x_ref[...]
        xn, _ = _rms(xf)
        h2 = (xn * g_ref[...]).astype(BF16)
        acc = xf

        def act(q):
            u = _dot(h2, wup_ref[q], NN)
            u_ref[:, q * fc:(q + 1) * fc] = u
            r = jnp.maximum(u, 0.0)
            return (r * r).astype(BF16)

        ahead = act(0)
        for q in range(N_CHIPS):
            nxt = act(q + 1) if q + 1 < N_CHIPS else None
            acc = acc + _dot(ahead, wdown_ref[q], NN)
            ahead = nxt
        x2_ref[...] = acc

    return pl.pallas_call(
        body, name="mlp_fwd", grid=(s // tm,),
        in_specs=[_rows(tm, D_MODEL), _resident((1, D_MODEL)), _resident(*W_UP_BLOCK), _resident(*W_DOWN_BLOCK)],
        out_specs=[_rows(tm, D_MODEL), _rows(tm, D_FF)],
        out_shape=[jax.ShapeDtypeStruct((s, D_MODEL), F32), jax.ShapeDtypeStruct((s, D_FF), F32)],
        compiler_params=_cparams(("arbitrary",)),
    )(x1, g, rest, rest)


def _loss_grad(y, target):
    s = y.shape[0]
    tm = min(512, s)

    def body(y_ref, t_ref, dy_ref, part_ref):
        err = y_ref[...] - t_ref[...]
        dy_ref[...] = err * (1.0 / D_MODEL)
        tot = jnp.sum(jnp.sum(err * err, axis=-1, keepdims=True), axis=0, keepdims=True)
        part_ref[...] = jnp.broadcast_to(tot.reshape(1, 1, 1), (1, 8, LANES))

    dy, part = pl.pallas_call(
        body, name="loss_grad", grid=(s // tm,),
        in_specs=[_rows(tm, D_MODEL), _rows(tm, D_MODEL)],
        out_specs=[_rows(tm, D_MODEL), pl.BlockSpec((1, 8, LANES), lambda i: (i, 0, 0))],
        out_shape=[jax.ShapeDtypeStruct((s, D_MODEL), F32), jax.ShapeDtypeStruct((s // tm, 8, LANES), F32)],
        compiler_params=_cparams(("arbitrary",)),
    )(y, target)
    return dy, (0.5 / D_MODEL) * jnp.sum(part[:, 0, 0])


def _mlp_bwd(dx2, x1, u, g, rest):
    s = x1.shape[0]
    tm = min(256, s)
    fc = D_FF // N_CHIPS

    def body(dx2_ref, x_ref, u_ref, g_ref, wup_ref, wdown_ref, dx1_ref, du_ref, a_ref, h2_ref, dg_ref, dxb_ref):
        @pl.when(pl.program_id(0) == 0)
        def _():
            dg_ref[...] = jnp.zeros_like(dg_ref)

        gam = g_ref[...]
        xn, rstd = _rms(x_ref[...])
        h2_ref[...] = (xn * gam).astype(BF16)
        dxf = dx2_ref[...]
        dxb = dxf.astype(BF16)
        dxb_ref[...] = dxb
        dh2 = jnp.zeros((tm, D_MODEL), F32)

        def grad_u(q):
            cols = slice(q * fc, (q + 1) * fc)
            da = _dot(dxb, wdown_ref[q], NT)
            r = jnp.maximum(u_ref[:, cols], 0.0)
            a_ref[:, cols] = (r * r).astype(BF16)
            du = (da * (2.0 * r)).astype(BF16)
            du_ref[:, cols] = du
            return du

        ahead = grad_u(0)
        for q in range(N_CHIPS):
            nxt = grad_u(q + 1) if q + 1 < N_CHIPS else None
            dh2 = dh2 + _dot(ahead, wup_ref[q], NT)
            ahead = nxt
        dg_ref[...] += jnp.sum(dh2 * xn, axis=0, keepdims=True)
        dx1_ref[...] = dxf + _rms_bwd(dh2, xn, rstd, gam)

    return pl.pallas_call(
        body, name="mlp_bwd", grid=(s // tm,),
        in_specs=[_rows(tm, D_MODEL), _rows(tm, D_MODEL), _rows(tm, D_FF), _resident((1, D_MODEL)),
                  _resident(*W_UP_BLOCK), _resident(*W_DOWN_BLOCK)],
        out_specs=[_rows(tm, D_MODEL), _rows(tm, D_FF), _rows(tm, D_FF), _rows(tm, D_MODEL),
                   pl.BlockSpec((1, D_MODEL), lambda i: (0, 0)), _rows(tm, D_MODEL)],
        out_shape=[jax.ShapeDtypeStruct((s, D_MODEL), F32), jax.ShapeDtypeStruct((s, D_FF), BF16),
                   jax.ShapeDtypeStruct((s, D_FF), BF16), jax.ShapeDtypeStruct((s, D_MODEL), BF16),
                   jax.ShapeDtypeStruct((1, D_MODEL), F32), jax.ShapeDtypeStruct((s, D_MODEL), BF16)],
        compiler_params=_cparams(("arbitrary",)),
    )(dx2, x1, u, g, rest, rest)


def _wgrad(a, b, name, tm, tn, shard_axis=None):
    s, m = a.shape
    n = b.shape[1]

    def body(a_ref, b_ref, o_ref, *narrow):
        res = _dot(a_ref[...].astype(BF16), b_ref[...].astype(BF16), TN)
        o_ref[...] = res.reshape(o_ref.shape)
        for n_ref in narrow:
            n_ref[...] = res.astype(BF16).reshape(n_ref.shape)

    if shard_axis is None:
        out_shape, out_spec = (m, n), pl.BlockSpec((tm, tn), lambda i, j: (i, j))
    elif shard_axis == 0:
        per = m // N_CHIPS // tm
        out_shape, out_spec = (N_CHIPS, m // N_CHIPS, n), pl.BlockSpec((1, tm, tn), lambda i, j: (i // per, i % per, j))
    else:
        per = n // N_CHIPS // tn
        out_shape, out_spec = (N_CHIPS, m, n // N_CHIPS), pl.BlockSpec((1, tm, tn), lambda i, j: (j // per, i, j % per))
    both = shard_axis is not None
    a_spec = _resident((s, tm)) if tm == m else pl.BlockSpec((s, tm), lambda i, j: (0, i))
    b_spec = _resident((s, tn)) if tn == n else pl.BlockSpec((s, tn), lambda i, j: (0, j))
    return pl.pallas_call(
        body, name=name, grid=(m // tm, n // tn),
        in_specs=[a_spec, b_spec],
        out_specs=[out_spec, out_spec] if both else out_spec,
        out_shape=[jax.ShapeDtypeStruct(out_shape, F32), jax.ShapeDtypeStruct(out_shape, BF16)] if both
        else jax.ShapeDtypeStruct(out_shape, F32),
        compiler_params=_cparams(("arbitrary", "arbitrary")),
    )(a, b)


def _merge_bwd(dx1, osb, oswa, gates, wsb, wswa, rest):
    s = dx1.shape[0]
    tm = min(512, s)

    def body(dx_ref, osb_ref, oswa_ref, gate_ref, wsb_ref, wswa_ref, wout_ref,
             dosb_ref, doswa_ref, dgl_ref, merged_ref, dysb_ref, dyswa_ref, dxb_ref):
        dxb = dx_ref[...].astype(BF16)
        dxb_ref[...] = dxb
        dm = jnp.concatenate([_dot(dxb, wout_ref[q], NT) for q in range(N_CHIPS)], axis=1)
        ysb = _dot(osb_ref[...].astype(BF16), wsb_ref[...], NN)
        yswa = _dot(oswa_ref[...], wswa_ref[...], NN)
        g0 = gate_ref[:, :D_MODEL]
        g1 = gate_ref[:, D_MODEL:]
        merged_ref[...] = (g0 * ysb + g1 * yswa).astype(BF16)
        dgl_ref[:, :D_MODEL] = (dm * ysb * (g0 * (1.0 - g0))).astype(BF16)
        dgl_ref[:, D_MODEL:] = (dm * yswa * (g1 * (1.0 - g1))).astype(BF16)
        dysb = (dm * g0).astype(BF16)
        dyswa = (dm * g1).astype(BF16)
        dysb_ref[...] = dysb
        dyswa_ref[...] = dyswa
        dosb_ref[...] = _dot(dysb, wsb_ref[...], NT)
        doswa_ref[...] = _dot(dyswa, wswa_ref[...], NT)

    return pl.pallas_call(
        body, name="merge_bwd", grid=(s // tm,),
        in_specs=[_rows(tm, D_MODEL), _rows(tm, SB_WIDTH), _rows(tm, SWA_Q_WIDTH), _rows(tm, 2 * D_MODEL),
                  _resident((SB_WIDTH, D_MODEL)), _resident((SWA_Q_WIDTH, D_MODEL)), _resident(*W_OUT_BLOCK)],
        out_specs=[_rows(tm, SB_WIDTH), _rows(tm, SWA_Q_WIDTH), _rows(tm, 2 * D_MODEL), _rows(tm, D_MODEL),
                   _rows(tm, D_MODEL), _rows(tm, D_MODEL), _rows(tm, D_MODEL)],
        out_shape=[jax.ShapeDtypeStruct((s, SB_WIDTH), F32), jax.ShapeDtypeStruct((s, SWA_Q_WIDTH), F32),
                   jax.ShapeDtypeStruct((s, 2 * D_MODEL), BF16), jax.ShapeDtypeStruct((s, D_MODEL), BF16),
                   jax.ShapeDtypeStruct((s, D_MODEL), BF16), jax.ShapeDtypeStruct((s, D_MODEL), BF16),
                   jax.ShapeDtypeStruct((s, D_MODEL), BF16)],
        compiler_params=_cparams(("arbitrary",)),
    )(dx1, osb, oswa, gates, wsb, wswa, rest)


def _sb_bwd(sbqkv, osb, dosb, send):
    s = sbqkv.shape[0]
    t = min(SB_TILE, s)
    nq = s // t
    ng = SB_WIDTH // SB_STEP_WIDTH
    heads = range(2 * SB_STEP_WIDTH // LANES)
    ns = len(send)

    def body(*refs):
        q_ref, k_ref, v_ref, o_ref, do_ref = refs[:5]
        srcs = refs[5:5 + ns]
        dq_ref, dk_ref, dv_ref = refs[5 + ns:8 + ns]
        outs = refs[8 + ns:8 + 2 * ns]
        j, i = pl.program_id(0), pl.program_id(1)
        if ns:
            ssem, rsem = refs[8 + 2 * ns:]
            _hosted_exchange(srcs, outs, ssem, rsem, (j == 0) & (i == 0), (j == ng - 1) & (i == nq - 1))

        @pl.when(i == 0)
        def _():
            dk_ref[...] = jnp.zeros_like(dk_ref)
            dv_ref[...] = jnp.zeros_like(dv_ref)

        left = _lane((1, LANES)) < HEAD_DIM
        halves = (left, jnp.logical_not(left))
        pair = lambda a, p: a[:, (p // 2) * LANES:(p // 2 + 1) * LANES]
        only = lambda a, p: jnp.where(halves[p % 2], pair(a, p), jnp.zeros((), a.dtype))
        scale = jnp.asarray(SCALE, BF16)
        q_all = q_ref[...] * scale
        do_all = do_ref[...].astype(BF16)
        prod = do_all.astype(F32) * o_ref[...]
        qh = [only(q_all, p) for p in heads]
        doh = [only(do_all, p) for p in heads]
        delta = [jnp.sum(only(prod, p), axis=-1, keepdims=True) for p in heads]
        r = lax.broadcasted_iota(jnp.int32, (t, t), 0)
        c = lax.broadcasted_iota(jnp.int32, (t, t), 1)
        u_from = (r >= c).astype(BF16)
        u_from2 = _tri_twice(t)

        def tile(n, carry, strict):
            runs, dqs = carry
            off = pl.multiple_of((i - n) * t, t)
            k = k_ref[pl.ds(off, t), :]
            v = v_ref[pl.ds(off, t), :]
            ks = k * scale
            terms = [_sb_tile_terms(qh[p], pair(k, p), strict) for p in heads]
            logw = [_log_weights(*terms[p], runs[p][0], u_from) for p in heads]
            dws = [_dot(doh[p], pair(v, p), NT) for p in heads]
            wbs = [_masked(strict, jnp.exp(logw[p][0])).astype(BF16) for p in heads]
            es = [dws[p] * wbs[p].astype(F32) for p in heads]
            rests = [runs[p][1] + _split_dot(es[p], u_from2) for p in heads]
            betas = [jnp.exp(terms[p][0] + terms[p][1]) for p in heads]
            dzs = [_masked(strict, es[p] - betas[p] * (es[p] - rests[p])).astype(BF16) for p in heads]
            dqs = list(dqs)
            for g in range(len(heads) // 2):
                a, b = 2 * g, 2 * g + 1
                cols = slice(g * LANES, (g + 1) * LANES)
                dv_ref[pl.ds(off, t), cols] += _dot(wbs[a], doh[a], TN) + _dot(wbs[b], doh[b], TN)
                dk_ref[pl.ds(off, t), cols] += _dot(dzs[a], qh[a], TN) + _dot(dzs[b], qh[b], TN)
                dqs[g] = dqs[g] + _dot(dzs[a], only(ks, a), NN) + _dot(dzs[b], only(ks, b), NN)
            new_runs = tuple((logw[p][1], rests[p][:, 0:1]) for p in heads)
            return new_runs, tuple(dqs)

        zero = jnp.zeros((t, 1), F32)
        wide = jnp.zeros((t, LANES), F32)
        carry = tile(0, (tuple((zero, -delta[p]) for p in heads), tuple(wide for _ in heads[::2])), c < r)
        _, dqs = lax.fori_loop(1, i + 1, lambda n, cr: tile(n, cr, None), carry)
        for g, dq in enumerate(dqs):
            dq_ref[:, g * LANES:(g + 1) * LANES] = dq.astype(BF16)

    w = SB_STEP_WIDTH
    blk = pl.BlockSpec((t, w), lambda j, i: (i, j))
    whole = pl.BlockSpec((s, w), lambda j, i: (0, j))
    return pl.pallas_call(
        body, name="sb_bwd_x%d" % ns, grid=(ng, nq),
        in_specs=[blk, pl.BlockSpec((s, w), lambda j, i: (0, ng + j)),
                  pl.BlockSpec((s, w), lambda j, i: (0, 2 * ng + j)), blk, blk] + [HBM] * ns,
        out_specs=[blk, whole, whole] + [HBM] * ns,
        out_shape=[jax.ShapeDtypeStruct((s, SB_WIDTH), BF16), jax.ShapeDtypeStruct((s, SB_WIDTH), F32),
                   jax.ShapeDtypeStruct((s, SB_WIDTH), F32)] + [jax.ShapeDtypeStruct(a.shape, a.dtype) for a in send],
        scratch_shapes=_exchange_scratch(ns) if ns else [],
        compiler_params=_cparams(("arbitrary", "arbitrary")),
    )(sbqkv, sbqkv, sbqkv, osb, dosb, *send)


def _swa_bwd(qn, kn, v, sinks, do):
    s = qn.shape[0]
    t = min(512, s)
    nb = t // SWA_BLOCK
    nheads = SWA_Q_WIDTH // HEAD_DIM

    def body(sink_ref, q_ref, kc_ref, kp_ref, vc_ref, vp_ref, do_ref, dq_ref, dk_ref, dv_ref, dsink_ref):
        i = pl.program_id(0)

        @pl.when(i == 0)
        def _():
            dk_ref[...] = jnp.zeros_like(dk_ref)
            dv_ref[...] = jnp.zeros_like(dv_ref)

        left = _lane((1, LANES)) < HEAD_DIM
        halves = (left, jnp.logical_not(left))
        dsink = [jnp.zeros((1, 1), F32) for _ in range(nheads)]
        for b in range(nb):
            kvar = _swa_variants(_swa_band(kc_ref, kp_ref, b), left)
            vvar = _swa_variants(_swa_band(vc_ref, vp_ref, b), left)
            rows = slice(b * SWA_BLOCK, (b + 1) * SWA_BLOCK)
            valid = _swa_valid(i * nb + b)
            heads = _swa_heads(q_ref, rows, halves)
            doh = []
            for hb, p, _, _ in heads:
                do2 = do_ref[rows, hb * LANES:(hb + 1) * LANES]
                doh.append(jnp.where(halves[p], do2, 0.0).astype(BF16))
            dots = [_dot(qh, kvar[g][p], NT) for _, p, g, qh in heads]
            dps = [_dot(doh[h], vvar[g][p], NT) for h, (_, p, g, _) in enumerate(heads)]
            dss, pbs = [], []
            for h in range(nheads):
                probs, psink = _swa_softmax(dots[h], sink_ref[h], valid)
                delta = jnp.sum(probs * dps[h], axis=-1, keepdims=True)
                dss.append((probs * (dps[h] - delta) * SCALE).astype(BF16))
                pbs.append(probs.astype(BF16))
                dsink[h] = dsink[h] - jnp.sum(psink * delta, axis=0, keepdims=True)
            dk_acc = [jnp.zeros((2 * SWA_BLOCK, LANES), F32) for _ in range(2)]
            dv_acc = [jnp.zeros((2 * SWA_BLOCK, LANES), F32) for _ in range(2)]
            dqs = [_dot(dss[h], kvar[g][p], NN) for h, (_, p, g, _) in enumerate(heads)]
            for h, (_, p, g, qh) in enumerate(heads):
                which = 0 if p == g else 1
                dk_acc[which] = dk_acc[which] + _dot(dss[h], qh, TN)
                dv_acc[which] = dv_acc[which] + _dot(pbs[h], doh[h], TN)
            for hb in range(nheads // 2):
                dq_ref[rows, hb * LANES:(hb + 1) * LANES] = dqs[2 * hb] + dqs[2 * hb + 1]
            dkb = dk_acc[0] + pltpu.roll(dk_acc[1], HEAD_DIM, 1)
            dvb = dv_acc[0] + pltpu.roll(dv_acc[1], HEAD_DIM, 1)
            start = pl.multiple_of((i * nb + b) * SWA_BLOCK, SWA_BLOCK)
            dk_ref[pl.ds(start, SWA_BLOCK), :] += dkb[SWA_BLOCK:]
            dv_ref[pl.ds(start, SWA_BLOCK), :] += dvb[SWA_BLOCK:]

            @pl.when(i * nb + b > 0)
            def _(dkb=dkb, dvb=dvb, start=start):
                before = pl.multiple_of(jnp.maximum(start - SWA_BLOCK, 0), SWA_BLOCK)
                dk_ref[pl.ds(before, SWA_BLOCK), :] += dkb[:SWA_BLOCK]
                dv_ref[pl.ds(before, SWA_BLOCK), :] += dvb[:SWA_BLOCK]

        for h in range(nheads):
            dsink_ref[0, h:h + 1, :] = jnp.broadcast_to(dsink[h], (1, LANES))

    cur, prev = _swa_specs(s, t)
    whole = pl.BlockSpec((s, LANES), lambda i: (0, 0))
    return pl.pallas_call(
        body, name="swa_bwd", grid=(s // t,),
        in_specs=[pl.BlockSpec(memory_space=pltpu.SMEM), _rows(t, SWA_Q_WIDTH), cur, prev, cur, prev,
                  _rows(t, SWA_Q_WIDTH)],
        out_specs=[_rows(t, SWA_Q_WIDTH), whole, whole, pl.BlockSpec((1, 8, LANES), lambda i: (i, 0, 0))],
        out_shape=[jax.ShapeDtypeStruct((s, SWA_Q_WIDTH), F32), jax.ShapeDtypeStruct((s, SWA_KV_WIDTH), F32),
                   jax.ShapeDtypeStruct((s, SWA_KV_WIDTH), F32), jax.ShapeDtypeStruct((s // t, 8, LANES), F32)],
        compiler_params=_cparams(("arbitrary",)),
    )(sinks, qn, kn, kn, v, v, do)


def _swa_post(raw, dqn, dkn, qg, kg, cos, sin):
    s = raw.shape[0]
    tm = min(512, s)
    nq = SWA_Q_WIDTH // LANES

    def body(raw_ref, dq_ref, dk_ref, qg_ref, kg_ref, cos_ref, sin_ref, out_ref, dg_ref):
        @pl.when(pl.program_id(0) == 0)
        def _():
            dg_ref[...] = jnp.zeros_like(dg_ref)

        cs, sn = cos_ref[...], sin_ref[...]
        dgq = jnp.zeros((1, LANES), F32)
        for b in range(nq):
            cols = slice(b * LANES, (b + 1) * LANES)
            dp, dg = _norm_rope_bwd(raw_ref[:, cols], dq_ref[:, cols], qg_ref[...], cs, sn)
            out_ref[:, cols] = dp.astype(BF16)
            dgq = dgq + dg
        cols = slice(SWA_Q_WIDTH, SWA_Q_WIDTH + LANES)
        dp, dgk = _norm_rope_bwd(raw_ref[:, cols], dk_ref[...], kg_ref[...], cs, sn)
        out_ref[:, cols] = dp.astype(BF16)
        dg_ref[0:1, :] += dgq
        dg_ref[1:2, :] += dgk

    return pl.pallas_call(
        body, name="swa_post", grid=(s // tm,),
        in_specs=[_rows(tm, 640), _rows(tm, SWA_Q_WIDTH), _rows(tm, LANES), _resident((1, LANES)),
                  _resident((1, LANES)), _rows(tm, LANES), _rows(tm, LANES)],
        out_specs=[_rows(tm, 640), pl.BlockSpec((8, LANES), lambda i: (0, 0))],
        out_shape=[jax.ShapeDtypeStruct((s, 640), BF16), jax.ShapeDtypeStruct((8, LANES), F32)],
        compiler_params=_cparams(("arbitrary",)),
    )(raw, dqn, dkn, qg, kg, cos, sin)


def _inproj_bwd(dsbq, dsbk, dsbv, dswqk, dswv, dgl, x, dx1, g, w):
    s = x.shape[0]
    tm = min(512, s)

    def body(dsbq_ref, dsbk_ref, dsbv_ref, dswqk_ref, dswv_ref, dgl_ref, x_ref, dx1_ref, g_ref, w_ref,
             dx_ref, dproj_ref, dg_ref):
        @pl.when(pl.program_id(0) == 0)
        def _():
            dg_ref[...] = jnp.zeros_like(dg_ref)

        dproj_ref[:, 0:512] = dsbq_ref[...]
        dproj_ref[:, 512:1024] = dsbk_ref[...].astype(BF16)
        dproj_ref[:, 1024:1536] = dsbv_ref[...].astype(BF16)
        dproj_ref[:, 1536:2176] = dswqk_ref[...]
        dproj_ref[:, 2176:2304] = dswv_ref[...].astype(BF16)
        dproj_ref[:, GATE_OFF:IN_WIDTH] = dgl_ref[...]
        dh = jnp.zeros((tm, D_MODEL), F32)
        for a in range(0, IN_WIDTH, 512):
            b = min(a + 512, IN_WIDTH)
            dh = dh + _dot(dproj_ref[:, a:b], w_ref[:, a:b], NT)
        gam = g_ref[...]
        xn, rstd = _rms(x_ref[...])
        dg_ref[...] += jnp.sum(dh * xn, axis=0, keepdims=True)
        dx_ref[...] = dx1_ref[...] + _rms_bwd(dh, xn, rstd, gam)

    return pl.pallas_call(
        body, name="inproj_bwd", grid=(s // tm,),
        in_specs=[_rows(tm, 512), _rows(tm, 512), _rows(tm, 512), _rows(tm, 640), _rows(tm, LANES),
                  _rows(tm, 2 * D_MODEL), _rows(tm, D_MODEL), _rows(tm, D_MODEL), _resident((1, D_MODEL)),
                  _resident((D_MODEL, IN_WIDTH))],
        out_specs=[_rows(tm, D_MODEL), _rows(tm, IN_WIDTH), pl.BlockSpec((1, D_MODEL), lambda i: (0, 0))],
        out_shape=[jax.ShapeDtypeStruct((s, D_MODEL), F32), jax.ShapeDtypeStruct((s, IN_WIDTH), BF16),
                   jax.ShapeDtypeStruct((1, D_MODEL), F32)],
        compiler_params=_cparams(("arbitrary",)),
    )(dsbq, dsbk, dsbv, dswqk, dswv, dgl, x, dx1, g, w)


def _gather_weights(shard):
    def body(src, out, ssem, rsem, lsem):
        once = pl.program_id(0) == 0
        _hosted_gather(src, out, ssem, rsem, lsem, once, once, once)

    return pl.pallas_call(
        body, name="gather_weights", grid=(1,), in_specs=[HBM], out_specs=HBM,
        out_shape=jax.ShapeDtypeStruct((N_CHIPS,) + shard.shape, shard.dtype),
        scratch_shapes=_gather_scratch(1),
        compiler_params=pltpu.CompilerParams(dimension_semantics=("arbitrary",), has_side_effects=True),
    )(shard)


def _pair_swap(arrs, send=()):
    n, ns = len(arrs), len(send)

    def body(*refs):
        mine, src = refs[:n], refs[n:n + ns]
        theirs, landed = refs[n + ns:2 * n + ns], refs[2 * n + ns:2 * (n + ns)]
        ssem, rsem = refs[2 * (n + ns):2 * (n + ns) + 2]
        once, never = pl.program_id(0) == 0, pl.program_id(0) < 0
        if ns:
            _hosted_exchange(src, landed, *refs[2 * (n + ns) + 2:], once, never)
        x, y, c, _ = _place()
        cps = [pltpu.make_async_remote_copy(mine[t], theirs[t], ssem.at[t], rsem.at[t],
                                            device_id=(x, y, 1 - c), device_id_type=MESH) for t in range(n)]
        for cp in cps:
            cp.start()
        for cp in cps:
            cp.wait()
        if ns:
            _hosted_exchange(src, landed, *refs[2 * (n + ns) + 2:], never, once)

    return pl.pallas_call(
        body, name="pair_swap_x%d" % ns, grid=(1,), in_specs=[HBM] * (n + ns), out_specs=[HBM] * (n + ns),
        out_shape=[jax.ShapeDtypeStruct(a.shape, a.dtype) for a in tuple(arrs) + tuple(send)],
        scratch_shapes=[pltpu.SemaphoreType.DMA((n,)), pltpu.SemaphoreType.DMA((n,))] + (_exchange_scratch(ns) if ns else []),
        compiler_params=pltpu.CompilerParams(dimension_semantics=("arbitrary",), has_side_effects=True),
    )(*arrs, *send)


SEM = pl.BlockSpec(memory_space=pltpu.SEMAPHORE)
HBM_ONLY = pl.BlockSpec(memory_space=pltpu.HBM)
DATAFLOW = pltpu.SideEffectType.DATAFLOW_SIDE_EFFECTING


def _exchange_start(src):
    def body(src_ref, land_ref, ssem, rsem, src_thru, land_thru, token):
        x, y, c, chips = _place()
        for j, (cx, cy) in enumerate(chips):
            pltpu.make_async_remote_copy(src_ref.at[2 * cx + cy], land_ref.at[2 * x + y], ssem.at[j], rsem.at[j],
                                         device_id=(cx, cy, c), device_id_type=MESH).start()
        token[...] = jnp.zeros_like(token)

    return pl.pallas_call(
        body, name="exchange_start",
        out_shape=(pltpu.SemaphoreType.DMA((3,)), pltpu.SemaphoreType.DMA((3,)), pltpu.HBM(src.shape, src.dtype),
                   pltpu.HBM(src.shape, src.dtype), jax.ShapeDtypeStruct((8, LANES), F32)),
        in_specs=(HBM_ONLY, HBM_ONLY), out_specs=(SEM, SEM, HBM_ONLY, HBM_ONLY, pl.BlockSpec(memory_space=pltpu.VMEM)),
        input_output_aliases={0: 2, 1: 3},
        compiler_params=pltpu.CompilerParams(has_side_effects=DATAFLOW),
    )(pltpu.with_memory_space_constraint(src, pltpu.HBM),
      pltpu.with_memory_space_constraint(lax.empty(src.shape, src.dtype), pltpu.HBM))


def _exchange_wait(ssem, rsem, src_thru, land_thru, after):
    def body(src_ref, land_ref, ssem, rsem, after_ref, src_dead, got_ref):
        x, y, c, chips = _place()
        for j, (cx, cy) in enumerate(chips):
            cp = pltpu.make_async_remote_copy(src_ref.at[2 * cx + cy], land_ref.at[2 * cx + cy], ssem.at[j], rsem.at[j],
                                              device_id=(cx, cy, c), device_id_type=MESH)
            cp.wait_send()
            cp.wait_recv()

    return pl.pallas_call(
        body, name="exchange_wait",
        out_shape=(pltpu.HBM(src_thru.shape, src_thru.dtype), pltpu.HBM(land_thru.shape, land_thru.dtype)),
        in_specs=(HBM_ONLY, HBM_ONLY, SEM, SEM, pl.BlockSpec(memory_space=pl.ANY)), out_specs=(HBM_ONLY, HBM_ONLY),
        input_output_aliases={0: 0, 1: 1},
        compiler_params=pltpu.CompilerParams(has_side_effects=DATAFLOW),
    )(src_thru, land_thru, ssem, rsem, after)[1]


def _allreduce_small(block):
    def body(src, out, buf, ssem, rsem):
        x, y, c, _ = _place()
        me = 4 * x + 2 * y + c
        buf[me] = src[...]
        cps = []
        for k in range(1, N_DEV):
            peer = (x ^ (k >> 2), y ^ ((k >> 1) & 1), c ^ (k & 1))
            cp = pltpu.make_async_remote_copy(src, buf.at[me], ssem.at[k - 1], rsem.at[k - 1], device_id=peer, device_id_type=MESH)
            cp.start()
            cps.append(cp)
        for k in range(1, N_DEV):
            got = buf.at[me ^ k]
            pltpu.make_async_remote_copy(got, got, ssem.at[k - 1], rsem.at[k - 1], device_id=(x, y, c), device_id_type=MESH).wait_recv()
        for cp in cps:
            cp.wait_send()
        tot = buf[0]
        for d in range(1, N_DEV):
            tot = tot + buf[d]
        out[...] = tot

    vm = pl.BlockSpec(memory_space=pltpu.VMEM)
    return pl.pallas_call(
        body, name="allreduce_small", in_specs=[vm], out_specs=vm,
        out_shape=jax.ShapeDtypeStruct(block.shape, F32),
        scratch_shapes=[pltpu.VMEM((N_DEV,) + block.shape, F32), pltpu.SemaphoreType.DMA((N_DEV - 1,)),
                        pltpu.SemaphoreType.DMA((N_DEV - 1,))],
        compiler_params=pltpu.CompilerParams(has_side_effects=True),
    )(block)


def _sum_chips(landed, own, into, layer, name):
    nq, k, n = landed.shape
    tr = min(256, k) if k % 256 == 0 else k // 4
    me = (2 * lax.axis_index("x") + lax.axis_index("y")).astype(jnp.int32).reshape(1)

    def body(me_ref, p0, p1, p2, p3, own_ref, into_ref, o_ref):
        mine = own_ref[0]
        terms = [jnp.where(me_ref[0] == q, mine, p[0].astype(F32)) for q, p in enumerate((p0, p1, p2, p3))]
        o_ref[0] = ((terms[0] + terms[1]) + terms[2]) + terms[3]

    spec = lambda q: pl.BlockSpec((1, tr, n), lambda i, m, q=q: (q, i, 0))
    return pl.pallas_call(
        body, name=name,
        grid_spec=pltpu.PrefetchScalarGridSpec(
            num_scalar_prefetch=1, grid=(k // tr,),
            in_specs=[spec(q) for q in range(nq)] + [pl.BlockSpec((1, tr, n), lambda i, m: (m[0], i, 0)), HBM],
            out_specs=pl.BlockSpec((1, tr, n), lambda i, m: (layer, i, 0))),
        out_shape=jax.ShapeDtypeStruct(into.shape, F32),
        input_output_aliases={6: 0},
        compiler_params=_cparams(("arbitrary",)),
    )(me, landed, landed, landed, landed, own, into)


def _adamw(w, g, g2, m, v, name):
    shape = w.shape
    cols = shape[-1]
    flat = lambda t: t.reshape(-1, cols)
    rows = flat(w).shape[0]
    tr = min(512, rows) if rows % 512 == 0 or rows < 512 else 256
    pair = g2 is not None

    def body(*refs):
        if pair:
            w_ref, g_ref, g2_ref, m_ref, v_ref, go_ref, d_ref, nm_ref, nv_ref = refs
            gr = g_ref[...] + g2_ref[...]
        else:
            w_ref, g_ref, m_ref, v_ref, go_ref, d_ref, nm_ref, nv_ref = refs
            gr = g_ref[...]
        go_ref[...] = gr
        nm = ADAM_B1 * m_ref[...] + (1.0 - ADAM_B1) * gr
        nv = ADAM_B2 * v_ref[...] + (1.0 - ADAM_B2) * (gr * gr)
        m_hat = nm / (1.0 - ADAM_B1 ** ADAM_STEP)
        v_hat = nv / (1.0 - ADAM_B2 ** ADAM_STEP)
        d_ref[...] = -ADAM_LR * (m_hat / (jnp.sqrt(v_hat) + ADAM_EPS) + ADAM_WD * w_ref[...])
        nm_ref[...] = nm
        nv_ref[...] = nv

    spec = pl.BlockSpec((tr, cols), lambda i: (i, 0))
    ins = [w, g] + ([g2] if pair else []) + [m, v]
    outs = pl.pallas_call(
        body, name=name, grid=(rows // tr,),
        in_specs=[spec] * len(ins), out_specs=[spec] * 4,
        out_shape=[jax.ShapeDtypeStruct((rows, cols), F32)] * 4,
        compiler_params=_cparams(("arbitrary",)),
    )(*[flat(t) for t in ins])
    return [o.reshape(shape) for o in outs]


def _pack_rest(big, l):
    rows = dict(PACK_ROWS)
    return jnp.concatenate([big[name][l].astype(BF16).reshape(rows[name], 1024) for name in REST], axis=0)


def _whole_columns(t):
    return jnp.moveaxis(t, 0, 1).reshape(t.shape[1], N_CHIPS * t.shape[2])


def _branch_weights(rest):
    out, at = [], REST_ROWS - 256
    for _ in range(2):
        out.append(_whole_columns(rest[:, at:at + 128, :].reshape((N_CHIPS,) + SHARD_SHAPES["w_branch_sb"])))
        at += 128
    return out


def _pack_small(d):
    flat = jnp.concatenate([d[n].reshape(-1) for n in SMALL_NAMES])
    return jnp.pad(flat, (0, SMALL_ROWS * LANES - flat.shape[0])).reshape(SMALL_ROWS, LANES)


def _unpack_small(block, like):
    flat, out, at = block.reshape(-1), {}, 0
    for n in SMALL_NAMES:
        size = like[n].size
        out[n] = flat[at:at + size].reshape(like[n].shape)
        at += size
    return out


def _rope_tables(s):
    inv_freq = 1.0 / (ROPE_THETA ** (jnp.arange(0, HEAD_DIM, 2, dtype=F32) / HEAD_DIM))
    ang = jnp.arange(s, dtype=F32)[:, None] * inv_freq[None, :]
    reps = LANES // (HEAD_DIM // 2)
    return jnp.tile(jnp.cos(ang), (1, reps)), jnp.tile(jnp.sin(ang), (1, reps))


def _forward_backward(x, target, big, small):
    s = x.shape[0]
    cos, sin = _rope_tables(s)
    two = lambda gvec: jnp.tile(gvec.reshape(1, HEAD_DIM), (1, 2))
    saved = []
    win_shard = lambda l: big["w_in"][l].astype(BF16)
    win = _whole_columns(_gather_weights(win_shard(0)))
    for l in range(DEPTH):
        gm = small["mix_norm_g"][l].reshape(1, D_MODEL)
        gl = small["mlp_norm_g"][l].reshape(1, D_MODEL)
        qg, kg = two(small["q_norm_g"][l]), two(small["k_norm_g"][l])
        h, sbqkv, raw, qn, kn, v, gates = _inproj_fwd(x, gm, win, qg, kg, cos, sin)
        osb, rest, *nxt = _sb_fwd(sbqkv, [_pack_rest(big, l)] + ([win_shard(l + 1)] if l + 1 < DEPTH else []))
        wsb, wswa = _branch_weights(rest)
        oswa = _swa_fwd(qn, kn, v, small["sinks"][l])
        x1 = _merge_fwd(x, osb, oswa, gates, wsb, wswa, rest)
        x2, u = _mlp_fwd(x1, gl, rest)
        saved.append((x, h, sbqkv, raw, qn, kn, v, gates, osb, oswa, x1, u, gm, gl, qg, kg, win, wsb, wswa, rest))
        x, win = x2, (_whole_columns(nxt[0]) if nxt else None)
    dx, loss = _loss_grad(x, target)

    got = {name: [None] * DEPTH for name in BIG}
    gsmall = {name: [None] * DEPTH for name in SMALL_NAMES}
    late = []
    for l in reversed(range(DEPTH)):
        x0, h, sbqkv, raw, qn, kn, v, gates, osb, oswa, x1, u, gm, gl, qg, kg, win, wsb, wswa, rest = saved[l]
        dx1, du, act, h2, dgl_mlp, dxb = _mlp_bwd(dx, x1, u, gl, rest)
        dosb, doswa, dgate, merged, dysb, dyswa, dx1b = _merge_bwd(dx1, osb, oswa, gates, wsb, wswa, rest)
        ready = [("w_down", l, *_wgrad(act, dxb, "wgrad_down", 512, 1024, shard_axis=0)),
                 ("w_up", l, *_wgrad(h2, du, "wgrad_up", 1024, 512, shard_axis=1)),
                 ("w_out", l, *_wgrad(merged, dx1b, "wgrad_out", 256, 1024, shard_axis=0)),
                 ("w_branch_sb", l, *_wgrad(osb, dysb, "wgrad_bsb", 512, 256, shard_axis=1)),
                 ("w_branch_swa", l, *_wgrad(oswa, dyswa, "wgrad_bswa", 512, 256, shard_axis=1))] + late
        dsbq, dsbk, dsbv, *landed = _sb_bwd(sbqkv, osb, dosb, [narrow for _, _, _, narrow in ready])
        for (name, layer, own, _), arr in zip(ready, landed):
            got[name][layer] = (arr, own)
        dqn, dkn, dswv, dsink = _swa_bwd(qn, kn, v, small["sinks"][l], doswa)
        dswqk, dgqk = _swa_post(raw, dqn, dkn, qg, kg, cos, sin)
        dx, dproj, dg_mix = _inproj_bwd(dsbq, dsbk, dsbv, dswqk, dswv, dgate, x0, dx1, gm, win)
        dwin = _wgrad(h, dproj, "wgrad_in", 1024, 256)
        dwin = dwin.T.reshape(N_CHIPS, IN_WIDTH // N_CHIPS, D_MODEL)
        late = [("w_in", l, dwin, dwin.astype(BF16))]
        gsmall["mix_norm_g"][l] = dg_mix[0]
        gsmall["mlp_norm_g"][l] = dgl_mlp[0]
        gsmall["q_norm_g"][l] = dgqk[0, :HEAD_DIM] + dgqk[0, HEAD_DIM:]
        gsmall["k_norm_g"][l] = dgqk[1, :HEAD_DIM] + dgqk[1, HEAD_DIM:]
        gsmall["sinks"][l] = jnp.sum(dsink[:, :, 0], axis=0)
    gsmall = {k: jnp.stack(vs) for k, vs in gsmall.items()}
    return loss, dx, got, late[0], gsmall


def kernel(x, mix_norm_g, w_in, q_norm_g, k_norm_g, sinks, w_branch_sb, w_branch_swa, w_out, mlp_norm_g, w_up, w_down, loss_target, m_mix_norm_g, m_w_in, m_q_norm_g, m_k_norm_g, m_sinks, m_w_branch_sb, m_w_branch_swa, m_w_out, m_mlp_norm_g, m_w_up, m_w_down, v_mix_norm_g, v_w_in, v_q_norm_g, v_k_norm_g, v_sinks, v_w_branch_sb, v_w_branch_swa, v_w_out, v_mlp_norm_g, v_w_up, v_w_down):
    big = dict(w_in=w_in, w_branch_sb=w_branch_sb, w_branch_swa=w_branch_swa, w_out=w_out, w_up=w_up, w_down=w_down)
    big_m = dict(w_in=m_w_in, w_branch_sb=m_w_branch_sb, w_branch_swa=m_w_branch_swa, w_out=m_w_out, w_up=m_w_up, w_down=m_w_down)
    big_v = dict(w_in=v_w_in, w_branch_sb=v_w_branch_sb, w_branch_swa=v_w_branch_swa, w_out=v_w_out, w_up=v_w_up, w_down=v_w_down)
    small = dict(mix_norm_g=mix_norm_g, q_norm_g=q_norm_g, k_norm_g=k_norm_g, sinks=sinks, mlp_norm_g=mlp_norm_g)
    small_m = dict(mix_norm_g=m_mix_norm_g, q_norm_g=m_q_norm_g, k_norm_g=m_k_norm_g, sinks=m_sinks, mlp_norm_g=m_mlp_norm_g)
    small_v = dict(mix_norm_g=v_mix_norm_g, q_norm_g=v_q_norm_g, k_norm_g=v_k_norm_g, sinks=v_sinks, mlp_norm_g=v_mlp_norm_g)

    loss_part, grad_x, got, last, gsmall = _forward_backward(x[0], loss_target[0], big, small)
    loss = lax.psum(loss_part, ("x", "y", "c"))

    mine = {}
    for name in BIG:
        shard = SHARD_SHAPES[name][::-1] if name == "w_in" else SHARD_SHAPES[name]
        tot = lax.empty((DEPTH,) + shard, F32)
        for l in range(DEPTH):
            if got[name][l] is not None:
                tot = _sum_chips(*got[name][l], tot, l, "sum_" + name)
        mine[name] = tot
    flip = lambda n, t: jnp.swapaxes(t, 1, 2) if n == "w_in" else t
    adamw = lambda n, theirs: [flip(n, t) for t in _adamw(flip(n, big[n]), mine[n], theirs, flip(n, big_m[n]),
                                                           flip(n, big_v[n]), "adamw_" + n)]
    name, layer, own, narrow = last
    ssem, rsem, src_thru, land_thru, _ = _exchange_start(narrow)
    theirs = dict(zip(REST, _pair_swap([mine[n] for n in REST])))
    upd = {n: adamw(n, theirs[n]) for n in REST}
    landed = _exchange_wait(ssem, rsem, src_thru, land_thru, upd["w_down"][1])
    mine[name] = _sum_chips(landed, own, mine[name], layer, "sum_" + name)
    upd[name] = adamw(name, _pair_swap([mine[name]])[0])
    g_small = _allreduce_small(_pack_small(gsmall))
    sm = _adamw(_pack_small(small), g_small, None, _pack_small(small_m), _pack_small(small_v), "adamw_small")
    upd_small = [_unpack_small(t, small) for t in sm]

    names = ("mix_norm_g", "w_in", "q_norm_g", "k_norm_g", "sinks", "w_branch_sb", "w_branch_swa", "w_out",
             "mlp_norm_g", "w_up", "w_down")
    pick = lambda n, i: upd[n][i] if n in upd else upd_small[i][n]
    return (loss, grad_x[None], *[pick(n, 0) for n in names], *[pick(n, 1) for n in names],
            *[pick(n, 2) for n in names], *[pick(n, 3) for n in names])
```
